```python
import math
import jax
import jax.numpy as jnp
from jax import lax
import numpy as np

D_MODEL = 2048
BATCH = 4
SEQ = 4096
DEPTH = 2

MEM_LEN = 256
HEAD_DIM = 128
ROPE_THETA = 500000.0
ROPE_FRACTION = 4
MIX_WIDTH = D_MODEL // 2
A_HEADS = MIX_WIDTH // HEAD_DIM
A_PATTERNS = ((128, 1), (512, 4), (2048, 16))
B_WINDOWS = (2, 4, 8, 16)
B_WIDTH = MIX_WIDTH
B_GROUP = B_WIDTH // len(B_WINDOWS)
C_QK_DIM = 64
C_V_DIM = 128
C_HEADS = MIX_WIDTH // C_V_DIM
C_HEAD_COLS = 4 * C_QK_DIM + C_V_DIM
C_QBLOCK = 128
D_WIDTH = MIX_WIDTH
CONV_WIDTH = 3
X_HEADS = 4
X_HEAD_DIM = 128
X_WIDTH = X_HEADS * X_HEAD_DIM
N_BRANCH = 5
A_COLS = 3 * MIX_WIDTH
C_COLS = C_HEADS * C_HEAD_COLS
D_COLS = 3 * D_WIDTH
IN_COLS = A_COLS + B_WIDTH + C_COLS + D_COLS + X_WIDTH
IN_SPLITS = (A_COLS, A_COLS + B_WIDTH, A_COLS + B_WIDTH + C_COLS,
             A_COLS + B_WIDTH + C_COLS + D_COLS)
N_EXPERTS = 64
TOP_K = 8
N_GROUPS = 8
TOP_GROUPS = 4
D_EXPERT = 512
ROUTE_SCALE = 2.5
MOE_BLOCK = 128
DN_ALPHA = (2.0 * DEPTH) ** 0.25
DN_BETA = (8.0 * DEPTH) ** -0.25
LN_EPS = 1e-5
NEG_INF = -1e30
F32 = jnp.float32

kernel_name = "hybrid_gated_dilated_pool_diff_conv_moe_encoder"


def layer_norm(x, g, b):
    xf = x.astype(F32)
    mu = jnp.mean(xf, -1, keepdims=True)
    xc = xf - mu
    var = jnp.mean(xc * xc, -1, keepdims=True)
    return (xc * lax.rsqrt(var + LN_EPS) * g.astype(F32) + b.astype(F32)).astype(x.dtype)


def rope_tables(seq, rot_dim):
    inv = 1.0 / (ROPE_THETA ** (jnp.arange(0, rot_dim, 2, dtype=F32) / rot_dim))
    ang = jnp.arange(seq, dtype=F32)[:, None] * inv[None, :]
    return jnp.cos(ang), jnp.sin(ang)


def partial_rope(t, cos, sin):
    half = cos.shape[-1]
    c = cos[None, :, None, :].astype(t.dtype)
    s = sin[None, :, None, :].astype(t.dtype)
    t1, t2 = t[..., :half], t[..., half:2 * half]
    return jnp.concatenate([t1 * c - t2 * s, t1 * s + t2 * c, t[..., 2 * half:]], axis=-1)


def dilated_branch(q, k, v, dil, half):
    B, S, H, E = q.shape
    L = S // dil
    nb = -(-L // half)
    Lp = nb * half

    def to_sub(t):
        t = t.reshape(B, L, dil, H, E).transpose(0, 2, 1, 3, 4)
        return jnp.pad(t, ((0, 0), (0, 0), (0, Lp - L), (0, 0), (0, 0)))

    def band(t):
        tp = jnp.pad(to_sub(t), ((0, 0), (0, 0), (half, half), (0, 0), (0, 0)))
        tp = tp.reshape(B, dil, nb + 2, half, H, E)
        return jnp.concatenate([tp[:, :, :-2], tp[:, :, 1:-1], tp[:, :, 2:]], axis=3)

    qs = to_sub(q).reshape(B, dil, nb, half, H, E)
    kb, vb = band(k), band(v)
    qi = jnp.arange(nb)[:, None] * half + jnp.arange(half)[None, :]
    ki = jnp.arange(nb)[:, None] * half - half + jnp.arange(3 * half)[None, :]
    kk = ki[:, None, :]
    valid = (jnp.abs(kk - qi[:, :, None]) <= half) & (kk >= 0) & (kk < L)
    s = jnp.einsum('brnqhe,brnkhe->brnhqk', qs, kb, preferred_element_type=F32) * (E ** -0.5)
    s = jnp.where(valid[None, None, :, None], s, NEG_INF)
    m = jnp.max(s, -1, keepdims=True)
    p = jnp.exp(s - m)
    den = jnp.sum(p, -1, keepdims=True)
    o = jnp.einsum('brnhqk,brnkhe->brnqhe', (p / den).astype(v.dtype), vb,
                   preferred_element_type=F32)
    lse = (m + jnp.log(den))[..., 0]
    o = o.reshape(B, dil, Lp, H, E)[:, :, :L].transpose(0, 2, 1, 3, 4).reshape(B, S, H, E)
    lse = lse.transpose(0, 1, 2, 4, 3).reshape(B, dil, Lp, H)[:, :, :L]
    lse = lse.transpose(0, 2, 1, 3).reshape(B, S, H)
    return o, lse


def dilated_attention(q, k, v):
    outs, lses = zip(*[dilated_branch(q, k, v, d, w // (2 * d)) for w, d in A_PATTERNS])
    wts = jax.nn.softmax(jnp.stack(lses, 0), axis=0)
    return jnp.einsum('pbsh,pbshe->bshe', wts, jnp.stack(outs, 0))


def pooling_mixer(u, pool_w, pool_scale):
    B, S, C = u.shape
    uf = u.astype(F32)
    cs = jnp.concatenate([jnp.zeros((B, 1, C), F32), jnp.cumsum(uf, axis=1)], axis=1)
    pos = jnp.arange(S)
    outs = []
    for g, w in enumerate(B_WINDOWS):
        lo = jnp.clip(pos - w // 2, 0, S)
        hi = jnp.clip(pos + w - w // 2, 0, S)
        csg = cs[..., g * B_GROUP:(g + 1) * B_GROUP]
        mean = (jnp.take(csg, hi, axis=1) - jnp.take(csg, lo, axis=1)) / (hi - lo).astype(F32)[None, :, None]
        outs.append(mean - uf[..., g * B_GROUP:(g + 1) * B_GROUP])
    pooled = jnp.stack(outs, axis=2).astype(u.dtype)
    mixed = jnp.einsum('bsgc,gcd->bsgd', pooled, pool_w, preferred_element_type=F32)
    return (mixed.reshape(B, S, C) * pool_scale.astype(F32)).astype(u.dtype)


def diff_attention(q1, q2, k1, k2, v, lam):
    B, S, H, E = q1.shape
    nq = S // C_QBLOCK
    qs = jnp.stack([q1, q2], 1).reshape(B, 2, nq, C_QBLOCK, H, E).transpose(2, 0, 1, 3, 4, 5)
    ks = jnp.stack([k1, k2], 1)

    def block(qb):
        s = jnp.einsum('biqhe,bikhe->bihqk', qb, ks, preferred_element_type=F32) * (E ** -0.5)
        p = jax.nn.softmax(s, axis=-1)
        a = p[:, 0] - lam * p[:, 1]
        return jnp.einsum('bhqk,bkhe->bqhe', a.astype(v.dtype), v, preferred_element_type=F32)

    o = lax.map(block, qs)
    return o.transpose(1, 0, 2, 3, 4).reshape(B, S, H, v.shape[-1])


def gated_short_conv(p, conv_w):
    bg, cg, h = jnp.split(p, 3, axis=-1)
    u = cg * h
    conv = lax.conv_general_dilated(
        u, conv_w[:, None, :].astype(u.dtype), window_strides=(1,),
        padding=((CONV_WIDTH // 2, CONV_WIDTH // 2),),
        dimension_numbers=('NWC', 'WIO', 'NWC'), feature_group_count=u.shape[-1])
    return bg * conv


def memory_cross_attention(px, mem, w_mem_kv):
    B, S, _ = px.shape
    M = mem.shape[1]
    q = px.reshape(B, S, X_HEADS, X_HEAD_DIM)
    kv = jnp.matmul(mem, w_mem_kv).reshape(B, M, 2, X_HEADS, X_HEAD_DIM)
    s = jnp.einsum('bshe,bmhe->bhsm', q, kv[:, :, 0], preferred_element_type=F32) * (X_HEAD_DIM ** -0.5)
    p = jax.nn.softmax(s, axis=-1)
    o = jnp.einsum('bhsm,bmhe->bshe', p.astype(px.dtype), kv[:, :, 1], preferred_element_type=F32)
    return o.reshape(B, S, X_WIDTH).astype(px.dtype)


def mixer_sublayer(x, mem, layer, rope_a, rope_c, w_in, pool_w, pool_scale, c_lambda, c_norm,
                   conv_w, w_mem_kv, w_branch, w_branch_x, w_gate, b_gate, w_out):
    B, S, D = x.shape
    dt = x.dtype
    proj = jnp.matmul(x, w_in)
    pa, pb, pc, pd, px = jnp.split(proj, IN_SPLITS, axis=-1)
    qa, ka, va = [t.reshape(B, S, A_HEADS, HEAD_DIM) for t in jnp.split(pa, 3, axis=-1)]
    ya = dilated_attention(partial_rope(qa, *rope_a), partial_rope(ka, *rope_a), va)
    ya = ya.reshape(B, S, MIX_WIDTH).astype(dt)
    yb = pooling_mixer(pb, pool_w, pool_scale)
    pc = pc.reshape(B, S, C_HEADS, C_HEAD_COLS)
    q1, q2, k1, k2, vc = jnp.split(pc, (C_QK_DIM, 2 * C_QK_DIM, 3 * C_QK_DIM, 4 * C_QK_DIM), axis=-1)
    lam_init = 0.8 - 0.6 * math.exp(-0.3 * layer)
    lq = c_lambda.astype(F32)
    lam = jnp.exp(jnp.sum(lq[0] * lq[1])) - jnp.exp(jnp.sum(lq[2] * lq[3])) + lam_init
    oc = diff_attention(partial_rope(q1, *rope_c), partial_rope(q2, *rope_c),
                        partial_rope(k1, *rope_c), partial_rope(k2, *rope_c), vc, lam)
    oc = oc * lax.rsqrt(jnp.mean(oc * oc, -1, keepdims=True) + LN_EPS) * c_norm.astype(F32) * (1.0 - lam_init)
    yc = oc.reshape(B, S, C_HEADS * C_V_DIM).astype(dt)
    yd = gated_short_conv(pd, conv_w)
    yx = memory_cross_attention(px, mem, w_mem_kv)
    branches = ((ya, w_branch[0]), (yb, w_branch[1]), (yc, w_branch[2]), (yd, w_branch[3]), (yx, w_branch_x))
    merged = None
    for i, (y, wp) in enumerate(branches):
        gate = jax.nn.sigmoid(jnp.matmul(x, w_gate[:, i * D_MODEL:(i + 1) * D_MODEL])
                              + b_gate[i * D_MODEL:(i + 1) * D_MODEL])
        term = gate * jnp.matmul(y, wp)
        merged = term if merged is None else merged + term
    return jnp.matmul(merged, w_out)


def moe_ffn(h, router_w, router_b, e_gate, e_up, e_down, s_gate, s_up, s_down):
    B, S, D = h.shape
    T = B * S
    t = h.reshape(T, D)
    scores = jax.nn.sigmoid(jnp.matmul(t, router_w, preferred_element_type=F32))
    biased = scores + router_b.astype(F32)
    grp_score = jnp.sum(lax.top_k(biased.reshape(T, N_GROUPS, N_EXPERTS // N_GROUPS), 2)[0], -1)
    _, top_g = lax.top_k(grp_score, TOP_GROUPS)
    gmask = jnp.sum(jax.nn.one_hot(top_g, N_GROUPS, dtype=F32), axis=1) > 0
    emask = jnp.repeat(gmask, N_EXPERTS // N_GROUPS, axis=1)
    _, idx = lax.top_k(jnp.where(emask, biased, -jnp.inf), TOP_K)
    w = jnp.take_along_axis(scores, idx, axis=1)
    w = w / jnp.sum(w, -1, keepdims=True) * ROUTE_SCALE
    A = T * TOP_K
    flat_e = idx.reshape(A)
    flat_t = jnp.repeat(jnp.arange(T, dtype=jnp.int32), TOP_K)
    flat_w = w.reshape(A)
    order = jnp.argsort(flat_e)
    se = flat_e[order]
    counts = jnp.bincount(flat_e, length=N_EXPERTS)
    padded = (counts + MOE_BLOCK - 1) // MOE_BLOCK * MOE_BLOCK
    start = jnp.cumsum(counts) - counts
    ends_p = jnp.cumsum(padded)
    pstart = ends_p - padded
    dest = pstart[se] + jnp.arange(A) - start[se]
    NB = -(-A // MOE_BLOCK) + N_EXPERTS
    slot_tok = jnp.full((NB * MOE_BLOCK,), T, jnp.int32).at[dest].set(flat_t[order])
    slot_w = jnp.zeros((NB * MOE_BLOCK,), F32).at[dest].set(flat_w[order])
    block_e = jnp.minimum(jnp.searchsorted(ends_p, jnp.arange(NB) * MOE_BLOCK, side='right'), N_EXPERTS - 1)
    t_pad = jnp.concatenate([t, jnp.zeros((1, D), t.dtype)], axis=0)

    def run_block(args):
        toks, e = args
        xb = t_pad[toks]
        a = jax.nn.silu(jnp.matmul(xb, e_gate[e])) * jnp.matmul(xb, e_up[e])
        return jnp.matmul(a, e_down[e])

    yb = lax.map(run_block, (slot_tok.reshape(NB, MOE_BLOCK), block_e))
    routed = jnp.zeros((T + 1, D), F32).at[slot_tok].add(
        yb.reshape(NB * MOE_BLOCK, D).astype(F32) * slot_w[:, None])[:T]
    shared = jnp.matmul(jax.nn.silu(jnp.matmul(t, s_gate)) * jnp.matmul(t, s_up), s_down)
    return (routed + shared.astype(F32)).reshape(B, S, D).astype(h.dtype)


def setup_inputs(seed: int = 0) -> dict:
    key = jax.random.key(seed)
    ks = jax.random.split(key, 26)
    L, D = DEPTH, D_MODEL

    def nrm(k, shape, scale):
        return jax.random.normal(k, shape, F32) * scale

    return {
        "x": nrm(ks[0], (BATCH, SEQ, D), 1.0),
        "mem": nrm(ks[1], (BATCH, MEM_LEN, D), 1.0),
        "w_in": nrm(ks[2], (L, D, IN_COLS), D ** -0.5),
        "pool_w": nrm(ks[3], (L, len(B_WINDOWS), B_GROUP, B_GROUP), B_GROUP ** -0.5),
        "pool_scale": 1.0 + nrm(ks[4], (L, B_WIDTH), 0.02),
        "c_lambda": nrm(ks[5], (L, 4, C_QK_DIM), 0.1),
        "c_norm": 1.0 + nrm(ks[6], (L, C_V_DIM), 0.02),
        "conv_w": nrm(ks[7], (L, CONV_WIDTH, D_WIDTH), CONV_WIDTH ** -0.5),
        "w_mem_kv": nrm(ks[8], (L, D, 2 * X_WIDTH), D ** -0.5),
        "w_branch": nrm(ks[9], (L, 4, MIX_WIDTH, D), MIX_WIDTH ** -0.5),
        "w_branch_x": nrm(ks[10], (L, X_WIDTH, D), X_WIDTH ** -0.5),
        "w_gate": nrm(ks[11], (L, D, N_BRANCH * D), D ** -0.5),
        "b_gate": nrm(ks[12], (L, N_BRANCH * D), 0.02),
        "w_out": nrm(ks[13], (L, D, D), D ** -0.5 * DN_BETA),
        "ln1_g": 1.0 + nrm(ks[14], (L, D), 0.02),
        "ln1_b": nrm(ks[15], (L, D), 0.02),
        "router_w": nrm(ks[16], (L, D, N_EXPERTS), D ** -0.5),
        "router_b": nrm(ks[17], (L, N_EXPERTS), 0.01),
        "e_gate": nrm(ks[18], (L, N_EXPERTS, D, D_EXPERT), D ** -0.5),
        "e_up": nrm(ks[19], (L, N_EXPERTS, D, D_EXPERT), D ** -0.5),
        "e_down": nrm(ks[20], (L, N_EXPERTS, D_EXPERT, D), D_EXPERT ** -0.5 * DN_BETA),
        "s_gate": nrm(ks[21], (L, D, D_EXPERT), D ** -0.5),
        "s_up": nrm(ks[22], (L, D, D_EXPERT), D ** -0.5),
        "s_down": nrm(ks[23], (L, D_EXPERT, D), D_EXPERT ** -0.5 * DN_BETA),
        "ln2_g": 1.0 + nrm(ks[24], (L, D), 0.02),
        "ln2_b": nrm(ks[25], (L, D), 0.02),
    }


def reference(x, mem, w_in, pool_w, pool_scale, c_lambda, c_norm, conv_w, w_mem_kv, w_branch,
              w_branch_x, w_gate, b_gate, w_out, ln1_g, ln1_b, router_w, router_b, e_gate, e_up,
              e_down, s_gate, s_up, s_down, ln2_g, ln2_b):
    S = x.shape[1]
    rope_a = rope_tables(S, HEAD_DIM // ROPE_FRACTION)
    rope_c = rope_tables(S, C_QK_DIM // ROPE_FRACTION)
    for l in range(DEPTH):
        y = mixer_sublayer(x, mem, l, rope_a, rope_c, w_in[l], pool_w[l], pool_scale[l], c_lambda[l],
                           c_norm[l], conv_w[l], w_mem_kv[l], w_branch[l], w_branch_x[l], w_gate[l],
                           b_gate[l], w_out[l])
        x = layer_norm(DN_ALPHA * x + y.astype(x.dtype), ln1_g[l], ln1_b[l])
        y = moe_ffn(x, router_w[l], router_b[l], e_gate[l], e_up[l], e_down[l], s_gate[l], s_up[l], s_down[l])
        x = layer_norm(DN_ALPHA * x + y, ln2_g[l], ln2_b[l])
    return x
```

```python
import functools
import math

import jax
import jax.numpy as jnp
from jax import lax
from jax.experimental import pallas as pl
from jax.experimental.pallas import tpu as pltpu

F32 = jnp.float32
BF16 = jnp.bfloat16

LANES = 128
HEAD_DIM = 128
ROPE_THETA = 500000.0
A_PATTERNS = ((128, 1), (512, 4), (2048, 16))
A_HALF = 64
B_WINDOWS = (2, 4, 8, 16)
C_QK_DIM = 64
N_EXPERTS = 64
N_GROUPS = 8
TOP_GROUPS = 4
TOP_K = 8
ROUTE_SCALE = 2.5
LN_EPS = 1e-5
NEG_INF = -1e30
VMEM_LIMIT_BYTES = 56 * 1024 * 1024
MOE_ROWS = 256


def _params(*sem):
    return pltpu.CompilerParams(dimension_semantics=sem, vmem_limit_bytes=VMEM_LIMIT_BYTES)


def _rope_tables(seq, rot_dim, period):
    half = rot_dim // 2
    inv = 1.0 / (ROPE_THETA ** (jnp.arange(0, rot_dim, 2, dtype=F32) / rot_dim))
    ang = jnp.arange(seq, dtype=F32)[:, None] * inv[None, :]
    cos, sin = jnp.cos(ang), jnp.sin(ang)
    lane = jnp.arange(LANES) % period
    first = lane < half
    second = (lane >= half) & (lane < 2 * half)
    idx = jnp.where(first, lane, jnp.where(second, lane - half, 0))
    cos_l, sin_l = cos[:, idx], sin[:, idx]
    ct = jnp.where(first | second, cos_l, 1.0)
    s1 = jnp.where(first, -sin_l, 0.0)
    s2 = jnp.where(second, sin_l, 0.0)
    return ct.astype(F32), s1.astype(F32), s2.astype(F32)


def _proj_kernel(x_ref, w_ref, ct_ref, s1_ref, s2_ref, o_ref, wb_ref, *, half, rope_blocks, rope_jmax):
    j = pl.program_id(0)
    i = pl.program_id(1)

    @pl.when(i == 0)
    def _():
        wb_ref[...] = w_ref[...].astype(BF16)

    acc = jnp.dot(x_ref[...], wb_ref[...], preferred_element_type=F32)

    def plain():
        o_ref[...] = acc.astype(o_ref.dtype)

    def roped():
        ct, s1, s2 = ct_ref[...], s1_ref[...], s2_ref[...]
        for b, use in enumerate(rope_blocks):
            blk = acc[:, b * LANES:(b + 1) * LANES]
            if use:
                blk = (blk * ct + pltpu.roll(blk, LANES - half, 1) * s1
                       + pltpu.roll(blk, half, 1) * s2)
            o_ref[:, b * LANES:(b + 1) * LANES] = blk.astype(o_ref.dtype)

    if not any(rope_blocks):
        plain()
    elif rope_jmax is None:
        roped()
    else:
        pl.when(j < rope_jmax)(roped)
        pl.when(j >= rope_jmax)(plain)


def _proj(xb, w, layer, col_blocks, tn, tables, seq, *, half=0, rope_blocks=None, rope_jmax=None, tm=512):
    m, k = xb.shape
    nj = len(col_blocks)
    tm = min(tm, m)
    rope_blocks = tuple(rope_blocks) if rope_blocks is not None else (False,) * (tn // LANES)
    col_blocks = tuple(col_blocks)
    first, contiguous = col_blocks[0], all(col_blocks[a] == col_blocks[0] + a for a in range(nj))
    if contiguous:
        w_map = lambda j, i: (layer, 0, first + j)
    else:
        n0 = next(a for a in range(1, nj) if col_blocks[a] != col_blocks[0] + a)
        second = col_blocks[n0]
        w_map = lambda j, i: (layer, 0, jnp.where(j < n0, first + j, second + j - n0))
    sblocks = seq // tm if seq >= tm else 1
    tab_spec = pl.BlockSpec((tm, LANES), lambda j, i: (i % sblocks, 0))
    kern = functools.partial(_proj_kernel, half=half, rope_blocks=rope_blocks, rope_jmax=rope_jmax)
    return pl.pallas_call(
        kern,
        grid=(nj, m // tm),
        in_specs=[pl.BlockSpec((tm, k), lambda j, i: (i, 0)),
                  pl.BlockSpec((None, k, tn), w_map),
                  tab_spec, tab_spec, tab_spec],
        out_specs=pl.BlockSpec((tm, tn), lambda j, i: (i, j)),
        out_shape=jax.ShapeDtypeStruct((m, nj * tn), BF16),
        scratch_shapes=[pltpu.VMEM((k, tn), BF16)],
        compiler_params=_params("arbitrary", "arbitrary"),
        name="proj",
    )(xb, w, *tables)


def _dilated_kernel(q_ref, kp_ref, km_ref, kn_ref, vp_ref, vm_ref, vn_ref, o_ref, lse_ref, *, tl, seq_len, heads):
    t = pl.program_id(2)
    l0 = t * tl
    sub = 128
    nsub = tl // sub
    scale = HEAD_DIM ** -0.5
    iq = lax.broadcasted_iota(jnp.int32, (sub, sub + 2 * A_HALF), 0)
    ik = lax.broadcasted_iota(jnp.int32, (sub, sub + 2 * A_HALF), 1)
    band = jnp.abs(ik - A_HALF - iq) <= A_HALF
    lane = lax.broadcasted_iota(jnp.int32, (sub, LANES), 1)
    valid = []
    for a in range(nsub):
        kk = l0 + a * sub - A_HALF + ik
        valid.append(band & (kk >= 0) & (kk < seq_len))
    lse_rows = [jnp.zeros((sub, LANES), F32) for _ in range(nsub)]
    for h in range(heads):
        cs = slice(h * HEAD_DIM, (h + 1) * HEAD_DIM)
        kcat = jnp.concatenate([kp_ref[:, cs], km_ref[:, cs], kn_ref[:, cs]], axis=0)
        vcat = jnp.concatenate([vp_ref[:, cs], vm_ref[:, cs], vn_ref[:, cs]], axis=0)
        for a in range(nsub):
            q = q_ref[a * sub:(a + 1) * sub, cs]
            kw = kcat[a * sub:a * sub + sub + 2 * A_HALF]
            vw = vcat[a * sub:a * sub + sub + 2 * A_HALF]
            s = lax.dot_general(q, kw, (((1,), (1,)), ((), ())), preferred_element_type=F32) * scale
            s = jnp.where(valid[a], s, NEG_INF)
            m = jnp.max(s, axis=-1, keepdims=True)
            p = jnp.exp(s - m)
            den = jnp.sum(p, axis=-1, keepdims=True)
            o = jnp.dot(p.astype(BF16), vw, preferred_element_type=F32) / den
            o_ref[a * sub:(a + 1) * sub, cs] = o.astype(o_ref.dtype)
            lse_rows[a] = jnp.where(lane == h, m + jnp.log(den), lse_rows[a])
    for a in range(nsub):
        lse_ref[a * sub:(a + 1) * sub, :] = lse_rows[a]


def _dilated_pattern(pa, batch, seq, dil):
    width = pa.shape[1] // 3
    heads = width // HEAD_DIM
    sl = seq // dil
    tl = min(256, sl)
    nt = sl // tl
    hb = tl // A_HALF
    last_halo = sl // A_HALF - 1
    view = pa.reshape(batch, sl, dil * 3 * width)

    def main(c):
        return pl.BlockSpec((None, tl, width), lambda b, r, t: (b, t, r * 3 + c))

    def prev(c):
        return pl.BlockSpec((None, A_HALF, width), lambda b, r, t: (b, jnp.maximum(t * hb - 1, 0), r * 3 + c))

    def nxt(c):
        return pl.BlockSpec((None, A_HALF, width),
                            lambda b, r, t: (b, jnp.minimum((t + 1) * hb, last_halo), r * 3 + c))

    kern = functools.partial(_dilated_kernel, tl=tl, seq_len=sl, heads=heads)
    o, lse = pl.pallas_call(
        kern,
        grid=(batch, dil, nt),
        in_specs=[main(0), prev(1), main(1), nxt(1), prev(2), main(2), nxt(2)],
        out_specs=[pl.BlockSpec((None, tl, width), lambda b, r, t: (b, t, r)),
                   pl.BlockSpec((None, tl, LANES), lambda b, r, t: (b, t, r))],
        out_shape=[jax.ShapeDtypeStruct((batch, sl, dil * width), BF16),
                   jax.ShapeDtypeStruct((batch, sl, dil * LANES), F32)],
        compiler_params=_params("arbitrary", "arbitrary", "arbitrary"),
        name=f"dilated_d{dil}",
    )(view, view, view, view, view, view, view)
    return o.reshape(batch * seq, width), lse.reshape(batch * seq, LANES)


def _dilated_combine_kernel(o0_ref, o1_ref, o2_ref, l0_ref, l1_ref, l2_ref, y_ref, *, heads):
    l0, l1, l2 = l0_ref[...], l1_ref[...], l2_ref[...]
    m = jnp.maximum(jnp.maximum(l0, l1), l2)
    e0, e1, e2 = jnp.exp(l0 - m), jnp.exp(l1 - m), jnp.exp(l2 - m)
    inv = 1.0 / (e0 + e1 + e2)
    w0, w1, w2 = e0 * inv, e1 * inv, e2 * inv
    for h in range(heads):
        cs = slice(h * HEAD_DIM, (h + 1) * HEAD_DIM)
        y = (w0[:, h:h + 1] * o0_ref[:, cs].astype(F32) + w1[:, h:h + 1] * o1_ref[:, cs].astype(F32)
             + w2[:, h:h + 1] * o2_ref[:, cs].astype(F32))
        y_ref[:, cs] = y.astype(y_ref.dtype)


def _dilated_attention(pa, batch, seq):
    outs, lses = zip(*[_dilated_pattern(pa, batch, seq, d) for _, d in A_PATTERNS])
    m, width = outs[0].shape
    tm = min(512, m)
    ospec = pl.BlockSpec((tm, width), lambda i: (i, 0))
    lspec = pl.BlockSpec((tm, LANES), lambda i: (i, 0))
    return pl.pallas_call(
        functools.partial(_dilated_combine_kernel, heads=width // HEAD_DIM),
        grid=(m // tm,),
        in_specs=[ospec, ospec, ospec, lspec, lspec, lspec],
        out_specs=ospec,
        out_shape=jax.ShapeDtypeStruct((m, width), BF16),
        compiler_params=_params("arbitrary"),
        name="dilated_combine",
    )(*outs, *lses)


def _pool_kernel(u_ref, w_ref, sc_ref, o_ref, pad_ref, *, seq):
    g = pl.program_id(1)
    pad = 8
    uf = u_ref[...].astype(F32)
    pad_ref[0:pad, :] = jnp.zeros((pad, uf.shape[1]), F32)
    pad_ref[pad + seq:pad + seq + pad, :] = jnp.zeros((pad, uf.shape[1]), F32)
    pad_ref[pad:pad + seq, :] = uf
    pos = lax.broadcasted_iota(jnp.int32, (seq, 1), 0)
    for gi, win in enumerate(B_WINDOWS):
        @pl.when(g == gi)
        def _(win=win):
            before, after = win // 2, win - win // 2
            tot = pad_ref[pad - before:pad - before + seq, :]
            for off in range(-before + 1, after):
                tot = tot + pad_ref[pad + off:pad + off + seq, :]
            cnt = (jnp.minimum(pos + after, seq) - jnp.maximum(pos - before, 0)).astype(F32)
            pooled = tot / cnt - uf
            mixed = jnp.dot(pooled.astype(BF16), w_ref[...].astype(BF16), preferred_element_type=F32)
            o_ref[...] = (mixed * sc_ref[...]).astype(o_ref.dtype)


def _pooling_mixer(pbdx, pool_w, pool_scale, batch, seq, col0):
    ng, cg = pool_w.shape[0], pool_w.shape[1]
    view = pbdx.reshape(batch, seq, pbdx.shape[1])
    cb = col0 // cg
    return pl.pallas_call(
        functools.partial(_pool_kernel, seq=seq),
        grid=(batch, ng),
        in_specs=[pl.BlockSpec((None, seq, cg), lambda b, g: (b, 0, cb + g)),
                  pl.BlockSpec((None, cg, cg), lambda b, g: (g, 0, 0)),
                  pl.BlockSpec((1, cg), lambda b, g: (0, g))],
        out_specs=pl.BlockSpec((None, seq, cg), lambda b, g: (b, 0, g)),
        out_shape=jax.ShapeDtypeStruct((batch, seq, ng * cg), BF16),
        scratch_shapes=[pltpu.VMEM((seq + 16, cg), F32)],
        compiler_params=_params("arbitrary", "arbitrary"),
        name="pooling",
    )(view, pool_w, pool_scale.reshape(1, ng * cg)).reshape(batch * seq, ng * cg)


def _conv_kernel(bg_ref, cg_ref, h_ref, w_ref, o_ref, pad_ref, *, seq):
    pad = 8
    u = cg_ref[...].astype(F32) * h_ref[...].astype(F32)
    cols = u.shape[1]
    pad_ref[0:pad, :] = jnp.zeros((pad, cols), F32)
    pad_ref[pad + seq:pad + seq + pad, :] = jnp.zeros((pad, cols), F32)
    pad_ref[pad:pad + seq, :] = u
    w = w_ref[...]
    conv = (pad_ref[pad - 1:pad - 1 + seq, :] * w[0:1, :] + u * w[1:2, :]
            + pad_ref[pad + 1:pad + 1 + seq, :] * w[2:3, :])
    o_ref[...] = (bg_ref[...].astype(F32) * conv).astype(o_ref.dtype)


def _gated_short_conv(pbdx, conv_w, batch, seq, col0):
    width = conv_w.shape[1]
    cb = 256
    nb = width // cb
    view = pbdx.reshape(batch, seq, pbdx.shape[1])
    base = col0 // cb

    def spec(part):
        return pl.BlockSpec((None, seq, cb), lambda b, c: (b, 0, base + part * nb + c))

    return pl.pallas_call(
        functools.partial(_conv_kernel, seq=seq),
        grid=(batch, nb),
        in_specs=[spec(0), spec(1), spec(2), pl.BlockSpec((3, cb), lambda b, c: (0, c))],
        out_specs=pl.BlockSpec((None, seq, cb), lambda b, c: (b, 0, c)),
        out_shape=jax.ShapeDtypeStruct((batch, seq, width), BF16),
        scratch_shapes=[pltpu.VMEM((seq + 16, cb), F32)],
        compiler_params=_params("arbitrary", "arbitrary"),
        name="short_conv",
    )(view, view, view, conv_w).reshape(batch * seq, width)


def _diff_kernel(q_ref, k_ref, v_ref, lam_ref, nrm_ref, o_ref, *, lam_init):
    lq = lam_ref[...].astype(F32)
    lam = (jnp.exp(jnp.sum(lq[0:1] * lq[1:2], axis=-1, keepdims=True))
           - jnp.exp(jnp.sum(lq[2:3] * lq[3:4], axis=-1, keepdims=True)) + lam_init)
    q = q_ref[...]
    k = k_ref[...]
    v = v_ref[...]
    lane = lax.broadcasted_iota(jnp.int32, q.shape, 1)
    scale = C_QK_DIM ** -0.5
    qs = q * jnp.asarray(scale, q.dtype)
    zero = jnp.zeros_like(qs)
    outs = []
    for part in range(2):
        qi = jnp.where((lane < C_QK_DIM) if part == 0 else (lane >= C_QK_DIM), qs, zero)
        s = lax.dot_general(qi, k, (((1,), (1,)), ((), ())), preferred_element_type=F32)
        m = jnp.max(s, axis=-1, keepdims=True)
        e = jnp.exp(s - m)
        den = jnp.sum(e, axis=-1, keepdims=True)
        outs.append(jnp.dot(e.astype(BF16), v, preferred_element_type=F32) / den)
    o = outs[0] - lam * outs[1]
    o = o * lax.rsqrt(jnp.mean(o * o, axis=-1, keepdims=True) + LN_EPS) * nrm_ref[...] * (1.0 - lam_init)
    o_ref[...] = o.astype(o_ref.dtype)


def _diff_attention(pc, c_lambda, c_norm, batch, seq, layer, tq=256):
    heads = pc.shape[1] // (3 * LANES)
    tq = min(tq, seq)
    lam_init = 0.8 - 0.6 * math.exp(-0.3 * layer)
    view = pc.reshape(batch, seq, pc.shape[1])
    return pl.pallas_call(
        functools.partial(_diff_kernel, lam_init=lam_init),
        grid=(batch, heads, seq // tq),
        in_specs=[pl.BlockSpec((None, tq, LANES), lambda b, h, t: (b, t, 3 * h)),
                  pl.BlockSpec((None, seq, LANES), lambda b, h, t: (b, 0, 3 * h + 1)),
                  pl.BlockSpec((None, seq, LANES), lambda b, h, t: (b, 0, 3 * h + 2)),
                  pl.BlockSpec((4, C_QK_DIM), lambda b, h, t: (0, 0)),
                  pl.BlockSpec((1, LANES), lambda b, h, t: (0, 0))],
        out_specs=pl.BlockSpec((None, tq, LANES), lambda b, h, t: (b, t, h)),
        out_shape=jax.ShapeDtypeStruct((batch, seq, heads * LANES), BF16),
        compiler_params=_params("arbitrary", "arbitrary", "arbitrary"),
        name="diff_attention",
    )(view, view, view, c_lambda, c_norm.reshape(1, LANES)).reshape(batch * seq, heads * LANES)


def _cross_kernel(q_ref, kv_ref, o_ref, *, heads):
    scale = HEAD_DIM ** -0.5
    width = heads * HEAD_DIM
    for h in range(heads):
        cs = slice(h * HEAD_DIM, (h + 1) * HEAD_DIM)
        k = kv_ref[:, h * HEAD_DIM:(h + 1) * HEAD_DIM]
        v = kv_ref[:, width + h * HEAD_DIM:width + (h + 1) * HEAD_DIM]
        s = lax.dot_general(q_ref[:, cs], k, (((1,), (1,)), ((), ())), preferred_element_type=F32) * scale
        m = jnp.max(s, axis=-1, keepdims=True)
        e = jnp.exp(s - m)
        den = jnp.sum(e, axis=-1, keepdims=True)
        o = jnp.dot(e.astype(BF16), v, preferred_element_type=F32) / den
        o_ref[:, cs] = o.astype(o_ref.dtype)


def _cross_attention(pbdx, kv, batch, seq, col0, width, tq=512):
    mem_len = kv.shape[0] // batch
    tq = min(tq, seq)
    view = pbdx.reshape(batch, seq, pbdx.shape[1])
    return pl.pallas_call(
        functools.partial(_cross_kernel, heads=width // HEAD_DIM),
        grid=(batch, seq // tq),
        in_specs=[pl.BlockSpec((None, tq, width), lambda b, t: (b, t, col0 // width)),
                  pl.BlockSpec((None, mem_len, 2 * width), lambda b, t: (b, 0, 0))],
        out_specs=pl.BlockSpec((None, tq, width), lambda b, t: (b, t, 0)),
        out_shape=jax.ShapeDtypeStruct((batch, seq, width), BF16),
        compiler_params=_params("arbitrary", "arbitrary"),
        name="cross_attention",
    )(view, kv.reshape(batch, mem_len, 2 * width)).reshape(batch * seq, width)


def _merge_kernel(x_ref, ya_ref, yb_ref, yc_ref, yd_ref, yx_ref,
                  g0, g1, g2, g3, g4, p0, p1, p2, p3, p4, b0, b1, b2, b3, b4, o_ref):
    x = x_ref[...]
    merged = None
    for y_ref, g_ref, p_ref, b_ref in ((ya_ref, g0, p0, b0), (yb_ref, g1, p1, b1), (yc_ref, g2, p2, b2),
                                       (yd_ref, g3, p3, b3), (yx_ref, g4, p4, b4)):
        gate = jax.nn.sigmoid(jnp.dot(x, g_ref[...], preferred_element_type=F32) + b_ref[...])
        term = gate * jnp.dot(y_ref[...], p_ref[...], preferred_element_type=F32)
        merged = term if merged is None else merged + term
    o_ref[...] = merged.astype(o_ref.dtype)


def _gated_merge(xb, ys, w_gate, w_branch, w_branch_x, b_gate, tm=512, tn=512):
    m, d = xb.shape
    tm = min(tm, m)
    nb = d // tn
    n_br = 5
    act = [pl.BlockSpec((tm, d), lambda j, i: (i, 0))]
    act += [pl.BlockSpec((tm, y.shape[1]), lambda j, i: (i, 0)) for y in ys]
    gates = [pl.BlockSpec((d, tn), functools.partial(lambda j, i, br: (0, br * nb + j), br=br)) for br in range(n_br)]
    projs = [pl.BlockSpec((None, w_branch.shape[1], tn), functools.partial(lambda j, i, br: (br, 0, j), br=br))
             for br in range(4)]
    projs.append(pl.BlockSpec((w_branch_x.shape[0], tn), lambda j, i: (0, j)))
    biases = [pl.BlockSpec((1, tn), functools.partial(lambda j, i, br: (0, br * nb + j), br=br)) for br in range(n_br)]
    bg2 = b_gate.reshape(1, n_br * d)
    return pl.pallas_call(
        _merge_kernel,
        grid=(nb, m // tm),
        in_specs=act + gates + projs + biases,
        out_specs=pl.BlockSpec((tm, tn), lambda j, i: (i, j)),
        out_shape=jax.ShapeDtypeStruct((m, d), BF16),
        compiler_params=_params("arbitrary", "arbitrary"),
        name="gated_merge",
    )(xb, *ys, *([w_gate] * n_br), *([w_branch] * 4), w_branch_x, *([bg2] * n_br))


def _layer_norm_rows(h, g, b):
    mu = jnp.mean(h, axis=-1, keepdims=True)
    hc = h - mu
    var = jnp.mean(hc * hc, axis=-1, keepdims=True)
    return hc * lax.rsqrt(var + LN_EPS) * g + b


def _out_ln_kernel(mg_ref, w_ref, x_ref, g_ref, b_ref, xo_ref, xb_ref, *, alpha):
    y = jnp.dot(mg_ref[...], w_ref[...], preferred_element_type=F32)
    xn = _layer_norm_rows(alpha * x_ref[...] + y, g_ref[...], b_ref[...])
    xo_ref[...] = xn
    xb_ref[...] = xn.astype(BF16)


def _out_proj_ln(merged, w_out_b, x, g, b, alpha, tm=256):
    m, d = x.shape
    tm = min(tm, m)
    row = pl.BlockSpec((tm, d), lambda i: (i, 0))
    vec = pl.BlockSpec((1, d), lambda i: (0, 0))
    return pl.pallas_call(
        functools.partial(_out_ln_kernel, alpha=alpha),
        grid=(m // tm,),
        in_specs=[row, pl.BlockSpec((d, d), lambda i: (0, 0)), row, vec, vec],
        out_specs=[row, row],
        out_shape=[jax.ShapeDtypeStruct((m, d), F32), jax.ShapeDtypeStruct((m, d), BF16)],
        compiler_params=_params("arbitrary"),
        name="out_proj_ln",
    )(merged, w_out_b, x, g.reshape(1, d), b.reshape(1, d))


def _split_bf16(a):
    hi = a.astype(BF16)
    lo = (a - hi.astype(F32)).astype(BF16)
    return hi, lo


def _router_kernel(x_ref, w_ref, b_ref, idx_ref, wt_ref):
    x = x_ref[...]
    w = w_ref[...]
    xh, xl = _split_bf16(x)
    wh, wl = _split_bf16(w)
    logits = (jnp.dot(xh, wh, preferred_element_type=F32) + jnp.dot(xl, wh, preferred_element_type=F32)
              + jnp.dot(xh, wl, preferred_element_type=F32))
    lt = logits.T[:N_EXPERTS, :]
    tm = lt.shape[1]
    scores = jax.nn.sigmoid(lt)
    biased = scores + b_ref[...]
    gsz = N_EXPERTS // N_GROUPS
    sub8 = lax.broadcasted_iota(jnp.int32, (gsz, tm), 0).astype(F32)
    grp_rows = []
    for g in range(N_GROUPS):
        blk = biased[g * gsz:(g + 1) * gsz, :]
        m1 = jnp.max(blk, axis=0, keepdims=True)
        i1 = jnp.min(jnp.where(blk == m1, sub8, float(gsz)), axis=0, keepdims=True)
        m2 = jnp.max(jnp.where(sub8 == i1, -jnp.inf, blk), axis=0, keepdims=True)
        grp_rows.append(m1 + m2)
    grp = jnp.concatenate(grp_rows, axis=0)
    subg = lax.broadcasted_iota(jnp.int32, (N_GROUPS, tm), 0).astype(F32)
    gsel = jnp.zeros((N_GROUPS, tm), F32)
    for _ in range(TOP_GROUPS):
        mg = jnp.max(grp, axis=0, keepdims=True)
        ig = jnp.min(jnp.where(grp == mg, subg, float(N_GROUPS)), axis=0, keepdims=True)
        hit = subg == ig
        gsel = jnp.where(hit, 1.0, gsel)
        grp = jnp.where(hit, -jnp.inf, grp)
    emask = jnp.concatenate([jnp.broadcast_to(gsel[g:g + 1, :], (gsz, tm)) for g in range(N_GROUPS)], axis=0)
    cand = jnp.where(emask > 0.5, biased, -jnp.inf)
    sube = lax.broadcasted_iota(jnp.int32, (N_EXPERTS, tm), 0).astype(F32)
    idx_rows, w_rows = [], []
    for _ in range(TOP_K):
        mc = jnp.max(cand, axis=0, keepdims=True)
        ic = jnp.min(jnp.where(cand == mc, sube, float(N_EXPERTS)), axis=0, keepdims=True)
        hit = sube == ic
        idx_rows.append(ic)
        w_rows.append(jnp.sum(jnp.where(hit, scores, 0.0), axis=0, keepdims=True))
        cand = jnp.where(hit, -jnp.inf, cand)
    wsel = jnp.concatenate(w_rows, axis=0)
    wsel = wsel / jnp.sum(wsel, axis=0, keepdims=True) * ROUTE_SCALE
    idx_ref[...] = jnp.concatenate(idx_rows, axis=0).astype(jnp.int32)
    wt_ref[...] = wsel


def _router(x, router_w, router_b, tm=512):
    m, d = x.shape
    tm = min(tm, m)
    w_pad = jnp.pad(router_w, ((0, 0), (0, LANES - N_EXPERTS)))
    return pl.pallas_call(
        _router_kernel,
        grid=(m // tm,),
        in_specs=[pl.BlockSpec((tm, d), lambda i: (i, 0)),
                  pl.BlockSpec((d, LANES), lambda i: (0, 0)),
                  pl.BlockSpec((N_EXPERTS, 1), lambda i: (0, 0))],
        out_specs=[pl.BlockSpec((TOP_K, tm), lambda i: (0, i)), pl.BlockSpec((TOP_K, tm), lambda i: (0, i))],
        out_shape=[jax.ShapeDtypeStruct((TOP_K, m), jnp.int32), jax.ShapeDtypeStruct((TOP_K, m), F32)],
        compiler_params=_params("arbitrary"),
        name="router",
    )(x, w_pad, router_b.reshape(N_EXPERTS, 1))


def _expert_kernel(be_ref, nb_ref, x_ref, wg_ref, wu_ref, wd_ref, o_ref, wgb, wub, wdb):
    i = pl.program_id(0)
    prev = be_ref[jnp.maximum(i - 1, 0)]
    fresh = (i == 0) | (be_ref[i] != prev)

    @pl.when(fresh)
    def _():
        wgb[...] = wg_ref[...].astype(BF16)
        wub[...] = wu_ref[...].astype(BF16)
        wdb[...] = wd_ref[...].astype(BF16)

    @pl.when(i < nb_ref[0])
    def _():
        x = x_ref[...]
        gate = jnp.dot(x, wgb[...], preferred_element_type=F32)
        up = jnp.dot(x, wub[...], preferred_element_type=F32)
        act = (jax.nn.silu(gate) * up).astype(BF16)
        o_ref[...] = jnp.dot(act, wdb[...], preferred_element_type=F32).astype(o_ref.dtype)

    @pl.when(i >= nb_ref[0])
    def _():
        o_ref[...] = jnp.zeros_like(o_ref)


def _expert_ffn(xs, block_e, nblocks, w_gate, w_up, w_down, layer, rows):
    ns, d = xs.shape
    de = w_gate.shape[3]
    grid_spec = pltpu.PrefetchScalarGridSpec(
        num_scalar_prefetch=2,
        grid=(ns // rows,),
        in_specs=[pl.BlockSpec((rows, d), lambda i, be, nb: (i, 0)),
                  pl.BlockSpec((None, None, d, de), lambda i, be, nb: (layer, be[i], 0, 0)),
                  pl.BlockSpec((None, None, d, de), lambda i, be, nb: (layer, be[i], 0, 0)),
                  pl.BlockSpec((None, None, de, d), lambda i, be, nb: (layer, be[i], 0, 0))],
        out_specs=pl.BlockSpec((rows, d), lambda i, be, nb: (i, 0)),
        scratch_shapes=[pltpu.VMEM((d, de), BF16), pltpu.VMEM((d, de), BF16), pltpu.VMEM((de, d), BF16)],
    )
    return pl.pallas_call(
        _expert_kernel,
        grid_spec=grid_spec,
        out_shape=jax.ShapeDtypeStruct((ns, d), BF16),
        compiler_params=_params("arbitrary"),
        name="expert_ffn",
    )(block_e, nblocks, xs, w_gate, w_up, w_down)


def _combine_ln_kernel(yg_ref, wt_ref, sh_ref, x_ref, g_ref, b_ref, xo_ref, xb_ref, *, alpha, d):
    wt = wt_ref[...]
    routed = sh_ref[...].astype(F32)
    for k in range(TOP_K):
        routed = routed + wt[:, k:k + 1] * yg_ref[:, k * d:(k + 1) * d].astype(F32)
    xn = _layer_norm_rows(alpha * x_ref[...] + routed, g_ref[...], b_ref[...])
    xo_ref[...] = xn
    xb_ref[...] = xn.astype(BF16)


def _combine_ln(yg, wt, shared, x, g, b, alpha, tm=128):
    m, d = x.shape
    tm = min(tm, m)
    row = pl.BlockSpec((tm, d), lambda i: (i, 0))
    vec = pl.BlockSpec((1, d), lambda i: (0, 0))
    return pl.pallas_call(
        functools.partial(_combine_ln_kernel, alpha=alpha, d=d),
        grid=(m // tm,),
        in_specs=[pl.BlockSpec((tm, TOP_K * d), lambda i: (i, 0)),
                  pl.BlockSpec((tm, TOP_K), lambda i: (i, 0)), row, row, vec, vec],
        out_specs=[row, row],
        out_shape=[jax.ShapeDtypeStruct((m, d), F32), jax.ShapeDtypeStruct((m, d), BF16)],
        compiler_params=_params("arbitrary"),
        name="combine_ln",
    )(yg, wt, shared, x, g.reshape(1, d), b.reshape(1, d))


def _moe_layer(x1, xb1, layer, router_w, router_b, e_gate, e_up, e_down, s_gate, s_up, s_down, g, b, alpha):
    t, d = x1.shape
    rows = min(MOE_ROWS, t)
    idx_t, wt_t = _router(x1, router_w, router_b)
    n_assign = t * TOP_K
    flat_e = idx_t.T.reshape(n_assign)
    order = jnp.argsort(flat_e, stable=True).astype(jnp.int32)
    se = flat_e[order]
    bounds = jnp.searchsorted(se, jnp.arange(N_EXPERTS + 1, dtype=jnp.int32), side='left').astype(jnp.int32)
    start, counts = bounds[:-1], bounds[1:] - bounds[:-1]
    padded = (counts + rows - 1) // rows * rows
    ends_p = jnp.cumsum(padded)
    pstart = ends_p - padded
    dest_sorted = pstart[se] + jnp.arange(n_assign, dtype=jnp.int32) - start[se]
    nblk = n_assign // rows + N_EXPERTS
    slot_tok = jnp.zeros((nblk * rows,), jnp.int32).at[dest_sorted].set(order // TOP_K)
    dest = jnp.zeros((n_assign,), jnp.int32).at[order].set(dest_sorted)
    block_e = jnp.minimum(jnp.searchsorted(ends_p, jnp.arange(nblk, dtype=jnp.int32) * rows, side='right'),
                          N_EXPERTS - 1).astype(jnp.int32)
    nused = (ends_p[-1] // rows).astype(jnp.int32).reshape(1)
    xs = jnp.take(xb1, slot_tok, axis=0, mode='clip')
    ys = _expert_ffn(xs, block_e, nused, e_gate, e_up, e_down, layer, rows)
    yg = jnp.take(ys, dest, axis=0, mode='clip').reshape(t, TOP_K * d)
    shared = _expert_ffn(xb1, jnp.zeros((t // rows,), jnp.int32), jnp.full((1,), t // rows, jnp.int32),
                         s_gate[:, None], s_up[:, None], s_down[:, None], layer, rows)
    return _combine_ln(yg, wt_t.T, shared, x1, g, b, alpha)


def _mixer_layer(x, xb, memb, layer, tabs_a, tabs_c, batch, seq, w_in, pool_w, pool_scale, c_lambda, c_norm,
                 conv_w, w_mem_kv, w_branch, w_branch_x, w_gate, b_gate, w_out, g, b, alpha):
    d = x.shape[1]
    mix = w_branch.shape[2]
    xw = w_branch_x.shape[1]
    a_cols, c_cols, d_cols = 3 * mix, 3 * mix, 3 * mix
    c0_b, c0_c, c0_d = a_cols, a_cols + mix, a_cols + mix + c_cols
    c0_x = c0_d + d_cols
    pa = _proj(xb, w_in, layer, range(3), mix, tabs_a, seq, half=HEAD_DIM // 8,
               rope_blocks=(True,) * (mix // LANES), rope_jmax=2)
    tn_c = 6 * LANES
    w_c = w_in[layer:layer + 1, :, c0_c:c0_c + c_cols]
    pc = _proj(xb, w_c, 0, range(c_cols // tn_c), tn_c, tabs_c, seq,
               half=C_QK_DIM // 8, rope_blocks=(True, True, False) * 2)
    tn_p = 512
    blocks = list(range(c0_b // tn_p, c0_b // tn_p + mix // tn_p)) + \
        list(range(c0_d // tn_p, c0_d // tn_p + (d_cols + xw) // tn_p))
    pbdx = _proj(xb, w_in, layer, blocks, tn_p, tabs_a, seq)
    ya = _dilated_attention(pa, batch, seq)
    yb = _pooling_mixer(pbdx, pool_w[layer], pool_scale[layer], batch, seq, 0)
    yc = _diff_attention(pc, c_lambda[layer], c_norm[layer], batch, seq, layer)
    yd = _gated_short_conv(pbdx, conv_w[layer], batch, seq, mix)
    kv = _proj(memb, w_mem_kv, layer, range(2 * xw // tn_p), tn_p, tabs_a, seq)
    yx = _cross_attention(pbdx, kv, batch, seq, mix + d_cols, xw)
    merged = _gated_merge(xb, (ya, yb, yc, yd, yx), w_gate[layer].astype(BF16), w_branch[layer].astype(BF16),
                          w_branch_x[layer].astype(BF16), b_gate[layer])
    return _out_proj_ln(merged, w_out[layer].astype(BF16), x, g[layer], b[layer], alpha)


def kernel(x, mem, w_in, pool_w, pool_scale, c_lambda, c_norm, conv_w, w_mem_kv, w_branch, w_branch_x,
           w_gate, b_gate, w_out, ln1_g, ln1_b, router_w, router_b, e_gate, e_up, e_down, s_gate, s_up,
           s_down, ln2_g, ln2_b):
    batch, seq, d = x.shape
    depth = w_in.shape[0]
    alpha = (2.0 * depth) ** 0.25
    tabs_a = _rope_tables(seq, HEAD_DIM // 4, LANES)
    tabs_c = _rope_tables(seq, C_QK_DIM // 4, C_QK_DIM)
    xf = x.reshape(batch * seq, d)
    xb = xf.astype(BF16)
    memb = mem.reshape(-1, d).astype(BF16)
    for l in range(depth):
        xf, xb = _mixer_layer(xf, xb, memb, l, tabs_a, tabs_c, batch, seq, w_in, pool_w, pool_scale,
                              c_lambda, c_norm, conv_w, w_mem_kv, w_branch, w_branch_x,
                              w_gate, b_gate, w_out, ln1_g, ln1_b, alpha)
        xf, xb = _moe_layer(xf, xb, l, router_w[l], router_b[l], e_gate, e_up, e_down, s_gate,
                            s_up, s_down, ln2_g[l], ln2_b[l], alpha)
    return xf.reshape(batch, seq, d)
```

```python
import functools
import math

import jax
import jax.numpy as jnp
from jax import lax
from jax.experimental import pallas as pl
from jax.experimental.pallas import tpu as pltpu

F32 = jnp.float32
BF16 = jnp.bfloat16

LANES = 128
HEAD_DIM = 128
ROPE_THETA = 500000.0
A_PATTERNS = ((128, 1), (512, 4), (2048, 16))
A_HALF = 64
B_WINDOWS = (2, 4, 8, 16)
C_QK_DIM = 64
N_EXPERTS = 64
N_GROUPS = 8
TOP_GROUPS = 4
TOP_K = 8
ROUTE_SCALE = 2.5
LN_EPS = 1e-5
NEG_INF = -1e30
VMEM_LIMIT_BYTES = 56 * 1024 * 1024
MOE_ROWS = 256


def _params(*sem):
    return pltpu.CompilerParams(dimension_semantics=sem, vmem_limit_bytes=VMEM_LIMIT_BYTES)


def _rope_tables(seq, rot_dim, period):
    half = rot_dim // 2
    inv = 1.0 / (ROPE_THETA ** (jnp.arange(0, rot_dim, 2, dtype=F32) / rot_dim))
    ang = jnp.arange(seq, dtype=F32)[:, None] * inv[None, :]
    cos, sin = jnp.cos(ang), jnp.sin(ang)
    lane = jnp.arange(LANES) % period
    first = lane < half
    second = (lane >= half) & (lane < 2 * half)
    idx = jnp.where(first, lane, jnp.where(second, lane - half, 0))
    cos_l, sin_l = cos[:, idx], sin[:, idx]
    ct = jnp.where(first | second, cos_l, 1.0)
    s1 = jnp.where(first, -sin_l, 0.0)
    s2 = jnp.where(second, sin_l, 0.0)
    return ct.astype(F32), s1.astype(F32), s2.astype(F32)


def _proj_kernel(x_ref, w_ref, ct_ref, s1_ref, s2_ref, o_ref, wb_ref, *, half, rope_blocks, rope_jmax):
    j = pl.program_id(0)
    i = pl.program_id(1)

    @pl.when(i == 0)
    def _():
        wb_ref[...] = w_ref[...].astype(BF16)

    acc = jnp.dot(x_ref[...], wb_ref[...], preferred_element_type=F32)

    def plain():
        o_ref[...] = acc.astype(o_ref.dtype)

    def roped():
        ct, s1, s2 = ct_ref[...], s1_ref[...], s2_ref[...]
        for b, use in enumerate(rope_blocks):
            blk = acc[:, b * LANES:(b + 1) * LANES]
            if use:
                blk = (blk * ct + pltpu.roll(blk, LANES - half, 1) * s1
                       + pltpu.roll(blk, half, 1) * s2)
            o_ref[:, b * LANES:(b + 1) * LANES] = blk.astype(o_ref.dtype)

    if not any(rope_blocks):
        plain()
    elif rope_jmax is None:
        roped()
    else:
        pl.when(j < rope_jmax)(roped)
        pl.when(j >= rope_jmax)(plain)


def _proj(xb, w, layer, col_blocks, tn, tables, seq, *, half=0, rope_blocks=None, rope_jmax=None, tm=512):
    m, k = xb.shape
    nj = len(col_blocks)
    tm = min(tm, m)
    rope_blocks = tuple(rope_blocks) if rope_blocks is not None else (False,) * (tn // LANES)
    col_blocks = tuple(col_blocks)
    first, contiguous = col_blocks[0], all(col_blocks[a] == col_blocks[0] + a for a in range(nj))
    if contiguous:
        w_map = lambda j, i: (layer, 0, first + j)
    else:
        n0 = next(a for a in range(1, nj) if col_blocks[a] != col_blocks[0] + a)
        second = col_blocks[n0]
        w_map = lambda j, i: (layer, 0, jnp.where(j < n0, first + j, second + j - n0))
    sblocks = seq // tm if seq >= tm else 1
    tab_spec = pl.BlockSpec((tm, LANES), lambda j, i: (i % sblocks, 0))
    kern = functools.partial(_proj_kernel, half=half, rope_blocks=rope_blocks, rope_jmax=rope_jmax)
    return pl.pallas_call(
        kern,
        grid=(nj, m // tm),
        in_specs=[pl.BlockSpec((tm, k), lambda j, i: (i, 0)),
                  pl.BlockSpec((None, k, tn), w_map),
                  tab_spec, tab_spec, tab_spec],
        out_specs=pl.BlockSpec((tm, tn), lambda j, i: (i, j)),
        out_shape=jax.ShapeDtypeStruct((m, nj * tn), BF16),
        scratch_shapes=[pltpu.VMEM((k, tn), BF16)],
        compiler_params=_params("arbitrary", "arbitrary"),
        name="proj",
    )(xb, w, *tables)


def _dilated_kernel(q_ref, kp_ref, km_ref, kn_ref, vp_ref, vm_ref, vn_ref, o_ref, lse_ref, *, tl, seq_len, heads):
    t = pl.program_id(2)
    l0 = t * tl
    sub = 128
    nsub = tl // sub
    scale = HEAD_DIM ** -0.5
    iq = lax.broadcasted_iota(jnp.int32, (sub, sub + 2 * A_HALF), 0)
    ik = lax.broadcasted_iota(jnp.int32, (sub, sub + 2 * A_HALF), 1)
    band = jnp.abs(ik - A_HALF - iq) <= A_HALF
    lane = lax.broadcasted_iota(jnp.int32, (sub, LANES), 1)
    valid = []
    for a in range(nsub):
        kk = l0 + a * sub - A_HALF + ik
        valid.append(band & (kk >= 0) & (kk < seq_len))
    lse_rows = [jnp.zeros((sub, LANES), F32) for _ in range(nsub)]
    for h in range(heads):
        cs = slice(h * HEAD_DIM, (h + 1) * HEAD_DIM)
        kcat = jnp.concatenate([kp_ref[:, cs], km_ref[:, cs], kn_ref[:, cs]], axis=0)
        vcat = jnp.concatenate([vp_ref[:, cs], vm_ref[:, cs], vn_ref[:, cs]], axis=0)
        for a in range(nsub):
            q = q_ref[a * sub:(a + 1) * sub, cs]
            kw = kcat[a * sub:a * sub + sub + 2 * A_HALF]
            vw = vcat[a * sub:a * sub + sub + 2 * A_HALF]
            s = lax.dot_general(q, kw, (((1,), (1,)), ((), ())), preferred_element_type=F32) * scale
            s = jnp.where(valid[a], s, NEG_INF)
            m = jnp.max(s, axis=-1, keepdims=True)
            p = jnp.exp(s - m)
            den = jnp.sum(p, axis=-1, keepdims=True)
            o = jnp.dot(p.astype(BF16), vw, preferred_element_type=F32) / den
            o_ref[a * sub:(a + 1) * sub, cs] = o.astype(o_ref.dtype)
            lse_rows[a] = jnp.where(lane == h, m + jnp.log(den), lse_rows[a])
    for a in range(nsub):
        lse_ref[a * sub:(a + 1) * sub, :] = lse_rows[a]


def _dilated_pattern(pa, batch, seq, dil):
    width = pa.shape[1] // 3
    heads = width // HEAD_DIM
    sl = seq // dil
    tl = min(256, sl)
    nt = sl // tl
    hb = tl // A_HALF
    last_halo = sl // A_HALF - 1
    view = pa.reshape(batch, sl, dil * 3 * width)

    def main(c):
        return pl.BlockSpec((None, tl, width), lambda b, r, t: (b, t, r * 3 + c))

    def prev(c):
        return pl.BlockSpec((None, A_HALF, width), lambda b, r, t: (b, jnp.maximum(t * hb - 1, 0), r * 3 + c))

    def nxt(c):
        return pl.BlockSpec((None, A_HALF, width),
                            lambda b, r, t: (b, jnp.minimum((t + 1) * hb, last_halo), r * 3 + c))

    kern = functools.partial(_dilated_kernel, tl=tl, seq_len=sl, heads=heads)
    o, lse = pl.pallas_call(
        kern,
        grid=(batch, dil, nt),
        in_specs=[main(0), prev(1), main(1), nxt(1), prev(2), main(2), nxt(2)],
        out_specs=[pl.BlockSpec((None, tl, width), lambda b, r, t: (b, t, r)),
                   pl.BlockSpec((None, tl, LANES), lambda b, r, t: (b, t, r))],
        out_shape=[jax.ShapeDtypeStruct((batch, sl, dil * width), BF16),
                   jax.ShapeDtypeStruct((batch, sl, dil * LANES), F32)],
        compiler_params=_params("arbitrary", "arbitrary", "arbitrary"),
        name=f"dilated_d{dil}",
    )(view, view, view, view, view, view, view)
    return o.reshape(batch * seq, width), lse.reshape(batch * seq, LANES)


def _dilated_combine_kernel(o0_ref, o1_ref, o2_ref, l0_ref, l1_ref, l2_ref, y_ref, *, heads):
    l0, l1, l2 = l0_ref[...], l1_ref[...], l2_ref[...]
    m = jnp.maximum(jnp.maximum(l0, l1), l2)
    e0, e1, e2 = jnp.exp(l0 - m), jnp.exp(l1 - m), jnp.exp(l2 - m)
    inv = 1.0 / (e0 + e1 + e2)
    w0, w1, w2 = e0 * inv, e1 * inv, e2 * inv
    for h in range(heads):
        cs = slice(h * HEAD_DIM, (h + 1) * HEAD_DIM)
        y = (w0[:, h:h + 1] * o0_ref[:, cs].astype(F32) + w1[:, h:h + 1] * o1_ref[:, cs].astype(F32)
             + w2[:, h:h + 1] * o2_ref[:, cs].astype(F32))
        y_ref[:, cs] = y.astype(y_ref.dtype)


def _dilated_attention(pa, batch, seq):
    outs, lses = zip(*[_dilated_pattern(pa, batch, seq, d) for _, d in A_PATTERNS])
    m, width = outs[0].shape
    tm = min(512, m)
    ospec = pl.BlockSpec((tm, width), lambda i: (i, 0))
    lspec = pl.BlockSpec((tm, LANES), lambda i: (i, 0))
    return pl.pallas_call(
        functools.partial(_dilated_combine_kernel, heads=width // HEAD_DIM),
        grid=(m // tm,),
        in_specs=[ospec, ospec, ospec, lspec, lspec, lspec],
        out_specs=ospec,
        out_shape=jax.ShapeDtypeStruct((m, width), BF16),
        compiler_params=_params("arbitrary"),
        name="dilated_combine",
    )(*outs, *lses)


def _pool_kernel(u_ref, w_ref, sc_ref, o_ref, pad_ref, *, seq):
    g = pl.program_id(1)
    pad = 8
    uf = u_ref[...].astype(F32)
    pad_ref[0:pad, :] = jnp.zeros((pad, uf.shape[1]), F32)
    pad_ref[pad + seq:pad + seq + pad, :] = jnp.zeros((pad, uf.shape[1]), F32)
    pad_ref[pad:pad + seq, :] = uf
    pos = lax.broadcasted_iota(jnp.int32, (seq, 1), 0)
    for gi, win in enumerate(B_WINDOWS):
        @pl.when(g == gi)
        def _(win=win):
            before, after = win // 2, win - win // 2
            tot = pad_ref[pad - before:pad - before + seq, :]
            for off in range(-before + 1, after):
                tot = tot + pad_ref[pad + off:pad + off + seq, :]
            cnt = (jnp.minimum(pos + after, seq) - jnp.maximum(pos - before, 0)).astype(F32)
            pooled = tot / cnt - uf
            mixed = jnp.dot(pooled.astype(BF16), w_ref[...].astype(BF16), preferred_element_type=F32)
            o_ref[...] = (mixed * sc_ref[...]).astype(o_ref.dtype)


def _pooling_mixer(pbdx, pool_w, pool_scale, batch, seq, col0):
    ng, cg = pool_w.shape[0], pool_w.shape[1]
    view = pbdx.reshape(batch, seq, pbdx.shape[1])
    cb = col0 // cg
    return pl.pallas_call(
        functools.partial(_pool_kernel, seq=seq),
        grid=(batch, ng),
        in_specs=[pl.BlockSpec((None, seq, cg), lambda b, g: (b, 0, cb + g)),
                  pl.BlockSpec((None, cg, cg), lambda b, g: (g, 0, 0)),
                  pl.BlockSpec((1, cg), lambda b, g: (0, g))],
        out_specs=pl.BlockSpec((None, seq, cg), lambda b, g: (b, 0, g)),
        out_shape=jax.ShapeDtypeStruct((batch, seq, ng * cg), BF16),
        scratch_shapes=[pltpu.VMEM((seq + 16, cg), F32)],
        compiler_params=_params("arbitrary", "arbitrary"),
        name="pooling",
    )(view, pool_w, pool_scale.reshape(1, ng * cg)).reshape(batch * seq, ng * cg)


def _conv_kernel(bg_ref, cg_ref, h_ref, w_ref, o_ref, pad_ref, *, seq):
    pad = 8
    u = cg_ref[...].astype(F32) * h_ref[...].astype(F32)
    cols = u.shape[1]
    pad_ref[0:pad, :] = jnp.zeros((pad, cols), F32)
    pad_ref[pad + seq:pad + seq + pad, :] = jnp.zeros((pad, cols), F32)
    pad_ref[pad:pad + seq, :] = u
    w = w_ref[...]
    conv = (pad_ref[pad - 1:pad - 1 + seq, :] * w[0:1, :] + u * w[1:2, :]
            + pad_ref[pad + 1:pad + 1 + seq, :] * w[2:3, :])
    o_ref[...] = (bg_ref[...].astype(F32) * conv).astype(o_ref.dtype)


def _gated_short_conv(pbdx, conv_w, batch, seq, col0):
    width = conv_w.shape[1]
    cb = 256
    nb = width // cb
    view = pbdx.reshape(batch, seq, pbdx.shape[1])
    base = col0 // cb

    def spec(part):
        return pl.BlockSpec((None, seq, cb), lambda b, c: (b, 0, base + part * nb + c))

    return pl.pallas_call(
        functools.partial(_conv_kernel, seq=seq),
        grid=(batch, nb),
        in_specs=[spec(0), spec(1), spec(2), pl.BlockSpec((3, cb), lambda b, c: (0, c))],
        out_specs=pl.BlockSpec((None, seq, cb), lambda b, c: (b, 0, c)),
        out_shape=jax.ShapeDtypeStruct((batch, seq, width), BF16),
        scratch_shapes=[pltpu.VMEM((seq + 16, cb), F32)],
        compiler_params=_params("arbitrary", "arbitrary"),
        name="short_conv",
    )(view, view, view, conv_w).reshape(batch * seq, width)


def _diff_kernel(q_ref, k_ref, v_ref, lam_ref, nrm_ref, o_ref, *, lam_init):
    lq = lam_ref[...].astype(F32)
    lam = (jnp.exp(jnp.sum(lq[0:1] * lq[1:2], axis=-1, keepdims=True))
           - jnp.exp(jnp.sum(lq[2:3] * lq[3:4], axis=-1, keepdims=True)) + lam_init)
    q = q_ref[...]
    k = k_ref[...]
    v = v_ref[...]
    lane = lax.broadcasted_iota(jnp.int32, q.shape, 1)
    scale = C_QK_DIM ** -0.5
    qs = q * jnp.asarray(scale, q.dtype)
    zero = jnp.zeros_like(qs)
    outs = []
    for part in range(2):
        qi = jnp.where((lane < C_QK_DIM) if part == 0 else (lane >= C_QK_DIM), qs, zero)
        s = lax.dot_general(qi, k, (((1,), (1,)), ((), ())), preferred_element_type=F32)
        m = jnp.max(s, axis=-1, keepdims=True)
        e = jnp.exp(s - m)
        den = jnp.sum(e, axis=-1, keepdims=True)
        outs.append(jnp.dot(e.astype(BF16), v, preferred_element_type=F32) / den)
    o = outs[0] - lam * outs[1]
    o = o * lax.rsqrt(jnp.mean(o * o, axis=-1, keepdims=True) + LN_EPS) * nrm_ref[...] * (1.0 - lam_init)
    o_ref[...] = o.astype(o_ref.dtype)


def _diff_attention(pc, c_lambda, c_norm, batch, seq, layer, tq=256):
    heads = pc.shape[1] // (3 * LANES)
    tq = min(tq, seq)
    lam_init = 0.8 - 0.6 * math.exp(-0.3 * layer)
    view = pc.reshape(batch, seq, pc.shape[1])
    return pl.pallas_call(
        functools.partial(_diff_kernel, lam_init=lam_init),
        grid=(batch, heads, seq // tq),
        in_specs=[pl.BlockSpec((None, tq, LANES), lambda b, h, t: (b, t, 3 * h)),
                  pl.BlockSpec((None, seq, LANES), lambda b, h, t: (b, 0, 3 * h + 1)),
                  pl.BlockSpec((None, seq, LANES), lambda b, h, t: (b, 0, 3 * h + 2)),
                  pl.BlockSpec((4, C_QK_DIM), lambda b, h, t: (0, 0)),
                  pl.BlockSpec((1, LANES), lambda b, h, t: (0, 0))],
        out_specs=pl.BlockSpec((None, tq, LANES), lambda b, h, t: (b, t, h)),
        out_shape=jax.ShapeDtypeStruct((batch, seq, heads * LANES), BF16),
        compiler_params=_params("arbitrary", "arbitrary", "arbitrary"),
        name="diff_attention",
    )(view, view, view, c_lambda, c_norm.reshape(1, LANES)).reshape(batch * seq, heads * LANES)


def _cross_kernel(q_ref, kv_ref, o_ref, *, heads):
    scale = HEAD_DIM ** -0.5
    width = heads * HEAD_DIM
    for h in range(heads):
        cs = slice(h * HEAD_DIM, (h + 1) * HEAD_DIM)
        k = kv_ref[:, h * HEAD_DIM:(h + 1) * HEAD_DIM]
        v = kv_ref[:, width + h * HEAD_DIM:width + (h + 1) * HEAD_DIM]
        s = lax.dot_general(q_ref[:, cs], k, (((1,), (1,)), ((), ())), preferred_element_type=F32) * scale
        m = jnp.max(s, axis=-1, keepdims=True)
        e = jnp.exp(s - m)
        den = jnp.sum(e, axis=-1, keepdims=True)
        o = jnp.dot(e.astype(BF16), v, preferred_element_type=F32) / den
        o_ref[:, cs] = o.astype(o_ref.dtype)


def _cross_attention(pbdx, kv, batch, seq, col0, width, tq=512):
    mem_len = kv.shape[0] // batch
    tq = min(tq, seq)
    view = pbdx.reshape(batch, seq, pbdx.shape[1])
    return pl.pallas_call(
        functools.partial(_cross_kernel, heads=width // HEAD_DIM),
        grid=(batch, seq // tq),
        in_specs=[pl.BlockSpec((None, tq, width), lambda b, t: (b, t, col0 // width)),
                  pl.BlockSpec((None, mem_len, 2 * width), lambda b, t: (b, 0, 0))],
        out_specs=pl.BlockSpec((None, tq, width), lambda b, t: (b, t, 0)),
        out_shape=jax.ShapeDtypeStruct((batch, seq, width), BF16),
        compiler_params=_params("arbitrary", "arbitrary"),
        name="cross_attention",
    )(view, kv.reshape(batch, mem_len, 2 * width)).reshape(batch * seq, width)


def _merge_kernel(x_ref, ya_ref, yb_ref, yc_ref, yd_ref, yx_ref,
                  g0, g1, g2, g3, g4, p0, p1, p2, p3, p4, b0, b1, b2, b3, b4, o_ref):
    x = x_ref[...]
    merged = None
    for y_ref, g_ref, p_ref, b_ref in ((ya_ref, g0, p0, b0), (yb_ref, g1, p1, b1), (yc_ref, g2, p2, b2),
                                       (yd_ref, g3, p3, b3), (yx_ref, g4, p4, b4)):
        gate = jax.nn.sigmoid(jnp.dot(x, g_ref[...], preferred_element_type=F32) + b_ref[...])
        term = gate * jnp.dot(y_ref[...], p_ref[...], preferred_element_type=F32)
        merged = term if merged is None else merged + term
    o_ref[...] = merged.astype(o_ref.dtype)


def _gated_merge(xb, ys, w_gate, w_branch, w_branch_x, b_gate, tm=512, tn=512):
    m, d = xb.shape
    tm = min(tm, m)
    nb = d // tn
    n_br = 5
    act = [pl.BlockSpec((tm, d), lambda j, i: (i, 0))]
    act += [pl.BlockSpec((tm, y.shape[1]), lambda j, i: (i, 0)) for y in ys]
    gates = [pl.BlockSpec((d, tn), functools.partial(lambda j, i, br: (0, br * nb + j), br=br)) for br in range(n_br)]
    projs = [pl.BlockSpec((None, w_branch.shape[1], tn), functools.partial(lambda j, i, br: (br, 0, j), br=br))
             for br in range(4)]
    projs.append(pl.BlockSpec((w_branch_x.shape[0], tn), lambda j, i: (0, j)))
    biases = [pl.BlockSpec((1, tn), functools.partial(lambda j, i, br: (0, br * nb + j), br=br)) for br in range(n_br)]
    bg2 = b_gate.reshape(1, n_br * d)
    return pl.pallas_call(
        _merge_kernel,
        grid=(nb, m // tm),
        in_specs=act + gates + projs + biases,
        out_specs=pl.BlockSpec((tm, tn), lambda j, i: (i, j)),
        out_shape=jax.ShapeDtypeStruct((m, d), BF16),
        compiler_params=_params("arbitrary", "arbitrary"),
        name="gated_merge",
    )(xb, *ys, *([w_gate] * n_br), *([w_branch] * 4), w_branch_x, *([bg2] * n_br))


def _layer_norm_rows(h, g, b):
    mu = jnp.mean(h, axis=-1, keepdims=True)
    hc = h - mu
    var = jnp.mean(hc * hc, axis=-1, keepdims=True)
    return hc * lax.rsqrt(var + LN_EPS) * g + b


def _out_ln_kernel(mg_ref, w_ref, x_ref, g_ref, b_ref, xo_ref, xb_ref, *, alpha):
    y = jnp.dot(mg_ref[...], w_ref[...], preferred_element_type=F32)
    xn = _layer_norm_rows(alpha * x_ref[...] + y, g_ref[...], b_ref[...])
    xo_ref[...] = xn
    xb_ref[...] = xn.astype(BF16)


def _out_proj_ln(merged, w_out_b, x, g, b, alpha, tm=256):
    m, d = x.shape
    tm = min(tm, m)
    row = pl.BlockSpec((tm, d), lambda i: (i, 0))
    vec = pl.BlockSpec((1, d), lambda i: (0, 0))
    return pl.pallas_call(
        functools.partial(_out_ln_kernel, alpha=alpha),
        grid=(m // tm,),
        in_specs=[row, pl.BlockSpec((d, d), lambda i: (0, 0)), row, vec, vec],
        out_specs=[row, row],
        out_shape=[jax.ShapeDtypeStruct((m, d), F32), jax.ShapeDtypeStruct((m, d), BF16)],
        compiler_params=_params("arbitrary"),
        name="out_proj_ln",
    )(merged, w_out_b, x, g.reshape(1, d), b.reshape(1, d))


def _split_bf16(a):
    hi = a.astype(BF16)
    lo = (a - hi.astype(F32)).astype(BF16)
    return hi, lo


def _router_kernel(x_ref, w_ref, b_ref, idx_ref, wt_ref):
    x = x_ref[...]
    w = w_ref[...]
    xh, xl = _split_bf16(x)
    wh, wl = _split_bf16(w)
    logits = (jnp.dot(xh, wh, preferred_element_type=F32) + jnp.dot(xl, wh, preferred_element_type=F32)
              + jnp.dot(xh, wl, preferred_element_type=F32))
    lt = logits.T[:N_EXPERTS, :]
    tm = lt.shape[1]
    scores = jax.nn.sigmoid(lt)
    biased = scores + b_ref[...]
    gsz = N_EXPERTS // N_GROUPS
    sub8 = lax.broadcasted_iota(jnp.int32, (gsz, tm), 0).astype(F32)
    grp_rows = []
    for g in range(N_GROUPS):
        blk = biased[g * gsz:(g + 1) * gsz, :]
        m1 = jnp.max(blk, axis=0, keepdims=True)
        i1 = jnp.min(jnp.where(blk == m1, sub8, float(gsz)), axis=0, keepdims=True)
        m2 = jnp.max(jnp.where(sub8 == i1, -jnp.inf, blk), axis=0, keepdims=True)
        grp_rows.append(m1 + m2)
    grp = jnp.concatenate(grp_rows, axis=0)
    subg = lax.broadcasted_iota(jnp.int32, (N_GROUPS, tm), 0).astype(F32)
    gsel = jnp.zeros((N_GROUPS, tm), F32)
    for _ in range(TOP_GROUPS):
        mg = jnp.max(grp, axis=0, keepdims=True)
        ig = jnp.min(jnp.where(grp == mg, subg, float(N_GROUPS)), axis=0, keepdims=True)
        hit = subg == ig
        gsel = jnp.where(hit, 1.0, gsel)
        grp = jnp.where(hit, -jnp.inf, grp)
    emask = jnp.concatenate([jnp.broadcast_to(gsel[g:g + 1, :], (gsz, tm)) for g in range(N_GROUPS)], axis=0)
    cand = jnp.where(emask > 0.5, biased, -jnp.inf)
    sube = lax.broadcasted_iota(jnp.int32, (N_EXPERTS, tm), 0).astype(F32)
    idx_rows, w_rows = [], []
    for _ in range(TOP_K):
        mc = jnp.max(cand, axis=0, keepdims=True)
        ic = jnp.min(jnp.where(cand == mc, sube, float(N_EXPERTS)), axis=0, keepdims=True)
        hit = sube == ic
        idx_rows.append(ic)
        w_rows.append(jnp.sum(jnp.where(hit, scores, 0.0), axis=0, keepdims=True))
        cand = jnp.where(hit, -jnp.inf, cand)
    wsel = jnp.concatenate(w_rows, axis=0)
    wsel = wsel / jnp.sum(wsel, axis=0, keepdims=True) * ROUTE_SCALE
    idx_ref[...] = jnp.concatenate(idx_rows, axis=0).astype(jnp.int32)
    wt_ref[...] = wsel


def _router(x, router_w, router_b, tm=512):
    m, d = x.shape
    tm = min(tm, m)
    w_pad = jnp.pad(router_w, ((0, 0), (0, LANES - N_EXPERTS)))
    return pl.pallas_call(
        _router_kernel,
        grid=(m // tm,),
        in_specs=[pl.BlockSpec((tm, d), lambda i: (i, 0)),
                  pl.BlockSpec((d, LANES), lambda i: (0, 0)),
                  pl.BlockSpec((N_EXPERTS, 1), lambda i: (0, 0))],
        out_specs=[pl.BlockSpec((TOP_K, tm), lambda i: (0, i)), pl.BlockSpec((TOP_K, tm), lambda i: (0, i))],
        out_shape=[jax.ShapeDtypeStruct((TOP_K, m), jnp.int32), jax.ShapeDtypeStruct((TOP_K, m), F32)],
        compiler_params=_params("arbitrary"),
        name="router",
    )(x, w_pad, router_b.reshape(N_EXPERTS, 1))


def _expert_kernel(be_ref, nb_ref, x_ref, wg_ref, wu_ref, wd_ref, o_ref, wgb, wub, wdb):
    i = pl.program_id(0)
    prev = be_ref[jnp.maximum(i - 1, 0)]
    fresh = (i == 0) | (be_ref[i] != prev)

    @pl.when(fresh)
    def _():
        wgb[...] = wg_ref[...].astype(BF16)
        wub[...] = wu_ref[...].astype(BF16)
        wdb[...] = wd_ref[...].astype(BF16)

    @pl.when(i < nb_ref[0])
    def _():
        x = x_ref[...]
        gate = jnp.dot(x, wgb[...], preferred_element_type=F32)
        up = jnp.dot(x, wub[...], preferred_element_type=F32)
        act = (jax.nn.silu(gate) * up).astype(BF16)
        o_ref[...] = jnp.dot(act, wdb[...], preferred_element_type=F32).astype(o_ref.dtype)

    @pl.when(i >= nb_ref[0])
    def _():
        o_ref[...] = jnp.zeros_like(o_ref)


def _expert_ffn(xs, block_e, nblocks, w_gate, w_up, w_down, layer, rows):
    ns, d = xs.shape
    de = w_gate.shape[3]
    grid_spec = pltpu.PrefetchScalarGridSpec(
        num_scalar_prefetch=2,
        grid=(ns // rows,),
        in_specs=[pl.BlockSpec((rows, d), lambda i, be, nb: (i, 0)),
                  pl.BlockSpec((None, None, d, de), lambda i, be, nb: (layer, be[i], 0, 0)),
                  pl.BlockSpec((None, None, d, de), lambda i, be, nb: (layer, be[i], 0, 0)),
                  pl.BlockSpec((None, None, de, d), lambda i, be, nb: (layer, be[i], 0, 0))],
        out_specs=pl.BlockSpec((rows, d), lambda i, be, nb: (i, 0)),
        scratch_shapes=[pltpu.VMEM((d, de), BF16), pltpu.VMEM((d, de), BF16), pltpu.VMEM((de, d), BF16)],
    )
    return pl.pallas_call(
        _expert_kernel,
        grid_spec=grid_spec,
        out_shape=jax.ShapeDtypeStruct((ns, d), BF16),
        compiler_params=_params("arbitrary"),
        name="expert_ffn",
    )(block_e, nblocks, xs, w_gate, w_up, w_down)


def _combine_ln_kernel(*refs, alpha):
    yg_refs = refs[:TOP_K]
    wt_ref, sh_ref, x_ref, g_ref, b_ref, xo_ref, xb_ref = refs[TOP_K:]
    wt = wt_ref[...]
    routed = sh_ref[...].astype(F32)
    for k in range(TOP_K):
        routed = routed + wt[:, k:k + 1] * yg_refs[k][...].astype(F32)
    xn = _layer_norm_rows(alpha * x_ref[...] + routed, g_ref[...], b_ref[...])
    xo_ref[...] = xn
    xb_ref[...] = xn.astype(BF16)


def _combine_ln(yg, wt, shared, x, g, b, alpha, tm=256):
    m, d = x.shape
    tm = min(tm, m)
    nb = m // tm
    row = pl.BlockSpec((tm, d), lambda i: (i, 0))
    vec = pl.BlockSpec((1, d), lambda i: (0, 0))
    yg_specs = [pl.BlockSpec((tm, d), functools.partial(lambda i, k: (k * nb + i, 0), k=k)) for k in range(TOP_K)]
    return pl.pallas_call(
        functools.partial(_combine_ln_kernel, alpha=alpha),
        grid=(nb,),
        in_specs=yg_specs + [pl.BlockSpec((tm, TOP_K), lambda i: (i, 0)), row, row, vec, vec],
        out_specs=[row, row],
        out_shape=[jax.ShapeDtypeStruct((m, d), F32), jax.ShapeDtypeStruct((m, d), BF16)],
        compiler_params=_params("arbitrary"),
        name="combine_ln",
    )(*([yg] * TOP_K), wt, shared, x, g.reshape(1, d), b.reshape(1, d))


def _moe_layer(x1, xb1, layer, router_w, router_b, e_gate, e_up, e_down, s_gate, s_up, s_down, g, b, alpha):
    t, d = x1.shape
    rows = min(MOE_ROWS, t)
    idx_t, wt_t = _router(x1, router_w, router_b)
    n_assign = t * TOP_K
    i32 = jnp.int32
    flat_e = idx_t.reshape(n_assign)
    iota_a = jnp.arange(n_assign, dtype=i32)
    se, order = lax.sort_key_val(flat_e, iota_a)
    experts = jnp.arange(N_EXPERTS, dtype=i32)
    ends = jnp.sum((se[None, :] <= experts[:, None]).astype(i32), axis=1)
    counts = ends - jnp.concatenate([jnp.zeros((1,), i32), ends[:-1]])
    padded = (counts + rows - 1) // rows * rows
    ends_p = jnp.cumsum(padded)
    pad_e = padded - counts
    dest_sorted = iota_a + jnp.sum(jnp.where(iota_a[:, None] >= ends[None, :], pad_e[None, :], 0), axis=1)
    _, dest = lax.sort_key_val(order, dest_sorted)
    nblk = n_assign // rows + N_EXPERTS
    slots = jnp.arange(nblk * rows, dtype=i32)
    src = slots - jnp.sum(jnp.where(slots[:, None] >= ends_p[None, :], pad_e[None, :], 0), axis=1)
    slot_tok = jnp.take(order, jnp.clip(src, 0, n_assign - 1), mode='clip') % t
    block_e = jnp.minimum(jnp.sum((ends_p[None, :] <= (jnp.arange(nblk, dtype=i32) * rows)[:, None]).astype(i32),
                                  axis=1), N_EXPERTS - 1)
    nused = (ends_p[-1] // rows).astype(i32).reshape(1)
    xs = jnp.take(xb1, slot_tok, axis=0, mode='clip')
    ys = _expert_ffn(xs, block_e, nused, e_gate, e_up, e_down, layer, rows)
    yg = jnp.take(ys, dest, axis=0, mode='clip')
    shared = _expert_ffn(xb1, jnp.zeros((t // rows,), i32), jnp.full((1,), t // rows, i32),
                         s_gate[:, None], s_up[:, None], s_down[:, None], layer, rows)
    return _combine_ln(yg, wt_t.T, shared, x1, g, b, alpha)


def _mixer_layer(x, xb, memb, layer, tabs_a, tabs_c, batch, seq, w_in, pool_w, pool_scale, c_lambda, c_norm,
                 conv_w, w_mem_kv, w_branch, w_branch_x, w_gate, b_gate, w_out, g, b, alpha):
    d = x.shape[1]
    mix = w_branch.shape[2]
    xw = w_branch_x.shape[1]
    a_cols, c_cols, d_cols = 3 * mix, 3 * mix, 3 * mix
    c0_b, c0_c, c0_d = a_cols, a_cols + mix, a_cols + mix + c_cols
    c0_x = c0_d + d_cols
    pa = _proj(xb, w_in, layer, range(3), mix, tabs_a, seq, half=HEAD_DIM // 8,
               rope_blocks=(True,) * (mix // LANES), rope_jmax=2)
    tn_c = 6 * LANES
    w_c = w_in[layer:layer + 1, :, c0_c:c0_c + c_cols]
    pc = _proj(xb, w_c, 0, range(c_cols // tn_c), tn_c, tabs_c, seq,
               half=C_QK_DIM // 8, rope_blocks=(True, True, False) * 2)
    tn_p = 512
    blocks = list(range(c0_b // tn_p, c0_b // tn_p + mix // tn_p)) + \
        list(range(c0_d // tn_p, c0_d // tn_p + (d_cols + xw) // tn_p))
    pbdx = _proj(xb, w_in, layer, blocks, tn_p, tabs_a, seq)
    ya = _dilated_attention(pa, batch, seq)
    yb = _pooling_mixer(pbdx, pool_w[layer], pool_scale[layer], batch, seq, 0)
    yc = _diff_attention(pc, c_lambda[layer], c_norm[layer], batch, seq, layer)
    yd = _gated_short_conv(pbdx, conv_w[layer], batch, seq, mix)
    kv = _proj(memb, w_mem_kv, layer, range(2 * xw // tn_p), tn_p, tabs_a, seq)
    yx = _cross_attention(pbdx, kv, batch, seq, mix + d_cols, xw)
    merged = _gated_merge(xb, (ya, yb, yc, yd, yx), w_gate[layer].astype(BF16), w_branch[layer].astype(BF16),
                          w_branch_x[layer].astype(BF16), b_gate[layer])
    return _out_proj_ln(merged, w_out[layer].astype(BF16), x, g[layer], b[layer], alpha)


def kernel(x, mem, w_in, pool_w, pool_scale, c_lambda, c_norm, conv_w, w_mem_kv, w_branch, w_branch_x,
           w_gate, b_gate, w_out, ln1_g, ln1_b, router_w, router_b, e_gate, e_up, e_down, s_gate, s_up,
           s_down, ln2_g, ln2_b):
    batch, seq, d = x.shape
    depth = w_in.shape[0]
    alpha = (2.0 * depth) ** 0.25
    tabs_a = _rope_tables(seq, HEAD_DIM // 4, LANES)
    tabs_c = _rope_tables(seq, C_QK_DIM // 4, C_QK_DIM)
    xf = x.reshape(batch * seq, d)
    xb = xf.astype(BF16)
    memb = mem.reshape(-1, d).astype(BF16)
    for l in range(depth):
        xf, xb = _mixer_layer(xf, xb, memb, l, tabs_a, tabs_c, batch, seq, w_in, pool_w, pool_scale,
                              c_lambda, c_norm, conv_w, w_mem_kv, w_branch, w_branch_x,
                              w_gate, b_gate, w_out, ln1_g, ln1_b, alpha)
        xf, xb = _moe_layer(xf, xb, l, router_w[l], router_b[l], e_gate, e_up, e_down, s_gate,
                            s_up, s_down, ln2_g[l], ln2_b[l], alpha)
    return xf.reshape(batch, seq, d)
```

```python
import functools
import math

import jax
import jax.numpy as jnp
from jax import lax
from jax.experimental import pallas as pl
from jax.experimental.pallas import tpu as pltpu

F32 = jnp.float32
BF16 = jnp.bfloat16

LANES = 128
HEAD_DIM = 128
ROPE_THETA = 500000.0
A_PATTERNS = ((128, 1), (512, 4), (2048, 16))
A_HALF = 64
B_WINDOWS = (2, 4, 8, 16)
C_QK_DIM = 64
N_EXPERTS = 64
N_GROUPS = 8
TOP_GROUPS = 4
TOP_K = 8
ROUTE_SCALE = 2.5
LN_EPS = 1e-5
NEG_INF = -1e30
VMEM_LIMIT_BYTES = 56 * 1024 * 1024
MOE_ROWS = 512


def _params(*sem):
    return pltpu.CompilerParams(dimension_semantics=sem, vmem_limit_bytes=VMEM_LIMIT_BYTES)


def _rope_tables(seq, rot_dim, period):
    half = rot_dim // 2
    inv = 1.0 / (ROPE_THETA ** (jnp.arange(0, rot_dim, 2, dtype=F32) / rot_dim))
    ang = jnp.arange(seq, dtype=F32)[:, None] * inv[None, :]
    cos, sin = jnp.cos(ang), jnp.sin(ang)
    lane = jnp.arange(LANES) % period
    first = lane < half
    second = (lane >= half) & (lane < 2 * half)
    idx = jnp.where(first, lane, jnp.where(second, lane - half, 0))
    cos_l, sin_l = cos[:, idx], sin[:, idx]
    ct = jnp.where(first | second, cos_l, 1.0)
    s1 = jnp.where(first, -sin_l, 0.0)
    s2 = jnp.where(second, sin_l, 0.0)
    return ct.astype(F32), s1.astype(F32), s2.astype(F32)


def _proj_kernel(x_ref, w_ref, *rest, half, rope_blocks, rope_jmax):
    tab_refs, o_ref, wb_ref = rest[:-2], rest[-2], rest[-1]
    j = pl.program_id(0)
    i = pl.program_id(1)

    @pl.when(i == 0)
    def _():
        wb_ref[...] = w_ref[...].astype(BF16)

    acc = jnp.dot(x_ref[...], wb_ref[...], preferred_element_type=F32)

    def plain():
        o_ref[...] = acc.astype(o_ref.dtype)

    def roped():
        tabs = [r[...] for r in tab_refs]
        for b, use in enumerate(rope_blocks):
            blk = acc[:, b * LANES:(b + 1) * LANES]
            if use:
                ct, s1, s2 = tabs[3 * (use - 1):3 * use]
                blk = (blk * ct + pltpu.roll(blk, LANES - half, 1) * s1
                       + pltpu.roll(blk, half, 1) * s2)
            o_ref[:, b * LANES:(b + 1) * LANES] = blk.astype(o_ref.dtype)

    if not any(rope_blocks):
        plain()
    elif rope_jmax is None:
        roped()
    else:
        pl.when(j < rope_jmax)(roped)
        pl.when(j >= rope_jmax)(plain)


def _proj(xb, w, layer, col_blocks, tn, tables, seq, *, half=0, rope_blocks=None, rope_jmax=None, tm=512):
    m, k = xb.shape
    nj = len(col_blocks)
    tm = min(tm, m)
    rope_blocks = tuple(int(u) for u in rope_blocks) if rope_blocks is not None else (0,) * (tn // LANES)
    col_blocks = tuple(col_blocks)
    first, contiguous = col_blocks[0], all(col_blocks[a] == col_blocks[0] + a for a in range(nj))
    if contiguous:
        w_map = lambda j, i: (layer, 0, first + j)
    else:
        n0 = next(a for a in range(1, nj) if col_blocks[a] != col_blocks[0] + a)
        second = col_blocks[n0]
        w_map = lambda j, i: (layer, 0, jnp.where(j < n0, first + j, second + j - n0))
    sblocks = seq // tm if seq >= tm else 1
    tab_spec = pl.BlockSpec((tm, LANES), lambda j, i: (i % sblocks, 0))
    kern = functools.partial(_proj_kernel, half=half, rope_blocks=rope_blocks, rope_jmax=rope_jmax)
    return pl.pallas_call(
        kern,
        grid=(nj, m // tm),
        in_specs=[pl.BlockSpec((tm, k), lambda j, i: (i, 0)),
                  pl.BlockSpec((None, k, tn), w_map)] + [tab_spec] * len(tables),
        out_specs=pl.BlockSpec((tm, tn), lambda j, i: (i, j)),
        out_shape=jax.ShapeDtypeStruct((m, nj * tn), BF16),
        scratch_shapes=[pltpu.VMEM((k, tn), BF16)],
        compiler_params=_params("arbitrary", "arbitrary"),
        name="proj",
    )(xb, w, *tables)


def _dilated_kernel(q_ref, kp_ref, km_ref, kn_ref, vp_ref, vm_ref, vn_ref, o_ref, lse_ref, *, tl, seq_len, heads):
    t = pl.program_id(2)
    l0 = t * tl
    sub = 128
    nsub = tl // sub
    scale = HEAD_DIM ** -0.5
    iq = lax.broadcasted_iota(jnp.int32, (sub, sub + 2 * A_HALF), 0)
    ik = lax.broadcasted_iota(jnp.int32, (sub, sub + 2 * A_HALF), 1)
    band = jnp.abs(ik - A_HALF - iq) <= A_HALF
    lane = lax.broadcasted_iota(jnp.int32, (sub, LANES), 1)
    valid = []
    for a in range(nsub):
        kk = l0 + a * sub - A_HALF + ik
        valid.append(band & (kk >= 0) & (kk < seq_len))
    lse_rows = [jnp.zeros((sub, LANES), F32) for _ in range(nsub)]
    for h in range(heads):
        cs = slice(h * HEAD_DIM, (h + 1) * HEAD_DIM)
        kcat = jnp.concatenate([kp_ref[:, cs], km_ref[:, cs], kn_ref[:, cs]], axis=0)
        vcat = jnp.concatenate([vp_ref[:, cs], vm_ref[:, cs], vn_ref[:, cs]], axis=0)
        for a in range(nsub):
            q = q_ref[a * sub:(a + 1) * sub, cs]
            kw = kcat[a * sub:a * sub + sub + 2 * A_HALF]
            vw = vcat[a * sub:a * sub + sub + 2 * A_HALF]
            s = lax.dot_general(q, kw, (((1,), (1,)), ((), ())), preferred_element_type=F32) * scale
            s = jnp.where(valid[a], s, NEG_INF)
            m = jnp.max(s, axis=-1, keepdims=True)
            p = jnp.exp(s - m)
            den = jnp.sum(p, axis=-1, keepdims=True)
            o = jnp.dot(p.astype(BF16), vw, preferred_element_type=F32) / den
            o_ref[a * sub:(a + 1) * sub, cs] = o.astype(o_ref.dtype)
            lse_rows[a] = jnp.where(lane == h, m + jnp.log(den), lse_rows[a])
    for a in range(nsub):
        lse_ref[a * sub:(a + 1) * sub, :] = lse_rows[a]


def _dilated_pattern(pa, batch, seq, dil):
    width = pa.shape[1] // 3
    heads = width // HEAD_DIM
    sl = seq // dil
    tl = min(256, sl)
    nt = sl // tl
    hb = tl // A_HALF
    last_halo = sl // A_HALF - 1
    view = pa.reshape(batch, sl, dil * 3 * width)

    def main(c):
        return pl.BlockSpec((None, tl, width), lambda b, r, t: (b, t, r * 3 + c))

    def prev(c):
        return pl.BlockSpec((None, A_HALF, width), lambda b, r, t: (b, jnp.maximum(t * hb - 1, 0), r * 3 + c))

    def nxt(c):
        return pl.BlockSpec((None, A_HALF, width),
                            lambda b, r, t: (b, jnp.minimum((t + 1) * hb, last_halo), r * 3 + c))

    kern = functools.partial(_dilated_kernel, tl=tl, seq_len=sl, heads=heads)
    o, lse = pl.pallas_call(
        kern,
        grid=(batch, dil, nt),
        in_specs=[main(0), prev(1), main(1), nxt(1), prev(2), main(2), nxt(2)],
        out_specs=[pl.BlockSpec((None, tl, width), lambda b, r, t: (b, t, r)),
                   pl.BlockSpec((None, tl, LANES), lambda b, r, t: (b, t, r))],
        out_shape=[jax.ShapeDtypeStruct((batch, sl, dil * width), BF16),
                   jax.ShapeDtypeStruct((batch, sl, dil * LANES), F32)],
        compiler_params=_params("arbitrary", "arbitrary", "arbitrary"),
        name=f"dilated_d{dil}",
    )(view, view, view, view, view, view, view)
    return o.reshape(batch * seq, width), lse.reshape(batch * seq, LANES)


def _dilated_combine_kernel(o0_ref, o1_ref, o2_ref, l0_ref, l1_ref, l2_ref, y_ref, *, heads):
    l0, l1, l2 = l0_ref[...], l1_ref[...], l2_ref[...]
    m = jnp.maximum(jnp.maximum(l0, l1), l2)
    e0, e1, e2 = jnp.exp(l0 - m), jnp.exp(l1 - m), jnp.exp(l2 - m)
    inv = 1.0 / (e0 + e1 + e2)
    w0, w1, w2 = e0 * inv, e1 * inv, e2 * inv
    for h in range(heads):
        cs = slice(h * HEAD_DIM, (h + 1) * HEAD_DIM)
        y = (w0[:, h:h + 1] * o0_ref[:, cs].astype(F32) + w1[:, h:h + 1] * o1_ref[:, cs].astype(F32)
             + w2[:, h:h + 1] * o2_ref[:, cs].astype(F32))
        y_ref[:, cs] = y.astype(y_ref.dtype)


def _dilated_attention(pa, batch, seq):
    outs, lses = zip(*[_dilated_pattern(pa, batch, seq, d) for _, d in A_PATTERNS])
    m, width = outs[0].shape
    tm = min(512, m)
    ospec = pl.BlockSpec((tm, width), lambda i: (i, 0))
    lspec = pl.BlockSpec((tm, LANES), lambda i: (i, 0))
    return pl.pallas_call(
        functools.partial(_dilated_combine_kernel, heads=width // HEAD_DIM),
        grid=(m // tm,),
        in_specs=[ospec, ospec, ospec, lspec, lspec, lspec],
        out_specs=ospec,
        out_shape=jax.ShapeDtypeStruct((m, width), BF16),
        compiler_params=_params("arbitrary"),
        name="dilated_combine",
    )(*outs, *lses)


def _pool_kernel(u_ref, w_ref, sc_ref, o_ref, pad_ref, *, seq):
    g = pl.program_id(1)
    pad = 8
    uf = u_ref[...].astype(F32)
    pad_ref[0:pad, :] = jnp.zeros((pad, uf.shape[1]), F32)
    pad_ref[pad + seq:pad + seq + pad, :] = jnp.zeros((pad, uf.shape[1]), F32)
    pad_ref[pad:pad + seq, :] = uf
    pos = lax.broadcasted_iota(jnp.int32, (seq, 1), 0)
    for gi, win in enumerate(B_WINDOWS):
        @pl.when(g == gi)
        def _(win=win):
            before, after = win // 2, win - win // 2
            tot = pad_ref[pad - before:pad - before + seq, :]
            for off in range(-before + 1, after):
                tot = tot + pad_ref[pad + off:pad + off + seq, :]
            cnt = (jnp.minimum(pos + after, seq) - jnp.maximum(pos - before, 0)).astype(F32)
            pooled = tot / cnt - uf
            mixed = jnp.dot(pooled.astype(BF16), w_ref[...].astype(BF16), preferred_element_type=F32)
            o_ref[...] = (mixed * sc_ref[...]).astype(o_ref.dtype)


def _pooling_mixer(pbdx, pool_w, pool_scale, batch, seq, col0):
    ng, cg = pool_w.shape[0], pool_w.shape[1]
    view = pbdx.reshape(batch, seq, pbdx.shape[1])
    cb = col0 // cg
    return pl.pallas_call(
        functools.partial(_pool_kernel, seq=seq),
        grid=(batch, ng),
        in_specs=[pl.BlockSpec((None, seq, cg), lambda b, g: (b, 0, cb + g)),
                  pl.BlockSpec((None, cg, cg), lambda b, g: (g, 0, 0)),
                  pl.BlockSpec((1, cg), lambda b, g: (0, g))],
        out_specs=pl.BlockSpec((None, seq, cg), lambda b, g: (b, 0, g)),
        out_shape=jax.ShapeDtypeStruct((batch, seq, ng * cg), BF16),
        scratch_shapes=[pltpu.VMEM((seq + 16, cg), F32)],
        compiler_params=_params("arbitrary", "arbitrary"),
        name="pooling",
    )(view, pool_w, pool_scale.reshape(1, ng * cg)).reshape(batch * seq, ng * cg)


def _conv_kernel(bg_ref, cg_ref, h_ref, w_ref, o_ref, pad_ref, *, seq):
    pad = 8
    u = cg_ref[...].astype(F32) * h_ref[...].astype(F32)
    cols = u.shape[1]
    pad_ref[0:pad, :] = jnp.zeros((pad, cols), F32)
    pad_ref[pad + seq:pad + seq + pad, :] = jnp.zeros((pad, cols), F32)
    pad_ref[pad:pad + seq, :] = u
    w = w_ref[...]
    conv = (pad_ref[pad - 1:pad - 1 + seq, :] * w[0:1, :] + u * w[1:2, :]
            + pad_ref[pad + 1:pad + 1 + seq, :] * w[2:3, :])
    o_ref[...] = (bg_ref[...].astype(F32) * conv).astype(o_ref.dtype)


def _gated_short_conv(pbdx, conv_w, batch, seq, col0):
    width = conv_w.shape[1]
    cb = 256
    nb = width // cb
    view = pbdx.reshape(batch, seq, pbdx.shape[1])
    base = col0 // cb

    def spec(part):
        return pl.BlockSpec((None, seq, cb), lambda b, c: (b, 0, base + part * nb + c))

    return pl.pallas_call(
        functools.partial(_conv_kernel, seq=seq),
        grid=(batch, nb),
        in_specs=[spec(0), spec(1), spec(2), pl.BlockSpec((3, cb), lambda b, c: (0, c))],
        out_specs=pl.BlockSpec((None, seq, cb), lambda b, c: (b, 0, c)),
        out_shape=jax.ShapeDtypeStruct((batch, seq, width), BF16),
        scratch_shapes=[pltpu.VMEM((seq + 16, cb), F32)],
        compiler_params=_params("arbitrary", "arbitrary"),
        name="short_conv",
    )(view, view, view, conv_w).reshape(batch * seq, width)


def _diff_kernel(q_ref, k_ref, v_ref, lam_ref, nrm_ref, o_ref, vone_ref, *, lam_init, chunk):
    t = pl.program_id(2)
    seq = k_ref.shape[0]
    tq = q_ref.shape[0]

    @pl.when(t == 0)
    def _():
        vone_ref[:, :LANES] = v_ref[...]
        vone_ref[:, LANES:] = jnp.ones((seq, LANES), BF16)

    lq = lam_ref[...].astype(F32)
    lam = (jnp.exp(jnp.sum(lq[0:1] * lq[1:2], axis=-1, keepdims=True))
           - jnp.exp(jnp.sum(lq[2:3] * lq[3:4], axis=-1, keepdims=True)) + lam_init)
    q = q_ref[...]
    lane = lax.broadcasted_iota(jnp.int32, q.shape, 1)
    zero = jnp.zeros_like(q)
    q2 = jnp.concatenate([jnp.where(lane < C_QK_DIM, q, zero), jnp.where(lane >= C_QK_DIM, q, zero)], axis=0)
    m = acc = None
    for c in range(seq // chunk):
        ks = slice(c * chunk, (c + 1) * chunk)
        s = lax.dot_general(q2, k_ref[ks, :], (((1,), (1,)), ((), ())), preferred_element_type=F32)
        mc = jnp.max(s, axis=-1, keepdims=True)
        m_new = mc if m is None else jnp.maximum(m, mc)
        e = jnp.exp2((s - m_new).astype(BF16))
        pv = jnp.dot(e, vone_ref[ks, :], preferred_element_type=F32)
        acc = pv if acc is None else acc * jnp.exp2(m - m_new) + pv
        m = m_new
    out = acc[:, :LANES] / acc[:, LANES:]
    o = out[:tq] - lam * out[tq:]
    o = o * lax.rsqrt(jnp.mean(o * o, axis=-1, keepdims=True) + LN_EPS) * nrm_ref[...] * (1.0 - lam_init)
    o_ref[...] = o.astype(o_ref.dtype)


def _diff_attention(pc, c_lambda, c_norm, batch, seq, layer, tq=256, chunk=1024):
    heads = pc.shape[1] // (3 * LANES)
    tq = min(tq, seq)
    chunk = min(chunk, seq)
    lam_init = 0.8 - 0.6 * math.exp(-0.3 * layer)
    view = pc.reshape(batch, seq, pc.shape[1])
    return pl.pallas_call(
        functools.partial(_diff_kernel, lam_init=lam_init, chunk=chunk),
        grid=(batch, heads, seq // tq),
        in_specs=[pl.BlockSpec((None, tq, LANES), lambda b, h, t: (b, t, 3 * h)),
                  pl.BlockSpec((None, seq, LANES), lambda b, h, t: (b, 0, 3 * h + 1)),
                  pl.BlockSpec((None, seq, LANES), lambda b, h, t: (b, 0, 3 * h + 2)),
                  pl.BlockSpec((4, C_QK_DIM), lambda b, h, t: (0, 0)),
                  pl.BlockSpec((1, LANES), lambda b, h, t: (0, 0))],
        out_specs=pl.BlockSpec((None, tq, LANES), lambda b, h, t: (b, t, h)),
        out_shape=jax.ShapeDtypeStruct((batch, seq, heads * LANES), BF16),
        scratch_shapes=[pltpu.VMEM((seq, 2 * LANES), BF16)],
        compiler_params=_params("arbitrary", "arbitrary", "arbitrary"),
        name="diff_attention",
    )(view, view, view, c_lambda, c_norm.reshape(1, LANES)).reshape(batch * seq, heads * LANES)


def _cross_kernel(q_ref, kv_ref, o_ref, *, heads):
    scale = HEAD_DIM ** -0.5
    width = heads * HEAD_DIM
    for h in range(heads):
        cs = slice(h * HEAD_DIM, (h + 1) * HEAD_DIM)
        k = kv_ref[:, h * HEAD_DIM:(h + 1) * HEAD_DIM]
        v = kv_ref[:, width + h * HEAD_DIM:width + (h + 1) * HEAD_DIM]
        s = lax.dot_general(q_ref[:, cs], k, (((1,), (1,)), ((), ())), preferred_element_type=F32) * scale
        m = jnp.max(s, axis=-1, keepdims=True)
        e = jnp.exp(s - m)
        den = jnp.sum(e, axis=-1, keepdims=True)
        o = jnp.dot(e.astype(BF16), v, preferred_element_type=F32) / den
        o_ref[:, cs] = o.astype(o_ref.dtype)


def _cross_attention(pbdx, kv, batch, seq, col0, width, tq=512):
    mem_len = kv.shape[0] // batch
    tq = min(tq, seq)
    view = pbdx.reshape(batch, seq, pbdx.shape[1])
    return pl.pallas_call(
        functools.partial(_cross_kernel, heads=width // HEAD_DIM),
        grid=(batch, seq // tq),
        in_specs=[pl.BlockSpec((None, tq, width), lambda b, t: (b, t, col0 // width)),
                  pl.BlockSpec((None, mem_len, 2 * width), lambda b, t: (b, 0, 0))],
        out_specs=pl.BlockSpec((None, tq, width), lambda b, t: (b, t, 0)),
        out_shape=jax.ShapeDtypeStruct((batch, seq, width), BF16),
        compiler_params=_params("arbitrary", "arbitrary"),
        name="cross_attention",
    )(view, kv.reshape(batch, mem_len, 2 * width)).reshape(batch * seq, width)


def _merge_kernel(x_ref, ya_ref, yb_ref, yc_ref, yd_ref, yx_ref,
                  g0, g1, g2, g3, g4, p0, p1, p2, p3, p4, b0, b1, b2, b3, b4, o_ref):
    x = x_ref[...]
    merged = None
    for y_ref, g_ref, p_ref, b_ref in ((ya_ref, g0, p0, b0), (yb_ref, g1, p1, b1), (yc_ref, g2, p2, b2),
                                       (yd_ref, g3, p3, b3), (yx_ref, g4, p4, b4)):
        gate = jax.nn.sigmoid(jnp.dot(x, g_ref[...], preferred_element_type=F32) + b_ref[...])
        term = gate * jnp.dot(y_ref[...], p_ref[...], preferred_element_type=F32)
        merged = term if merged is None else merged + term
    o_ref[...] = merged.astype(o_ref.dtype)


def _gated_merge(xb, ys, w_gate, w_branch, w_branch_x, b_gate, tm=512, tn=512):
    m, d = xb.shape
    tm = min(tm, m)
    nb = d // tn
    n_br = 5
    act = [pl.BlockSpec((tm, d), lambda j, i: (i, 0))]
    act += [pl.BlockSpec((tm, y.shape[1]), lambda j, i: (i, 0)) for y in ys]
    gates = [pl.BlockSpec((d, tn), functools.partial(lambda j, i, br: (0, br * nb + j), br=br)) for br in range(n_br)]
    projs = [pl.BlockSpec((None, w_branch.shape[1], tn), functools.partial(lambda j, i, br: (br, 0, j), br=br))
             for br in range(4)]
    projs.append(pl.BlockSpec((w_branch_x.shape[0], tn), lambda j, i: (0, j)))
    biases = [pl.BlockSpec((1, tn), functools.partial(lambda j, i, br: (0, br * nb + j), br=br)) for br in range(n_br)]
    bg2 = b_gate.reshape(1, n_br * d)
    return pl.pallas_call(
        _merge_kernel,
        grid=(nb, m // tm),
        in_specs=act + gates + projs + biases,
        out_specs=pl.BlockSpec((tm, tn), lambda j, i: (i, j)),
        out_shape=jax.ShapeDtypeStruct((m, d), BF16),
        compiler_params=_params("arbitrary", "arbitrary"),
        name="gated_merge",
    )(xb, *ys, *([w_gate] * n_br), *([w_branch] * 4), w_branch_x, *([bg2] * n_br))


def _layer_norm_rows(h, g, b):
    mu = jnp.mean(h, axis=-1, keepdims=True)
    hc = h - mu
    var = jnp.mean(hc * hc, axis=-1, keepdims=True)
    return hc * lax.rsqrt(var + LN_EPS) * g + b


def _out_ln_kernel(mg_ref, w_ref, x_ref, g_ref, b_ref, xo_ref, xb_ref, *, alpha):
    y = jnp.dot(mg_ref[...], w_ref[...], preferred_element_type=F32)
    xn = _layer_norm_rows(alpha * x_ref[...] + y, g_ref[...], b_ref[...])
    xo_ref[...] = xn
    xb_ref[...] = xn.astype(BF16)


def _out_proj_ln(merged, w_out_b, x, g, b, alpha, tm=256):
    m, d = x.shape
    tm = min(tm, m)
    row = pl.BlockSpec((tm, d), lambda i: (i, 0))
    vec = pl.BlockSpec((1, d), lambda i: (0, 0))
    return pl.pallas_call(
        functools.partial(_out_ln_kernel, alpha=alpha),
        grid=(m // tm,),
        in_specs=[row, pl.BlockSpec((d, d), lambda i: (0, 0)), row, vec, vec],
        out_specs=[row, row],
        out_shape=[jax.ShapeDtypeStruct((m, d), F32), jax.ShapeDtypeStruct((m, d), BF16)],
        compiler_params=_params("arbitrary"),
        name="out_proj_ln",
    )(merged, w_out_b, x, g.reshape(1, d), b.reshape(1, d))


def _split_bf16(a):
    hi = a.astype(BF16)
    lo = (a - hi.astype(F32)).astype(BF16)
    return hi, lo


def _router_kernel(x_ref, w_ref, b_ref, idx_ref, wt_ref):
    x = x_ref[...]
    w = w_ref[...]
    xh, xl = _split_bf16(x)
    wh, wl = _split_bf16(w)
    logits = (jnp.dot(xh, wh, preferred_element_type=F32) + jnp.dot(xl, wh, preferred_element_type=F32)
              + jnp.dot(xh, wl, preferred_element_type=F32))
    lt = logits.T[:N_EXPERTS, :]
    tm = lt.shape[1]
    scores = jax.nn.sigmoid(lt)
    biased = scores + b_ref[...]
    gsz = N_EXPERTS // N_GROUPS
    sub8 = lax.broadcasted_iota(jnp.int32, (gsz, tm), 0).astype(F32)
    grp_rows = []
    for g in range(N_GROUPS):
        blk = biased[g * gsz:(g + 1) * gsz, :]
        m1 = jnp.max(blk, axis=0, keepdims=True)
        i1 = jnp.min(jnp.where(blk == m1, sub8, float(gsz)), axis=0, keepdims=True)
        m2 = jnp.max(jnp.where(sub8 == i1, -jnp.inf, blk), axis=0, keepdims=True)
        grp_rows.append(m1 + m2)
    grp = jnp.concatenate(grp_rows, axis=0)
    subg = lax.broadcasted_iota(jnp.int32, (N_GROUPS, tm), 0).astype(F32)
    gsel = jnp.zeros((N_GROUPS, tm), F32)
    for _ in range(TOP_GROUPS):
        mg = jnp.max(grp, axis=0, keepdims=True)
        ig = jnp.min(jnp.where(grp == mg, subg, float(N_GROUPS)), axis=0, keepdims=True)
        hit = subg == ig
        gsel = jnp.where(hit, 1.0, gsel)
        grp = jnp.where(hit, -jnp.inf, grp)
    emask = jnp.concatenate([jnp.broadcast_to(gsel[g:g + 1, :], (gsz, tm)) for g in range(N_GROUPS)], axis=0)
    cand = jnp.where(emask > 0.5, biased, -jnp.inf)
    sube = lax.broadcasted_iota(jnp.int32, (N_EXPERTS, tm), 0).astype(F32)
    idx_rows, w_rows = [], []
    for _ in range(TOP_K):
        mc = jnp.max(cand, axis=0, keepdims=True)
        ic = jnp.min(jnp.where(cand == mc, sube, float(N_EXPERTS)), axis=0, keepdims=True)
        hit = sube == ic
        idx_rows.append(ic)
        w_rows.append(jnp.sum(jnp.where(hit, scores, 0.0), axis=0, keepdims=True))
        cand = jnp.where(hit, -jnp.inf, cand)
    wsel = jnp.concatenate(w_rows, axis=0)
    wsel = wsel / jnp.sum(wsel, axis=0, keepdims=True) * ROUTE_SCALE
    idx_ref[...] = jnp.concatenate(idx_rows, axis=0).astype(jnp.int32)
    wt_ref[...] = wsel


def _router(x, router_w, router_b, tm=512):
    m, d = x.shape
    tm = min(tm, m)
    w_pad = jnp.pad(router_w, ((0, 0), (0, LANES - N_EXPERTS)))
    return pl.pallas_call(
        _router_kernel,
        grid=(m // tm,),
        in_specs=[pl.BlockSpec((tm, d), lambda i: (i, 0)),
                  pl.BlockSpec((d, LANES), lambda i: (0, 0)),
                  pl.BlockSpec((N_EXPERTS, 1), lambda i: (0, 0))],
        out_specs=[pl.BlockSpec((TOP_K, tm), lambda i: (0, i)), pl.BlockSpec((TOP_K, tm), lambda i: (0, i))],
        out_shape=[jax.ShapeDtypeStruct((TOP_K, m), jnp.int32), jax.ShapeDtypeStruct((TOP_K, m), F32)],
        compiler_params=_params("arbitrary"),
        name="router",
    )(x, w_pad, router_b.reshape(N_EXPERTS, 1))


def _expert_kernel(be_ref, nb_ref, x_ref, wg_ref, wu_ref, wd_ref, o_ref, wgb, wub, wdb):
    i = pl.program_id(0)
    prev = be_ref[jnp.maximum(i - 1, 0)]
    fresh = (i == 0) | (be_ref[i] != prev)

    @pl.when(fresh)
    def _():
        wgb[...] = wg_ref[...].astype(BF16)
        wub[...] = wu_ref[...].astype(BF16)
        wdb[...] = wd_ref[...].astype(BF16)

    @pl.when(i < nb_ref[0])
    def _():
        x = x_ref[...]
        gate = jnp.dot(x, wgb[...], preferred_element_type=F32)
        up = jnp.dot(x, wub[...], preferred_element_type=F32)
        act = (jax.nn.silu(gate) * up).astype(BF16)
        o_ref[...] = jnp.dot(act, wdb[...], preferred_element_type=F32).astype(o_ref.dtype)

    @pl.when(i >= nb_ref[0])
    def _():
        o_ref[...] = jnp.zeros_like(o_ref)


def _expert_ffn(xs, block_e, nblocks, w_gate, w_up, w_down, layer, rows):
    ns, d = xs.shape
    de = w_gate.shape[3]
    grid_spec = pltpu.PrefetchScalarGridSpec(
        num_scalar_prefetch=2,
        grid=(ns // rows,),
        in_specs=[pl.BlockSpec((rows, d), lambda i, be, nb: (i, 0)),
                  pl.BlockSpec((None, None, d, de), lambda i, be, nb: (layer, be[i], 0, 0)),
                  pl.BlockSpec((None, None, d, de), lambda i, be, nb: (layer, be[i], 0, 0)),
                  pl.BlockSpec((None, None, de, d), lambda i, be, nb: (layer, be[i], 0, 0))],
        out_specs=pl.BlockSpec((rows, d), lambda i, be, nb: (i, 0)),
        scratch_shapes=[pltpu.VMEM((d, de), BF16), pltpu.VMEM((d, de), BF16), pltpu.VMEM((de, d), BF16)],
    )
    return pl.pallas_call(
        _expert_kernel,
        grid_spec=grid_spec,
        out_shape=jax.ShapeDtypeStruct((ns, d), BF16),
        compiler_params=_params("arbitrary"),
        name="expert_ffn",
    )(block_e, nblocks, xs, w_gate, w_up, w_down)


def _combine_ln_kernel(*refs, alpha):
    yg_refs = refs[:TOP_K]
    wt_ref, sh_ref, x_ref, g_ref, b_ref, xo_ref, xb_ref = refs[TOP_K:]
    wt = wt_ref[...]
    routed = sh_ref[...].astype(F32)
    for k in range(TOP_K):
        routed = routed + wt[:, k:k + 1] * yg_refs[k][...].astype(F32)
    xn = _layer_norm_rows(alpha * x_ref[...] + routed, g_ref[...], b_ref[...])
    xo_ref[...] = xn
    xb_ref[...] = xn.astype(BF16)


def _combine_ln(yg, wt, shared, x, g, b, alpha, tm=256):
    m, d = x.shape
    tm = min(tm, m)
    nb = m // tm
    row = pl.BlockSpec((tm, d), lambda i: (i, 0))
    vec = pl.BlockSpec((1, d), lambda i: (0, 0))
    yg_specs = [pl.BlockSpec((tm, d), functools.partial(lambda i, k: (k * nb + i, 0), k=k)) for k in range(TOP_K)]
    return pl.pallas_call(
        functools.partial(_combine_ln_kernel, alpha=alpha),
        grid=(nb,),
        in_specs=yg_specs + [pl.BlockSpec((tm, TOP_K), lambda i: (i, 0)), row, row, vec, vec],
        out_specs=[row, row],
        out_shape=[jax.ShapeDtypeStruct((m, d), F32), jax.ShapeDtypeStruct((m, d), BF16)],
        compiler_params=_params("arbitrary"),
        name="combine_ln",
    )(*([yg] * TOP_K), wt, shared, x, g.reshape(1, d), b.reshape(1, d))


def _moe_layer(x1, xb1, layer, router_w, router_b, e_gate, e_up, e_down, s_gate, s_up, s_down, g, b, alpha):
    t, d = x1.shape
    rows = min(MOE_ROWS, t)
    idx_t, wt_t = _router(x1, router_w, router_b)
    n_assign = t * TOP_K
    i32 = jnp.int32
    flat_e = idx_t.reshape(n_assign)
    iota_a = jnp.arange(n_assign, dtype=i32)
    se, order = lax.sort_key_val(flat_e, iota_a)
    experts = jnp.arange(N_EXPERTS, dtype=i32)
    ends = jnp.sum((se[None, :] <= experts[:, None]).astype(i32), axis=1)
    counts = ends - jnp.concatenate([jnp.zeros((1,), i32), ends[:-1]])
    padded = (counts + rows - 1) // rows * rows
    ends_p = jnp.cumsum(padded)
    pad_e = padded - counts
    dest_sorted = iota_a + jnp.sum(jnp.where(iota_a[:, None] >= ends[None, :], pad_e[None, :], 0), axis=1)
    _, dest = lax.sort_key_val(order, dest_sorted)
    nblk = n_assign // rows + N_EXPERTS
    blk0 = jnp.arange(nblk, dtype=i32) * rows
    win0 = blk0 - jnp.sum(jnp.where(blk0[:, None] >= ends_p[None, :], pad_e[None, :], 0), axis=1)
    win0 = jnp.clip(win0, 0, n_assign)
    order_ext = jnp.concatenate([order, jnp.zeros((rows,), i32)])
    slot_tok = jax.vmap(lambda s0: lax.dynamic_slice_in_dim(order_ext, s0, rows))(win0).reshape(nblk * rows) % t
    block_e = jnp.minimum(jnp.sum((ends_p[None, :] <= blk0[:, None]).astype(i32), axis=1), N_EXPERTS - 1)
    nused = (ends_p[-1] // rows).astype(i32).reshape(1)
    xs = jnp.take(xb1, slot_tok, axis=0, mode='clip')
    ys = _expert_ffn(xs, block_e, nused, e_gate, e_up, e_down, layer, rows)
    yg = jnp.take(ys, dest, axis=0, mode='clip')
    shared = _expert_ffn(xb1, jnp.zeros((t // rows,), i32), jnp.full((1,), t // rows, i32),
                         s_gate[:, None], s_up[:, None], s_down[:, None], layer, rows)
    return _combine_ln(yg, wt_t.T, shared, x1, g, b, alpha)


def _mixer_layer(x, xb, memb, layer, tabs_a, tabs_c, batch, seq, w_in, pool_w, pool_scale, c_lambda, c_norm,
                 conv_w, w_mem_kv, w_branch, w_branch_x, w_gate, b_gate, w_out, g, b, alpha):
    d = x.shape[1]
    mix = w_branch.shape[2]
    xw = w_branch_x.shape[1]
    a_cols, c_cols, d_cols = 3 * mix, 3 * mix, 3 * mix
    c0_b, c0_c, c0_d = a_cols, a_cols + mix, a_cols + mix + c_cols
    c0_x = c0_d + d_cols
    pa = _proj(xb, w_in, layer, range(3), mix, tabs_a, seq, half=HEAD_DIM // 8,
               rope_blocks=(1,) * (mix // LANES), rope_jmax=2)
    tn_c = 6 * LANES
    w_c = w_in[layer:layer + 1, :, c0_c:c0_c + c_cols]
    pc = _proj(xb, w_c, 0, range(c_cols // tn_c), tn_c, tabs_c, seq,
               half=C_QK_DIM // 8, rope_blocks=(1, 2, 0) * 2)
    tn_p = 512
    blocks = list(range(c0_b // tn_p, c0_b // tn_p + mix // tn_p)) + \
        list(range(c0_d // tn_p, c0_d // tn_p + (d_cols + xw) // tn_p))
    pbdx = _proj(xb, w_in, layer, blocks, tn_p, tabs_a, seq)
    ya = _dilated_attention(pa, batch, seq)
    yb = _pooling_mixer(pbdx, pool_w[layer], pool_scale[layer], batch, seq, 0)
    yc = _diff_attention(pc, c_lambda[layer], c_norm[layer], batch, seq, layer)
    yd = _gated_short_conv(pbdx, conv_w[layer], batch, seq, mix)
    kv = _proj(memb, w_mem_kv, layer, range(2 * xw // tn_p), tn_p, tabs_a, seq)
    yx = _cross_attention(pbdx, kv, batch, seq, mix + d_cols, xw)
    merged = _gated_merge(xb, (ya, yb, yc, yd, yx), w_gate[layer].astype(BF16), w_branch[layer].astype(BF16),
                          w_branch_x[layer].astype(BF16), b_gate[layer])
    return _out_proj_ln(merged, w_out[layer].astype(BF16), x, g[layer], b[layer], alpha)


def kernel(x, mem, w_in, pool_w, pool_scale, c_lambda, c_norm, conv_w, w_mem_kv, w_branch, w_branch_x,
           w_gate, b_gate, w_out, ln1_g, ln1_b, router_w, router_b, e_gate, e_up, e_down, s_gate, s_up,
           s_down, ln2_g, ln2_b):
    batch, seq, d = x.shape
    depth = w_in.shape[0]
    alpha = (2.0 * depth) ** 0.25
    tabs_a = _rope_tables(seq, HEAD_DIM // 4, LANES)
    tabs_ck = _rope_tables(seq, C_QK_DIM // 4, C_QK_DIM)
    q_scale = C_QK_DIM ** -0.5 * math.log2(math.e)
    tabs_c = tuple(tab * q_scale for tab in tabs_ck) + tabs_ck
    xf = x.reshape(batch * seq, d)
    xb = xf.astype(BF16)
    memb = mem.reshape(-1, d).astype(BF16)
    for l in range(depth):
        xf, xb = _mixer_layer(xf, xb, memb, l, tabs_a, tabs_c, batch, seq, w_in, pool_w, pool_scale,
                              c_lambda, c_norm, conv_w, w_mem_kv, w_branch, w_branch_x,
                              w_gate, b_gate, w_out, ln1_g, ln1_b, alpha)
        xf, xb = _moe_layer(xf, xb, l, router_w[l], router_b[l], e_gate, e_up, e_down, s_gate,
                            s_up, s_down, ln2_g[l], ln2_b[l], alpha)
    return xf.reshape(batch, seq, d)
```

```python
import functools
import math

import jax
import jax.numpy as jnp
from jax import lax
from jax.experimental import pallas as pl
from jax.experimental.pallas import tpu as pltpu

F32 = jnp.float32
BF16 = jnp.bfloat16

LANES = 128
HEAD_DIM = 128
ROPE_THETA = 500000.0
A_PATTERNS = ((128, 1), (512, 4), (2048, 16))
A_HALF = 64
B_WINDOWS = (2, 4, 8, 16)
C_QK_DIM = 64
N_EXPERTS = 64
N_GROUPS = 8
TOP_GROUPS = 4
TOP_K = 8
ROUTE_SCALE = 2.5
LN_EPS = 1e-5
NEG_INF = -1e30
VMEM_LIMIT_BYTES = 56 * 1024 * 1024
MOE_ROWS = 512


def _params(*sem):
    return pltpu.CompilerParams(dimension_semantics=sem, vmem_limit_bytes=VMEM_LIMIT_BYTES)


def _rope_tables(seq, rot_dim, period):
    half = rot_dim // 2
    inv = 1.0 / (ROPE_THETA ** (jnp.arange(0, rot_dim, 2, dtype=F32) / rot_dim))
    ang = jnp.arange(seq, dtype=F32)[:, None] * inv[None, :]
    cos, sin = jnp.cos(ang), jnp.sin(ang)
    lane = jnp.arange(LANES) % period
    first = lane < half
    second = (lane >= half) & (lane < 2 * half)
    idx = jnp.where(first, lane, jnp.where(second, lane - half, 0))
    cos_l, sin_l = cos[:, idx], sin[:, idx]
    ct = jnp.where(first | second, cos_l, 1.0)
    s1 = jnp.where(first, -sin_l, 0.0)
    s2 = jnp.where(second, sin_l, 0.0)
    return ct.astype(F32), s1.astype(F32), s2.astype(F32)


def _proj_kernel(x_ref, w_ref, *rest, half, rope_blocks, rope_jmax):
    tab_refs, o_ref, wb_ref = rest[:-2], rest[-2], rest[-1]
    j = pl.program_id(0)
    i = pl.program_id(1)

    @pl.when(i == 0)
    def _():
        wb_ref[...] = w_ref[...].astype(BF16)

    acc = jnp.dot(x_ref[...], wb_ref[...], preferred_element_type=F32)

    def plain():
        o_ref[...] = acc.astype(o_ref.dtype)

    def roped():
        tabs = [r[...] for r in tab_refs]
        for b, use in enumerate(rope_blocks):
            blk = acc[:, b * LANES:(b + 1) * LANES]
            if use:
                ct, s1, s2 = tabs[3 * (use - 1):3 * use]
                blk = (blk * ct + pltpu.roll(blk, LANES - half, 1) * s1
                       + pltpu.roll(blk, half, 1) * s2)
            o_ref[:, b * LANES:(b + 1) * LANES] = blk.astype(o_ref.dtype)

    if not any(rope_blocks):
        plain()
    elif rope_jmax is None:
        roped()
    else:
        pl.when(j < rope_jmax)(roped)
        pl.when(j >= rope_jmax)(plain)


def _proj(xb, w, layer, col_blocks, tn, tables, seq, *, half=0, rope_blocks=None, rope_jmax=None, tm=512):
    m, k = xb.shape
    nj = len(col_blocks)
    tm = min(tm, m)
    rope_blocks = tuple(int(u) for u in rope_blocks) if rope_blocks is not None else (0,) * (tn // LANES)
    col_blocks = tuple(col_blocks)
    first, contiguous = col_blocks[0], all(col_blocks[a] == col_blocks[0] + a for a in range(nj))
    if contiguous:
        w_map = lambda j, i: (layer, 0, first + j)
    else:
        n0 = next(a for a in range(1, nj) if col_blocks[a] != col_blocks[0] + a)
        second = col_blocks[n0]
        w_map = lambda j, i: (layer, 0, jnp.where(j < n0, first + j, second + j - n0))
    sblocks = seq // tm if seq >= tm else 1
    tab_spec = pl.BlockSpec((tm, LANES), lambda j, i: (i % sblocks, 0))
    kern = functools.partial(_proj_kernel, half=half, rope_blocks=rope_blocks, rope_jmax=rope_jmax)
    return pl.pallas_call(
        kern,
        grid=(nj, m // tm),
        in_specs=[pl.BlockSpec((tm, k), lambda j, i: (i, 0)),
                  pl.BlockSpec((None, k, tn), w_map)] + [tab_spec] * len(tables),
        out_specs=pl.BlockSpec((tm, tn), lambda j, i: (i, j)),
        out_shape=jax.ShapeDtypeStruct((m, nj * tn), BF16),
        scratch_shapes=[pltpu.VMEM((k, tn), BF16)],
        compiler_params=_params("arbitrary", "arbitrary"),
        name="proj",
    )(xb, w, *tables)


def _dilated_kernel(q_ref, kp_ref, km_ref, kn_ref, vp_ref, vm_ref, vn_ref, o_ref, lse_ref, *, tl, seq_len, heads):
    t = pl.program_id(2)
    l0 = t * tl
    sub = 128
    nsub = tl // sub
    scale = HEAD_DIM ** -0.5
    iq = lax.broadcasted_iota(jnp.int32, (sub, sub + 2 * A_HALF), 0)
    ik = lax.broadcasted_iota(jnp.int32, (sub, sub + 2 * A_HALF), 1)
    band = jnp.abs(ik - A_HALF - iq) <= A_HALF
    lane = lax.broadcasted_iota(jnp.int32, (sub, LANES), 1)
    valid = []
    for a in range(nsub):
        kk = l0 + a * sub - A_HALF + ik
        valid.append(band & (kk >= 0) & (kk < seq_len))
    lse_rows = [jnp.zeros((sub, LANES), F32) for _ in range(nsub)]
    for h in range(heads):
        cs = slice(h * HEAD_DIM, (h + 1) * HEAD_DIM)
        kcat = jnp.concatenate([kp_ref[:, cs], km_ref[:, cs], kn_ref[:, cs]], axis=0)
        vcat = jnp.concatenate([vp_ref[:, cs], vm_ref[:, cs], vn_ref[:, cs]], axis=0)
        for a in range(nsub):
            q = q_ref[a * sub:(a + 1) * sub, cs]
            kw = kcat[a * sub:a * sub + sub + 2 * A_HALF]
            vw = vcat[a * sub:a * sub + sub + 2 * A_HALF]
            s = lax.dot_general(q, kw, (((1,), (1,)), ((), ())), preferred_element_type=F32) * scale
            s = jnp.where(valid[a], s, NEG_INF)
            m = jnp.max(s, axis=-1, keepdims=True)
            p = jnp.exp(s - m)
            den = jnp.sum(p, axis=-1, keepdims=True)
            o = jnp.dot(p.astype(BF16), vw, preferred_element_type=F32) / den
            o_ref[a * sub:(a + 1) * sub, cs] = o.astype(o_ref.dtype)
            lse_rows[a] = jnp.where(lane == h, m + jnp.log(den), lse_rows[a])
    for a in range(nsub):
        lse_ref[a * sub:(a + 1) * sub, :] = lse_rows[a]


def _dilated_pattern(pa, batch, seq, dil):
    width = pa.shape[1] // 3
    heads = width // HEAD_DIM
    sl = seq // dil
    tl = min(256, sl)
    nt = sl // tl
    hb = tl // A_HALF
    last_halo = sl // A_HALF - 1
    view = pa.reshape(batch, sl, dil * 3 * width)

    def main(c):
        return pl.BlockSpec((None, tl, width), lambda b, r, t: (b, t, r * 3 + c))

    def prev(c):
        return pl.BlockSpec((None, A_HALF, width), lambda b, r, t: (b, jnp.maximum(t * hb - 1, 0), r * 3 + c))

    def nxt(c):
        return pl.BlockSpec((None, A_HALF, width),
                            lambda b, r, t: (b, jnp.minimum((t + 1) * hb, last_halo), r * 3 + c))

    kern = functools.partial(_dilated_kernel, tl=tl, seq_len=sl, heads=heads)
    o, lse = pl.pallas_call(
        kern,
        grid=(batch, dil, nt),
        in_specs=[main(0), prev(1), main(1), nxt(1), prev(2), main(2), nxt(2)],
        out_specs=[pl.BlockSpec((None, tl, width), lambda b, r, t: (b, t, r)),
                   pl.BlockSpec((None, tl, LANES), lambda b, r, t: (b, t, r))],
        out_shape=[jax.ShapeDtypeStruct((batch, sl, dil * width), BF16),
                   jax.ShapeDtypeStruct((batch, sl, dil * LANES), F32)],
        compiler_params=_params("arbitrary", "arbitrary", "arbitrary"),
        name=f"dilated_d{dil}",
    )(view, view, view, view, view, view, view)
    return o.reshape(batch * seq, width), lse.reshape(batch * seq, LANES)


def _dilated_combine_kernel(o0_ref, o1_ref, o2_ref, l0_ref, l1_ref, l2_ref, y_ref, *, heads):
    l0, l1, l2 = l0_ref[...], l1_ref[...], l2_ref[...]
    m = jnp.maximum(jnp.maximum(l0, l1), l2)
    e0, e1, e2 = jnp.exp(l0 - m), jnp.exp(l1 - m), jnp.exp(l2 - m)
    inv = 1.0 / (e0 + e1 + e2)
    w0, w1, w2 = e0 * inv, e1 * inv, e2 * inv
    for h in range(heads):
        cs = slice(h * HEAD_DIM, (h + 1) * HEAD_DIM)
        y = (w0[:, h:h + 1] * o0_ref[:, cs].astype(F32) + w1[:, h:h + 1] * o1_ref[:, cs].astype(F32)
             + w2[:, h:h + 1] * o2_ref[:, cs].astype(F32))
        y_ref[:, cs] = y.astype(y_ref.dtype)


def _dilated_attention(pa, batch, seq):
    outs, lses = zip(*[_dilated_pattern(pa, batch, seq, d) for _, d in A_PATTERNS])
    m, width = outs[0].shape
    tm = min(512, m)
    ospec = pl.BlockSpec((tm, width), lambda i: (i, 0))
    lspec = pl.BlockSpec((tm, LANES), lambda i: (i, 0))
    return pl.pallas_call(
        functools.partial(_dilated_combine_kernel, heads=width // HEAD_DIM),
        grid=(m // tm,),
        in_specs=[ospec, ospec, ospec, lspec, lspec, lspec],
        out_specs=ospec,
        out_shape=jax.ShapeDtypeStruct((m, width), BF16),
        compiler_params=_params("arbitrary"),
        name="dilated_combine",
    )(*outs, *lses)


def _pool_kernel(u_ref, w_ref, sc_ref, o_ref, pad_ref, *, seq):
    g = pl.program_id(1)
    pad = 8
    uf = u_ref[...].astype(F32)
    pad_ref[0:pad, :] = jnp.zeros((pad, uf.shape[1]), F32)
    pad_ref[pad + seq:pad + seq + pad, :] = jnp.zeros((pad, uf.shape[1]), F32)
    pad_ref[pad:pad + seq, :] = uf
    pos = lax.broadcasted_iota(jnp.int32, (seq, 1), 0)
    for gi, win in enumerate(B_WINDOWS):
        @pl.when(g == gi)
        def _(win=win):
            before, after = win // 2, win - win // 2
            tot = pad_ref[pad - before:pad - before + seq, :]
            for off in range(-before + 1, after):
                tot = tot + pad_ref[pad + off:pad + off + seq, :]
            cnt = (jnp.minimum(pos + after, seq) - jnp.maximum(pos - before, 0)).astype(F32)
            pooled = tot / cnt - uf
            mixed = jnp.dot(pooled.astype(BF16), w_ref[...].astype(BF16), preferred_element_type=F32)
            o_ref[...] = (mixed * sc_ref[...]).astype(o_ref.dtype)


def _pooling_mixer(pbdx, pool_w, pool_scale, batch, seq, col0):
    ng, cg = pool_w.shape[0], pool_w.shape[1]
    view = pbdx.reshape(batch, seq, pbdx.shape[1])
    cb = col0 // cg
    return pl.pallas_call(
        functools.partial(_pool_kernel, seq=seq),
        grid=(batch, ng),
        in_specs=[pl.BlockSpec((None, seq, cg), lambda b, g: (b, 0, cb + g)),
                  pl.BlockSpec((None, cg, cg), lambda b, g: (g, 0, 0)),
                  pl.BlockSpec((1, cg), lambda b, g: (0, g))],
        out_specs=pl.BlockSpec((None, seq, cg), lambda b, g: (b, 0, g)),
        out_shape=jax.ShapeDtypeStruct((batch, seq, ng * cg), BF16),
        scratch_shapes=[pltpu.VMEM((seq + 16, cg), F32)],
        compiler_params=_params("arbitrary", "arbitrary"),
        name="pooling",
    )(view, pool_w, pool_scale.reshape(1, ng * cg)).reshape(batch * seq, ng * cg)


def _conv_kernel(bg_ref, cg_ref, h_ref, w_ref, o_ref, pad_ref, *, seq):
    pad = 8
    u = cg_ref[...].astype(F32) * h_ref[...].astype(F32)
    cols = u.shape[1]
    pad_ref[0:pad, :] = jnp.zeros((pad, cols), F32)
    pad_ref[pad + seq:pad + seq + pad, :] = jnp.zeros((pad, cols), F32)
    pad_ref[pad:pad + seq, :] = u
    w = w_ref[...]
    conv = (pad_ref[pad - 1:pad - 1 + seq, :] * w[0:1, :] + u * w[1:2, :]
            + pad_ref[pad + 1:pad + 1 + seq, :] * w[2:3, :])
    o_ref[...] = (bg_ref[...].astype(F32) * conv).astype(o_ref.dtype)


def _gated_short_conv(pbdx, conv_w, batch, seq, col0):
    width = conv_w.shape[1]
    cb = 256
    nb = width // cb
    view = pbdx.reshape(batch, seq, pbdx.shape[1])
    base = col0 // cb

    def spec(part):
        return pl.BlockSpec((None, seq, cb), lambda b, c: (b, 0, base + part * nb + c))

    return pl.pallas_call(
        functools.partial(_conv_kernel, seq=seq),
        grid=(batch, nb),
        in_specs=[spec(0), spec(1), spec(2), pl.BlockSpec((3, cb), lambda b, c: (0, c))],
        out_specs=pl.BlockSpec((None, seq, cb), lambda b, c: (b, 0, c)),
        out_shape=jax.ShapeDtypeStruct((batch, seq, width), BF16),
        scratch_shapes=[pltpu.VMEM((seq + 16, cb), F32)],
        compiler_params=_params("arbitrary", "arbitrary"),
        name="short_conv",
    )(view, view, view, conv_w).reshape(batch * seq, width)


def _diff_kernel(q_ref, k_ref, v_ref, lam_ref, nrm_ref, o_ref, vone_ref, *, lam_init, chunk):
    t = pl.program_id(2)
    seq = k_ref.shape[0]
    tq = q_ref.shape[0]

    @pl.when(t == 0)
    def _():
        vone_ref[:, :LANES] = v_ref[...]
        vone_ref[:, LANES:] = jnp.ones((seq, LANES), BF16)

    lq = lam_ref[...].astype(F32)
    lam = (jnp.exp(jnp.sum(lq[0:1] * lq[1:2], axis=-1, keepdims=True))
           - jnp.exp(jnp.sum(lq[2:3] * lq[3:4], axis=-1, keepdims=True)) + lam_init)
    q = q_ref[...]
    lane = lax.broadcasted_iota(jnp.int32, q.shape, 1)
    zero = jnp.zeros_like(q)
    q2 = jnp.concatenate([jnp.where(lane < C_QK_DIM, q, zero), jnp.where(lane >= C_QK_DIM, q, zero)], axis=0)
    m = acc = None
    for c in range(seq // chunk):
        ks = slice(c * chunk, (c + 1) * chunk)
        s = lax.dot_general(q2, k_ref[ks, :], (((1,), (1,)), ((), ())), preferred_element_type=F32)
        mc = jnp.max(s, axis=-1, keepdims=True)
        m_new = mc if m is None else jnp.maximum(m, mc)
        e = jnp.exp2((s - m_new).astype(BF16))
        pv = jnp.dot(e, vone_ref[ks, :], preferred_element_type=F32)
        acc = pv if acc is None else acc * jnp.exp2(m - m_new) + pv
        m = m_new
    out = acc[:, :LANES] / acc[:, LANES:]
    o = out[:tq] - lam * out[tq:]
    o = o * lax.rsqrt(jnp.mean(o * o, axis=-1, keepdims=True) + LN_EPS) * nrm_ref[...] * (1.0 - lam_init)
    o_ref[...] = o.astype(o_ref.dtype)


def _diff_attention(pc, c_lambda, c_norm, batch, seq, layer, tq=256, chunk=1024):
    heads = pc.shape[1] // (3 * LANES)
    tq = min(tq, seq)
    chunk = min(chunk, seq)
    lam_init = 0.8 - 0.6 * math.exp(-0.3 * layer)
    view = pc.reshape(batch, seq, pc.shape[1])
    return pl.pallas_call(
        functools.partial(_diff_kernel, lam_init=lam_init, chunk=chunk),
        grid=(batch, heads, seq // tq),
        in_specs=[pl.BlockSpec((None, tq, LANES), lambda b, h, t: (b, t, 3 * h)),
                  pl.BlockSpec((None, seq, LANES), lambda b, h, t: (b, 0, 3 * h + 1)),
                  pl.BlockSpec((None, seq, LANES), lambda b, h, t: (b, 0, 3 * h + 2)),
                  pl.BlockSpec((4, C_QK_DIM), lambda b, h, t: (0, 0)),
                  pl.BlockSpec((1, LANES), lambda b, h, t: (0, 0))],
        out_specs=pl.BlockSpec((None, tq, LANES), lambda b, h, t: (b, t, h)),
        out_shape=jax.ShapeDtypeStruct((batch, seq, heads * LANES), BF16),
        scratch_shapes=[pltpu.VMEM((seq, 2 * LANES), BF16)],
        compiler_params=_params("arbitrary", "arbitrary", "arbitrary"),
        name="diff_attention",
    )(view, view, view, c_lambda, c_norm.reshape(1, LANES)).reshape(batch * seq, heads * LANES)


def _cross_kernel(q_ref, kv_ref, o_ref, *, heads):
    scale = HEAD_DIM ** -0.5
    width = heads * HEAD_DIM
    for h in range(heads):
        cs = slice(h * HEAD_DIM, (h + 1) * HEAD_DIM)
        k = kv_ref[:, h * HEAD_DIM:(h + 1) * HEAD_DIM]
        v = kv_ref[:, width + h * HEAD_DIM:width + (h + 1) * HEAD_DIM]
        s = lax.dot_general(q_ref[:, cs], k, (((1,), (1,)), ((), ())), preferred_element_type=F32) * scale
        m = jnp.max(s, axis=-1, keepdims=True)
        e = jnp.exp(s - m)
        den = jnp.sum(e, axis=-1, keepdims=True)
        o = jnp.dot(e.astype(BF16), v, preferred_element_type=F32) / den
        o_ref[:, cs] = o.astype(o_ref.dtype)


def _cross_attention(pbdx, kv, batch, seq, col0, width, tq=512):
    mem_len = kv.shape[0] // batch
    tq = min(tq, seq)
    view = pbdx.reshape(batch, seq, pbdx.shape[1])
    return pl.pallas_call(
        functools.partial(_cross_kernel, heads=width // HEAD_DIM),
        grid=(batch, seq // tq),
        in_specs=[pl.BlockSpec((None, tq, width), lambda b, t: (b, t, col0 // width)),
                  pl.BlockSpec((None, mem_len, 2 * width), lambda b, t: (b, 0, 0))],
        out_specs=pl.BlockSpec((None, tq, width), lambda b, t: (b, t, 0)),
        out_shape=jax.ShapeDtypeStruct((batch, seq, width), BF16),
        compiler_params=_params("arbitrary", "arbitrary"),
        name="cross_attention",
    )(view, kv.reshape(batch, mem_len, 2 * width)).reshape(batch * seq, width)


def _merge_kernel(x_ref, ya_ref, yb_ref, yc_ref, yd_ref, yx_ref,
                  g0, g1, g2, g3, g4, p0, p1, p2, p3, p4, b0, b1, b2, b3, b4, o_ref):
    x = x_ref[...]
    merged = None
    for y_ref, g_ref, p_ref, b_ref in ((ya_ref, g0, p0, b0), (yb_ref, g1, p1, b1), (yc_ref, g2, p2, b2),
                                       (yd_ref, g3, p3, b3), (yx_ref, g4, p4, b4)):
        gate = jax.nn.sigmoid(jnp.dot(x, g_ref[...], preferred_element_type=F32) + b_ref[...])
        term = gate * jnp.dot(y_ref[...], p_ref[...], preferred_element_type=F32)
        merged = term if merged is None else merged + term
    o_ref[...] = merged.astype(o_ref.dtype)


def _gated_merge(xb, ys, w_gate, w_branch, w_branch_x, b_gate, tm=512, tn=512):
    m, d = xb.shape
    tm = min(tm, m)
    nb = d // tn
    n_br = 5
    act = [pl.BlockSpec((tm, d), lambda j, i: (i, 0))]
    act += [pl.BlockSpec((tm, y.shape[1]), lambda j, i: (i, 0)) for y in ys]
    gates = [pl.BlockSpec((d, tn), functools.partial(lambda j, i, br: (0, br * nb + j), br=br)) for br in range(n_br)]
    projs = [pl.BlockSpec((None, w_branch.shape[1], tn), functools.partial(lambda j, i, br: (br, 0, j), br=br))
             for br in range(4)]
    projs.append(pl.BlockSpec((w_branch_x.shape[0], tn), lambda j, i: (0, j)))
    biases = [pl.BlockSpec((1, tn), functools.partial(lambda j, i, br: (0, br * nb + j), br=br)) for br in range(n_br)]
    bg2 = b_gate.reshape(1, n_br * d)
    return pl.pallas_call(
        _merge_kernel,
        grid=(nb, m // tm),
        in_specs=act + gates + projs + biases,
        out_specs=pl.BlockSpec((tm, tn), lambda j, i: (i, j)),
        out_shape=jax.ShapeDtypeStruct((m, d), BF16),
        compiler_params=_params("arbitrary", "arbitrary"),
        name="gated_merge",
    )(xb, *ys, *([w_gate] * n_br), *([w_branch] * 4), w_branch_x, *([bg2] * n_br))


def _layer_norm_rows(h, g, b):
    mu = jnp.mean(h, axis=-1, keepdims=True)
    hc = h - mu
    var = jnp.mean(hc * hc, axis=-1, keepdims=True)
    return hc * lax.rsqrt(var + LN_EPS) * g + b


def _out_ln_kernel(mg_ref, w_ref, x_ref, g_ref, b_ref, xo_ref, xb_ref, *, alpha):
    y = jnp.dot(mg_ref[...], w_ref[...], preferred_element_type=F32)
    xn = _layer_norm_rows(alpha * x_ref[...] + y, g_ref[...], b_ref[...])
    xo_ref[...] = xn
    xb_ref[...] = xn.astype(BF16)


def _out_proj_ln(merged, w_out_b, x, g, b, alpha, tm=256):
    m, d = x.shape
    tm = min(tm, m)
    row = pl.BlockSpec((tm, d), lambda i: (i, 0))
    vec = pl.BlockSpec((1, d), lambda i: (0, 0))
    return pl.pallas_call(
        functools.partial(_out_ln_kernel, alpha=alpha),
        grid=(m // tm,),
        in_specs=[row, pl.BlockSpec((d, d), lambda i: (0, 0)), row, vec, vec],
        out_specs=[row, row],
        out_shape=[jax.ShapeDtypeStruct((m, d), F32), jax.ShapeDtypeStruct((m, d), BF16)],
        compiler_params=_params("arbitrary"),
        name="out_proj_ln",
    )(merged, w_out_b, x, g.reshape(1, d), b.reshape(1, d))


def _split_bf16(a):
    hi = a.astype(BF16)
    lo = (a - hi.astype(F32)).astype(BF16)
    return hi, lo


def _router_kernel(x_ref, w_ref, b_ref, idx_ref, wt_ref, cnt_ref):
    @pl.when(pl.program_id(0) == 0)
    def _():
        cnt_ref[...] = jnp.zeros_like(cnt_ref)

    x = x_ref[...]
    w = w_ref[...]
    xh, xl = _split_bf16(x)
    wh, wl = _split_bf16(w)
    logits = (jnp.dot(xh, wh, preferred_element_type=F32) + jnp.dot(xl, wh, preferred_element_type=F32)
              + jnp.dot(xh, wl, preferred_element_type=F32))
    lt = logits.T[:N_EXPERTS, :]
    tm = lt.shape[1]
    scores = jax.nn.sigmoid(lt)
    biased = scores + b_ref[...]
    gsz = N_EXPERTS // N_GROUPS
    sub8 = lax.broadcasted_iota(jnp.int32, (gsz, tm), 0).astype(F32)
    grp_rows = []
    for g in range(N_GROUPS):
        blk = biased[g * gsz:(g + 1) * gsz, :]
        m1 = jnp.max(blk, axis=0, keepdims=True)
        i1 = jnp.min(jnp.where(blk == m1, sub8, float(gsz)), axis=0, keepdims=True)
        m2 = jnp.max(jnp.where(sub8 == i1, -jnp.inf, blk), axis=0, keepdims=True)
        grp_rows.append(m1 + m2)
    grp = jnp.concatenate(grp_rows, axis=0)
    subg = lax.broadcasted_iota(jnp.int32, (N_GROUPS, tm), 0).astype(F32)
    gsel = jnp.zeros((N_GROUPS, tm), F32)
    for _ in range(TOP_GROUPS):
        mg = jnp.max(grp, axis=0, keepdims=True)
        ig = jnp.min(jnp.where(grp == mg, subg, float(N_GROUPS)), axis=0, keepdims=True)
        hit = subg == ig
        gsel = jnp.where(hit, 1.0, gsel)
        grp = jnp.where(hit, -jnp.inf, grp)
    emask = jnp.concatenate([jnp.broadcast_to(gsel[g:g + 1, :], (gsz, tm)) for g in range(N_GROUPS)], axis=0)
    cand = jnp.where(emask > 0.5, biased, -jnp.inf)
    sube = lax.broadcasted_iota(jnp.int32, (N_EXPERTS, tm), 0).astype(F32)
    idx_rows, w_rows = [], []
    chosen = jnp.zeros((N_EXPERTS, tm), F32)
    for _ in range(TOP_K):
        mc = jnp.max(cand, axis=0, keepdims=True)
        ic = jnp.min(jnp.where(cand == mc, sube, float(N_EXPERTS)), axis=0, keepdims=True)
        hit = sube == ic
        idx_rows.append(ic)
        w_rows.append(jnp.sum(jnp.where(hit, scores, 0.0), axis=0, keepdims=True))
        chosen = jnp.where(hit, 1.0, chosen)
        cand = jnp.where(hit, -jnp.inf, cand)
    cnt_ref[...] += jnp.sum(chosen, axis=1, keepdims=True)
    wsel = jnp.concatenate(w_rows, axis=0)
    wsel = wsel / jnp.sum(wsel, axis=0, keepdims=True) * ROUTE_SCALE
    idx_ref[...] = jnp.concatenate(idx_rows, axis=0).astype(jnp.int32)
    wt_ref[...] = wsel


def _router(x, router_w, router_b, tm=512):
    m, d = x.shape
    tm = min(tm, m)
    w_pad = jnp.pad(router_w, ((0, 0), (0, LANES - N_EXPERTS)))
    return pl.pallas_call(
        _router_kernel,
        grid=(m // tm,),
        in_specs=[pl.BlockSpec((tm, d), lambda i: (i, 0)),
                  pl.BlockSpec((d, LANES), lambda i: (0, 0)),
                  pl.BlockSpec((N_EXPERTS, 1), lambda i: (0, 0))],
        out_specs=[pl.BlockSpec((TOP_K, tm), lambda i: (0, i)), pl.BlockSpec((TOP_K, tm), lambda i: (0, i)),
                   pl.BlockSpec((N_EXPERTS, LANES), lambda i: (0, 0))],
        out_shape=[jax.ShapeDtypeStruct((TOP_K, m), jnp.int32), jax.ShapeDtypeStruct((TOP_K, m), F32),
                   jax.ShapeDtypeStruct((N_EXPERTS, LANES), F32)],
        compiler_params=_params("arbitrary"),
        name="router",
    )(x, w_pad, router_b.reshape(N_EXPERTS, 1))


def _slot_kernel(idx_ref, base_ref, dest_ref, tri_ref, run_ref):
    tm = idx_ref.shape[1]

    @pl.when(pl.program_id(0) == 0)
    def _():
        row = lax.broadcasted_iota(jnp.int32, (tm, tm), 0)
        col = lax.broadcasted_iota(jnp.int32, (tm, tm), 1)
        tri_ref[...] = jnp.where(row < col, 1.0, 0.0).astype(BF16)
        run_ref[...] = base_ref[...]

    sube = lax.broadcasted_iota(jnp.int32, (N_EXPERTS, tm), 0)
    idx = idx_ref[...]
    base = run_ref[...]
    rows = []
    for k in range(TOP_K):
        hit = sube == idx[k:k + 1, :]
        before = jnp.dot(jnp.where(hit, 1.0, 0.0).astype(BF16), tri_ref[...], preferred_element_type=F32)
        rows.append(jnp.sum(jnp.where(hit, base + before, 0.0), axis=0, keepdims=True))
        base = base + jnp.sum(jnp.where(hit, 1.0, 0.0), axis=1, keepdims=True)
    run_ref[...] = base
    dest_ref[...] = jnp.concatenate(rows, axis=0).astype(jnp.int32)


def _assign_slots(idx_t, base, tm=512):
    kk, m = idx_t.shape
    tm = min(tm, m)
    return pl.pallas_call(
        _slot_kernel,
        grid=(m // tm,),
        in_specs=[pl.BlockSpec((kk, tm), lambda i: (0, i)), pl.BlockSpec((N_EXPERTS, 1), lambda i: (0, 0))],
        out_specs=pl.BlockSpec((kk, tm), lambda i: (0, i)),
        out_shape=jax.ShapeDtypeStruct((kk, m), jnp.int32),
        scratch_shapes=[pltpu.VMEM((tm, tm), BF16), pltpu.VMEM((N_EXPERTS, 1), F32)],
        compiler_params=_params("arbitrary"),
        name="assign_slots",
    )(idx_t, base.astype(F32).reshape(N_EXPERTS, 1))


def _expert_kernel(be_ref, nb_ref, x_ref, wg_ref, wu_ref, wd_ref, o_ref, wgb, wub, wdb):
    i = pl.program_id(0)
    prev = be_ref[jnp.maximum(i - 1, 0)]
    fresh = (i == 0) | (be_ref[i] != prev)

    @pl.when(fresh)
    def _():
        wgb[...] = wg_ref[...].astype(BF16)
        wub[...] = wu_ref[...].astype(BF16)
        wdb[...] = wd_ref[...].astype(BF16)

    @pl.when(i < nb_ref[0])
    def _():
        x = x_ref[...]
        gate = jnp.dot(x, wgb[...], preferred_element_type=F32)
        up = jnp.dot(x, wub[...], preferred_element_type=F32)
        act = (jax.nn.silu(gate) * up).astype(BF16)
        o_ref[...] = jnp.dot(act, wdb[...], preferred_element_type=F32).astype(o_ref.dtype)

    @pl.when(i >= nb_ref[0])
    def _():
        o_ref[...] = jnp.zeros_like(o_ref)


def _expert_ffn(xs, block_e, nblocks, w_gate, w_up, w_down, layer, rows):
    ns, d = xs.shape
    de = w_gate.shape[3]
    grid_spec = pltpu.PrefetchScalarGridSpec(
        num_scalar_prefetch=2,
        grid=(ns // rows,),
        in_specs=[pl.BlockSpec((rows, d), lambda i, be, nb: (i, 0)),
                  pl.BlockSpec((None, None, d, de), lambda i, be, nb: (layer, be[i], 0, 0)),
                  pl.BlockSpec((None, None, d, de), lambda i, be, nb: (layer, be[i], 0, 0)),
                  pl.BlockSpec((None, None, de, d), lambda i, be, nb: (layer, be[i], 0, 0))],
        out_specs=pl.BlockSpec((rows, d), lambda i, be, nb: (i, 0)),
        scratch_shapes=[pltpu.VMEM((d, de), BF16), pltpu.VMEM((d, de), BF16), pltpu.VMEM((de, d), BF16)],
    )
    return pl.pallas_call(
        _expert_kernel,
        grid_spec=grid_spec,
        out_shape=jax.ShapeDtypeStruct((ns, d), BF16),
        compiler_params=_params("arbitrary"),
        name="expert_ffn",
    )(block_e, nblocks, xs, w_gate, w_up, w_down)


def _combine_ln_kernel(*refs, alpha):
    yg_refs = refs[:TOP_K]
    wt_ref, sh_ref, x_ref, g_ref, b_ref, xo_ref, xb_ref = refs[TOP_K:]
    wt = wt_ref[...]
    routed = sh_ref[...].astype(F32)
    for k in range(TOP_K):
        routed = routed + wt[:, k:k + 1] * yg_refs[k][...].astype(F32)
    xn = _layer_norm_rows(alpha * x_ref[...] + routed, g_ref[...], b_ref[...])
    xo_ref[...] = xn
    xb_ref[...] = xn.astype(BF16)


def _combine_ln(yg, wt, shared, x, g, b, alpha, tm=256):
    m, d = x.shape
    tm = min(tm, m)
    nb = m // tm
    row = pl.BlockSpec((tm, d), lambda i: (i, 0))
    vec = pl.BlockSpec((1, d), lambda i: (0, 0))
    yg_specs = [pl.BlockSpec((tm, d), functools.partial(lambda i, k: (k * nb + i, 0), k=k)) for k in range(TOP_K)]
    return pl.pallas_call(
        functools.partial(_combine_ln_kernel, alpha=alpha),
        grid=(nb,),
        in_specs=yg_specs + [pl.BlockSpec((tm, TOP_K), lambda i: (i, 0)), row, row, vec, vec],
        out_specs=[row, row],
        out_shape=[jax.ShapeDtypeStruct((m, d), F32), jax.ShapeDtypeStruct((m, d), BF16)],
        compiler_params=_params("arbitrary"),
        name="combine_ln",
    )(*([yg] * TOP_K), wt, shared, x, g.reshape(1, d), b.reshape(1, d))


def _moe_layer(x1, xb1, layer, router_w, router_b, e_gate, e_up, e_down, s_gate, s_up, s_down, g, b, alpha):
    t, d = x1.shape
    rows = min(MOE_ROWS, t)
    idx_t, wt_t, cnt = _router(x1, router_w, router_b)
    n_assign = t * TOP_K
    i32 = jnp.int32
    counts = cnt[:, 0].astype(i32)
    padded = (counts + rows - 1) // rows * rows
    ends_p = jnp.cumsum(padded)
    pad_e = padded - counts
    dest = _assign_slots(idx_t, ends_p - padded).reshape(n_assign)
    nblk = n_assign // rows + N_EXPERTS
    n_pad = nblk * rows - n_assign
    pad_i = jnp.arange(n_pad, dtype=i32)
    pad_before = jnp.cumsum(pad_e) - pad_e
    pad_slot = pad_i + jnp.sum(jnp.where(pad_before[None, :] <= pad_i[:, None], counts[None, :], 0), axis=1)
    keys = jnp.concatenate([dest, pad_slot])
    toks = jnp.concatenate([jnp.arange(n_assign, dtype=i32) % t, pad_i % t])
    _, slot_tok = lax.sort_key_val(keys, toks)
    blk0 = jnp.arange(nblk, dtype=i32) * rows
    block_e = jnp.minimum(jnp.sum((ends_p[None, :] <= blk0[:, None]).astype(i32), axis=1), N_EXPERTS - 1)
    nused = (ends_p[-1] // rows).astype(i32).reshape(1)
    xs = jnp.take(xb1, slot_tok, axis=0, mode='clip')
    ys = _expert_ffn(xs, block_e, nused, e_gate, e_up, e_down, layer, rows)
    yg = jnp.take(ys, dest, axis=0, mode='clip')
    shared = _expert_ffn(xb1, jnp.zeros((t // rows,), i32), jnp.full((1,), t // rows, i32),
                         s_gate[:, None], s_up[:, None], s_down[:, None], layer, rows)
    return _combine_ln(yg, wt_t.T, shared, x1, g, b, alpha)


def _mixer_layer(x, xb, memb, layer, tabs_a, tabs_c, batch, seq, w_in, pool_w, pool_scale, c_lambda, c_norm,
                 conv_w, w_mem_kv, w_branch, w_branch_x, w_gate, b_gate, w_out, g, b, alpha):
    d = x.shape[1]
    mix = w_branch.shape[2]
    xw = w_branch_x.shape[1]
    a_cols, c_cols, d_cols = 3 * mix, 3 * mix, 3 * mix
    c0_b, c0_c, c0_d = a_cols, a_cols + mix, a_cols + mix + c_cols
    c0_x = c0_d + d_cols
    pa = _proj(xb, w_in, layer, range(3), mix, tabs_a, seq, half=HEAD_DIM // 8,
               rope_blocks=(1,) * (mix // LANES), rope_jmax=2)
    tn_c = 6 * LANES
    w_c = w_in[layer:layer + 1, :, c0_c:c0_c + c_cols]
    pc = _proj(xb, w_c, 0, range(c_cols // tn_c), tn_c, tabs_c, seq,
               half=C_QK_DIM // 8, rope_blocks=(1, 2, 0) * 2)
    tn_p = 512
    blocks = list(range(c0_b // tn_p, c0_b // tn_p + mix // tn_p)) + \
        list(range(c0_d // tn_p, c0_d // tn_p + (d_cols + xw) // tn_p))
    pbdx = _proj(xb, w_in, layer, blocks, tn_p, tabs_a, seq)
    ya = _dilated_attention(pa, batch, seq)
    yb = _pooling_mixer(pbdx, pool_w[layer], pool_scale[layer], batch, seq, 0)
    yc = _diff_attention(pc, c_lambda[layer], c_norm[layer], batch, seq, layer)
    yd = _gated_short_conv(pbdx, conv_w[layer], batch, seq, mix)
    kv = _proj(memb, w_mem_kv, layer, range(2 * xw // tn_p), tn_p, tabs_a, seq)
    yx = _cross_attention(pbdx, kv, batch, seq, mix + d_cols, xw)
    merged = _gated_merge(xb, (ya, yb, yc, yd, yx), w_gate[layer].astype(BF16), w_branch[layer].astype(BF16),
                          w_branch_x[layer].astype(BF16), b_gate[layer])
    return _out_proj_ln(merged, w_out[layer].astype(BF16), x, g[layer], b[layer], alpha)


def kernel(x, mem, w_in, pool_w, pool_scale, c_lambda, c_norm, conv_w, w_mem_kv, w_branch, w_branch_x,
           w_gate, b_gate, w_out, ln1_g, ln1_b, router_w, router_b, e_gate, e_up, e_down, s_gate, s_up,
           s_down, ln2_g, ln2_b):
    batch, seq, d = x.shape
    depth = w_in.shape[0]
    alpha = (2.0 * depth) ** 0.25
    tabs_a = _rope_tables(seq, HEAD_DIM // 4, LANES)
    tabs_ck = _rope_tables(seq, C_QK_DIM // 4, C_QK_DIM)
    q_scale = C_QK_DIM ** -0.5 * math.log2(math.e)
    tabs_c = tuple(tab * q_scale for tab in tabs_ck) + tabs_ck
    xf = x.reshape(batch * seq, d)
    xb = xf.astype(BF16)
    memb = mem.reshape(-1, d).astype(BF16)
    for l in range(depth):
        xf, xb = _mixer_layer(xf, xb, memb, l, tabs_a, tabs_c, batch, seq, w_in, pool_w, pool_scale,
                              c_lambda, c_norm, conv_w, w_mem_kv, w_branch, w_branch_x,
                              w_gate, b_gate, w_out, ln1_g, ln1_b, alpha)
        xf, xb = _moe_layer(xf, xb, l, router_w[l], router_b[l], e_gate, e_up, e_down, s_gate,
                            s_up, s_down, ln2_g[l], ln2_b[l], alpha)
    return xf.reshape(batch, seq, d)
```

```python
import functools
import math

import jax
import jax.numpy as jnp
from jax import lax
from jax.experimental import pallas as pl
from jax.experimental.pallas import tpu as pltpu

F32 = jnp.float32
BF16 = jnp.bfloat16

LANES = 128
MXU_WIDTH = 256
HEAD_DIM = 128
ROPE_THETA = 500000.0
A_PATTERNS = ((128, 1), (512, 4), (2048, 16))
A_HALF = 64
B_WINDOWS = (2, 4, 8, 16)
C_QK_DIM = 64
N_EXPERTS = 64
N_GROUPS = 8
TOP_GROUPS = 4
TOP_K = 8
ROUTE_SCALE = 2.5
LN_EPS = 1e-5
NEG_INF = -1e30
VMEM_LIMIT_BYTES = 56 * 1024 * 1024
MOE_ROWS = 512


def _params(*sem):
    return pltpu.CompilerParams(dimension_semantics=sem, vmem_limit_bytes=VMEM_LIMIT_BYTES)


def _rope_tables(seq, rot_dim, period):
    half = rot_dim // 2
    inv = 1.0 / (ROPE_THETA ** (jnp.arange(0, rot_dim, 2, dtype=F32) / rot_dim))
    ang = jnp.arange(seq, dtype=F32)[:, None] * inv[None, :]
    cos, sin = jnp.cos(ang), jnp.sin(ang)
    lane = jnp.arange(LANES) % period
    first = lane < half
    second = (lane >= half) & (lane < 2 * half)
    idx = jnp.where(first, lane, jnp.where(second, lane - half, 0))
    cos_l, sin_l = cos[:, idx], sin[:, idx]
    ct = jnp.where(first | second, cos_l, 1.0)
    s1 = jnp.where(first, -sin_l, 0.0)
    s2 = jnp.where(second, sin_l, 0.0)
    return ct.astype(F32), s1.astype(F32), s2.astype(F32)


def _proj_kernel(x_ref, w_ref, *rest, half, rope_blocks, rope_jmax):
    tab_refs, o_ref, wb_ref = rest[:-2], rest[-2], rest[-1]
    j = pl.program_id(0)
    i = pl.program_id(1)

    @pl.when(i == 0)
    def _():
        wb_ref[...] = w_ref[...].astype(BF16)

    def plain():
        o_ref[...] = jnp.dot(x_ref[...], wb_ref[...], preferred_element_type=F32).astype(o_ref.dtype)

    def roped():
        tabs = [r[...] for r in tab_refs]
        per = MXU_WIDTH // LANES
        for c in range(len(rope_blocks) // per):
            acc = jnp.dot(x_ref[...], wb_ref[:, c * MXU_WIDTH:(c + 1) * MXU_WIDTH], preferred_element_type=F32)
            for b in range(c * per, (c + 1) * per):
                use = rope_blocks[b]
                blk = acc[:, (b - c * per) * LANES:(b - c * per + 1) * LANES]
                if use:
                    ct, s1, s2 = tabs[3 * (use - 1):3 * use]
                    blk = (blk * ct + pltpu.roll(blk, LANES - half, 1) * s1
                           + pltpu.roll(blk, half, 1) * s2)
                o_ref[:, b * LANES:(b + 1) * LANES] = blk.astype(o_ref.dtype)

    if not any(rope_blocks):
        plain()
    elif rope_jmax is None:
        roped()
    else:
        pl.when(j < rope_jmax)(roped)
        pl.when(j >= rope_jmax)(plain)


def _proj(xb, w, layer, col_blocks, tn, tables, seq, *, half=0, rope_blocks=None, rope_jmax=None, tm=512):
    m, k = xb.shape
    nj = len(col_blocks)
    tm = min(tm, m)
    rope_blocks = tuple(int(u) for u in rope_blocks) if rope_blocks is not None else (0,) * (tn // LANES)
    col_blocks = tuple(col_blocks)
    first, contiguous = col_blocks[0], all(col_blocks[a] == col_blocks[0] + a for a in range(nj))
    if contiguous:
        w_map = lambda j, i: (layer, 0, first + j)
    else:
        n0 = next(a for a in range(1, nj) if col_blocks[a] != col_blocks[0] + a)
        second = col_blocks[n0]
        w_map = lambda j, i: (layer, 0, jnp.where(j < n0, first + j, second + j - n0))
    sblocks = seq // tm if seq >= tm else 1
    tab_spec = pl.BlockSpec((tm, LANES), lambda j, i: (i % sblocks, 0))
    kern = functools.partial(_proj_kernel, half=half, rope_blocks=rope_blocks, rope_jmax=rope_jmax)
    return pl.pallas_call(
        kern,
        grid=(nj, m // tm),
        in_specs=[pl.BlockSpec((tm, k), lambda j, i: (i, 0)),
                  pl.BlockSpec((None, k, tn), w_map)] + [tab_spec] * len(tables),
        out_specs=pl.BlockSpec((tm, tn), lambda j, i: (i, j)),
        out_shape=jax.ShapeDtypeStruct((m, nj * tn), BF16),
        scratch_shapes=[pltpu.VMEM((k, tn), BF16)],
        compiler_params=_params("arbitrary", "arbitrary"),
        name="proj",
    )(xb, w, *tables)


def _dilated_kernel(q_ref, kp_ref, km_ref, kn_ref, vp_ref, vm_ref, vn_ref, o_ref, lse_ref, *, tl, seq_len, heads):
    t = pl.program_id(2)
    l0 = t * tl
    sub = 128
    nsub = tl // sub
    scale = HEAD_DIM ** -0.5
    iq = lax.broadcasted_iota(jnp.int32, (sub, sub + 2 * A_HALF), 0)
    ik = lax.broadcasted_iota(jnp.int32, (sub, sub + 2 * A_HALF), 1)
    band = jnp.abs(ik - A_HALF - iq) <= A_HALF
    lane = lax.broadcasted_iota(jnp.int32, (sub, LANES), 1)
    valid = []
    for a in range(nsub):
        kk = l0 + a * sub - A_HALF + ik
        valid.append(band & (kk >= 0) & (kk < seq_len))
    lse_rows = [jnp.zeros((sub, LANES), F32) for _ in range(nsub)]
    for h in range(heads):
        cs = slice(h * HEAD_DIM, (h + 1) * HEAD_DIM)
        kcat = jnp.concatenate([kp_ref[:, cs], km_ref[:, cs], kn_ref[:, cs]], axis=0)
        vcat = jnp.concatenate([vp_ref[:, cs], vm_ref[:, cs], vn_ref[:, cs]], axis=0)
        for a in range(nsub):
            q = q_ref[a * sub:(a + 1) * sub, cs]
            kw = kcat[a * sub:a * sub + sub + 2 * A_HALF]
            vw = vcat[a * sub:a * sub + sub + 2 * A_HALF]
            s = lax.dot_general(q, kw, (((1,), (1,)), ((), ())), preferred_element_type=F32) * scale
            s = jnp.where(valid[a], s, NEG_INF)
            m = jnp.max(s, axis=-1, keepdims=True)
            p = jnp.exp(s - m)
            den = jnp.sum(p, axis=-1, keepdims=True)
            o = jnp.dot(p.astype(BF16), vw, preferred_element_type=F32) / den
            o_ref[a * sub:(a + 1) * sub, cs] = o.astype(o_ref.dtype)
            lse_rows[a] = jnp.where(lane == h, m + jnp.log(den), lse_rows[a])
    for a in range(nsub):
        lse_ref[a * sub:(a + 1) * sub, :] = lse_rows[a]


def _dilated_pattern(pa, batch, seq, dil):
    width = pa.shape[1] // 3
    heads = width // HEAD_DIM
    sl = seq // dil
    tl = min(256, sl)
    nt = sl // tl
    hb = tl // A_HALF
    last_halo = sl // A_HALF - 1
    view = pa.reshape(batch, sl, dil * 3 * width)

    def main(c):
        return pl.BlockSpec((None, tl, width), lambda b, r, t: (b, t, r * 3 + c))

    def prev(c):
        return pl.BlockSpec((None, A_HALF, width), lambda b, r, t: (b, jnp.maximum(t * hb - 1, 0), r * 3 + c))

    def nxt(c):
        return pl.BlockSpec((None, A_HALF, width),
                            lambda b, r, t: (b, jnp.minimum((t + 1) * hb, last_halo), r * 3 + c))

    kern = functools.partial(_dilated_kernel, tl=tl, seq_len=sl, heads=heads)
    o, lse = pl.pallas_call(
        kern,
        grid=(batch, dil, nt),
        in_specs=[main(0), prev(1), main(1), nxt(1), prev(2), main(2), nxt(2)],
        out_specs=[pl.BlockSpec((None, tl, width), lambda b, r, t: (b, t, r)),
                   pl.BlockSpec((None, tl, LANES), lambda b, r, t: (b, t, r))],
        out_shape=[jax.ShapeDtypeStruct((batch, sl, dil * width), BF16),
                   jax.ShapeDtypeStruct((batch, sl, dil * LANES), F32)],
        compiler_params=_params("arbitrary", "arbitrary", "arbitrary"),
        name=f"dilated_d{dil}",
    )(view, view, view, view, view, view, view)
    return o.reshape(batch * seq, width), lse.reshape(batch * seq, LANES)


def _dilated_combine_kernel(o0_ref, o1_ref, o2_ref, l0_ref, l1_ref, l2_ref, y_ref, *, heads):
    l0, l1, l2 = l0_ref[...], l1_ref[...], l2_ref[...]
    m = jnp.maximum(jnp.maximum(l0, l1), l2)
    e0, e1, e2 = jnp.exp(l0 - m), jnp.exp(l1 - m), jnp.exp(l2 - m)
    inv = 1.0 / (e0 + e1 + e2)
    w0, w1, w2 = e0 * inv, e1 * inv, e2 * inv
    for h in range(heads):
        cs = slice(h * HEAD_DIM, (h + 1) * HEAD_DIM)
        y = (w0[:, h:h + 1] * o0_ref[:, cs].astype(F32) + w1[:, h:h + 1] * o1_ref[:, cs].astype(F32)
             + w2[:, h:h + 1] * o2_ref[:, cs].astype(F32))
        y_ref[:, cs] = y.astype(y_ref.dtype)


def _dilated_attention(pa, batch, seq):
    outs, lses = zip(*[_dilated_pattern(pa, batch, seq, d) for _, d in A_PATTERNS])
    m, width = outs[0].shape
    tm = min(512, m)
    ospec = pl.BlockSpec((tm, width), lambda i: (i, 0))
    lspec = pl.BlockSpec((tm, LANES), lambda i: (i, 0))
    return pl.pallas_call(
        functools.partial(_dilated_combine_kernel, heads=width // HEAD_DIM),
        grid=(m // tm,),
        in_specs=[ospec, ospec, ospec, lspec, lspec, lspec],
        out_specs=ospec,
        out_shape=jax.ShapeDtypeStruct((m, width), BF16),
        compiler_params=_params("arbitrary"),
        name="dilated_combine",
    )(*outs, *lses)


def _pool_kernel(u_ref, w_ref, sc_ref, o_ref, pad_ref, *, seq):
    g = pl.program_id(1)
    pad = 8
    uf = u_ref[...].astype(F32)
    pad_ref[0:pad, :] = jnp.zeros((pad, uf.shape[1]), F32)
    pad_ref[pad + seq:pad + seq + pad, :] = jnp.zeros((pad, uf.shape[1]), F32)
    pad_ref[pad:pad + seq, :] = uf
    pos = lax.broadcasted_iota(jnp.int32, (seq, 1), 0)
    for gi, win in enumerate(B_WINDOWS):
        @pl.when(g == gi)
        def _(win=win):
            before, after = win // 2, win - win // 2
            tot = pad_ref[pad - before:pad - before + seq, :]
            for off in range(-before + 1, after):
                tot = tot + pad_ref[pad + off:pad + off + seq, :]
            cnt = (jnp.minimum(pos + after, seq) - jnp.maximum(pos - before, 0)).astype(F32)
            pooled = tot / cnt - uf
            mixed = jnp.dot(pooled.astype(BF16), w_ref[...].astype(BF16), preferred_element_type=F32)
            o_ref[...] = (mixed * sc_ref[...]).astype(o_ref.dtype)


def _pooling_mixer(pbdx, pool_w, pool_scale, batch, seq, col0):
    ng, cg = pool_w.shape[0], pool_w.shape[1]
    view = pbdx.reshape(batch, seq, pbdx.shape[1])
    cb = col0 // cg
    return pl.pallas_call(
        functools.partial(_pool_kernel, seq=seq),
        grid=(batch, ng),
        in_specs=[pl.BlockSpec((None, seq, cg), lambda b, g: (b, 0, cb + g)),
                  pl.BlockSpec((None, cg, cg), lambda b, g: (g, 0, 0)),
                  pl.BlockSpec((1, cg), lambda b, g: (0, g))],
        out_specs=pl.BlockSpec((None, seq, cg), lambda b, g: (b, 0, g)),
        out_shape=jax.ShapeDtypeStruct((batch, seq, ng * cg), BF16),
        scratch_shapes=[pltpu.VMEM((seq + 16, cg), F32)],
        compiler_params=_params("arbitrary", "arbitrary"),
        name="pooling",
    )(view, pool_w, pool_scale.reshape(1, ng * cg)).reshape(batch * seq, ng * cg)


def _conv_kernel(bg_ref, cg_ref, h_ref, w_ref, o_ref, pad_ref, *, seq):
    pad = 8
    u = cg_ref[...].astype(F32) * h_ref[...].astype(F32)
    cols = u.shape[1]
    pad_ref[0:pad, :] = jnp.zeros((pad, cols), F32)
    pad_ref[pad + seq:pad + seq + pad, :] = jnp.zeros((pad, cols), F32)
    pad_ref[pad:pad + seq, :] = u
    w = w_ref[...]
    conv = (pad_ref[pad - 1:pad - 1 + seq, :] * w[0:1, :] + u * w[1:2, :]
            + pad_ref[pad + 1:pad + 1 + seq, :] * w[2:3, :])
    o_ref[...] = (bg_ref[...].astype(F32) * conv).astype(o_ref.dtype)


def _gated_short_conv(pbdx, conv_w, batch, seq, col0):
    width = conv_w.shape[1]
    cb = 256
    nb = width // cb
    view = pbdx.reshape(batch, seq, pbdx.shape[1])
    base = col0 // cb

    def spec(part):
        return pl.BlockSpec((None, seq, cb), lambda b, c: (b, 0, base + part * nb + c))

    return pl.pallas_call(
        functools.partial(_conv_kernel, seq=seq),
        grid=(batch, nb),
        in_specs=[spec(0), spec(1), spec(2), pl.BlockSpec((3, cb), lambda b, c: (0, c))],
        out_specs=pl.BlockSpec((None, seq, cb), lambda b, c: (b, 0, c)),
        out_shape=jax.ShapeDtypeStruct((batch, seq, width), BF16),
        scratch_shapes=[pltpu.VMEM((seq + 16, cb), F32)],
        compiler_params=_params("arbitrary", "arbitrary"),
        name="short_conv",
    )(view, view, view, conv_w).reshape(batch * seq, width)


def _diff_kernel(q_ref, k_ref, v_ref, lam_ref, nrm_ref, o_ref, vone_ref, *, lam_init, chunk):
    t = pl.program_id(2)
    seq = k_ref.shape[0]
    tq = q_ref.shape[0]

    @pl.when(t == 0)
    def _():
        vone_ref[:, :LANES] = v_ref[...]
        vone_ref[:, LANES:] = jnp.ones((seq, LANES), BF16)

    lq = lam_ref[...].astype(F32)
    lam = (jnp.exp(jnp.sum(lq[0:1] * lq[1:2], axis=-1, keepdims=True))
           - jnp.exp(jnp.sum(lq[2:3] * lq[3:4], axis=-1, keepdims=True)) + lam_init)
    q = q_ref[...]
    lane = lax.broadcasted_iota(jnp.int32, q.shape, 1)
    zero = jnp.zeros_like(q)
    q2 = jnp.concatenate([jnp.where(lane < C_QK_DIM, q, zero), jnp.where(lane >= C_QK_DIM, q, zero)], axis=0)
    m = acc = None
    for c in range(seq // chunk):
        ks = slice(c * chunk, (c + 1) * chunk)
        s = lax.dot_general(q2, k_ref[ks, :], (((1,), (1,)), ((), ())), preferred_element_type=F32)
        mc = jnp.max(s, axis=-1, keepdims=True)
        m_new = mc if m is None else jnp.maximum(m, mc)
        e = jnp.exp2((s - m_new).astype(BF16))
        pv = jnp.dot(e, vone_ref[ks, :], preferred_element_type=F32)
        acc = pv if acc is None else acc * jnp.exp2(m - m_new) + pv
        m = m_new
    out = acc[:, :LANES] / acc[:, LANES:]
    o = out[:tq] - lam * out[tq:]
    o = o * lax.rsqrt(jnp.mean(o * o, axis=-1, keepdims=True) + LN_EPS) * nrm_ref[...] * (1.0 - lam_init)
    o_ref[...] = o.astype(o_ref.dtype)


def _diff_attention(pc, c_lambda, c_norm, batch, seq, layer, tq=256, chunk=1024):
    heads = pc.shape[1] // (3 * LANES)
    tq = min(tq, seq)
    chunk = min(chunk, seq)
    lam_init = 0.8 - 0.6 * math.exp(-0.3 * layer)
    view = pc.reshape(batch, seq, pc.shape[1])
    return pl.pallas_call(
        functools.partial(_diff_kernel, lam_init=lam_init, chunk=chunk),
        grid=(batch, heads, seq // tq),
        in_specs=[pl.BlockSpec((None, tq, LANES), lambda b, h, t: (b, t, 3 * h)),
                  pl.BlockSpec((None, seq, LANES), lambda b, h, t: (b, 0, 3 * h + 1)),
                  pl.BlockSpec((None, seq, LANES), lambda b, h, t: (b, 0, 3 * h + 2)),
                  pl.BlockSpec((4, C_QK_DIM), lambda b, h, t: (0, 0)),
                  pl.BlockSpec((1, LANES), lambda b, h, t: (0, 0))],
        out_specs=pl.BlockSpec((None, tq, LANES), lambda b, h, t: (b, t, h)),
        out_shape=jax.ShapeDtypeStruct((batch, seq, heads * LANES), BF16),
        scratch_shapes=[pltpu.VMEM((seq, 2 * LANES), BF16)],
        compiler_params=_params("arbitrary", "arbitrary", "arbitrary"),
        name="diff_attention",
    )(view, view, view, c_lambda, c_norm.reshape(1, LANES)).reshape(batch * seq, heads * LANES)


def _cross_kernel(q_ref, kv_ref, o_ref, *, heads):
    scale = HEAD_DIM ** -0.5
    width = heads * HEAD_DIM
    for h in range(heads):
        cs = slice(h * HEAD_DIM, (h + 1) * HEAD_DIM)
        k = kv_ref[:, h * HEAD_DIM:(h + 1) * HEAD_DIM]
        v = kv_ref[:, width + h * HEAD_DIM:width + (h + 1) * HEAD_DIM]
        s = lax.dot_general(q_ref[:, cs], k, (((1,), (1,)), ((), ())), preferred_element_type=F32) * scale
        m = jnp.max(s, axis=-1, keepdims=True)
        e = jnp.exp(s - m)
        den = jnp.sum(e, axis=-1, keepdims=True)
        o = jnp.dot(e.astype(BF16), v, preferred_element_type=F32) / den
        o_ref[:, cs] = o.astype(o_ref.dtype)


def _cross_attention(pbdx, kv, batch, seq, col0, width, tq=512):
    mem_len = kv.shape[0] // batch
    tq = min(tq, seq)
    view = pbdx.reshape(batch, seq, pbdx.shape[1])
    return pl.pallas_call(
        functools.partial(_cross_kernel, heads=width // HEAD_DIM),
        grid=(batch, seq // tq),
        in_specs=[pl.BlockSpec((None, tq, width), lambda b, t: (b, t, col0 // width)),
                  pl.BlockSpec((None, mem_len, 2 * width), lambda b, t: (b, 0, 0))],
        out_specs=pl.BlockSpec((None, tq, width), lambda b, t: (b, t, 0)),
        out_shape=jax.ShapeDtypeStruct((batch, seq, width), BF16),
        compiler_params=_params("arbitrary", "arbitrary"),
        name="cross_attention",
    )(view, kv.reshape(batch, mem_len, 2 * width)).reshape(batch * seq, width)


def _merge_kernel(x_ref, ya_ref, yb_ref, yc_ref, yd_ref, yx_ref,
                  g0, g1, g2, g3, g4, p0, p1, p2, p3, p4, b0, b1, b2, b3, b4, o_ref):
    x = x_ref[...]
    merged = None
    for y_ref, g_ref, p_ref, b_ref in ((ya_ref, g0, p0, b0), (yb_ref, g1, p1, b1), (yc_ref, g2, p2, b2),
                                       (yd_ref, g3, p3, b3), (yx_ref, g4, p4, b4)):
        gate = jax.nn.sigmoid(jnp.dot(x, g_ref[...], preferred_element_type=F32) + b_ref[...])
        term = gate * jnp.dot(y_ref[...], p_ref[...], preferred_element_type=F32)
        merged = term if merged is None else merged + term
    o_ref[...] = merged.astype(o_ref.dtype)


def _gated_merge(xb, ys, w_gate, w_branch, w_branch_x, b_gate, tm=512, tn=512):
    m, d = xb.shape
    tm = min(tm, m)
    nb = d // tn
    n_br = 5
    act = [pl.BlockSpec((tm, d), lambda j, i: (i, 0))]
    act += [pl.BlockSpec((tm, y.shape[1]), lambda j, i: (i, 0)) for y in ys]
    gates = [pl.BlockSpec((d, tn), functools.partial(lambda j, i, br: (0, br * nb + j), br=br)) for br in range(n_br)]
    projs = [pl.BlockSpec((None, w_branch.shape[1], tn), functools.partial(lambda j, i, br: (br, 0, j), br=br))
             for br in range(4)]
    projs.append(pl.BlockSpec((w_branch_x.shape[0], tn), lambda j, i: (0, j)))
    biases = [pl.BlockSpec((1, tn), functools.partial(lambda j, i, br: (0, br * nb + j), br=br)) for br in range(n_br)]
    bg2 = b_gate.reshape(1, n_br * d)
    return pl.pallas_call(
        _merge_kernel,
        grid=(nb, m // tm),
        in_specs=act + gates + projs + biases,
        out_specs=pl.BlockSpec((tm, tn), lambda j, i: (i, j)),
        out_shape=jax.ShapeDtypeStruct((m, d), BF16),
        compiler_params=_params("arbitrary", "arbitrary"),
        name="gated_merge",
    )(xb, *ys, *([w_gate] * n_br), *([w_branch] * 4), w_branch_x, *([bg2] * n_br))


def _layer_norm_rows(h, g, b):
    mu = jnp.mean(h, axis=-1, keepdims=True)
    hc = h - mu
    var = jnp.mean(hc * hc, axis=-1, keepdims=True)
    return hc * lax.rsqrt(var + LN_EPS) * g + b


def _out_ln_kernel(mg_ref, w_ref, x_ref, g_ref, b_ref, xo_ref, xb_ref, *, alpha):
    y = jnp.dot(mg_ref[...], w_ref[...], preferred_element_type=F32)
    xn = _layer_norm_rows(alpha * x_ref[...] + y, g_ref[...], b_ref[...])
    xo_ref[...] = xn
    xb_ref[...] = xn.astype(BF16)


def _out_proj_ln(merged, w_out_b, x, g, b, alpha, tm=256):
    m, d = x.shape
    tm = min(tm, m)
    row = pl.BlockSpec((tm, d), lambda i: (i, 0))
    vec = pl.BlockSpec((1, d), lambda i: (0, 0))
    return pl.pallas_call(
        functools.partial(_out_ln_kernel, alpha=alpha),
        grid=(m // tm,),
        in_specs=[row, pl.BlockSpec((d, d), lambda i: (0, 0)), row, vec, vec],
        out_specs=[row, row],
        out_shape=[jax.ShapeDtypeStruct((m, d), F32), jax.ShapeDtypeStruct((m, d), BF16)],
        compiler_params=_params("arbitrary"),
        name="out_proj_ln",
    )(merged, w_out_b, x, g.reshape(1, d), b.reshape(1, d))


def _split_bf16(a):
    hi = a.astype(BF16)
    lo = (a - hi.astype(F32)).astype(BF16)
    return hi, lo


def _router_kernel(x_ref, w_ref, b_ref, idx_ref, wt_ref, cnt_ref):
    @pl.when(pl.program_id(0) == 0)
    def _():
        cnt_ref[...] = jnp.zeros_like(cnt_ref)

    x = x_ref[...]
    w = w_ref[...]
    xh, xl = _split_bf16(x)
    wh, wl = _split_bf16(w)
    logits = (jnp.dot(xh, wh, preferred_element_type=F32) + jnp.dot(xl, wh, preferred_element_type=F32)
              + jnp.dot(xh, wl, preferred_element_type=F32))
    lt = logits.T[:N_EXPERTS, :]
    tm = lt.shape[1]
    scores = jax.nn.sigmoid(lt)
    biased = scores + b_ref[...]
    gsz = N_EXPERTS // N_GROUPS
    sub8 = lax.broadcasted_iota(jnp.int32, (gsz, tm), 0).astype(F32)
    grp_rows = []
    for g in range(N_GROUPS):
        blk = biased[g * gsz:(g + 1) * gsz, :]
        m1 = jnp.max(blk, axis=0, keepdims=True)
        i1 = jnp.min(jnp.where(blk == m1, sub8, float(gsz)), axis=0, keepdims=True)
        m2 = jnp.max(jnp.where(sub8 == i1, -jnp.inf, blk), axis=0, keepdims=True)
        grp_rows.append(m1 + m2)
    grp = jnp.concatenate(grp_rows, axis=0)
    subg = lax.broadcasted_iota(jnp.int32, (N_GROUPS, tm), 0).astype(F32)
    gsel = jnp.zeros((N_GROUPS, tm), F32)
    for _ in range(TOP_GROUPS):
        mg = jnp.max(grp, axis=0, keepdims=True)
        ig = jnp.min(jnp.where(grp == mg, subg, float(N_GROUPS)), axis=0, keepdims=True)
        hit = subg == ig
        gsel = jnp.where(hit, 1.0, gsel)
        grp = jnp.where(hit, -jnp.inf, grp)
    emask = jnp.concatenate([jnp.broadcast_to(gsel[g:g + 1, :], (gsz, tm)) for g in range(N_GROUPS)], axis=0)
    cand = jnp.where(emask > 0.5, biased, -jnp.inf)
    sube = lax.broadcasted_iota(jnp.int32, (N_EXPERTS, tm), 0).astype(F32)
    idx_rows, w_rows = [], []
    chosen = jnp.zeros((N_EXPERTS, tm), F32)
    for _ in range(TOP_K):
        mc = jnp.max(cand, axis=0, keepdims=True)
        ic = jnp.min(jnp.where(cand == mc, sube, float(N_EXPERTS)), axis=0, keepdims=True)
        hit = sube == ic
        idx_rows.append(ic)
        w_rows.append(jnp.sum(jnp.where(hit, scores, 0.0), axis=0, keepdims=True))
        chosen = jnp.where(hit, 1.0, chosen)
        cand = jnp.where(hit, -jnp.inf, cand)
    cnt_ref[...] += jnp.sum(chosen, axis=1, keepdims=True)
    wsel = jnp.concatenate(w_rows, axis=0)
    wsel = wsel / jnp.sum(wsel, axis=0, keepdims=True) * ROUTE_SCALE
    idx_ref[...] = jnp.concatenate(idx_rows, axis=0).astype(jnp.int32)
    wt_ref[...] = wsel


def _router(x, router_w, router_b, tm=512):
    m, d = x.shape
    tm = min(tm, m)
    w_pad = jnp.pad(router_w, ((0, 0), (0, LANES - N_EXPERTS)))
    return pl.pallas_call(
        _router_kernel,
        grid=(m // tm,),
        in_specs=[pl.BlockSpec((tm, d), lambda i: (i, 0)),
                  pl.BlockSpec((d, LANES), lambda i: (0, 0)),
                  pl.BlockSpec((N_EXPERTS, 1), lambda i: (0, 0))],
        out_specs=[pl.BlockSpec((TOP_K, tm), lambda i: (0, i)), pl.BlockSpec((TOP_K, tm), lambda i: (0, i)),
                   pl.BlockSpec((N_EXPERTS, LANES), lambda i: (0, 0))],
        out_shape=[jax.ShapeDtypeStruct((TOP_K, m), jnp.int32), jax.ShapeDtypeStruct((TOP_K, m), F32),
                   jax.ShapeDtypeStruct((N_EXPERTS, LANES), F32)],
        compiler_params=_params("arbitrary"),
        name="router",
    )(x, w_pad, router_b.reshape(N_EXPERTS, 1))


def _slot_kernel(idx_ref, base_ref, dest_ref, tri_ref, run_ref):
    tm = idx_ref.shape[1]

    @pl.when(pl.program_id(0) == 0)
    def _():
        row = lax.broadcasted_iota(jnp.int32, (tm, tm), 0)
        col = lax.broadcasted_iota(jnp.int32, (tm, tm), 1)
        tri_ref[...] = jnp.where(row < col, 1.0, 0.0).astype(BF16)
        run_ref[...] = base_ref[...]

    sube = lax.broadcasted_iota(jnp.int32, (N_EXPERTS, tm), 0)
    idx = idx_ref[...]
    base = run_ref[...]
    rows = []
    for k in range(TOP_K):
        hit = sube == idx[k:k + 1, :]
        before = jnp.dot(jnp.where(hit, 1.0, 0.0).astype(BF16), tri_ref[...], preferred_element_type=F32)
        rows.append(jnp.sum(jnp.where(hit, base + before, 0.0), axis=0, keepdims=True))
        base = base + jnp.sum(jnp.where(hit, 1.0, 0.0), axis=1, keepdims=True)
    run_ref[...] = base
    dest_ref[...] = jnp.concatenate(rows, axis=0).astype(jnp.int32)


def _assign_slots(idx_t, base, tm=512):
    kk, m = idx_t.shape
    tm = min(tm, m)
    return pl.pallas_call(
        _slot_kernel,
        grid=(m // tm,),
        in_specs=[pl.BlockSpec((kk, tm), lambda i: (0, i)), pl.BlockSpec((N_EXPERTS, 1), lambda i: (0, 0))],
        out_specs=pl.BlockSpec((kk, tm), lambda i: (0, i)),
        out_shape=jax.ShapeDtypeStruct((kk, m), jnp.int32),
        scratch_shapes=[pltpu.VMEM((tm, tm), BF16), pltpu.VMEM((N_EXPERTS, 1), F32)],
        compiler_params=_params("arbitrary"),
        name="assign_slots",
    )(idx_t, base.astype(F32).reshape(N_EXPERTS, 1))


def _expert_kernel(be_ref, nb_ref, x_ref, wg_ref, wu_ref, wd_ref, o_ref):
    del be_ref
    i = pl.program_id(0)

    @pl.when(i < nb_ref[0])
    def _():
        x = x_ref[...]
        y = None
        for c in range(wg_ref.shape[1] // MXU_WIDTH):
            cs = slice(c * MXU_WIDTH, (c + 1) * MXU_WIDTH)
            gate = jnp.dot(x, wg_ref[:, cs].astype(BF16), preferred_element_type=F32)
            up = jnp.dot(x, wu_ref[:, cs].astype(BF16), preferred_element_type=F32)
            act = (jax.nn.silu(gate) * up).astype(BF16)
            part = jnp.dot(act, wd_ref[cs, :].astype(BF16), preferred_element_type=F32)
            y = part if y is None else y + part
        o_ref[...] = y.astype(o_ref.dtype)

    @pl.when(i >= nb_ref[0])
    def _():
        o_ref[...] = jnp.zeros_like(o_ref)


def _expert_ffn(xs, block_e, nblocks, w_gate, w_up, w_down, layer, rows):
    ns, d = xs.shape
    de = w_gate.shape[3]
    grid_spec = pltpu.PrefetchScalarGridSpec(
        num_scalar_prefetch=2,
        grid=(ns // rows,),
        in_specs=[pl.BlockSpec((rows, d), lambda i, be, nb: (i, 0)),
                  pl.BlockSpec((None, None, d, de), lambda i, be, nb: (layer, be[i], 0, 0)),
                  pl.BlockSpec((None, None, d, de), lambda i, be, nb: (layer, be[i], 0, 0)),
                  pl.BlockSpec((None, None, de, d), lambda i, be, nb: (layer, be[i], 0, 0))],
        out_specs=pl.BlockSpec((rows, d), lambda i, be, nb: (i, 0)),
    )
    return pl.pallas_call(
        _expert_kernel,
        grid_spec=grid_spec,
        out_shape=jax.ShapeDtypeStruct((ns, d), BF16),
        compiler_params=_params("arbitrary"),
        name="expert_ffn",
    )(block_e, nblocks, xs, w_gate, w_up, w_down)


def _combine_ln_kernel(*refs, alpha):
    yg_refs = refs[:TOP_K]
    wt_ref, sh_ref, x_ref, g_ref, b_ref, xo_ref, xb_ref = refs[TOP_K:]
    wt = wt_ref[...]
    routed = sh_ref[...].astype(F32)
    for k in range(TOP_K):
        routed = routed + wt[:, k:k + 1] * yg_refs[k][...].astype(F32)
    xn = _layer_norm_rows(alpha * x_ref[...] + routed, g_ref[...], b_ref[...])
    xo_ref[...] = xn
    xb_ref[...] = xn.astype(BF16)


def _combine_ln(yg, wt, shared, x, g, b, alpha, tm=256):
    m, d = x.shape
    tm = min(tm, m)
    nb = m // tm
    row = pl.BlockSpec((tm, d), lambda i: (i, 0))
    vec = pl.BlockSpec((1, d), lambda i: (0, 0))
    yg_specs = [pl.BlockSpec((tm, d), functools.partial(lambda i, k: (k * nb + i, 0), k=k)) for k in range(TOP_K)]
    return pl.pallas_call(
        functools.partial(_combine_ln_kernel, alpha=alpha),
        grid=(nb,),
        in_specs=yg_specs + [pl.BlockSpec((tm, TOP_K), lambda i: (i, 0)), row, row, vec, vec],
        out_specs=[row, row],
        out_shape=[jax.ShapeDtypeStruct((m, d), F32), jax.ShapeDtypeStruct((m, d), BF16)],
        compiler_params=_params("arbitrary"),
        name="combine_ln",
    )(*([yg] * TOP_K), wt, shared, x, g.reshape(1, d), b.reshape(1, d))


def _moe_layer(x1, xb1, layer, router_w, router_b, e_gate, e_up, e_down, s_gate, s_up, s_down, g, b, alpha):
    t, d = x1.shape
    rows = min(MOE_ROWS, t)
    idx_t, wt_t, cnt = _router(x1, router_w, router_b)
    n_assign = t * TOP_K
    i32 = jnp.int32
    counts = cnt[:, 0].astype(i32)
    padded = (counts + rows - 1) // rows * rows
    ends_p = jnp.cumsum(padded)
    pad_e = padded - counts
    dest = _assign_slots(idx_t, ends_p - padded).reshape(n_assign)
    nblk = n_assign // rows + N_EXPERTS
    n_pad = nblk * rows - n_assign
    pad_i = jnp.arange(n_pad, dtype=i32)
    pad_before = jnp.cumsum(pad_e) - pad_e
    pad_slot = pad_i + jnp.sum(jnp.where(pad_before[None, :] <= pad_i[:, None], counts[None, :], 0), axis=1)
    keys = jnp.concatenate([dest, pad_slot])
    toks = jnp.concatenate([jnp.arange(n_assign, dtype=i32) % t, pad_i % t])
    _, slot_tok = lax.sort_key_val(keys, toks)
    blk0 = jnp.arange(nblk, dtype=i32) * rows
    block_e = jnp.minimum(jnp.sum((ends_p[None, :] <= blk0[:, None]).astype(i32), axis=1), N_EXPERTS - 1)
    nused = (ends_p[-1] // rows).astype(i32).reshape(1)
    xs = jnp.take(xb1, slot_tok, axis=0, mode='clip')
    ys = _expert_ffn(xs, block_e, nused, e_gate, e_up, e_down, layer, rows)
    yg = jnp.take(ys, dest, axis=0, mode='clip')
    shared = _expert_ffn(xb1, jnp.zeros((t // rows,), i32), jnp.full((1,), t // rows, i32),
                         s_gate[:, None], s_up[:, None], s_down[:, None], layer, rows)
    return _combine_ln(yg, wt_t.T, shared, x1, g, b, alpha)


def _mixer_layer(x, xb, memb, layer, tabs_a, tabs_c, batch, seq, w_in, pool_w, pool_scale, c_lambda, c_norm,
                 conv_w, w_mem_kv, w_branch, w_branch_x, w_gate, b_gate, w_out, g, b, alpha):
    d = x.shape[1]
    mix = w_branch.shape[2]
    xw = w_branch_x.shape[1]
    a_cols, c_cols, d_cols = 3 * mix, 3 * mix, 3 * mix
    c0_b, c0_c, c0_d = a_cols, a_cols + mix, a_cols + mix + c_cols
    c0_x = c0_d + d_cols
    pa = _proj(xb, w_in, layer, range(3), mix, tabs_a, seq, half=HEAD_DIM // 8,
               rope_blocks=(1,) * (mix // LANES), rope_jmax=2)
    tn_c = 6 * LANES
    w_c = w_in[layer:layer + 1, :, c0_c:c0_c + c_cols]
    pc = _proj(xb, w_c, 0, range(c_cols // tn_c), tn_c, tabs_c, seq,
               half=C_QK_DIM // 8, rope_blocks=(1, 2, 0) * 2)
    tn_p = 512
    blocks = list(range(c0_b // tn_p, c0_b // tn_p + mix // tn_p)) + \
        list(range(c0_d // tn_p, c0_d // tn_p + (d_cols + xw) // tn_p))
    pbdx = _proj(xb, w_in, layer, blocks, tn_p, tabs_a, seq)
    ya = _dilated_attention(pa, batch, seq)
    yb = _pooling_mixer(pbdx, pool_w[layer], pool_scale[layer], batch, seq, 0)
    yc = _diff_attention(pc, c_lambda[layer], c_norm[layer], batch, seq, layer)
    yd = _gated_short_conv(pbdx, conv_w[layer], batch, seq, mix)
    kv = _proj(memb, w_mem_kv, layer, range(2 * xw // tn_p), tn_p, tabs_a, seq)
    yx = _cross_attention(pbdx, kv, batch, seq, mix + d_cols, xw)
    merged = _gated_merge(xb, (ya, yb, yc, yd, yx), w_gate[layer].astype(BF16), w_branch[layer].astype(BF16),
                          w_branch_x[layer].astype(BF16), b_gate[layer])
    return _out_proj_ln(merged, w_out[layer].astype(BF16), x, g[layer], b[layer], alpha)


def kernel(x, mem, w_in, pool_w, pool_scale, c_lambda, c_norm, conv_w, w_mem_kv, w_branch, w_branch_x,
           w_gate, b_gate, w_out, ln1_g, ln1_b, router_w, router_b, e_gate, e_up, e_down, s_gate, s_up,
           s_down, ln2_g, ln2_b):
    batch, seq, d = x.shape
    depth = w_in.shape[0]
    alpha = (2.0 * depth) ** 0.25
    tabs_a = _rope_tables(seq, HEAD_DIM // 4, LANES)
    tabs_ck = _rope_tables(seq, C_QK_DIM // 4, C_QK_DIM)
    q_scale = C_QK_DIM ** -0.5 * math.log2(math.e)
    tabs_c = tuple(tab * q_scale for tab in tabs_ck) + tabs_ck
    xf = x.reshape(batch * seq, d)
    xb = xf.astype(BF16)
    memb = mem.reshape(-1, d).astype(BF16)
    for l in range(depth):
        xf, xb = _mixer_layer(xf, xb, memb, l, tabs_a, tabs_c, batch, seq, w_in, pool_w, pool_scale,
                              c_lambda, c_norm, conv_w, w_mem_kv, w_branch, w_branch_x,
                              w_gate, b_gate, w_out, ln1_g, ln1_b, alpha)
        xf, xb = _moe_layer(xf, xb, l, router_w[l], router_b[l], e_gate, e_up, e_down, s_gate,
                            s_up, s_down, ln2_g[l], ln2_b[l], alpha)
    return xf.reshape(batch, seq, d)
```

```python
import functools
import math

import jax
import jax.numpy as jnp
from jax import lax
from jax.experimental import pallas as pl
from jax.experimental.pallas import tpu as pltpu

F32 = jnp.float32
BF16 = jnp.bfloat16

LANES = 128
MXU_WIDTH = 256
HEAD_DIM = 128
ROPE_THETA = 500000.0
A_PATTERNS = ((128, 1), (512, 4), (2048, 16))
A_HALF = 64
B_WINDOWS = (2, 4, 8, 16)
C_QK_DIM = 64
N_EXPERTS = 64
N_GROUPS = 8
TOP_GROUPS = 4
TOP_K = 8
ROUTE_SCALE = 2.5
LN_EPS = 1e-5
NEG_INF = -1e30
VMEM_LIMIT_BYTES = 56 * 1024 * 1024
MOE_ROWS = 512


def _params(*sem):
    return pltpu.CompilerParams(dimension_semantics=sem, vmem_limit_bytes=VMEM_LIMIT_BYTES)


def _rope_tables(seq, rot_dim, period):
    half = rot_dim // 2
    inv = 1.0 / (ROPE_THETA ** (jnp.arange(0, rot_dim, 2, dtype=F32) / rot_dim))
    ang = jnp.arange(seq, dtype=F32)[:, None] * inv[None, :]
    cos, sin = jnp.cos(ang), jnp.sin(ang)
    lane = jnp.arange(LANES) % period
    first = lane < half
    second = (lane >= half) & (lane < 2 * half)
    idx = jnp.where(first, lane, jnp.where(second, lane - half, 0))
    cos_l, sin_l = cos[:, idx], sin[:, idx]
    ct = jnp.where(first | second, cos_l, 1.0)
    st = jnp.where(first | second, sin_l, 0.0)
    return ct.astype(F32), st.astype(F32)


def _rot_matrix(rot_dim, period):
    half = rot_dim // 2
    src = jnp.arange(LANES)[:, None]
    dst = jnp.arange(LANES)[None, :]
    pos = dst % period
    r = jnp.where((pos < half) & (src == dst + half), -1.0, 0.0)
    r = r + jnp.where((pos >= half) & (pos < 2 * half) & (src == dst - half), 1.0, 0.0)
    return r.astype(BF16)


def _proj_kernel(x_ref, w_ref, rot_ref, *rest, rope_blocks, rope_jmax):
    tab_refs, o_ref, wb_ref = rest[:-2], rest[-2], rest[-1]
    j = pl.program_id(0)
    i = pl.program_id(1)

    @pl.when(i == 0)
    def _():
        wb_ref[...] = w_ref[...].astype(BF16)

    def plain():
        o_ref[...] = jnp.dot(x_ref[...], wb_ref[...], preferred_element_type=F32).astype(o_ref.dtype)

    def roped():
        tabs = [r[...] for r in tab_refs]
        rot = rot_ref[...]
        per = MXU_WIDTH // LANES
        for c in range(len(rope_blocks) // per):
            acc = jnp.dot(x_ref[...], wb_ref[:, c * MXU_WIDTH:(c + 1) * MXU_WIDTH], preferred_element_type=F32)
            for b in range(c * per, (c + 1) * per):
                use = rope_blocks[b]
                blk = acc[:, (b - c * per) * LANES:(b - c * per + 1) * LANES]
                if use:
                    ct, st = tabs[2 * (use - 1):2 * use]
                    swapped = jnp.dot(blk.astype(BF16), rot, preferred_element_type=F32)
                    blk = blk * ct + swapped * st
                o_ref[:, b * LANES:(b + 1) * LANES] = blk.astype(o_ref.dtype)

    if not any(rope_blocks):
        plain()
    elif rope_jmax is None:
        roped()
    else:
        pl.when(j < rope_jmax)(roped)
        pl.when(j >= rope_jmax)(plain)


def _proj(xb, w, layer, col_blocks, tn, tables, rot, seq, *, rope_blocks=None, rope_jmax=None, tm=512):
    m, k = xb.shape
    nj = len(col_blocks)
    tm = min(tm, m)
    rope_blocks = tuple(int(u) for u in rope_blocks) if rope_blocks is not None else (0,) * (tn // LANES)
    col_blocks = tuple(col_blocks)
    first, contiguous = col_blocks[0], all(col_blocks[a] == col_blocks[0] + a for a in range(nj))
    if contiguous:
        w_map = lambda j, i: (layer, 0, first + j)
    else:
        n0 = next(a for a in range(1, nj) if col_blocks[a] != col_blocks[0] + a)
        second = col_blocks[n0]
        w_map = lambda j, i: (layer, 0, jnp.where(j < n0, first + j, second + j - n0))
    sblocks = seq // tm if seq >= tm else 1
    tab_spec = pl.BlockSpec((tm, LANES), lambda j, i: (i % sblocks, 0))
    kern = functools.partial(_proj_kernel, rope_blocks=rope_blocks, rope_jmax=rope_jmax)
    return pl.pallas_call(
        kern,
        grid=(nj, m // tm),
        in_specs=[pl.BlockSpec((tm, k), lambda j, i: (i, 0)),
                  pl.BlockSpec((None, k, tn), w_map),
                  pl.BlockSpec((LANES, LANES), lambda j, i: (0, 0))] + [tab_spec] * len(tables),
        out_specs=pl.BlockSpec((tm, tn), lambda j, i: (i, j)),
        out_shape=jax.ShapeDtypeStruct((m, nj * tn), BF16),
        scratch_shapes=[pltpu.VMEM((k, tn), BF16)],
        compiler_params=_params("arbitrary", "arbitrary"),
        name="proj",
    )(xb, w, rot, *tables)


def _dilated_kernel(q_ref, kp_ref, km_ref, kn_ref, vp_ref, vm_ref, vn_ref, o_ref, lse_ref, *, tl, seq_len, heads):
    t = pl.program_id(2)
    l0 = t * tl
    sub = 128
    nsub = tl // sub
    scale = HEAD_DIM ** -0.5
    iq = lax.broadcasted_iota(jnp.int32, (sub, sub + 2 * A_HALF), 0)
    ik = lax.broadcasted_iota(jnp.int32, (sub, sub + 2 * A_HALF), 1)
    band = jnp.abs(ik - A_HALF - iq) <= A_HALF
    lane = lax.broadcasted_iota(jnp.int32, (sub, LANES), 1)
    valid = []
    for a in range(nsub):
        kk = l0 + a * sub - A_HALF + ik
        valid.append(band & (kk >= 0) & (kk < seq_len))
    lse_rows = [jnp.zeros((sub, LANES), F32) for _ in range(nsub)]
    for h in range(heads):
        cs = slice(h * HEAD_DIM, (h + 1) * HEAD_DIM)
        kcat = jnp.concatenate([kp_ref[:, cs], km_ref[:, cs], kn_ref[:, cs]], axis=0)
        vcat = jnp.concatenate([vp_ref[:, cs], vm_ref[:, cs], vn_ref[:, cs]], axis=0)
        for a in range(nsub):
            q = q_ref[a * sub:(a + 1) * sub, cs]
            kw = kcat[a * sub:a * sub + sub + 2 * A_HALF]
            vw = vcat[a * sub:a * sub + sub + 2 * A_HALF]
            s = lax.dot_general(q, kw, (((1,), (1,)), ((), ())), preferred_element_type=F32) * scale
            s = jnp.where(valid[a], s, NEG_INF)
            m = jnp.max(s, axis=-1, keepdims=True)
            p = jnp.exp(s - m)
            den = jnp.sum(p, axis=-1, keepdims=True)
            o = jnp.dot(p.astype(BF16), vw, preferred_element_type=F32) / den
            o_ref[a * sub:(a + 1) * sub, cs] = o.astype(o_ref.dtype)
            lse_rows[a] = jnp.where(lane == h, m + jnp.log(den), lse_rows[a])
    for a in range(nsub):
        lse_ref[a * sub:(a + 1) * sub, :] = lse_rows[a]


def _dilated_pattern(pa, batch, seq, dil):
    width = pa.shape[1] // 3
    heads = width // HEAD_DIM
    sl = seq // dil
    tl = min(256, sl)
    nt = sl // tl
    hb = tl // A_HALF
    last_halo = sl // A_HALF - 1
    view = pa.reshape(batch, sl, dil * 3 * width)

    def main(c):
        return pl.BlockSpec((None, tl, width), lambda b, r, t: (b, t, r * 3 + c))

    def prev(c):
        return pl.BlockSpec((None, A_HALF, width), lambda b, r, t: (b, jnp.maximum(t * hb - 1, 0), r * 3 + c))

    def nxt(c):
        return pl.BlockSpec((None, A_HALF, width),
                            lambda b, r, t: (b, jnp.minimum((t + 1) * hb, last_halo), r * 3 + c))

    kern = functools.partial(_dilated_kernel, tl=tl, seq_len=sl, heads=heads)
    o, lse = pl.pallas_call(
        kern,
        grid=(batch, dil, nt),
        in_specs=[main(0), prev(1), main(1), nxt(1), prev(2), main(2), nxt(2)],
        out_specs=[pl.BlockSpec((None, tl, width), lambda b, r, t: (b, t, r)),
                   pl.BlockSpec((None, tl, LANES), lambda b, r, t: (b, t, r))],
        out_shape=[jax.ShapeDtypeStruct((batch, sl, dil * width), BF16),
                   jax.ShapeDtypeStruct((batch, sl, dil * LANES), F32)],
        compiler_params=_params("arbitrary", "arbitrary", "arbitrary"),
        name=f"dilated_d{dil}",
    )(view, view, view, view, view, view, view)
    return o.reshape(batch * seq, width), lse.reshape(batch * seq, LANES)


def _dilated_combine_kernel(o0_ref, o1_ref, o2_ref, l0_ref, l1_ref, l2_ref, y_ref, *, heads):
    l0, l1, l2 = l0_ref[...], l1_ref[...], l2_ref[...]
    m = jnp.maximum(jnp.maximum(l0, l1), l2)
    e0, e1, e2 = jnp.exp(l0 - m), jnp.exp(l1 - m), jnp.exp(l2 - m)
    inv = 1.0 / (e0 + e1 + e2)
    w0, w1, w2 = e0 * inv, e1 * inv, e2 * inv
    for h in range(heads):
        cs = slice(h * HEAD_DIM, (h + 1) * HEAD_DIM)
        y = (w0[:, h:h + 1] * o0_ref[:, cs].astype(F32) + w1[:, h:h + 1] * o1_ref[:, cs].astype(F32)
             + w2[:, h:h + 1] * o2_ref[:, cs].astype(F32))
        y_ref[:, cs] = y.astype(y_ref.dtype)


def _dilated_attention(pa, batch, seq):
    outs, lses = zip(*[_dilated_pattern(pa, batch, seq, d) for _, d in A_PATTERNS])
    m, width = outs[0].shape
    tm = min(512, m)
    ospec = pl.BlockSpec((tm, width), lambda i: (i, 0))
    lspec = pl.BlockSpec((tm, LANES), lambda i: (i, 0))
    return pl.pallas_call(
        functools.partial(_dilated_combine_kernel, heads=width // HEAD_DIM),
        grid=(m // tm,),
        in_specs=[ospec, ospec, ospec, lspec, lspec, lspec],
        out_specs=ospec,
        out_shape=jax.ShapeDtypeStruct((m, width), BF16),
        compiler_params=_params("arbitrary"),
        name="dilated_combine",
    )(*outs, *lses)


def _pool_kernel(u_ref, w_ref, sc_ref, o_ref, pad_ref, *, seq):
    g = pl.program_id(1)
    pad = 8
    uf = u_ref[...].astype(F32)
    pad_ref[0:pad, :] = jnp.zeros((pad, uf.shape[1]), F32)
    pad_ref[pad + seq:pad + seq + pad, :] = jnp.zeros((pad, uf.shape[1]), F32)
    pad_ref[pad:pad + seq, :] = uf
    pos = lax.broadcasted_iota(jnp.int32, (seq, 1), 0)
    for gi, win in enumerate(B_WINDOWS):
        @pl.when(g == gi)
        def _(win=win):
            before, after = win // 2, win - win // 2
            tot = pad_ref[pad - before:pad - before + seq, :]
            for off in range(-before + 1, after):
                tot = tot + pad_ref[pad + off:pad + off + seq, :]
            cnt = (jnp.minimum(pos + after, seq) - jnp.maximum(pos - before, 0)).astype(F32)
            pooled = tot / cnt - uf
            mixed = jnp.dot(pooled.astype(BF16), w_ref[...].astype(BF16), preferred_element_type=F32)
            o_ref[...] = (mixed * sc_ref[...]).astype(o_ref.dtype)


def _pooling_mixer(pbdx, pool_w, pool_scale, batch, seq, col0):
    ng, cg = pool_w.shape[0], pool_w.shape[1]
    view = pbdx.reshape(batch, seq, pbdx.shape[1])
    cb = col0 // cg
    return pl.pallas_call(
        functools.partial(_pool_kernel, seq=seq),
        grid=(batch, ng),
        in_specs=[pl.BlockSpec((None, seq, cg), lambda b, g: (b, 0, cb + g)),
                  pl.BlockSpec((None, cg, cg), lambda b, g: (g, 0, 0)),
                  pl.BlockSpec((1, cg), lambda b, g: (0, g))],
        out_specs=pl.BlockSpec((None, seq, cg), lambda b, g: (b, 0, g)),
        out_shape=jax.ShapeDtypeStruct((batch, seq, ng * cg), BF16),
        scratch_shapes=[pltpu.VMEM((seq + 16, cg), F32)],
        compiler_params=_params("arbitrary", "arbitrary"),
        name="pooling",
    )(view, pool_w, pool_scale.reshape(1, ng * cg)).reshape(batch * seq, ng * cg)


def _conv_kernel(bg_ref, cg_ref, h_ref, w_ref, o_ref, pad_ref, *, seq):
    pad = 8
    u = cg_ref[...].astype(F32) * h_ref[...].astype(F32)
    cols = u.shape[1]
    pad_ref[0:pad, :] = jnp.zeros((pad, cols), F32)
    pad_ref[pad + seq:pad + seq + pad, :] = jnp.zeros((pad, cols), F32)
    pad_ref[pad:pad + seq, :] = u
    w = w_ref[...]
    conv = (pad_ref[pad - 1:pad - 1 + seq, :] * w[0:1, :] + u * w[1:2, :]
            + pad_ref[pad + 1:pad + 1 + seq, :] * w[2:3, :])
    o_ref[...] = (bg_ref[...].astype(F32) * conv).astype(o_ref.dtype)


def _gated_short_conv(pbdx, conv_w, batch, seq, col0):
    width = conv_w.shape[1]
    cb = 256
    nb = width // cb
    view = pbdx.reshape(batch, seq, pbdx.shape[1])
    base = col0 // cb

    def spec(part):
        return pl.BlockSpec((None, seq, cb), lambda b, c: (b, 0, base + part * nb + c))

    return pl.pallas_call(
        functools.partial(_conv_kernel, seq=seq),
        grid=(batch, nb),
        in_specs=[spec(0), spec(1), spec(2), pl.BlockSpec((3, cb), lambda b, c: (0, c))],
        out_specs=pl.BlockSpec((None, seq, cb), lambda b, c: (b, 0, c)),
        out_shape=jax.ShapeDtypeStruct((batch, seq, width), BF16),
        scratch_shapes=[pltpu.VMEM((seq + 16, cb), F32)],
        compiler_params=_params("arbitrary", "arbitrary"),
        name="short_conv",
    )(view, view, view, conv_w).reshape(batch * seq, width)


def _diff_kernel(q_ref, k_ref, v_ref, lam_ref, nrm_ref, o_ref, vone_ref, *, lam_init, chunk):
    t = pl.program_id(2)
    seq = k_ref.shape[0]
    tq = q_ref.shape[0]

    @pl.when(t == 0)
    def _():
        vone_ref[:, :LANES] = v_ref[...]
        vone_ref[:, LANES:] = jnp.ones((seq, LANES), BF16)

    lq = lam_ref[...].astype(F32)
    lam = (jnp.exp(jnp.sum(lq[0:1] * lq[1:2], axis=-1, keepdims=True))
           - jnp.exp(jnp.sum(lq[2:3] * lq[3:4], axis=-1, keepdims=True)) + lam_init)
    q = q_ref[...]
    lane = lax.broadcasted_iota(jnp.int32, q.shape, 1)
    zero = jnp.zeros_like(q)
    q2 = jnp.concatenate([jnp.where(lane < C_QK_DIM, q, zero), jnp.where(lane >= C_QK_DIM, q, zero)], axis=0)
    m = acc = None
    for c in range(seq // chunk):
        ks = slice(c * chunk, (c + 1) * chunk)
        s = lax.dot_general(q2, k_ref[ks, :], (((1,), (1,)), ((), ())), preferred_element_type=F32)
        mc = jnp.max(s, axis=-1, keepdims=True)
        m_new = mc if m is None else jnp.maximum(m, mc)
        e = jnp.exp2((s - m_new).astype(BF16))
        pv = jnp.dot(e, vone_ref[ks, :], preferred_element_type=F32)
        acc = pv if acc is None else acc * jnp.exp2(m - m_new) + pv
        m = m_new
    out = acc[:, :LANES] / acc[:, LANES:]
    o = out[:tq] - lam * out[tq:]
    o = o * lax.rsqrt(jnp.mean(o * o, axis=-1, keepdims=True) + LN_EPS) * nrm_ref[...] * (1.0 - lam_init)
    o_ref[...] = o.astype(o_ref.dtype)


def _diff_attention(pc, c_lambda, c_norm, batch, seq, layer, tq=256, chunk=1024):
    heads = pc.shape[1] // (3 * LANES)
    tq = min(tq, seq)
    chunk = min(chunk, seq)
    lam_init = 0.8 - 0.6 * math.exp(-0.3 * layer)
    view = pc.reshape(batch, seq, pc.shape[1])
    return pl.pallas_call(
        functools.partial(_diff_kernel, lam_init=lam_init, chunk=chunk),
        grid=(batch, heads, seq // tq),
        in_specs=[pl.BlockSpec((None, tq, LANES), lambda b, h, t: (b, t, 3 * h)),
                  pl.BlockSpec((None, seq, LANES), lambda b, h, t: (b, 0, 3 * h + 1)),
                  pl.BlockSpec((None, seq, LANES), lambda b, h, t: (b, 0, 3 * h + 2)),
                  pl.BlockSpec((4, C_QK_DIM), lambda b, h, t: (0, 0)),
                  pl.BlockSpec((1, LANES), lambda b, h, t: (0, 0))],
        out_specs=pl.BlockSpec((None, tq, LANES), lambda b, h, t: (b, t, h)),
        out_shape=jax.ShapeDtypeStruct((batch, seq, heads * LANES), BF16),
        scratch_shapes=[pltpu.VMEM((seq, 2 * LANES), BF16)],
        compiler_params=_params("arbitrary", "arbitrary", "arbitrary"),
        name="diff_attention",
    )(view, view, view, c_lambda, c_norm.reshape(1, LANES)).reshape(batch * seq, heads * LANES)


def _cross_kernel(q_ref, kv_ref, o_ref, *, heads):
    scale = HEAD_DIM ** -0.5
    width = heads * HEAD_DIM
    for h in range(heads):
        cs = slice(h * HEAD_DIM, (h + 1) * HEAD_DIM)
        k = kv_ref[:, h * HEAD_DIM:(h + 1) * HEAD_DIM]
        v = kv_ref[:, width + h * HEAD_DIM:width + (h + 1) * HEAD_DIM]
        s = lax.dot_general(q_ref[:, cs], k, (((1,), (1,)), ((), ())), preferred_element_type=F32) * scale
        m = jnp.max(s, axis=-1, keepdims=True)
        e = jnp.exp(s - m)
        den = jnp.sum(e, axis=-1, keepdims=True)
        o = jnp.dot(e.astype(BF16), v, preferred_element_type=F32) / den
        o_ref[:, cs] = o.astype(o_ref.dtype)


def _cross_attention(pbdx, kv, batch, seq, col0, width, tq=512):
    mem_len = kv.shape[0] // batch
    tq = min(tq, seq)
    view = pbdx.reshape(batch, seq, pbdx.shape[1])
    return pl.pallas_call(
        functools.partial(_cross_kernel, heads=width // HEAD_DIM),
        grid=(batch, seq // tq),
        in_specs=[pl.BlockSpec((None, tq, width), lambda b, t: (b, t, col0 // width)),
                  pl.BlockSpec((None, mem_len, 2 * width), lambda b, t: (b, 0, 0))],
        out_specs=pl.BlockSpec((None, tq, width), lambda b, t: (b, t, 0)),
        out_shape=jax.ShapeDtypeStruct((batch, seq, width), BF16),
        compiler_params=_params("arbitrary", "arbitrary"),
        name="cross_attention",
    )(view, kv.reshape(batch, mem_len, 2 * width)).reshape(batch * seq, width)


def _merge_kernel(x_ref, ya_ref, yb_ref, yc_ref, yd_ref, yx_ref,
                  g0, g1, g2, g3, g4, p0, p1, p2, p3, p4, b0, b1, b2, b3, b4, o_ref):
    x = x_ref[...]
    merged = None
    for y_ref, g_ref, p_ref, b_ref in ((ya_ref, g0, p0, b0), (yb_ref, g1, p1, b1), (yc_ref, g2, p2, b2),
                                       (yd_ref, g3, p3, b3), (yx_ref, g4, p4, b4)):
        gate = jax.nn.sigmoid(jnp.dot(x, g_ref[...], preferred_element_type=F32) + b_ref[...])
        term = gate * jnp.dot(y_ref[...], p_ref[...], preferred_element_type=F32)
        merged = term if merged is None else merged + term
    o_ref[...] = merged.astype(o_ref.dtype)


def _gated_merge(xb, ys, w_gate, w_branch, w_branch_x, b_gate, tm=512, tn=512):
    m, d = xb.shape
    tm = min(tm, m)
    nb = d // tn
    n_br = 5
    act = [pl.BlockSpec((tm, d), lambda j, i: (i, 0))]
    act += [pl.BlockSpec((tm, y.shape[1]), lambda j, i: (i, 0)) for y in ys]
    gates = [pl.BlockSpec((d, tn), functools.partial(lambda j, i, br: (0, br * nb + j), br=br)) for br in range(n_br)]
    projs = [pl.BlockSpec((None, w_branch.shape[1], tn), functools.partial(lambda j, i, br: (br, 0, j), br=br))
             for br in range(4)]
    projs.append(pl.BlockSpec((w_branch_x.shape[0], tn), lambda j, i: (0, j)))
    biases = [pl.BlockSpec((1, tn), functools.partial(lambda j, i, br: (0, br * nb + j), br=br)) for br in range(n_br)]
    bg2 = b_gate.reshape(1, n_br * d)
    return pl.pallas_call(
        _merge_kernel,
        grid=(nb, m // tm),
        in_specs=act + gates + projs + biases,
        out_specs=pl.BlockSpec((tm, tn), lambda j, i: (i, j)),
        out_shape=jax.ShapeDtypeStruct((m, d), BF16),
        compiler_params=_params("arbitrary", "arbitrary"),
        name="gated_merge",
    )(xb, *ys, *([w_gate] * n_br), *([w_branch] * 4), w_branch_x, *([bg2] * n_br))


def _layer_norm_rows(h, g, b):
    mu = jnp.mean(h, axis=-1, keepdims=True)
    hc = h - mu
    var = jnp.mean(hc * hc, axis=-1, keepdims=True)
    return hc * lax.rsqrt(var + LN_EPS) * g + b


def _out_ln_kernel(mg_ref, w_ref, x_ref, g_ref, b_ref, xo_ref, xb_ref, *, alpha):
    y = jnp.dot(mg_ref[...], w_ref[...], preferred_element_type=F32)
    xn = _layer_norm_rows(alpha * x_ref[...] + y, g_ref[...], b_ref[...])
    xo_ref[...] = xn
    xb_ref[...] = xn.astype(BF16)


def _out_proj_ln(merged, w_out_b, x, g, b, alpha, tm=256):
    m, d = x.shape
    tm = min(tm, m)
    row = pl.BlockSpec((tm, d), lambda i: (i, 0))
    vec = pl.BlockSpec((1, d), lambda i: (0, 0))
    return pl.pallas_call(
        functools.partial(_out_ln_kernel, alpha=alpha),
        grid=(m // tm,),
        in_specs=[row, pl.BlockSpec((d, d), lambda i: (0, 0)), row, vec, vec],
        out_specs=[row, row],
        out_shape=[jax.ShapeDtypeStruct((m, d), F32), jax.ShapeDtypeStruct((m, d), BF16)],
        compiler_params=_params("arbitrary"),
        name="out_proj_ln",
    )(merged, w_out_b, x, g.reshape(1, d), b.reshape(1, d))


def _split_bf16(a):
    hi = a.astype(BF16)
    lo = (a - hi.astype(F32)).astype(BF16)
    return hi, lo


def _router_kernel(x_ref, w_ref, b_ref, idx_ref, wt_ref, cnt_ref):
    @pl.when(pl.program_id(0) == 0)
    def _():
        cnt_ref[...] = jnp.zeros_like(cnt_ref)

    x = x_ref[...]
    w = w_ref[...]
    xh, xl = _split_bf16(x)
    wh, wl = _split_bf16(w)
    logits = (jnp.dot(xh, wh, preferred_element_type=F32) + jnp.dot(xl, wh, preferred_element_type=F32)
              + jnp.dot(xh, wl, preferred_element_type=F32))
    lt = logits.T[:N_EXPERTS, :]
    tm = lt.shape[1]
    scores = jax.nn.sigmoid(lt)
    biased = scores + b_ref[...]
    gsz = N_EXPERTS // N_GROUPS
    sub8 = lax.broadcasted_iota(jnp.int32, (gsz, tm), 0).astype(F32)
    grp_rows = []
    for g in range(N_GROUPS):
        blk = biased[g * gsz:(g + 1) * gsz, :]
        m1 = jnp.max(blk, axis=0, keepdims=True)
        i1 = jnp.min(jnp.where(blk == m1, sub8, float(gsz)), axis=0, keepdims=True)
        m2 = jnp.max(jnp.where(sub8 == i1, -jnp.inf, blk), axis=0, keepdims=True)
        grp_rows.append(m1 + m2)
    grp = jnp.concatenate(grp_rows, axis=0)
    subg = lax.broadcasted_iota(jnp.int32, (N_GROUPS, tm), 0).astype(F32)
    gsel = jnp.zeros((N_GROUPS, tm), F32)
    for _ in range(TOP_GROUPS):
        mg = jnp.max(grp, axis=0, keepdims=True)
        ig = jnp.min(jnp.where(grp == mg, subg, float(N_GROUPS)), axis=0, keepdims=True)
        hit = subg == ig
        gsel = jnp.where(hit, 1.0, gsel)
        grp = jnp.where(hit, -jnp.inf, grp)
    emask = jnp.concatenate([jnp.broadcast_to(gsel[g:g + 1, :], (gsz, tm)) for g in range(N_GROUPS)], axis=0)
    cand = jnp.where(emask > 0.5, biased, -jnp.inf)
    sube = lax.broadcasted_iota(jnp.int32, (N_EXPERTS, tm), 0).astype(F32)
    idx_rows, w_rows = [], []
    chosen = jnp.zeros((N_EXPERTS, tm), F32)
    for _ in range(TOP_K):
        mc = jnp.max(cand, axis=0, keepdims=True)
        ic = jnp.min(jnp.where(cand == mc, sube, float(N_EXPERTS)), axis=0, keepdims=True)
        hit = sube == ic
        idx_rows.append(ic)
        w_rows.append(jnp.sum(jnp.where(hit, scores, 0.0), axis=0, keepdims=True))
        chosen = jnp.where(hit, 1.0, chosen)
        cand = jnp.where(hit, -jnp.inf, cand)
    cnt_ref[...] += jnp.sum(chosen, axis=1, keepdims=True)
    wsel = jnp.concatenate(w_rows, axis=0)
    wsel = wsel / jnp.sum(wsel, axis=0, keepdims=True) * ROUTE_SCALE
    idx_ref[...] = jnp.concatenate(idx_rows, axis=0).astype(jnp.int32)
    wt_ref[...] = wsel


def _router(x, router_w, router_b, tm=512):
    m, d = x.shape
    tm = min(tm, m)
    w_pad = jnp.pad(router_w, ((0, 0), (0, LANES - N_EXPERTS)))
    return pl.pallas_call(
        _router_kernel,
        grid=(m // tm,),
        in_specs=[pl.BlockSpec((tm, d), lambda i: (i, 0)),
                  pl.BlockSpec((d, LANES), lambda i: (0, 0)),
                  pl.BlockSpec((N_EXPERTS, 1), lambda i: (0, 0))],
        out_specs=[pl.BlockSpec((TOP_K, tm), lambda i: (0, i)), pl.BlockSpec((TOP_K, tm), lambda i: (0, i)),
                   pl.BlockSpec((N_EXPERTS, LANES), lambda i: (0, 0))],
        out_shape=[jax.ShapeDtypeStruct((TOP_K, m), jnp.int32), jax.ShapeDtypeStruct((TOP_K, m), F32),
                   jax.ShapeDtypeStruct((N_EXPERTS, LANES), F32)],
        compiler_params=_params("arbitrary"),
        name="router",
    )(x, w_pad, router_b.reshape(N_EXPERTS, 1))


def _slot_kernel(idx_ref, base_ref, dest_ref, tri_ref, run_ref):
    tm = idx_ref.shape[1]

    @pl.when(pl.program_id(0) == 0)
    def _():
        row = lax.broadcasted_iota(jnp.int32, (tm, tm), 0)
        col = lax.broadcasted_iota(jnp.int32, (tm, tm), 1)
        tri_ref[...] = jnp.where(row < col, 1.0, 0.0).astype(BF16)
        run_ref[...] = base_ref[...]

    sube = lax.broadcasted_iota(jnp.int32, (N_EXPERTS, tm), 0)
    idx = idx_ref[...]
    base = run_ref[...]
    hits = [sube == idx[k:k + 1, :] for k in range(TOP_K)]
    chosen = jnp.zeros((N_EXPERTS, tm), F32)
    for hit in hits:
        chosen = jnp.where(hit, 1.0, chosen)
    before = jnp.dot(chosen.astype(BF16), tri_ref[...], preferred_element_type=F32)
    slot = base + before
    rows = [jnp.sum(jnp.where(hit, slot, 0.0), axis=0, keepdims=True) for hit in hits]
    run_ref[...] = base + jnp.sum(chosen, axis=1, keepdims=True)
    dest_ref[...] = jnp.concatenate(rows, axis=0).astype(jnp.int32)


def _assign_slots(idx_t, base, tm=512):
    kk, m = idx_t.shape
    tm = min(tm, m)
    return pl.pallas_call(
        _slot_kernel,
        grid=(m // tm,),
        in_specs=[pl.BlockSpec((kk, tm), lambda i: (0, i)), pl.BlockSpec((N_EXPERTS, 1), lambda i: (0, 0))],
        out_specs=pl.BlockSpec((kk, tm), lambda i: (0, i)),
        out_shape=jax.ShapeDtypeStruct((kk, m), jnp.int32),
        scratch_shapes=[pltpu.VMEM((tm, tm), BF16), pltpu.VMEM((N_EXPERTS, 1), F32)],
        compiler_params=_params("arbitrary"),
        name="assign_slots",
    )(idx_t, base.astype(F32).reshape(N_EXPERTS, 1))


def _expert_kernel(be_ref, nb_ref, x_ref, wg_ref, wu_ref, wd_ref, o_ref):
    del be_ref
    i = pl.program_id(0)

    @pl.when(i < nb_ref[0])
    def _():
        x = x_ref[...]
        y = None
        for c in range(wg_ref.shape[1] // MXU_WIDTH):
            cs = slice(c * MXU_WIDTH, (c + 1) * MXU_WIDTH)
            gate = jnp.dot(x, wg_ref[:, cs].astype(BF16), preferred_element_type=F32)
            up = jnp.dot(x, wu_ref[:, cs].astype(BF16), preferred_element_type=F32)
            act = (jax.nn.silu(gate) * up).astype(BF16)
            part = jnp.dot(act, wd_ref[cs, :].astype(BF16), preferred_element_type=F32)
            y = part if y is None else y + part
        o_ref[...] = y.astype(o_ref.dtype)

    @pl.when(i >= nb_ref[0])
    def _():
        o_ref[...] = jnp.zeros_like(o_ref)


def _expert_ffn(xs, block_e, nblocks, w_gate, w_up, w_down, layer, rows):
    ns, d = xs.shape
    de = w_gate.shape[3]
    grid_spec = pltpu.PrefetchScalarGridSpec(
        num_scalar_prefetch=2,
        grid=(ns // rows,),
        in_specs=[pl.BlockSpec((rows, d), lambda i, be, nb: (i, 0)),
                  pl.BlockSpec((None, None, d, de), lambda i, be, nb: (layer, be[i], 0, 0)),
                  pl.BlockSpec((None, None, d, de), lambda i, be, nb: (layer, be[i], 0, 0)),
                  pl.BlockSpec((None, None, de, d), lambda i, be, nb: (layer, be[i], 0, 0))],
        out_specs=pl.BlockSpec((rows, d), lambda i, be, nb: (i, 0)),
    )
    return pl.pallas_call(
        _expert_kernel,
        grid_spec=grid_spec,
        out_shape=jax.ShapeDtypeStruct((ns, d), BF16),
        compiler_params=_params("arbitrary"),
        name="expert_ffn",
    )(block_e, nblocks, xs, w_gate, w_up, w_down)


def _combine_ln_kernel(*refs, alpha):
    yg_refs = refs[:TOP_K]
    wt_ref, sh_ref, x_ref, g_ref, b_ref, xo_ref, xb_ref = refs[TOP_K:]
    wt = wt_ref[...]
    routed = sh_ref[...].astype(F32)
    for k in range(TOP_K):
        routed = routed + wt[:, k:k + 1] * yg_refs[k][...].astype(F32)
    xn = _layer_norm_rows(alpha * x_ref[...] + routed, g_ref[...], b_ref[...])
    xo_ref[...] = xn
    xb_ref[...] = xn.astype(BF16)


def _combine_ln(yg, wt, shared, x, g, b, alpha, tm=256):
    m, d = x.shape
    tm = min(tm, m)
    nb = m // tm
    row = pl.BlockSpec((tm, d), lambda i: (i, 0))
    vec = pl.BlockSpec((1, d), lambda i: (0, 0))
    yg_specs = [pl.BlockSpec((tm, d), functools.partial(lambda i, k: (k * nb + i, 0), k=k)) for k in range(TOP_K)]
    return pl.pallas_call(
        functools.partial(_combine_ln_kernel, alpha=alpha),
        grid=(nb,),
        in_specs=yg_specs + [pl.BlockSpec((tm, TOP_K), lambda i: (i, 0)), row, row, vec, vec],
        out_specs=[row, row],
        out_shape=[jax.ShapeDtypeStruct((m, d), F32), jax.ShapeDtypeStruct((m, d), BF16)],
        compiler_params=_params("arbitrary"),
        name="combine_ln",
    )(*([yg] * TOP_K), wt, shared, x, g.reshape(1, d), b.reshape(1, d))


def _moe_layer(x1, xb1, layer, router_w, router_b, e_gate, e_up, e_down, s_gate, s_up, s_down, g, b, alpha):
    t, d = x1.shape
    rows = min(MOE_ROWS, t)
    idx_t, wt_t, cnt = _router(x1, router_w, router_b)
    n_assign = t * TOP_K
    i32 = jnp.int32
    counts = cnt[:, 0].astype(i32)
    padded = (counts + rows - 1) // rows * rows
    ends_p = jnp.cumsum(padded)
    pad_e = padded - counts
    dest = _assign_slots(idx_t, ends_p - padded).reshape(n_assign)
    nblk = n_assign // rows + N_EXPERTS
    n_pad = nblk * rows - n_assign
    pad_i = jnp.arange(n_pad, dtype=i32)
    pad_before = jnp.cumsum(pad_e) - pad_e
    pad_slot = pad_i + jnp.sum(jnp.where(pad_before[None, :] <= pad_i[:, None], counts[None, :], 0), axis=1)
    keys = jnp.concatenate([dest, pad_slot])
    toks = jnp.concatenate([jnp.arange(n_assign, dtype=i32) % t, pad_i % t])
    _, slot_tok = lax.sort_key_val(keys, toks)
    blk0 = jnp.arange(nblk, dtype=i32) * rows
    block_e = jnp.minimum(jnp.sum((ends_p[None, :] <= blk0[:, None]).astype(i32), axis=1), N_EXPERTS - 1)
    nused = (ends_p[-1] // rows).astype(i32).reshape(1)
    xs = jnp.take(xb1, slot_tok, axis=0, mode='clip')
    ys = _expert_ffn(xs, block_e, nused, e_gate, e_up, e_down, layer, rows)
    yg = jnp.take(ys, dest, axis=0, mode='clip')
    shared = _expert_ffn(xb1, jnp.zeros((t // rows,), i32), jnp.full((1,), t // rows, i32),
                         s_gate[:, None], s_up[:, None], s_down[:, None], layer, rows)
    return _combine_ln(yg, wt_t.T, shared, x1, g, b, alpha)


def _mixer_layer(x, xb, memb, layer, tabs_a, tabs_c, batch, seq, w_in, pool_w, pool_scale, c_lambda, c_norm,
                 conv_w, w_mem_kv, w_branch, w_branch_x, w_gate, b_gate, w_out, g, b, alpha):
    d = x.shape[1]
    mix = w_branch.shape[2]
    xw = w_branch_x.shape[1]
    a_cols, c_cols, d_cols = 3 * mix, 3 * mix, 3 * mix
    c0_b, c0_c, c0_d = a_cols, a_cols + mix, a_cols + mix + c_cols
    c0_x = c0_d + d_cols
    (tabs_a, rot_a), (tabs_c, rot_c) = tabs_a, tabs_c
    pa = _proj(xb, w_in, layer, range(3), mix, tabs_a, rot_a, seq,
               rope_blocks=(1,) * (mix // LANES), rope_jmax=2)
    tn_c = 6 * LANES
    w_c = w_in[layer:layer + 1, :, c0_c:c0_c + c_cols]
    pc = _proj(xb, w_c, 0, range(c_cols // tn_c), tn_c, tabs_c, rot_c, seq, rope_blocks=(1, 2, 0) * 2)
    tn_p = 512
    blocks = list(range(c0_b // tn_p, c0_b // tn_p + mix // tn_p)) + \
        list(range(c0_d // tn_p, c0_d // tn_p + (d_cols + xw) // tn_p))
    pbdx = _proj(xb, w_in, layer, blocks, tn_p, tabs_a, rot_a, seq)
    ya = _dilated_attention(pa, batch, seq)
    yb = _pooling_mixer(pbdx, pool_w[layer], pool_scale[layer], batch, seq, 0)
    yc = _diff_attention(pc, c_lambda[layer], c_norm[layer], batch, seq, layer)
    yd = _gated_short_conv(pbdx, conv_w[layer], batch, seq, mix)
    kv = _proj(memb, w_mem_kv, layer, range(2 * xw // tn_p), tn_p, tabs_a, rot_a, seq)
    yx = _cross_attention(pbdx, kv, batch, seq, mix + d_cols, xw)
    merged = _gated_merge(xb, (ya, yb, yc, yd, yx), w_gate[layer].astype(BF16), w_branch[layer].astype(BF16),
                          w_branch_x[layer].astype(BF16), b_gate[layer])
    return _out_proj_ln(merged, w_out[layer].astype(BF16), x, g[layer], b[layer], alpha)


def kernel(x, mem, w_in, pool_w, pool_scale, c_lambda, c_norm, conv_w, w_mem_kv, w_branch, w_branch_x,
           w_gate, b_gate, w_out, ln1_g, ln1_b, router_w, router_b, e_gate, e_up, e_down, s_gate, s_up,
           s_down, ln2_g, ln2_b):
    batch, seq, d = x.shape
    depth = w_in.shape[0]
    alpha = (2.0 * depth) ** 0.25
    tabs_a = (_rope_tables(seq, HEAD_DIM // 4, LANES), _rot_matrix(HEAD_DIM // 4, LANES))
    tabs_ck = _rope_tables(seq, C_QK_DIM // 4, C_QK_DIM)
    q_scale = C_QK_DIM ** -0.5 * math.log2(math.e)
    tabs_c = (tuple(tab * q_scale for tab in tabs_ck) + tabs_ck, _rot_matrix(C_QK_DIM // 4, C_QK_DIM))
    xf = x.reshape(batch * seq, d)
    xb = xf.astype(BF16)
    memb = mem.reshape(-1, d).astype(BF16)
    for l in range(depth):
        xf, xb = _mixer_layer(xf, xb, memb, l, tabs_a, tabs_c, batch, seq, w_in, pool_w, pool_scale,
                              c_lambda, c_norm, conv_w, w_mem_kv, w_branch, w_branch_x,
                              w_gate, b_gate, w_out, ln1_g, ln1_b, alpha)
        xf, xb = _moe_layer(xf, xb, l, router_w[l], router_b[l], e_gate, e_up, e_down, s_gate,
                            s_up, s_down, ln2_g[l], ln2_b[l], alpha)
    return xf.reshape(batch, seq, d)
```

```python
import functools
import math

import jax
import jax.numpy as jnp
from jax import lax
from jax.experimental import pallas as pl
from jax.experimental.pallas import tpu as pltpu

F32 = jnp.float32
BF16 = jnp.bfloat16

LANES = 128
MXU_WIDTH = 256
HEAD_DIM = 128
ROPE_THETA = 500000.0
A_PATTERNS = ((128, 1), (512, 4), (2048, 16))
A_HALF = 64
B_WINDOWS = (2, 4, 8, 16)
C_QK_DIM = 64
N_EXPERTS = 64
N_GROUPS = 8
TOP_GROUPS = 4
TOP_K = 8
ROUTE_SCALE = 2.5
LN_EPS = 1e-5
NEG_INF = -1e30
VMEM_LIMIT_BYTES = 56 * 1024 * 1024
MOE_ROWS = 512
BATCH_GROUPS = 2


def _params(*sem):
    return pltpu.CompilerParams(dimension_semantics=sem, vmem_limit_bytes=VMEM_LIMIT_BYTES)


def _rope_tables(seq, rot_dim, period):
    half = rot_dim // 2
    inv = 1.0 / (ROPE_THETA ** (jnp.arange(0, rot_dim, 2, dtype=F32) / rot_dim))
    ang = jnp.arange(seq, dtype=F32)[:, None] * inv[None, :]
    cos, sin = jnp.cos(ang), jnp.sin(ang)
    lane = jnp.arange(LANES) % period
    first = lane < half
    second = (lane >= half) & (lane < 2 * half)
    idx = jnp.where(first, lane, jnp.where(second, lane - half, 0))
    cos_l, sin_l = cos[:, idx], sin[:, idx]
    ct = jnp.where(first | second, cos_l, 1.0)
    s1 = jnp.where(first, -sin_l, 0.0)
    s2 = jnp.where(second, sin_l, 0.0)
    return ct.astype(F32), s1.astype(F32), s2.astype(F32)


def _proj_kernel(x_ref, w_ref, *rest, half, rope_blocks, rope_jmax):
    tab_refs, o_ref, wb_ref = rest[:-2], rest[-2], rest[-1]
    j = pl.program_id(0)
    i = pl.program_id(1)

    @pl.when(i == 0)
    def _():
        wb_ref[...] = w_ref[...].astype(BF16)

    def plain():
        o_ref[...] = jnp.dot(x_ref[...], wb_ref[...], preferred_element_type=F32).astype(o_ref.dtype)

    def roped():
        tabs = [r[...] for r in tab_refs]
        per = MXU_WIDTH // LANES
        for c in range(len(rope_blocks) // per):
            acc = jnp.dot(x_ref[...], wb_ref[:, c * MXU_WIDTH:(c + 1) * MXU_WIDTH], preferred_element_type=F32)
            for b in range(c * per, (c + 1) * per):
                use = rope_blocks[b]
                blk = acc[:, (b - c * per) * LANES:(b - c * per + 1) * LANES]
                if use:
                    ct, s1, s2 = tabs[3 * (use - 1):3 * use]
                    blk = (blk * ct + pltpu.roll(blk, LANES - half, 1) * s1
                           + pltpu.roll(blk, half, 1) * s2)
                o_ref[:, b * LANES:(b + 1) * LANES] = blk.astype(o_ref.dtype)

    if not any(rope_blocks):
        plain()
    elif rope_jmax is None:
        roped()
    else:
        pl.when(j < rope_jmax)(roped)
        pl.when(j >= rope_jmax)(plain)


def _proj(xb, w, layer, col_blocks, tn, tables, seq, *, half=0, rope_blocks=None, rope_jmax=None, tm=512):
    m, k = xb.shape
    nj = len(col_blocks)
    tm = min(tm, m)
    rope_blocks = tuple(int(u) for u in rope_blocks) if rope_blocks is not None else (0,) * (tn // LANES)
    col_blocks = tuple(col_blocks)
    first, contiguous = col_blocks[0], all(col_blocks[a] == col_blocks[0] + a for a in range(nj))
    if contiguous:
        w_map = lambda j, i: (layer, 0, first + j)
    else:
        n0 = next(a for a in range(1, nj) if col_blocks[a] != col_blocks[0] + a)
        second = col_blocks[n0]
        w_map = lambda j, i: (layer, 0, jnp.where(j < n0, first + j, second + j - n0))
    sblocks = seq // tm if seq >= tm else 1
    tab_spec = pl.BlockSpec((tm, LANES), lambda j, i: (i % sblocks, 0))
    kern = functools.partial(_proj_kernel, half=half, rope_blocks=rope_blocks, rope_jmax=rope_jmax)
    return pl.pallas_call(
        kern,
        grid=(nj, m // tm),
        in_specs=[pl.BlockSpec((tm, k), lambda j, i: (i, 0)),
                  pl.BlockSpec((None, k, tn), w_map)] + [tab_spec] * len(tables),
        out_specs=pl.BlockSpec((tm, tn), lambda j, i: (i, j)),
        out_shape=jax.ShapeDtypeStruct((m, nj * tn), BF16),
        scratch_shapes=[pltpu.VMEM((k, tn), BF16)],
        compiler_params=_params("arbitrary", "arbitrary"),
        name="proj",
    )(xb, w, *tables)


def _dilated_kernel(q_ref, kp_ref, km_ref, kn_ref, vp_ref, vm_ref, vn_ref, o_ref, lse_ref, *, tl, seq_len, heads):
    t = pl.program_id(2)
    l0 = t * tl
    sub = 128
    nsub = tl // sub
    scale = HEAD_DIM ** -0.5
    iq = lax.broadcasted_iota(jnp.int32, (sub, sub + 2 * A_HALF), 0)
    ik = lax.broadcasted_iota(jnp.int32, (sub, sub + 2 * A_HALF), 1)
    band = jnp.abs(ik - A_HALF - iq) <= A_HALF
    lane = lax.broadcasted_iota(jnp.int32, (sub, LANES), 1)
    valid = []
    for a in range(nsub):
        kk = l0 + a * sub - A_HALF + ik
        valid.append(band & (kk >= 0) & (kk < seq_len))
    lse_rows = [jnp.zeros((sub, LANES), F32) for _ in range(nsub)]
    for h in range(heads):
        cs = slice(h * HEAD_DIM, (h + 1) * HEAD_DIM)
        kcat = jnp.concatenate([kp_ref[:, cs], km_ref[:, cs], kn_ref[:, cs]], axis=0)
        vcat = jnp.concatenate([vp_ref[:, cs], vm_ref[:, cs], vn_ref[:, cs]], axis=0)
        for a in range(nsub):
            q = q_ref[a * sub:(a + 1) * sub, cs]
            kw = kcat[a * sub:a * sub + sub + 2 * A_HALF]
            vw = vcat[a * sub:a * sub + sub + 2 * A_HALF]
            s = lax.dot_general(q, kw, (((1,), (1,)), ((), ())), preferred_element_type=F32) * scale
            s = jnp.where(valid[a], s, NEG_INF)
            m = jnp.max(s, axis=-1, keepdims=True)
            p = jnp.exp(s - m)
            den = jnp.sum(p, axis=-1, keepdims=True)
            o = jnp.dot(p.astype(BF16), vw, preferred_element_type=F32) / den
            o_ref[a * sub:(a + 1) * sub, cs] = o.astype(o_ref.dtype)
            lse_rows[a] = jnp.where(lane == h, m + jnp.log(den), lse_rows[a])
    for a in range(nsub):
        lse_ref[a * sub:(a + 1) * sub, :] = lse_rows[a]


def _dilated_pattern(pa, batch, seq, dil):
    width = pa.shape[1] // 3
    heads = width // HEAD_DIM
    sl = seq // dil
    tl = min(256, sl)
    nt = sl // tl
    hb = tl // A_HALF
    last_halo = sl // A_HALF - 1
    view = pa.reshape(batch, sl, dil * 3 * width)

    def main(c):
        return pl.BlockSpec((None, tl, width), lambda b, r, t: (b, t, r * 3 + c))

    def prev(c):
        return pl.BlockSpec((None, A_HALF, width), lambda b, r, t: (b, jnp.maximum(t * hb - 1, 0), r * 3 + c))

    def nxt(c):
        return pl.BlockSpec((None, A_HALF, width),
                            lambda b, r, t: (b, jnp.minimum((t + 1) * hb, last_halo), r * 3 + c))

    kern = functools.partial(_dilated_kernel, tl=tl, seq_len=sl, heads=heads)
    o, lse = pl.pallas_call(
        kern,
        grid=(batch, dil, nt),
        in_specs=[main(0), prev(1), main(1), nxt(1), prev(2), main(2), nxt(2)],
        out_specs=[pl.BlockSpec((None, tl, width), lambda b, r, t: (b, t, r)),
                   pl.BlockSpec((None, tl, LANES), lambda b, r, t: (b, t, r))],
        out_shape=[jax.ShapeDtypeStruct((batch, sl, dil * width), BF16),
                   jax.ShapeDtypeStruct((batch, sl, dil * LANES), F32)],
        compiler_params=_params("arbitrary", "arbitrary", "arbitrary"),
        name=f"dilated_d{dil}",
    )(view, view, view, view, view, view, view)
    return o.reshape(batch * seq, width), lse.reshape(batch * seq, LANES)


def _dilated_combine_kernel(o0_ref, o1_ref, o2_ref, l0_ref, l1_ref, l2_ref, y_ref, *, heads):
    l0, l1, l2 = l0_ref[...], l1_ref[...], l2_ref[...]
    m = jnp.maximum(jnp.maximum(l0, l1), l2)
    e0, e1, e2 = jnp.exp(l0 - m), jnp.exp(l1 - m), jnp.exp(l2 - m)
    inv = 1.0 / (e0 + e1 + e2)
    w0, w1, w2 = e0 * inv, e1 * inv, e2 * inv
    for h in range(heads):
        cs = slice(h * HEAD_DIM, (h + 1) * HEAD_DIM)
        y = (w0[:, h:h + 1] * o0_ref[:, cs].astype(F32) + w1[:, h:h + 1] * o1_ref[:, cs].astype(F32)
             + w2[:, h:h + 1] * o2_ref[:, cs].astype(F32))
        y_ref[:, cs] = y.astype(y_ref.dtype)


def _dilated_attention(pa, batch, seq):
    outs, lses = zip(*[_dilated_pattern(pa, batch, seq, d) for _, d in A_PATTERNS])
    m, width = outs[0].shape
    tm = min(512, m)
    ospec = pl.BlockSpec((tm, width), lambda i: (i, 0))
    lspec = pl.BlockSpec((tm, LANES), lambda i: (i, 0))
    return pl.pallas_call(
        functools.partial(_dilated_combine_kernel, heads=width // HEAD_DIM),
        grid=(m // tm,),
        in_specs=[ospec, ospec, ospec, lspec, lspec, lspec],
        out_specs=ospec,
        out_shape=jax.ShapeDtypeStruct((m, width), BF16),
        compiler_params=_params("arbitrary"),
        name="dilated_combine",
    )(*outs, *lses)


def _pool_kernel(u_ref, w_ref, sc_ref, o_ref, pad_ref, *, seq):
    g = pl.program_id(1)
    pad = 8
    uf = u_ref[...].astype(F32)
    pad_ref[0:pad, :] = jnp.zeros((pad, uf.shape[1]), F32)
    pad_ref[pad + seq:pad + seq + pad, :] = jnp.zeros((pad, uf.shape[1]), F32)
    pad_ref[pad:pad + seq, :] = uf
    pos = lax.broadcasted_iota(jnp.int32, (seq, 1), 0)
    for gi, win in enumerate(B_WINDOWS):
        @pl.when(g == gi)
        def _(win=win):
            before, after = win // 2, win - win // 2
            tot = pad_ref[pad - before:pad - before + seq, :]
            for off in range(-before + 1, after):
                tot = tot + pad_ref[pad + off:pad + off + seq, :]
            cnt = (jnp.minimum(pos + after, seq) - jnp.maximum(pos - before, 0)).astype(F32)
            pooled = tot / cnt - uf
            mixed = jnp.dot(pooled.astype(BF16), w_ref[...].astype(BF16), preferred_element_type=F32)
            o_ref[...] = (mixed * sc_ref[...]).astype(o_ref.dtype)


def _pooling_mixer(pbdx, pool_w, pool_scale, batch, seq, col0):
    ng, cg = pool_w.shape[0], pool_w.shape[1]
    view = pbdx.reshape(batch, seq, pbdx.shape[1])
    cb = col0 // cg
    return pl.pallas_call(
        functools.partial(_pool_kernel, seq=seq),
        grid=(batch, ng),
        in_specs=[pl.BlockSpec((None, seq, cg), lambda b, g: (b, 0, cb + g)),
                  pl.BlockSpec((None, cg, cg), lambda b, g: (g, 0, 0)),
                  pl.BlockSpec((1, cg), lambda b, g: (0, g))],
        out_specs=pl.BlockSpec((None, seq, cg), lambda b, g: (b, 0, g)),
        out_shape=jax.ShapeDtypeStruct((batch, seq, ng * cg), BF16),
        scratch_shapes=[pltpu.VMEM((seq + 16, cg), F32)],
        compiler_params=_params("arbitrary", "arbitrary"),
        name="pooling",
    )(view, pool_w, pool_scale.reshape(1, ng * cg)).reshape(batch * seq, ng * cg)


def _conv_kernel(bg_ref, cg_ref, h_ref, w_ref, o_ref, pad_ref, *, seq):
    pad = 8
    u = cg_ref[...].astype(F32) * h_ref[...].astype(F32)
    cols = u.shape[1]
    pad_ref[0:pad, :] = jnp.zeros((pad, cols), F32)
    pad_ref[pad + seq:pad + seq + pad, :] = jnp.zeros((pad, cols), F32)
    pad_ref[pad:pad + seq, :] = u
    w = w_ref[...]
    conv = (pad_ref[pad - 1:pad - 1 + seq, :] * w[0:1, :] + u * w[1:2, :]
            + pad_ref[pad + 1:pad + 1 + seq, :] * w[2:3, :])
    o_ref[...] = (bg_ref[...].astype(F32) * conv).astype(o_ref.dtype)


def _gated_short_conv(pbdx, conv_w, batch, seq, col0):
    width = conv_w.shape[1]
    cb = 256
    nb = width // cb
    view = pbdx.reshape(batch, seq, pbdx.shape[1])
    base = col0 // cb

    def spec(part):
        return pl.BlockSpec((None, seq, cb), lambda b, c: (b, 0, base + part * nb + c))

    return pl.pallas_call(
        functools.partial(_conv_kernel, seq=seq),
        grid=(batch, nb),
        in_specs=[spec(0), spec(1), spec(2), pl.BlockSpec((3, cb), lambda b, c: (0, c))],
        out_specs=pl.BlockSpec((None, seq, cb), lambda b, c: (b, 0, c)),
        out_shape=jax.ShapeDtypeStruct((batch, seq, width), BF16),
        scratch_shapes=[pltpu.VMEM((seq + 16, cb), F32)],
        compiler_params=_params("arbitrary", "arbitrary"),
        name="short_conv",
    )(view, view, view, conv_w).reshape(batch * seq, width)


def _diff_kernel(q_ref, k_ref, v_ref, lam_ref, nrm_ref, o_ref, vone_ref, *, lam_init, chunk):
    t = pl.program_id(2)
    seq = k_ref.shape[0]
    tq = q_ref.shape[0]

    @pl.when(t == 0)
    def _():
        vone_ref[:, :LANES] = v_ref[...]
        vone_ref[:, LANES:] = jnp.ones((seq, LANES), BF16)

    lq = lam_ref[...].astype(F32)
    lam = (jnp.exp(jnp.sum(lq[0:1] * lq[1:2], axis=-1, keepdims=True))
           - jnp.exp(jnp.sum(lq[2:3] * lq[3:4], axis=-1, keepdims=True)) + lam_init)
    q = q_ref[...]
    lane = lax.broadcasted_iota(jnp.int32, q.shape, 1)
    zero = jnp.zeros_like(q)
    q2 = jnp.concatenate([jnp.where(lane < C_QK_DIM, q, zero), jnp.where(lane >= C_QK_DIM, q, zero)], axis=0)
    m = acc = None
    for c in range(seq // chunk):
        ks = slice(c * chunk, (c + 1) * chunk)
        s = lax.dot_general(q2, k_ref[ks, :], (((1,), (1,)), ((), ())), preferred_element_type=F32)
        mc = jnp.max(s, axis=-1, keepdims=True)
        m_new = mc if m is None else jnp.maximum(m, mc)
        e = jnp.exp2((s - m_new).astype(BF16))
        pv = jnp.dot(e, vone_ref[ks, :], preferred_element_type=F32)
        acc = pv if acc is None else acc * jnp.exp2(m - m_new) + pv
        m = m_new
    out = acc[:, :LANES] / acc[:, LANES:]
    o = out[:tq] - lam * out[tq:]
    o = o * lax.rsqrt(jnp.mean(o * o, axis=-1, keepdims=True) + LN_EPS) * nrm_ref[...] * (1.0 - lam_init)
    o_ref[...] = o.astype(o_ref.dtype)


def _diff_attention(pc, c_lambda, c_norm, batch, seq, layer, tq=512, chunk=1024):
    heads = pc.shape[1] // (3 * LANES)
    tq = min(tq, seq)
    chunk = min(chunk, seq)
    lam_init = 0.8 - 0.6 * math.exp(-0.3 * layer)
    view = pc.reshape(batch, seq, pc.shape[1])
    return pl.pallas_call(
        functools.partial(_diff_kernel, lam_init=lam_init, chunk=chunk),
        grid=(batch, heads, seq // tq),
        in_specs=[pl.BlockSpec((None, tq, LANES), lambda b, h, t: (b, t, 3 * h)),
                  pl.BlockSpec((None, seq, LANES), lambda b, h, t: (b, 0, 3 * h + 1)),
                  pl.BlockSpec((None, seq, LANES), lambda b, h, t: (b, 0, 3 * h + 2)),
                  pl.BlockSpec((4, C_QK_DIM), lambda b, h, t: (0, 0)),
                  pl.BlockSpec((1, LANES), lambda b, h, t: (0, 0))],
        out_specs=pl.BlockSpec((None, tq, LANES), lambda b, h, t: (b, t, h)),
        out_shape=jax.ShapeDtypeStruct((batch, seq, heads * LANES), BF16),
        scratch_shapes=[pltpu.VMEM((seq, 2 * LANES), BF16)],
        compiler_params=_params("arbitrary", "arbitrary", "arbitrary"),
        name="diff_attention",
    )(view, view, view, c_lambda, c_norm.reshape(1, LANES)).reshape(batch * seq, heads * LANES)


def _cross_kernel(q_ref, kv_ref, o_ref, *, heads):
    scale = HEAD_DIM ** -0.5
    width = heads * HEAD_DIM
    for h in range(heads):
        cs = slice(h * HEAD_DIM, (h + 1) * HEAD_DIM)
        k = kv_ref[:, h * HEAD_DIM:(h + 1) * HEAD_DIM]
        v = kv_ref[:, width + h * HEAD_DIM:width + (h + 1) * HEAD_DIM]
        s = lax.dot_general(q_ref[:, cs], k, (((1,), (1,)), ((), ())), preferred_element_type=F32) * scale
        m = jnp.max(s, axis=-1, keepdims=True)
        e = jnp.exp(s - m)
        den = jnp.sum(e, axis=-1, keepdims=True)
        o = jnp.dot(e.astype(BF16), v, preferred_element_type=F32) / den
        o_ref[:, cs] = o.astype(o_ref.dtype)


def _cross_attention(pbdx, kv, batch, seq, col0, width, tq=512):
    mem_len = kv.shape[0] // batch
    tq = min(tq, seq)
    view = pbdx.reshape(batch, seq, pbdx.shape[1])
    return pl.pallas_call(
        functools.partial(_cross_kernel, heads=width // HEAD_DIM),
        grid=(batch, seq // tq),
        in_specs=[pl.BlockSpec((None, tq, width), lambda b, t: (b, t, col0 // width)),
                  pl.BlockSpec((None, mem_len, 2 * width), lambda b, t: (b, 0, 0))],
        out_specs=pl.BlockSpec((None, tq, width), lambda b, t: (b, t, 0)),
        out_shape=jax.ShapeDtypeStruct((batch, seq, width), BF16),
        compiler_params=_params("arbitrary", "arbitrary"),
        name="cross_attention",
    )(view, kv.reshape(batch, mem_len, 2 * width)).reshape(batch * seq, width)


def _merge_kernel(x_ref, ya_ref, yb_ref, yc_ref, yd_ref, yx_ref,
                  g0, g1, g2, g3, g4, p0, p1, p2, p3, p4, b0, b1, b2, b3, b4, o_ref):
    x = x_ref[...]
    merged = None
    for y_ref, g_ref, p_ref, b_ref in ((ya_ref, g0, p0, b0), (yb_ref, g1, p1, b1), (yc_ref, g2, p2, b2),
                                       (yd_ref, g3, p3, b3), (yx_ref, g4, p4, b4)):
        gate = jax.nn.sigmoid(jnp.dot(x, g_ref[...], preferred_element_type=F32) + b_ref[...])
        term = gate * jnp.dot(y_ref[...], p_ref[...], preferred_element_type=F32)
        merged = term if merged is None else merged + term
    o_ref[...] = merged.astype(o_ref.dtype)


def _gated_merge(xb, ys, w_gate, w_branch, w_branch_x, b_gate, tm=512, tn=512):
    m, d = xb.shape
    tm = min(tm, m)
    nb = d // tn
    n_br = 5
    act = [pl.BlockSpec((tm, d), lambda j, i: (i, 0))]
    act += [pl.BlockSpec((tm, y.shape[1]), lambda j, i: (i, 0)) for y in ys]
    gates = [pl.BlockSpec((d, tn), functools.partial(lambda j, i, br: (0, br * nb + j), br=br)) for br in range(n_br)]
    projs = [pl.BlockSpec((None, w_branch.shape[1], tn), functools.partial(lambda j, i, br: (br, 0, j), br=br))
             for br in range(4)]
    projs.append(pl.BlockSpec((w_branch_x.shape[0], tn), lambda j, i: (0, j)))
    biases = [pl.BlockSpec((1, tn), functools.partial(lambda j, i, br: (0, br * nb + j), br=br)) for br in range(n_br)]
    bg2 = b_gate.reshape(1, n_br * d)
    return pl.pallas_call(
        _merge_kernel,
        grid=(nb, m // tm),
        in_specs=act + gates + projs + biases,
        out_specs=pl.BlockSpec((tm, tn), lambda j, i: (i, j)),
        out_shape=jax.ShapeDtypeStruct((m, d), BF16),
        compiler_params=_params("arbitrary", "arbitrary"),
        name="gated_merge",
    )(xb, *ys, *([w_gate] * n_br), *([w_branch] * 4), w_branch_x, *([bg2] * n_br))


def _layer_norm_rows(h, g, b):
    mu = jnp.mean(h, axis=-1, keepdims=True)
    hc = h - mu
    var = jnp.mean(hc * hc, axis=-1, keepdims=True)
    return hc * lax.rsqrt(var + LN_EPS) * g + b


def _out_ln_kernel(mg_ref, w_ref, x_ref, g_ref, b_ref, xo_ref, xb_ref, *, alpha):
    y = jnp.dot(mg_ref[...], w_ref[...], preferred_element_type=F32)
    xn = _layer_norm_rows(alpha * x_ref[...] + y, g_ref[...], b_ref[...])
    xo_ref[...] = xn
    xb_ref[...] = xn.astype(BF16)


def _out_proj_ln(merged, w_out_b, x, g, b, alpha, tm=256):
    m, d = x.shape
    tm = min(tm, m)
    row = pl.BlockSpec((tm, d), lambda i: (i, 0))
    vec = pl.BlockSpec((1, d), lambda i: (0, 0))
    return pl.pallas_call(
        functools.partial(_out_ln_kernel, alpha=alpha),
        grid=(m // tm,),
        in_specs=[row, pl.BlockSpec((d, d), lambda i: (0, 0)), row, vec, vec],
        out_specs=[row, row],
        out_shape=[jax.ShapeDtypeStruct((m, d), F32), jax.ShapeDtypeStruct((m, d), BF16)],
        compiler_params=_params("arbitrary"),
        name="out_proj_ln",
    )(merged, w_out_b, x, g.reshape(1, d), b.reshape(1, d))


def _split_bf16(a):
    hi = a.astype(BF16)
    lo = (a - hi.astype(F32)).astype(BF16)
    return hi, lo


def _router_kernel(x_ref, w_ref, b_ref, idx_ref, wt_ref, cnt_ref):
    @pl.when(pl.program_id(0) == 0)
    def _():
        cnt_ref[...] = jnp.zeros_like(cnt_ref)

    x = x_ref[...]
    w = w_ref[...]
    xh, xl = _split_bf16(x)
    wh, wl = _split_bf16(w)
    logits = (jnp.dot(xh, wh, preferred_element_type=F32) + jnp.dot(xl, wh, preferred_element_type=F32)
              + jnp.dot(xh, wl, preferred_element_type=F32))
    lt = logits.T[:N_EXPERTS, :]
    tm = lt.shape[1]
    scores = jax.nn.sigmoid(lt)
    biased = scores + b_ref[...]
    gsz = N_EXPERTS // N_GROUPS
    sub8 = lax.broadcasted_iota(jnp.int32, (gsz, tm), 0).astype(F32)
    grp_rows = []
    for g in range(N_GROUPS):
        blk = biased[g * gsz:(g + 1) * gsz, :]
        m1 = jnp.max(blk, axis=0, keepdims=True)
        i1 = jnp.min(jnp.where(blk == m1, sub8, float(gsz)), axis=0, keepdims=True)
        m2 = jnp.max(jnp.where(sub8 == i1, -jnp.inf, blk), axis=0, keepdims=True)
        grp_rows.append(m1 + m2)
    grp = jnp.concatenate(grp_rows, axis=0)
    subg = lax.broadcasted_iota(jnp.int32, (N_GROUPS, tm), 0).astype(F32)
    gsel = jnp.zeros((N_GROUPS, tm), F32)
    for _ in range(TOP_GROUPS):
        mg = jnp.max(grp, axis=0, keepdims=True)
        ig = jnp.min(jnp.where(grp == mg, subg, float(N_GROUPS)), axis=0, keepdims=True)
        hit = subg == ig
        gsel = jnp.where(hit, 1.0, gsel)
        grp = jnp.where(hit, -jnp.inf, grp)
    emask = jnp.concatenate([jnp.broadcast_to(gsel[g:g + 1, :], (gsz, tm)) for g in range(N_GROUPS)], axis=0)
    cand = jnp.where(emask > 0.5, biased, -jnp.inf)
    sube = lax.broadcasted_iota(jnp.int32, (N_EXPERTS, tm), 0).astype(F32)
    idx_rows, w_rows = [], []
    chosen = jnp.zeros((N_EXPERTS, tm), F32)
    for _ in range(TOP_K):
        mc = jnp.max(cand, axis=0, keepdims=True)
        ic = jnp.min(jnp.where(cand == mc, sube, float(N_EXPERTS)), axis=0, keepdims=True)
        hit = sube == ic
        idx_rows.append(ic)
        w_rows.append(jnp.sum(jnp.where(hit, scores, 0.0), axis=0, keepdims=True))
        chosen = jnp.where(hit, 1.0, chosen)
        cand = jnp.where(hit, -jnp.inf, cand)
    cnt_ref[...] += jnp.sum(chosen, axis=1, keepdims=True)
    wsel = jnp.concatenate(w_rows, axis=0)
    wsel = wsel / jnp.sum(wsel, axis=0, keepdims=True) * ROUTE_SCALE
    idx_ref[...] = jnp.concatenate(idx_rows, axis=0).astype(jnp.int32)
    wt_ref[...] = wsel


def _router(x, router_w, router_b, tm=512):
    m, d = x.shape
    tm = min(tm, m)
    w_pad = jnp.pad(router_w, ((0, 0), (0, LANES - N_EXPERTS)))
    return pl.pallas_call(
        _router_kernel,
        grid=(m // tm,),
        in_specs=[pl.BlockSpec((tm, d), lambda i: (i, 0)),
                  pl.BlockSpec((d, LANES), lambda i: (0, 0)),
                  pl.BlockSpec((N_EXPERTS, 1), lambda i: (0, 0))],
        out_specs=[pl.BlockSpec((TOP_K, tm), lambda i: (0, i)), pl.BlockSpec((TOP_K, tm), lambda i: (0, i)),
                   pl.BlockSpec((N_EXPERTS, LANES), lambda i: (0, 0))],
        out_shape=[jax.ShapeDtypeStruct((TOP_K, m), jnp.int32), jax.ShapeDtypeStruct((TOP_K, m), F32),
                   jax.ShapeDtypeStruct((N_EXPERTS, LANES), F32)],
        compiler_params=_params("arbitrary"),
        name="router",
    )(x, w_pad, router_b.reshape(N_EXPERTS, 1))


def _slot_kernel(idx_ref, base_ref, dest_ref, tri_ref, run_ref):
    tm = idx_ref.shape[1]

    @pl.when(pl.program_id(0) == 0)
    def _():
        row = lax.broadcasted_iota(jnp.int32, (tm, tm), 0)
        col = lax.broadcasted_iota(jnp.int32, (tm, tm), 1)
        tri_ref[...] = jnp.where(row < col, 1.0, 0.0).astype(BF16)
        run_ref[...] = base_ref[...]

    sube = lax.broadcasted_iota(jnp.int32, (N_EXPERTS, tm), 0)
    idx = idx_ref[...]
    base = run_ref[...]
    hits = [sube == idx[k:k + 1, :] for k in range(TOP_K)]
    chosen = jnp.zeros((N_EXPERTS, tm), F32)
    for hit in hits:
        chosen = jnp.where(hit, 1.0, chosen)
    before = jnp.dot(chosen.astype(BF16), tri_ref[...], preferred_element_type=F32)
    slot = base + before
    rows = [jnp.sum(jnp.where(hit, slot, 0.0), axis=0, keepdims=True) for hit in hits]
    run_ref[...] = base + jnp.sum(chosen, axis=1, keepdims=True)
    dest_ref[...] = jnp.concatenate(rows, axis=0).astype(jnp.int32)


def _assign_slots(idx_t, base, tm=512):
    kk, m = idx_t.shape
    tm = min(tm, m)
    return pl.pallas_call(
        _slot_kernel,
        grid=(m // tm,),
        in_specs=[pl.BlockSpec((kk, tm), lambda i: (0, i)), pl.BlockSpec((N_EXPERTS, 1), lambda i: (0, 0))],
        out_specs=pl.BlockSpec((kk, tm), lambda i: (0, i)),
        out_shape=jax.ShapeDtypeStruct((kk, m), jnp.int32),
        scratch_shapes=[pltpu.VMEM((tm, tm), BF16), pltpu.VMEM((N_EXPERTS, 1), F32)],
        compiler_params=_params("arbitrary"),
        name="assign_slots",
    )(idx_t, base.astype(F32).reshape(N_EXPERTS, 1))


def _expert_kernel(be_ref, nb_ref, x_ref, wg_ref, wu_ref, wd_ref, o_ref):
    del be_ref
    i = pl.program_id(0)

    @pl.when(i < nb_ref[0])
    def _():
        x = x_ref[...]
        y = None
        for c in range(wg_ref.shape[1] // MXU_WIDTH):
            cs = slice(c * MXU_WIDTH, (c + 1) * MXU_WIDTH)
            gate = jnp.dot(x, wg_ref[:, cs].astype(BF16), preferred_element_type=F32)
            up = jnp.dot(x, wu_ref[:, cs].astype(BF16), preferred_element_type=F32)
            act = (jax.nn.silu(gate) * up).astype(BF16)
            part = jnp.dot(act, wd_ref[cs, :].astype(BF16), preferred_element_type=F32)
            y = part if y is None else y + part
        o_ref[...] = y.astype(o_ref.dtype)

    @pl.when(i >= nb_ref[0])
    def _():
        o_ref[...] = jnp.zeros_like(o_ref)


def _expert_ffn(xs, block_e, nblocks, w_gate, w_up, w_down, layer, rows):
    ns, d = xs.shape
    de = w_gate.shape[3]
    grid_spec = pltpu.PrefetchScalarGridSpec(
        num_scalar_prefetch=2,
        grid=(ns // rows,),
        in_specs=[pl.BlockSpec((rows, d), lambda i, be, nb: (i, 0)),
                  pl.BlockSpec((None, None, d, de), lambda i, be, nb: (layer, be[i], 0, 0)),
                  pl.BlockSpec((None, None, d, de), lambda i, be, nb: (layer, be[i], 0, 0)),
                  pl.BlockSpec((None, None, de, d), lambda i, be, nb: (layer, be[i], 0, 0))],
        out_specs=pl.BlockSpec((rows, d), lambda i, be, nb: (i, 0)),
    )
    return pl.pallas_call(
        _expert_kernel,
        grid_spec=grid_spec,
        out_shape=jax.ShapeDtypeStruct((ns, d), BF16),
        compiler_params=_params("arbitrary"),
        name="expert_ffn",
    )(block_e, nblocks, xs, w_gate, w_up, w_down)


def _combine_ln_kernel(*refs, alpha):
    yg_refs = refs[:TOP_K]
    wt_ref, sh_ref, x_ref, g_ref, b_ref, xo_ref, xb_ref = refs[TOP_K:]
    wt = wt_ref[...]
    routed = sh_ref[...].astype(F32)
    for k in range(TOP_K):
        routed = routed + wt[:, k:k + 1] * yg_refs[k][...].astype(F32)
    xn = _layer_norm_rows(alpha * x_ref[...] + routed, g_ref[...], b_ref[...])
    xo_ref[...] = xn
    xb_ref[...] = xn.astype(BF16)


def _combine_ln(yg, wt, shared, x, g, b, alpha, tm=256):
    m, d = x.shape
    tm = min(tm, m)
    nb = m // tm
    row = pl.BlockSpec((tm, d), lambda i: (i, 0))
    vec = pl.BlockSpec((1, d), lambda i: (0, 0))
    yg_specs = [pl.BlockSpec((tm, d), functools.partial(lambda i, k: (k * nb + i, 0), k=k)) for k in range(TOP_K)]
    return pl.pallas_call(
        functools.partial(_combine_ln_kernel, alpha=alpha),
        grid=(nb,),
        in_specs=yg_specs + [pl.BlockSpec((tm, TOP_K), lambda i: (i, 0)), row, row, vec, vec],
        out_specs=[row, row],
        out_shape=[jax.ShapeDtypeStruct((m, d), F32), jax.ShapeDtypeStruct((m, d), BF16)],
        compiler_params=_params("arbitrary"),
        name="combine_ln",
    )(*([yg] * TOP_K), wt, shared, x, g.reshape(1, d), b.reshape(1, d))


def _moe_layer(x1, xb1, layer, router_w, router_b, e_gate, e_up, e_down, s_gate, s_up, s_down, g, b, alpha):
    t, d = x1.shape
    rows = min(MOE_ROWS, t)
    idx_t, wt_t, cnt = _router(x1, router_w, router_b)
    n_assign = t * TOP_K
    i32 = jnp.int32
    counts = cnt[:, 0].astype(i32)
    padded = (counts + rows - 1) // rows * rows
    ends_p = jnp.cumsum(padded)
    pad_e = padded - counts
    dest = _assign_slots(idx_t, ends_p - padded).reshape(n_assign)
    nblk = n_assign // rows + N_EXPERTS
    n_pad = nblk * rows - n_assign
    pad_i = jnp.arange(n_pad, dtype=i32)
    pad_before = jnp.cumsum(pad_e) - pad_e
    pad_slot = pad_i + jnp.sum(jnp.where(pad_before[None, :] <= pad_i[:, None], counts[None, :], 0), axis=1)
    keys = jnp.concatenate([dest, pad_slot])
    toks = jnp.concatenate([jnp.arange(n_assign, dtype=i32) % t, pad_i % t])
    _, slot_tok = lax.sort_key_val(keys, toks)
    blk0 = jnp.arange(nblk, dtype=i32) * rows
    block_e = jnp.minimum(jnp.sum((ends_p[None, :] <= blk0[:, None]).astype(i32), axis=1), N_EXPERTS - 1)
    nused = (ends_p[-1] // rows).astype(i32).reshape(1)
    xs = jnp.take(xb1, slot_tok, axis=0, mode='clip')
    ys = _expert_ffn(xs, block_e, nused, e_gate, e_up, e_down, layer, rows)
    yg = jnp.take(ys, dest, axis=0, mode='clip')
    shared = _expert_ffn(xb1, jnp.zeros((t // rows,), i32), jnp.full((1,), t // rows, i32),
                         s_gate[:, None], s_up[:, None], s_down[:, None], layer, rows)
    return _combine_ln(yg, wt_t.T, shared, x1, g, b, alpha)


def _mixer_layer(x, xb, memb, layer, tabs_a, tabs_c, batch, seq, w_in, w_c, pool_w, pool_scale, c_lambda, c_norm,
                 conv_w, w_mem_kv, merge_w, b_gate, w_out_b, g, b, alpha):
    mix = merge_w[1].shape[1]
    xw = merge_w[2].shape[0]
    a_cols, c_cols, d_cols = 3 * mix, 3 * mix, 3 * mix
    c0_b, c0_d = a_cols, a_cols + mix + c_cols
    pa = _proj(xb, w_in, layer, range(3), mix, tabs_a, seq, half=HEAD_DIM // 8,
               rope_blocks=(1,) * (mix // LANES), rope_jmax=2)
    tn_c = 6 * LANES
    pc = _proj(xb, w_c, 0, range(c_cols // tn_c), tn_c, tabs_c, seq,
               half=C_QK_DIM // 8, rope_blocks=(1, 2, 0) * 2)
    tn_p = 512
    blocks = list(range(c0_b // tn_p, c0_b // tn_p + mix // tn_p)) + \
        list(range(c0_d // tn_p, c0_d // tn_p + (d_cols + xw) // tn_p))
    pbdx = _proj(xb, w_in, layer, blocks, tn_p, tabs_a, seq)
    ya = _dilated_attention(pa, batch, seq)
    yb = _pooling_mixer(pbdx, pool_w[layer], pool_scale[layer], batch, seq, 0)
    yc = _diff_attention(pc, c_lambda[layer], c_norm[layer], batch, seq, layer)
    yd = _gated_short_conv(pbdx, conv_w[layer], batch, seq, mix)
    kv = _proj(memb, w_mem_kv, layer, range(2 * xw // tn_p), tn_p, tabs_a, seq)
    yx = _cross_attention(pbdx, kv, batch, seq, mix + d_cols, xw)
    merged = _gated_merge(xb, (ya, yb, yc, yd, yx), *merge_w, b_gate[layer])
    return _out_proj_ln(merged, w_out_b, x, g[layer], b[layer], alpha)


def kernel(x, mem, w_in, pool_w, pool_scale, c_lambda, c_norm, conv_w, w_mem_kv, w_branch, w_branch_x,
           w_gate, b_gate, w_out, ln1_g, ln1_b, router_w, router_b, e_gate, e_up, e_down, s_gate, s_up,
           s_down, ln2_g, ln2_b):
    batch, seq, d = x.shape
    depth = w_in.shape[0]
    alpha = (2.0 * depth) ** 0.25
    tabs_a = _rope_tables(seq, HEAD_DIM // 4, LANES)
    tabs_ck = _rope_tables(seq, C_QK_DIM // 4, C_QK_DIM)
    q_scale = C_QK_DIM ** -0.5 * math.log2(math.e)
    tabs_c = tuple(tab * q_scale for tab in tabs_ck) + tabs_ck
    mix = w_branch.shape[2]
    groups = BATCH_GROUPS if batch % BATCH_GROUPS == 0 else 1
    gb = batch // groups
    state = []
    for gi in range(groups):
        xg = x[gi * gb:(gi + 1) * gb].reshape(gb * seq, d)
        state.append((xg, xg.astype(BF16), mem[gi * gb:(gi + 1) * gb].reshape(-1, d).astype(BF16)))
    for l in range(depth):
        w_c = w_in[l:l + 1, :, 4 * mix:7 * mix]
        merge_w = (w_gate[l].astype(BF16), w_branch[l].astype(BF16), w_branch_x[l].astype(BF16))
        w_out_b = w_out[l].astype(BF16)
        for gi in range(groups):
            xf, xb, memb = state[gi]
            xf, xb = _mixer_layer(xf, xb, memb, l, tabs_a, tabs_c, gb, seq, w_in, w_c, pool_w, pool_scale,
                                  c_lambda, c_norm, conv_w, w_mem_kv, merge_w, b_gate, w_out_b,
                                  ln1_g, ln1_b, alpha)
            xf, xb = _moe_layer(xf, xb, l, router_w[l], router_b[l], e_gate, e_up, e_down, s_gate,
                                s_up, s_down, ln2_g[l], ln2_b[l], alpha)
            state[gi] = (xf, xb, memb)
    return jnp.concatenate([s[0] for s in state], axis=0).reshape(batch, seq, d)
```

```python
import functools
import math

import jax
import jax.numpy as jnp
from jax import lax
from jax.experimental import pallas as pl
from jax.experimental.pallas import tpu as pltpu

F32 = jnp.float32
BF16 = jnp.bfloat16

LANES = 128
MXU_WIDTH = 256
HEAD_DIM = 128
ROPE_THETA = 500000.0
A_PATTERNS = ((128, 1), (512, 4), (2048, 16))
A_HALF = 64
B_WINDOWS = (2, 4, 8, 16)
C_QK_DIM = 64
N_EXPERTS = 64
N_GROUPS = 8
TOP_GROUPS = 4
TOP_K = 8
ROUTE_SCALE = 2.5
LN_EPS = 1e-5
NEG_INF = -1e30
VMEM_LIMIT_BYTES = 56 * 1024 * 1024
MOE_ROWS = 512
BATCH_GROUPS = 1


def _params(*sem):
    return pltpu.CompilerParams(dimension_semantics=sem, vmem_limit_bytes=VMEM_LIMIT_BYTES)


def _rope_tables(seq, rot_dim, period):
    half = rot_dim // 2
    inv = 1.0 / (ROPE_THETA ** (jnp.arange(0, rot_dim, 2, dtype=F32) / rot_dim))
    ang = jnp.arange(seq, dtype=F32)[:, None] * inv[None, :]
    cos, sin = jnp.cos(ang), jnp.sin(ang)
    lane = jnp.arange(LANES) % period
    first = lane < half
    second = (lane >= half) & (lane < 2 * half)
    idx = jnp.where(first, lane, jnp.where(second, lane - half, 0))
    cos_l, sin_l = cos[:, idx], sin[:, idx]
    ct = jnp.where(first | second, cos_l, 1.0)
    s1 = jnp.where(first, -sin_l, 0.0)
    s2 = jnp.where(second, sin_l, 0.0)
    return ct.astype(F32), s1.astype(F32), s2.astype(F32)


def _proj_kernel(x_ref, w_ref, *rest, half, rope_blocks, rope_jmax):
    tab_refs, o_ref, wb_ref = rest[:-2], rest[-2], rest[-1]
    j = pl.program_id(0)
    i = pl.program_id(1)

    @pl.when(i == 0)
    def _():
        wb_ref[...] = w_ref[...].astype(BF16)

    def plain():
        o_ref[...] = jnp.dot(x_ref[...], wb_ref[...], preferred_element_type=F32).astype(o_ref.dtype)

    def roped():
        tabs = [r[...] for r in tab_refs]
        per = MXU_WIDTH // LANES
        for c in range(len(rope_blocks) // per):
            acc = jnp.dot(x_ref[...], wb_ref[:, c * MXU_WIDTH:(c + 1) * MXU_WIDTH], preferred_element_type=F32)
            for b in range(c * per, (c + 1) * per):
                use = rope_blocks[b]
                blk = acc[:, (b - c * per) * LANES:(b - c * per + 1) * LANES]
                if use:
                    ct, s1, s2 = tabs[3 * (use - 1):3 * use]
                    blk = (blk * ct + pltpu.roll(blk, LANES - half, 1) * s1
                           + pltpu.roll(blk, half, 1) * s2)
                o_ref[:, b * LANES:(b + 1) * LANES] = blk.astype(o_ref.dtype)

    if not any(rope_blocks):
        plain()
    elif rope_jmax is None:
        roped()
    else:
        pl.when(j < rope_jmax)(roped)
        pl.when(j >= rope_jmax)(plain)


def _proj(xb, w, layer, col_blocks, tn, tables, seq, *, half=0, rope_blocks=None, rope_jmax=None, tm=512):
    m, k = xb.shape
    nj = len(col_blocks)
    tm = min(tm, m)
    rope_blocks = tuple(int(u) for u in rope_blocks) if rope_blocks is not None else (0,) * (tn // LANES)
    col_blocks = tuple(col_blocks)
    first, contiguous = col_blocks[0], all(col_blocks[a] == col_blocks[0] + a for a in range(nj))
    if contiguous:
        w_map = lambda j, i: (layer, 0, first + j)
    else:
        n0 = next(a for a in range(1, nj) if col_blocks[a] != col_blocks[0] + a)
        second = col_blocks[n0]
        w_map = lambda j, i: (layer, 0, jnp.where(j < n0, first + j, second + j - n0))
    sblocks = seq // tm if seq >= tm else 1
    tab_spec = pl.BlockSpec((tm, LANES), lambda j, i: (i % sblocks, 0))
    kern = functools.partial(_proj_kernel, half=half, rope_blocks=rope_blocks, rope_jmax=rope_jmax)
    return pl.pallas_call(
        kern,
        grid=(nj, m // tm),
        in_specs=[pl.BlockSpec((tm, k), lambda j, i: (i, 0)),
                  pl.BlockSpec((None, k, tn), w_map)] + [tab_spec] * len(tables),
        out_specs=pl.BlockSpec((tm, tn), lambda j, i: (i, j)),
        out_shape=jax.ShapeDtypeStruct((m, nj * tn), BF16),
        scratch_shapes=[pltpu.VMEM((k, tn), BF16)],
        compiler_params=_params("arbitrary", "arbitrary"),
        name="proj",
    )(xb, w, *tables)


def _dilated_kernel(q_ref, kp_ref, km_ref, kn_ref, vp_ref, vm_ref, vn_ref, o_ref, lse_ref, *, tl, seq_len, heads):
    t = pl.program_id(2)
    l0 = t * tl
    sub = 128
    nsub = tl // sub
    scale = HEAD_DIM ** -0.5
    iq = lax.broadcasted_iota(jnp.int32, (sub, sub + 2 * A_HALF), 0)
    ik = lax.broadcasted_iota(jnp.int32, (sub, sub + 2 * A_HALF), 1)
    band = jnp.abs(ik - A_HALF - iq) <= A_HALF
    lane = lax.broadcasted_iota(jnp.int32, (sub, LANES), 1)
    valid = []
    for a in range(nsub):
        kk = l0 + a * sub - A_HALF + ik
        valid.append(band & (kk >= 0) & (kk < seq_len))
    lse_rows = [jnp.zeros((sub, LANES), F32) for _ in range(nsub)]
    for h in range(heads):
        cs = slice(h * HEAD_DIM, (h + 1) * HEAD_DIM)
        kcat = jnp.concatenate([kp_ref[:, cs], km_ref[:, cs], kn_ref[:, cs]], axis=0)
        vcat = jnp.concatenate([vp_ref[:, cs], vm_ref[:, cs], vn_ref[:, cs]], axis=0)
        for a in range(nsub):
            q = q_ref[a * sub:(a + 1) * sub, cs]
            kw = kcat[a * sub:a * sub + sub + 2 * A_HALF]
            vw = vcat[a * sub:a * sub + sub + 2 * A_HALF]
            s = lax.dot_general(q, kw, (((1,), (1,)), ((), ())), preferred_element_type=F32) * scale
            s = jnp.where(valid[a], s, NEG_INF)
            m = jnp.max(s, axis=-1, keepdims=True)
            p = jnp.exp(s - m)
            den = jnp.sum(p, axis=-1, keepdims=True)
            o = jnp.dot(p.astype(BF16), vw, preferred_element_type=F32) / den
            o_ref[a * sub:(a + 1) * sub, cs] = o.astype(o_ref.dtype)
            lse_rows[a] = jnp.where(lane == h, m + jnp.log(den), lse_rows[a])
    for a in range(nsub):
        lse_ref[a * sub:(a + 1) * sub, :] = lse_rows[a]


def _dilated_pattern(pa, batch, seq, dil):
    width = pa.shape[1] // 3
    heads = width // HEAD_DIM
    sl = seq // dil
    tl = min(256, sl)
    nt = sl // tl
    hb = tl // A_HALF
    last_halo = sl // A_HALF - 1
    view = pa.reshape(batch, sl, dil * 3 * width)

    def main(c):
        return pl.BlockSpec((None, tl, width), lambda b, r, t: (b, t, r * 3 + c))

    def prev(c):
        return pl.BlockSpec((None, A_HALF, width), lambda b, r, t: (b, jnp.maximum(t * hb - 1, 0), r * 3 + c))

    def nxt(c):
        return pl.BlockSpec((None, A_HALF, width),
                            lambda b, r, t: (b, jnp.minimum((t + 1) * hb, last_halo), r * 3 + c))

    kern = functools.partial(_dilated_kernel, tl=tl, seq_len=sl, heads=heads)
    o, lse = pl.pallas_call(
        kern,
        grid=(batch, dil, nt),
        in_specs=[main(0), prev(1), main(1), nxt(1), prev(2), main(2), nxt(2)],
        out_specs=[pl.BlockSpec((None, tl, width), lambda b, r, t: (b, t, r)),
                   pl.BlockSpec((None, tl, LANES), lambda b, r, t: (b, t, r))],
        out_shape=[jax.ShapeDtypeStruct((batch, sl, dil * width), BF16),
                   jax.ShapeDtypeStruct((batch, sl, dil * LANES), F32)],
        compiler_params=_params("arbitrary", "arbitrary", "arbitrary"),
        name=f"dilated_d{dil}",
    )(view, view, view, view, view, view, view)
    return o.reshape(batch * seq, width), lse.reshape(batch * seq, LANES)


def _dilated_combine_kernel(o0_ref, o1_ref, o2_ref, l0_ref, l1_ref, l2_ref, y_ref, *, heads):
    l0, l1, l2 = l0_ref[...], l1_ref[...], l2_ref[...]
    m = jnp.maximum(jnp.maximum(l0, l1), l2)
    e0, e1, e2 = jnp.exp(l0 - m), jnp.exp(l1 - m), jnp.exp(l2 - m)
    inv = 1.0 / (e0 + e1 + e2)
    w0, w1, w2 = e0 * inv, e1 * inv, e2 * inv
    for h in range(heads):
        cs = slice(h * HEAD_DIM, (h + 1) * HEAD_DIM)
        y = (w0[:, h:h + 1] * o0_ref[:, cs].astype(F32) + w1[:, h:h + 1] * o1_ref[:, cs].astype(F32)
             + w2[:, h:h + 1] * o2_ref[:, cs].astype(F32))
        y_ref[:, cs] = y.astype(y_ref.dtype)


def _dilated_attention(pa, batch, seq):
    outs, lses = zip(*[_dilated_pattern(pa, batch, seq, d) for _, d in A_PATTERNS])
    m, width = outs[0].shape
    tm = min(512, m)
    ospec = pl.BlockSpec((tm, width), lambda i: (i, 0))
    lspec = pl.BlockSpec((tm, LANES), lambda i: (i, 0))
    return pl.pallas_call(
        functools.partial(_dilated_combine_kernel, heads=width // HEAD_DIM),
        grid=(m // tm,),
        in_specs=[ospec, ospec, ospec, lspec, lspec, lspec],
        out_specs=ospec,
        out_shape=jax.ShapeDtypeStruct((m, width), BF16),
        compiler_params=_params("arbitrary"),
        name="dilated_combine",
    )(*outs, *lses)


def _pool_kernel(u_ref, w_ref, sc_ref, o_ref, pad_ref, *, seq):
    g = pl.program_id(1)
    pad = 8
    uf = u_ref[...].astype(F32)
    pad_ref[0:pad, :] = jnp.zeros((pad, uf.shape[1]), F32)
    pad_ref[pad + seq:pad + seq + pad, :] = jnp.zeros((pad, uf.shape[1]), F32)
    pad_ref[pad:pad + seq, :] = uf
    pos = lax.broadcasted_iota(jnp.int32, (seq, 1), 0)
    for gi, win in enumerate(B_WINDOWS):
        @pl.when(g == gi)
        def _(win=win):
            before, after = win // 2, win - win // 2
            tot = pad_ref[pad - before:pad - before + seq, :]
            for off in range(-before + 1, after):
                tot = tot + pad_ref[pad + off:pad + off + seq, :]
            cnt = (jnp.minimum(pos + after, seq) - jnp.maximum(pos - before, 0)).astype(F32)
            pooled = tot / cnt - uf
            mixed = jnp.dot(pooled.astype(BF16), w_ref[...].astype(BF16), preferred_element_type=F32)
            o_ref[...] = (mixed * sc_ref[...]).astype(o_ref.dtype)


def _pooling_mixer(pbdx, pool_w, pool_scale, batch, seq, col0):
    ng, cg = pool_w.shape[0], pool_w.shape[1]
    view = pbdx.reshape(batch, seq, pbdx.shape[1])
    cb = col0 // cg
    return pl.pallas_call(
        functools.partial(_pool_kernel, seq=seq),
        grid=(batch, ng),
        in_specs=[pl.BlockSpec((None, seq, cg), lambda b, g: (b, 0, cb + g)),
                  pl.BlockSpec((None, cg, cg), lambda b, g: (g, 0, 0)),
                  pl.BlockSpec((1, cg), lambda b, g: (0, g))],
        out_specs=pl.BlockSpec((None, seq, cg), lambda b, g: (b, 0, g)),
        out_shape=jax.ShapeDtypeStruct((batch, seq, ng * cg), BF16),
        scratch_shapes=[pltpu.VMEM((seq + 16, cg), F32)],
        compiler_params=_params("arbitrary", "arbitrary"),
        name="pooling",
    )(view, pool_w, pool_scale.reshape(1, ng * cg)).reshape(batch * seq, ng * cg)


def _conv_kernel(bg_ref, cg_ref, h_ref, w_ref, o_ref, pad_ref, *, seq):
    pad = 8
    u = cg_ref[...].astype(F32) * h_ref[...].astype(F32)
    cols = u.shape[1]
    pad_ref[0:pad, :] = jnp.zeros((pad, cols), F32)
    pad_ref[pad + seq:pad + seq + pad, :] = jnp.zeros((pad, cols), F32)
    pad_ref[pad:pad + seq, :] = u
    w = w_ref[...]
    conv = (pad_ref[pad - 1:pad - 1 + seq, :] * w[0:1, :] + u * w[1:2, :]
            + pad_ref[pad + 1:pad + 1 + seq, :] * w[2:3, :])
    o_ref[...] = (bg_ref[...].astype(F32) * conv).astype(o_ref.dtype)


def _gated_short_conv(pbdx, conv_w, batch, seq, col0):
    width = conv_w.shape[1]
    cb = 256
    nb = width // cb
    view = pbdx.reshape(batch, seq, pbdx.shape[1])
    base = col0 // cb

    def spec(part):
        return pl.BlockSpec((None, seq, cb), lambda b, c: (b, 0, base + part * nb + c))

    return pl.pallas_call(
        functools.partial(_conv_kernel, seq=seq),
        grid=(batch, nb),
        in_specs=[spec(0), spec(1), spec(2), pl.BlockSpec((3, cb), lambda b, c: (0, c))],
        out_specs=pl.BlockSpec((None, seq, cb), lambda b, c: (b, 0, c)),
        out_shape=jax.ShapeDtypeStruct((batch, seq, width), BF16),
        scratch_shapes=[pltpu.VMEM((seq + 16, cb), F32)],
        compiler_params=_params("arbitrary", "arbitrary"),
        name="short_conv",
    )(view, view, view, conv_w).reshape(batch * seq, width)


def _diff_kernel(q_ref, k_ref, v_ref, lam_ref, nrm_ref, o_ref, vone_ref, *, lam_init, chunk):
    t = pl.program_id(2)
    seq = k_ref.shape[0]
    tq = q_ref.shape[0]

    @pl.when(t == 0)
    def _():
        vone_ref[:, :LANES] = v_ref[...]
        vone_ref[:, LANES:] = jnp.ones((seq, LANES), BF16)

    lq = lam_ref[...].astype(F32)
    lam = (jnp.exp(jnp.sum(lq[0:1] * lq[1:2], axis=-1, keepdims=True))
           - jnp.exp(jnp.sum(lq[2:3] * lq[3:4], axis=-1, keepdims=True)) + lam_init)
    q = q_ref[...]
    lane = lax.broadcasted_iota(jnp.int32, q.shape, 1)
    zero = jnp.zeros_like(q)
    q2 = jnp.concatenate([jnp.where(lane < C_QK_DIM, q, zero), jnp.where(lane >= C_QK_DIM, q, zero)], axis=0)
    m = acc = None
    for c in range(seq // chunk):
        ks = slice(c * chunk, (c + 1) * chunk)
        s = lax.dot_general(q2, k_ref[ks, :], (((1,), (1,)), ((), ())), preferred_element_type=F32)
        mc = jnp.max(s, axis=-1, keepdims=True)
        m_new = mc if m is None else jnp.maximum(m, mc)
        e = jnp.exp2((s - m_new).astype(BF16))
        pv = jnp.dot(e, vone_ref[ks, :], preferred_element_type=F32)
        acc = pv if acc is None else acc * jnp.exp2(m - m_new) + pv
        m = m_new
    out = acc[:, :LANES] / acc[:, LANES:]
    o = out[:tq] - lam * out[tq:]
    o = o * lax.rsqrt(jnp.mean(o * o, axis=-1, keepdims=True) + LN_EPS) * nrm_ref[...] * (1.0 - lam_init)
    o_ref[...] = o.astype(o_ref.dtype)


def _diff_attention(pc, c_lambda, c_norm, batch, seq, layer, tq=512, chunk=1024):
    heads = pc.shape[1] // (3 * LANES)
    tq = min(tq, seq)
    chunk = min(chunk, seq)
    lam_init = 0.8 - 0.6 * math.exp(-0.3 * layer)
    view = pc.reshape(batch, seq, pc.shape[1])
    return pl.pallas_call(
        functools.partial(_diff_kernel, lam_init=lam_init, chunk=chunk),
        grid=(batch, heads, seq // tq),
        in_specs=[pl.BlockSpec((None, tq, LANES), lambda b, h, t: (b, t, 3 * h)),
                  pl.BlockSpec((None, seq, LANES), lambda b, h, t: (b, 0, 3 * h + 1)),
                  pl.BlockSpec((None, seq, LANES), lambda b, h, t: (b, 0, 3 * h + 2)),
                  pl.BlockSpec((4, C_QK_DIM), lambda b, h, t: (0, 0)),
                  pl.BlockSpec((1, LANES), lambda b, h, t: (0, 0))],
        out_specs=pl.BlockSpec((None, tq, LANES), lambda b, h, t: (b, t, h)),
        out_shape=jax.ShapeDtypeStruct((batch, seq, heads * LANES), BF16),
        scratch_shapes=[pltpu.VMEM((seq, 2 * LANES), BF16)],
        compiler_params=_params("arbitrary", "arbitrary", "arbitrary"),
        name="diff_attention",
    )(view, view, view, c_lambda, c_norm.reshape(1, LANES)).reshape(batch * seq, heads * LANES)


def _cross_kernel(q_ref, kv_ref, o_ref, *, heads):
    scale = HEAD_DIM ** -0.5
    width = heads * HEAD_DIM
    for h in range(heads):
        cs = slice(h * HEAD_DIM, (h + 1) * HEAD_DIM)
        k = kv_ref[:, h * HEAD_DIM:(h + 1) * HEAD_DIM]
        v = kv_ref[:, width + h * HEAD_DIM:width + (h + 1) * HEAD_DIM]
        s = lax.dot_general(q_ref[:, cs], k, (((1,), (1,)), ((), ())), preferred_element_type=F32) * scale
        m = jnp.max(s, axis=-1, keepdims=True)
        e = jnp.exp(s - m)
        den = jnp.sum(e, axis=-1, keepdims=True)
        o = jnp.dot(e.astype(BF16), v, preferred_element_type=F32) / den
        o_ref[:, cs] = o.astype(o_ref.dtype)


def _cross_attention(pbdx, kv, batch, seq, col0, width, tq=512):
    mem_len = kv.shape[0] // batch
    tq = min(tq, seq)
    view = pbdx.reshape(batch, seq, pbdx.shape[1])
    return pl.pallas_call(
        functools.partial(_cross_kernel, heads=width // HEAD_DIM),
        grid=(batch, seq // tq),
        in_specs=[pl.BlockSpec((None, tq, width), lambda b, t: (b, t, col0 // width)),
                  pl.BlockSpec((None, mem_len, 2 * width), lambda b, t: (b, 0, 0))],
        out_specs=pl.BlockSpec((None, tq, width), lambda b, t: (b, t, 0)),
        out_shape=jax.ShapeDtypeStruct((batch, seq, width), BF16),
        compiler_params=_params("arbitrary", "arbitrary"),
        name="cross_attention",
    )(view, kv.reshape(batch, mem_len, 2 * width)).reshape(batch * seq, width)


def _merge_kernel(x_ref, ya_ref, yb_ref, yc_ref, yd_ref, yx_ref,
                  g0, g1, g2, g3, g4, p0, p1, p2, p3, p4, b0, b1, b2, b3, b4, o_ref):
    x = x_ref[...]
    merged = None
    for y_ref, g_ref, p_ref, b_ref in ((ya_ref, g0, p0, b0), (yb_ref, g1, p1, b1), (yc_ref, g2, p2, b2),
                                       (yd_ref, g3, p3, b3), (yx_ref, g4, p4, b4)):
        gate = jax.nn.sigmoid(jnp.dot(x, g_ref[...], preferred_element_type=F32) + b_ref[...])
        term = gate * jnp.dot(y_ref[...], p_ref[...], preferred_element_type=F32)
        merged = term if merged is None else merged + term
    o_ref[...] = merged.astype(o_ref.dtype)


def _gated_merge(xb, ys, w_gate, w_branch, w_branch_x, b_gate, tm=512, tn=512):
    m, d = xb.shape
    tm = min(tm, m)
    nb = d // tn
    n_br = 5
    act = [pl.BlockSpec((tm, d), lambda j, i: (i, 0))]
    act += [pl.BlockSpec((tm, y.shape[1]), lambda j, i: (i, 0)) for y in ys]
    gates = [pl.BlockSpec((d, tn), functools.partial(lambda j, i, br: (0, br * nb + j), br=br)) for br in range(n_br)]
    projs = [pl.BlockSpec((None, w_branch.shape[1], tn), functools.partial(lambda j, i, br: (br, 0, j), br=br))
             for br in range(4)]
    projs.append(pl.BlockSpec((w_branch_x.shape[0], tn), lambda j, i: (0, j)))
    biases = [pl.BlockSpec((1, tn), functools.partial(lambda j, i, br: (0, br * nb + j), br=br)) for br in range(n_br)]
    bg2 = b_gate.reshape(1, n_br * d)
    return pl.pallas_call(
        _merge_kernel,
        grid=(nb, m // tm),
        in_specs=act + gates + projs + biases,
        out_specs=pl.BlockSpec((tm, tn), lambda j, i: (i, j)),
        out_shape=jax.ShapeDtypeStruct((m, d), BF16),
        compiler_params=_params("arbitrary", "arbitrary"),
        name="gated_merge",
    )(xb, *ys, *([w_gate] * n_br), *([w_branch] * 4), w_branch_x, *([bg2] * n_br))


def _layer_norm_rows(h, g, b):
    mu = jnp.mean(h, axis=-1, keepdims=True)
    hc = h - mu
    var = jnp.mean(hc * hc, axis=-1, keepdims=True)
    return hc * lax.rsqrt(var + LN_EPS) * g + b


def _out_ln_kernel(mg_ref, w_ref, x_ref, g_ref, b_ref, xo_ref, xb_ref, *, alpha):
    y = jnp.dot(mg_ref[...], w_ref[...], preferred_element_type=F32)
    xn = _layer_norm_rows(alpha * x_ref[...] + y, g_ref[...], b_ref[...])
    xo_ref[...] = xn
    xb_ref[...] = xn.astype(BF16)


def _out_proj_ln(merged, w_out_b, x, g, b, alpha, tm=256):
    m, d = x.shape
    tm = min(tm, m)
    row = pl.BlockSpec((tm, d), lambda i: (i, 0))
    vec = pl.BlockSpec((1, d), lambda i: (0, 0))
    return pl.pallas_call(
        functools.partial(_out_ln_kernel, alpha=alpha),
        grid=(m // tm,),
        in_specs=[row, pl.BlockSpec((d, d), lambda i: (0, 0)), row, vec, vec],
        out_specs=[row, row],
        out_shape=[jax.ShapeDtypeStruct((m, d), F32), jax.ShapeDtypeStruct((m, d), BF16)],
        compiler_params=_params("arbitrary"),
        name="out_proj_ln",
    )(merged, w_out_b, x, g.reshape(1, d), b.reshape(1, d))


def _split_bf16(a):
    hi = a.astype(BF16)
    lo = (a - hi.astype(F32)).astype(BF16)
    return hi, lo


def _router_kernel(x_ref, w_ref, b_ref, idx_ref, wt_ref, cnt_ref):
    @pl.when(pl.program_id(0) == 0)
    def _():
        cnt_ref[...] = jnp.zeros_like(cnt_ref)

    x = x_ref[...]
    w = w_ref[...]
    xh, xl = _split_bf16(x)
    wh, wl = _split_bf16(w)
    logits = (jnp.dot(xh, wh, preferred_element_type=F32) + jnp.dot(xl, wh, preferred_element_type=F32)
              + jnp.dot(xh, wl, preferred_element_type=F32))
    lt = logits.T[:N_EXPERTS, :]
    tm = lt.shape[1]
    scores = jax.nn.sigmoid(lt)
    biased = scores + b_ref[...]
    gsz = N_EXPERTS // N_GROUPS
    sub8 = lax.broadcasted_iota(jnp.int32, (gsz, tm), 0).astype(F32)
    grp_rows = []
    for g in range(N_GROUPS):
        blk = biased[g * gsz:(g + 1) * gsz, :]
        m1 = jnp.max(blk, axis=0, keepdims=True)
        i1 = jnp.min(jnp.where(blk == m1, sub8, float(gsz)), axis=0, keepdims=True)
        m2 = jnp.max(jnp.where(sub8 == i1, -jnp.inf, blk), axis=0, keepdims=True)
        grp_rows.append(m1 + m2)
    grp = jnp.concatenate(grp_rows, axis=0)
    subg = lax.broadcasted_iota(jnp.int32, (N_GROUPS, tm), 0).astype(F32)
    gsel = jnp.zeros((N_GROUPS, tm), F32)
    for _ in range(TOP_GROUPS):
        mg = jnp.max(grp, axis=0, keepdims=True)
        ig = jnp.min(jnp.where(grp == mg, subg, float(N_GROUPS)), axis=0, keepdims=True)
        hit = subg == ig
        gsel = jnp.where(hit, 1.0, gsel)
        grp = jnp.where(hit, -jnp.inf, grp)
    emask = jnp.concatenate([jnp.broadcast_to(gsel[g:g + 1, :], (gsz, tm)) for g in range(N_GROUPS)], axis=0)
    cand = jnp.where(emask > 0.5, biased, -jnp.inf)
    sube = lax.broadcasted_iota(jnp.int32, (N_EXPERTS, tm), 0).astype(F32)
    idx_rows, w_rows = [], []
    chosen = jnp.zeros((N_EXPERTS, tm), F32)
    for _ in range(TOP_K):
        mc = jnp.max(cand, axis=0, keepdims=True)
        ic = jnp.min(jnp.where(cand == mc, sube, float(N_EXPERTS)), axis=0, keepdims=True)
        hit = sube == ic
        idx_rows.append(ic)
        w_rows.append(jnp.sum(jnp.where(hit, scores, 0.0), axis=0, keepdims=True))
        chosen = jnp.where(hit, 1.0, chosen)
        cand = jnp.where(hit, -jnp.inf, cand)
    cnt_ref[...] += jnp.sum(chosen, axis=1, keepdims=True)
    wsel = jnp.concatenate(w_rows, axis=0)
    wsel = wsel / jnp.sum(wsel, axis=0, keepdims=True) * ROUTE_SCALE
    idx_ref[...] = jnp.concatenate(idx_rows, axis=0).astype(jnp.int32)
    wt_ref[...] = wsel


def _router(x, router_w, router_b, tm=512):
    m, d = x.shape
    tm = min(tm, m)
    w_pad = jnp.pad(router_w, ((0, 0), (0, LANES - N_EXPERTS)))
    return pl.pallas_call(
        _router_kernel,
        grid=(m // tm,),
        in_specs=[pl.BlockSpec((tm, d), lambda i: (i, 0)),
                  pl.BlockSpec((d, LANES), lambda i: (0, 0)),
                  pl.BlockSpec((N_EXPERTS, 1), lambda i: (0, 0))],
        out_specs=[pl.BlockSpec((TOP_K, tm), lambda i: (0, i)), pl.BlockSpec((TOP_K, tm), lambda i: (0, i)),
                   pl.BlockSpec((N_EXPERTS, LANES), lambda i: (0, 0))],
        out_shape=[jax.ShapeDtypeStruct((TOP_K, m), jnp.int32), jax.ShapeDtypeStruct((TOP_K, m), F32),
                   jax.ShapeDtypeStruct((N_EXPERTS, LANES), F32)],
        compiler_params=_params("arbitrary"),
        name="router",
    )(x, w_pad, router_b.reshape(N_EXPERTS, 1))


def _slot_kernel(idx_ref, base_ref, dest_ref, tri_ref, run_ref):
    tm = idx_ref.shape[1]

    @pl.when(pl.program_id(0) == 0)
    def _():
        row = lax.broadcasted_iota(jnp.int32, (tm, tm), 0)
        col = lax.broadcasted_iota(jnp.int32, (tm, tm), 1)
        tri_ref[...] = jnp.where(row < col, 1.0, 0.0).astype(BF16)
        run_ref[...] = base_ref[...]

    sube = lax.broadcasted_iota(jnp.int32, (N_EXPERTS, tm), 0)
    idx = idx_ref[...]
    base = run_ref[...]
    hits = [sube == idx[k:k + 1, :] for k in range(TOP_K)]
    chosen = jnp.zeros((N_EXPERTS, tm), F32)
    for hit in hits:
        chosen = jnp.where(hit, 1.0, chosen)
    before = jnp.dot(chosen.astype(BF16), tri_ref[...], preferred_element_type=F32)
    slot = base + before
    rows = [jnp.sum(jnp.where(hit, slot, 0.0), axis=0, keepdims=True) for hit in hits]
    run_ref[...] = base + jnp.sum(chosen, axis=1, keepdims=True)
    dest_ref[...] = jnp.concatenate(rows, axis=0).astype(jnp.int32)


def _assign_slots(idx_t, base, tm=512):
    kk, m = idx_t.shape
    tm = min(tm, m)
    return pl.pallas_call(
        _slot_kernel,
        grid=(m // tm,),
        in_specs=[pl.BlockSpec((kk, tm), lambda i: (0, i)), pl.BlockSpec((N_EXPERTS, 1), lambda i: (0, 0))],
        out_specs=pl.BlockSpec((kk, tm), lambda i: (0, i)),
        out_shape=jax.ShapeDtypeStruct((kk, m), jnp.int32),
        scratch_shapes=[pltpu.VMEM((tm, tm), BF16), pltpu.VMEM((N_EXPERTS, 1), F32)],
        compiler_params=_params("arbitrary"),
        name="assign_slots",
    )(idx_t, base.astype(F32).reshape(N_EXPERTS, 1))


def _expert_kernel(be_ref, nb_ref, x_ref, wg_ref, wu_ref, wd_ref, o_ref):
    del be_ref
    i = pl.program_id(0)

    @pl.when(i < nb_ref[0])
    def _():
        x = x_ref[...]
        y = None
        for c in range(wg_ref.shape[1] // MXU_WIDTH):
            cs = slice(c * MXU_WIDTH, (c + 1) * MXU_WIDTH)
            gate = jnp.dot(x, wg_ref[:, cs].astype(BF16), preferred_element_type=F32)
            up = jnp.dot(x, wu_ref[:, cs].astype(BF16), preferred_element_type=F32)
            act = (jax.nn.silu(gate) * up).astype(BF16)
            part = jnp.dot(act, wd_ref[cs, :].astype(BF16), preferred_element_type=F32)
            y = part if y is None else y + part
        o_ref[...] = y.astype(o_ref.dtype)

    @pl.when(i >= nb_ref[0])
    def _():
        o_ref[...] = jnp.zeros_like(o_ref)


def _expert_ffn(xs, block_e, nblocks, w_gate, w_up, w_down, layer, rows):
    ns, d = xs.shape
    de = w_gate.shape[3]
    grid_spec = pltpu.PrefetchScalarGridSpec(
        num_scalar_prefetch=2,
        grid=(ns // rows,),
        in_specs=[pl.BlockSpec((rows, d), lambda i, be, nb: (i, 0)),
                  pl.BlockSpec((None, None, d, de), lambda i, be, nb: (layer, be[i], 0, 0)),
                  pl.BlockSpec((None, None, d, de), lambda i, be, nb: (layer, be[i], 0, 0)),
                  pl.BlockSpec((None, None, de, d), lambda i, be, nb: (layer, be[i], 0, 0))],
        out_specs=pl.BlockSpec((rows, d), lambda i, be, nb: (i, 0)),
    )
    return pl.pallas_call(
        _expert_kernel,
        grid_spec=grid_spec,
        out_shape=jax.ShapeDtypeStruct((ns, d), BF16),
        compiler_params=_params("arbitrary"),
        name="expert_ffn",
    )(block_e, nblocks, xs, w_gate, w_up, w_down)


def _combine_ln_kernel(*refs, alpha):
    yg_refs = refs[:TOP_K]
    wt_ref, sh_ref, x_ref, g_ref, b_ref, xo_ref, xb_ref = refs[TOP_K:]
    wt = wt_ref[...]
    routed = sh_ref[...].astype(F32)
    for k in range(TOP_K):
        routed = routed + wt[:, k:k + 1] * yg_refs[k][...].astype(F32)
    xn = _layer_norm_rows(alpha * x_ref[...] + routed, g_ref[...], b_ref[...])
    xo_ref[...] = xn
    xb_ref[...] = xn.astype(BF16)


def _combine_ln(yg, wt, shared, x, g, b, alpha, tm=256):
    m, d = x.shape
    tm = min(tm, m)
    nb = m // tm
    row = pl.BlockSpec((tm, d), lambda i: (i, 0))
    vec = pl.BlockSpec((1, d), lambda i: (0, 0))
    yg_specs = [pl.BlockSpec((tm, d), functools.partial(lambda i, k: (k * nb + i, 0), k=k)) for k in range(TOP_K)]
    return pl.pallas_call(
        functools.partial(_combine_ln_kernel, alpha=alpha),
        grid=(nb,),
        in_specs=yg_specs + [pl.BlockSpec((tm, TOP_K), lambda i: (i, 0)), row, row, vec, vec],
        out_specs=[row, row],
        out_shape=[jax.ShapeDtypeStruct((m, d), F32), jax.ShapeDtypeStruct((m, d), BF16)],
        compiler_params=_params("arbitrary"),
        name="combine_ln",
    )(*([yg] * TOP_K), wt, shared, x, g.reshape(1, d), b.reshape(1, d))


def _moe_layer(x1, xb1, layer, router_w, router_b, e_gate, e_up, e_down, s_gate, s_up, s_down, g, b, alpha):
    t, d = x1.shape
    rows = min(MOE_ROWS, t)
    idx_t, wt_t, cnt = _router(x1, router_w, router_b)
    n_assign = t * TOP_K
    i32 = jnp.int32
    counts = cnt[:, 0].astype(i32)
    padded = (counts + rows - 1) // rows * rows
    ends_p = jnp.cumsum(padded)
    pad_e = padded - counts
    dest = _assign_slots(idx_t, ends_p - padded).reshape(n_assign)
    nblk = n_assign // rows + N_EXPERTS
    n_pad = nblk * rows - n_assign
    pad_i = jnp.arange(n_pad, dtype=i32)
    pad_before = jnp.cumsum(pad_e) - pad_e
    pad_slot = pad_i + jnp.sum(jnp.where(pad_before[None, :] <= pad_i[:, None], counts[None, :], 0), axis=1)
    keys = jnp.concatenate([dest, pad_slot])
    toks = jnp.concatenate([jnp.arange(n_assign, dtype=i32) % t, pad_i % t])
    _, slot_tok = lax.sort_key_val(keys, toks)
    blk0 = jnp.arange(nblk, dtype=i32) * rows
    block_e = jnp.minimum(jnp.sum((ends_p[None, :] <= blk0[:, None]).astype(i32), axis=1), N_EXPERTS - 1)
    nused = (ends_p[-1] // rows).astype(i32).reshape(1)
    xs = jnp.take(xb1, slot_tok, axis=0, mode='clip')
    ys = _expert_ffn(xs, block_e, nused, e_gate, e_up, e_down, layer, rows)
    yg = jnp.take(ys, dest, axis=0, mode='clip')
    shared = _expert_ffn(xb1, jnp.zeros((t // rows,), i32), jnp.full((1,), t // rows, i32),
                         s_gate[:, None], s_up[:, None], s_down[:, None], layer, rows)
    return _combine_ln(yg, wt_t.T, shared, x1, g, b, alpha)


def _mixer_layer(x, xb, memb, layer, tabs_a, tabs_c, batch, seq, w_in, w_c, pool_w, pool_scale, c_lambda, c_norm,
                 conv_w, w_mem_kv, merge_w, b_gate, w_out_b, g, b, alpha):
    mix = merge_w[1].shape[1]
    xw = merge_w[2].shape[0]
    a_cols, c_cols, d_cols = 3 * mix, 3 * mix, 3 * mix
    c0_b, c0_d = a_cols, a_cols + mix + c_cols
    pa = _proj(xb, w_in, layer, range(3), mix, tabs_a, seq, half=HEAD_DIM // 8,
               rope_blocks=(1,) * (mix // LANES), rope_jmax=2)
    tn_c = 6 * LANES
    pc = _proj(xb, w_c, 0, range(c_cols // tn_c), tn_c, tabs_c, seq,
               half=C_QK_DIM // 8, rope_blocks=(1, 2, 0) * 2)
    tn_p = 512
    blocks = list(range(c0_b // tn_p, c0_b // tn_p + mix // tn_p)) + \
        list(range(c0_d // tn_p, c0_d // tn_p + (d_cols + xw) // tn_p))
    pbdx = _proj(xb, w_in, layer, blocks, tn_p, tabs_a, seq)
    ya = _dilated_attention(pa, batch, seq)
    yb = _pooling_mixer(pbdx, pool_w[layer], pool_scale[layer], batch, seq, 0)
    yc = _diff_attention(pc, c_lambda[layer], c_norm[layer], batch, seq, layer)
    yd = _gated_short_conv(pbdx, conv_w[layer], batch, seq, mix)
    kv = _proj(memb, w_mem_kv, layer, range(2 * xw // tn_p), tn_p, tabs_a, seq)
    yx = _cross_attention(pbdx, kv, batch, seq, mix + d_cols, xw)
    merged = _gated_merge(xb, (ya, yb, yc, yd, yx), *merge_w, b_gate[layer])
    return _out_proj_ln(merged, w_out_b, x, g[layer], b[layer], alpha)


def kernel(x, mem, w_in, pool_w, pool_scale, c_lambda, c_norm, conv_w, w_mem_kv, w_branch, w_branch_x,
           w_gate, b_gate, w_out, ln1_g, ln1_b, router_w, router_b, e_gate, e_up, e_down, s_gate, s_up,
           s_down, ln2_g, ln2_b):
    batch, seq, d = x.shape
    depth = w_in.shape[0]
    alpha = (2.0 * depth) ** 0.25
    tabs_a = _rope_tables(seq, HEAD_DIM // 4, LANES)
    tabs_ck = _rope_tables(seq, C_QK_DIM // 4, C_QK_DIM)
    q_scale = C_QK_DIM ** -0.5 * math.log2(math.e)
    tabs_c = tuple(tab * q_scale for tab in tabs_ck) + tabs_ck
    mix = w_branch.shape[2]
    groups = BATCH_GROUPS if batch % BATCH_GROUPS == 0 else 1
    gb = batch // groups
    state = []
    for gi in range(groups):
        xg = x[gi * gb:(gi + 1) * gb].reshape(gb * seq, d)
        state.append((xg, xg.astype(BF16), mem[gi * gb:(gi + 1) * gb].reshape(-1, d).astype(BF16)))
    for l in range(depth):
        w_c = w_in[l:l + 1, :, 4 * mix:7 * mix]
        merge_w = (w_gate[l].astype(BF16), w_branch[l].astype(BF16), w_branch_x[l].astype(BF16))
        w_out_b = w_out[l].astype(BF16)
        for gi in range(groups):
            xf, xb, memb = state[gi]
            xf, xb = _mixer_layer(xf, xb, memb, l, tabs_a, tabs_c, gb, seq, w_in, w_c, pool_w, pool_scale,
                                  c_lambda, c_norm, conv_w, w_mem_kv, merge_w, b_gate, w_out_b,
                                  ln1_g, ln1_b, alpha)
            xf, xb = _moe_layer(xf, xb, l, router_w[l], router_b[l], e_gate, e_up, e_down, s_gate,
                                s_up, s_down, ln2_g[l], ln2_b[l], alpha)
            state[gi] = (xf, xb, memb)
    return jnp.concatenate([s[0] for s in state], axis=0).reshape(batch, seq, d)
```

```python
import functools
import math

import jax
import jax.numpy as jnp
from jax import lax
from jax.experimental import pallas as pl
from jax.experimental.pallas import tpu as pltpu

F32 = jnp.float32
BF16 = jnp.bfloat16

LANES = 128
MXU_WIDTH = 256
HEAD_DIM = 128
ROPE_THETA = 500000.0
A_PATTERNS = ((128, 1), (512, 4), (2048, 16))
A_HALF = 64
B_WINDOWS = (2, 4, 8, 16)
C_QK_DIM = 64
N_EXPERTS = 64
N_GROUPS = 8
TOP_GROUPS = 4
TOP_K = 8
ROUTE_SCALE = 2.5
LN_EPS = 1e-5
NEG_INF = -1e30
VMEM_LIMIT_BYTES = 56 * 1024 * 1024
MOE_ROWS = 256
BATCH_GROUPS = 1


def _params(*sem):
    return pltpu.CompilerParams(dimension_semantics=sem, vmem_limit_bytes=VMEM_LIMIT_BYTES)


def _rope_tables(seq, rot_dim, period):
    half = rot_dim // 2
    inv = 1.0 / (ROPE_THETA ** (jnp.arange(0, rot_dim, 2, dtype=F32) / rot_dim))
    ang = jnp.arange(seq, dtype=F32)[:, None] * inv[None, :]
    cos, sin = jnp.cos(ang), jnp.sin(ang)
    lane = jnp.arange(LANES) % period
    first = lane < half
    second = (lane >= half) & (lane < 2 * half)
    idx = jnp.where(first, lane, jnp.where(second, lane - half, 0))
    cos_l, sin_l = cos[:, idx], sin[:, idx]
    ct = jnp.where(first | second, cos_l, 1.0)
    s1 = jnp.where(first, -sin_l, 0.0)
    s2 = jnp.where(second, sin_l, 0.0)
    return ct.astype(F32), s1.astype(F32), s2.astype(F32)


def _proj_kernel(x_ref, w_ref, *rest, half, rope_blocks, rope_jmax):
    tab_refs, o_ref, wb_ref = rest[:-2], rest[-2], rest[-1]
    j = pl.program_id(0)
    i = pl.program_id(1)

    @pl.when(i == 0)
    def _():
        wb_ref[...] = w_ref[...].astype(BF16)

    def plain():
        o_ref[...] = jnp.dot(x_ref[...], wb_ref[...], preferred_element_type=F32).astype(o_ref.dtype)

    def roped():
        tabs = [r[...] for r in tab_refs]
        per = MXU_WIDTH // LANES
        for c in range(len(rope_blocks) // per):
            acc = jnp.dot(x_ref[...], wb_ref[:, c * MXU_WIDTH:(c + 1) * MXU_WIDTH], preferred_element_type=F32)
            for b in range(c * per, (c + 1) * per):
                use = rope_blocks[b]
                blk = acc[:, (b - c * per) * LANES:(b - c * per + 1) * LANES]
                if use:
                    ct, s1, s2 = tabs[3 * (use - 1):3 * use]
                    blk = (blk * ct + pltpu.roll(blk, LANES - half, 1) * s1
                           + pltpu.roll(blk, half, 1) * s2)
                o_ref[:, b * LANES:(b + 1) * LANES] = blk.astype(o_ref.dtype)

    if not any(rope_blocks):
        plain()
    elif rope_jmax is None:
        roped()
    else:
        pl.when(j < rope_jmax)(roped)
        pl.when(j >= rope_jmax)(plain)


def _proj(xb, w, layer, col_blocks, tn, tables, seq, *, half=0, rope_blocks=None, rope_jmax=None, tm=512):
    m, k = xb.shape
    nj = len(col_blocks)
    tm = min(tm, m)
    rope_blocks = tuple(int(u) for u in rope_blocks) if rope_blocks is not None else (0,) * (tn // LANES)
    col_blocks = tuple(col_blocks)
    first, contiguous = col_blocks[0], all(col_blocks[a] == col_blocks[0] + a for a in range(nj))
    if contiguous:
        w_map = lambda j, i: (layer, 0, first + j)
    else:
        n0 = next(a for a in range(1, nj) if col_blocks[a] != col_blocks[0] + a)
        second = col_blocks[n0]
        w_map = lambda j, i: (layer, 0, jnp.where(j < n0, first + j, second + j - n0))
    sblocks = seq // tm if seq >= tm else 1
    tab_spec = pl.BlockSpec((tm, LANES), lambda j, i: (i % sblocks, 0))
    kern = functools.partial(_proj_kernel, half=half, rope_blocks=rope_blocks, rope_jmax=rope_jmax)
    return pl.pallas_call(
        kern,
        grid=(nj, m // tm),
        in_specs=[pl.BlockSpec((tm, k), lambda j, i: (i, 0)),
                  pl.BlockSpec((None, k, tn), w_map)] + [tab_spec] * len(tables),
        out_specs=pl.BlockSpec((tm, tn), lambda j, i: (i, j)),
        out_shape=jax.ShapeDtypeStruct((m, nj * tn), BF16),
        scratch_shapes=[pltpu.VMEM((k, tn), BF16)],
        compiler_params=_params("arbitrary", "arbitrary"),
        name="proj",
    )(xb, w, *tables)


def _dilated_kernel(q_ref, kp_ref, km_ref, kn_ref, vp_ref, vm_ref, vn_ref, o_ref, lse_ref, *, tl, seq_len, heads):
    t = pl.program_id(2)
    l0 = t * tl
    sub = 128
    nsub = tl // sub
    scale = HEAD_DIM ** -0.5
    iq = lax.broadcasted_iota(jnp.int32, (sub, sub + 2 * A_HALF), 0)
    ik = lax.broadcasted_iota(jnp.int32, (sub, sub + 2 * A_HALF), 1)
    band = jnp.abs(ik - A_HALF - iq) <= A_HALF
    lane = lax.broadcasted_iota(jnp.int32, (sub, LANES), 1)
    valid = []
    for a in range(nsub):
        kk = l0 + a * sub - A_HALF + ik
        valid.append(band & (kk >= 0) & (kk < seq_len))
    lse_rows = [jnp.zeros((sub, LANES), F32) for _ in range(nsub)]
    for h in range(heads):
        cs = slice(h * HEAD_DIM, (h + 1) * HEAD_DIM)
        kcat = jnp.concatenate([kp_ref[:, cs], km_ref[:, cs], kn_ref[:, cs]], axis=0)
        vcat = jnp.concatenate([vp_ref[:, cs], vm_ref[:, cs], vn_ref[:, cs]], axis=0)
        for a in range(nsub):
            q = q_ref[a * sub:(a + 1) * sub, cs]
            kw = kcat[a * sub:a * sub + sub + 2 * A_HALF]
            vw = vcat[a * sub:a * sub + sub + 2 * A_HALF]
            s = lax.dot_general(q, kw, (((1,), (1,)), ((), ())), preferred_element_type=F32) * scale
            s = jnp.where(valid[a], s, NEG_INF)
            m = jnp.max(s, axis=-1, keepdims=True)
            p = jnp.exp(s - m)
            den = jnp.sum(p, axis=-1, keepdims=True)
            o = jnp.dot(p.astype(BF16), vw, preferred_element_type=F32) / den
            o_ref[a * sub:(a + 1) * sub, cs] = o.astype(o_ref.dtype)
            lse_rows[a] = jnp.where(lane == h, m + jnp.log(den), lse_rows[a])
    for a in range(nsub):
        lse_ref[a * sub:(a + 1) * sub, :] = lse_rows[a]


def _dilated_pattern(pa, batch, seq, dil):
    width = pa.shape[1] // 3
    heads = width // HEAD_DIM
    sl = seq // dil
    tl = min(256, sl)
    nt = sl // tl
    hb = tl // A_HALF
    last_halo = sl // A_HALF - 1
    view = pa.reshape(batch, sl, dil * 3 * width)

    def main(c):
        return pl.BlockSpec((None, tl, width), lambda b, r, t: (b, t, r * 3 + c))

    def prev(c):
        return pl.BlockSpec((None, A_HALF, width), lambda b, r, t: (b, jnp.maximum(t * hb - 1, 0), r * 3 + c))

    def nxt(c):
        return pl.BlockSpec((None, A_HALF, width),
                            lambda b, r, t: (b, jnp.minimum((t + 1) * hb, last_halo), r * 3 + c))

    kern = functools.partial(_dilated_kernel, tl=tl, seq_len=sl, heads=heads)
    o, lse = pl.pallas_call(
        kern,
        grid=(batch, dil, nt),
        in_specs=[main(0), prev(1), main(1), nxt(1), prev(2), main(2), nxt(2)],
        out_specs=[pl.BlockSpec((None, tl, width), lambda b, r, t: (b, t, r)),
                   pl.BlockSpec((None, tl, LANES), lambda b, r, t: (b, t, r))],
        out_shape=[jax.ShapeDtypeStruct((batch, sl, dil * width), BF16),
                   jax.ShapeDtypeStruct((batch, sl, dil * LANES), F32)],
        compiler_params=_params("arbitrary", "arbitrary", "arbitrary"),
        name=f"dilated_d{dil}",
    )(view, view, view, view, view, view, view)
    return o.reshape(batch * seq, width), lse.reshape(batch * seq, LANES)


def _dilated_combine_kernel(o0_ref, o1_ref, o2_ref, l0_ref, l1_ref, l2_ref, y_ref, *, heads):
    l0, l1, l2 = l0_ref[...], l1_ref[...], l2_ref[...]
    m = jnp.maximum(jnp.maximum(l0, l1), l2)
    e0, e1, e2 = jnp.exp(l0 - m), jnp.exp(l1 - m), jnp.exp(l2 - m)
    inv = 1.0 / (e0 + e1 + e2)
    w0, w1, w2 = e0 * inv, e1 * inv, e2 * inv
    for h in range(heads):
        cs = slice(h * HEAD_DIM, (h + 1) * HEAD_DIM)
        y = (w0[:, h:h + 1] * o0_ref[:, cs].astype(F32) + w1[:, h:h + 1] * o1_ref[:, cs].astype(F32)
             + w2[:, h:h + 1] * o2_ref[:, cs].astype(F32))
        y_ref[:, cs] = y.astype(y_ref.dtype)


def _dilated_attention(pa, batch, seq):
    outs, lses = zip(*[_dilated_pattern(pa, batch, seq, d) for _, d in A_PATTERNS])
    m, width = outs[0].shape
    tm = min(512, m)
    ospec = pl.BlockSpec((tm, width), lambda i: (i, 0))
    lspec = pl.BlockSpec((tm, LANES), lambda i: (i, 0))
    return pl.pallas_call(
        functools.partial(_dilated_combine_kernel, heads=width // HEAD_DIM),
        grid=(m // tm,),
        in_specs=[ospec, ospec, ospec, lspec, lspec, lspec],
        out_specs=ospec,
        out_shape=jax.ShapeDtypeStruct((m, width), BF16),
        compiler_params=_params("arbitrary"),
        name="dilated_combine",
    )(*outs, *lses)


def _pool_kernel(u_ref, w_ref, sc_ref, o_ref, pad_ref, *, seq):
    g = pl.program_id(1)
    pad = 8
    uf = u_ref[...].astype(F32)
    pad_ref[0:pad, :] = jnp.zeros((pad, uf.shape[1]), F32)
    pad_ref[pad + seq:pad + seq + pad, :] = jnp.zeros((pad, uf.shape[1]), F32)
    pad_ref[pad:pad + seq, :] = uf
    pos = lax.broadcasted_iota(jnp.int32, (seq, 1), 0)
    for gi, win in enumerate(B_WINDOWS):
        @pl.when(g == gi)
        def _(win=win):
            before, after = win // 2, win - win // 2
            tot = pad_ref[pad - before:pad - before + seq, :]
            for off in range(-before + 1, after):
                tot = tot + pad_ref[pad + off:pad + off + seq, :]
            cnt = (jnp.minimum(pos + after, seq) - jnp.maximum(pos - before, 0)).astype(F32)
            pooled = tot / cnt - uf
            mixed = jnp.dot(pooled.astype(BF16), w_ref[...].astype(BF16), preferred_element_type=F32)
            o_ref[...] = (mixed * sc_ref[...]).astype(o_ref.dtype)


def _pooling_mixer(pbdx, pool_w, pool_scale, batch, seq, col0):
    ng, cg = pool_w.shape[0], pool_w.shape[1]
    view = pbdx.reshape(batch, seq, pbdx.shape[1])
    cb = col0 // cg
    return pl.pallas_call(
        functools.partial(_pool_kernel, seq=seq),
        grid=(batch, ng),
        in_specs=[pl.BlockSpec((None, seq, cg), lambda b, g: (b, 0, cb + g)),
                  pl.BlockSpec((None, cg, cg), lambda b, g: (g, 0, 0)),
                  pl.BlockSpec((1, cg), lambda b, g: (0, g))],
        out_specs=pl.BlockSpec((None, seq, cg), lambda b, g: (b, 0, g)),
        out_shape=jax.ShapeDtypeStruct((batch, seq, ng * cg), BF16),
        scratch_shapes=[pltpu.VMEM((seq + 16, cg), F32)],
        compiler_params=_params("arbitrary", "arbitrary"),
        name="pooling",
    )(view, pool_w, pool_scale.reshape(1, ng * cg)).reshape(batch * seq, ng * cg)


def _conv_kernel(bg_ref, cg_ref, h_ref, w_ref, o_ref, pad_ref, *, seq):
    pad = 8
    u = cg_ref[...].astype(F32) * h_ref[...].astype(F32)
    cols = u.shape[1]
    pad_ref[0:pad, :] = jnp.zeros((pad, cols), F32)
    pad_ref[pad + seq:pad + seq + pad, :] = jnp.zeros((pad, cols), F32)
    pad_ref[pad:pad + seq, :] = u
    w = w_ref[...]
    conv = (pad_ref[pad - 1:pad - 1 + seq, :] * w[0:1, :] + u * w[1:2, :]
            + pad_ref[pad + 1:pad + 1 + seq, :] * w[2:3, :])
    o_ref[...] = (bg_ref[...].astype(F32) * conv).astype(o_ref.dtype)


def _gated_short_conv(pbdx, conv_w, batch, seq, col0):
    width = conv_w.shape[1]
    cb = 256
    nb = width // cb
    view = pbdx.reshape(batch, seq, pbdx.shape[1])
    base = col0 // cb

    def spec(part):
        return pl.BlockSpec((None, seq, cb), lambda b, c: (b, 0, base + part * nb + c))

    return pl.pallas_call(
        functools.partial(_conv_kernel, seq=seq),
        grid=(batch, nb),
        in_specs=[spec(0), spec(1), spec(2), pl.BlockSpec((3, cb), lambda b, c: (0, c))],
        out_specs=pl.BlockSpec((None, seq, cb), lambda b, c: (b, 0, c)),
        out_shape=jax.ShapeDtypeStruct((batch, seq, width), BF16),
        scratch_shapes=[pltpu.VMEM((seq + 16, cb), F32)],
        compiler_params=_params("arbitrary", "arbitrary"),
        name="short_conv",
    )(view, view, view, conv_w).reshape(batch * seq, width)


def _diff_kernel(q_ref, k_ref, v_ref, lam_ref, nrm_ref, o_ref, vone_ref, *, lam_init, chunk):
    t = pl.program_id(2)
    seq = k_ref.shape[0]
    tq = q_ref.shape[0]

    @pl.when(t == 0)
    def _():
        vone_ref[:, :LANES] = v_ref[...]
        vone_ref[:, LANES:] = jnp.ones((seq, LANES), BF16)

    lq = lam_ref[...].astype(F32)
    lam = (jnp.exp(jnp.sum(lq[0:1] * lq[1:2], axis=-1, keepdims=True))
           - jnp.exp(jnp.sum(lq[2:3] * lq[3:4], axis=-1, keepdims=True)) + lam_init)
    q = q_ref[...]
    lane = lax.broadcasted_iota(jnp.int32, q.shape, 1)
    zero = jnp.zeros_like(q)
    q2 = jnp.concatenate([jnp.where(lane < C_QK_DIM, q, zero), jnp.where(lane >= C_QK_DIM, q, zero)], axis=0)
    m = acc = None
    for c in range(seq // chunk):
        ks = slice(c * chunk, (c + 1) * chunk)
        s = lax.dot_general(q2, k_ref[ks, :], (((1,), (1,)), ((), ())), preferred_element_type=F32)
        mc = jnp.max(s, axis=-1, keepdims=True)
        m_new = mc if m is None else jnp.maximum(m, mc)
        e = jnp.exp2((s - m_new).astype(BF16))
        pv = jnp.dot(e, vone_ref[ks, :], preferred_element_type=F32)
        acc = pv if acc is None else acc * jnp.exp2(m - m_new) + pv
        m = m_new
    out = acc[:, :LANES] / acc[:, LANES:]
    o = out[:tq] - lam * out[tq:]
    o = o * lax.rsqrt(jnp.mean(o * o, axis=-1, keepdims=True) + LN_EPS) * nrm_ref[...] * (1.0 - lam_init)
    o_ref[...] = o.astype(o_ref.dtype)


def _diff_attention(pc, c_lambda, c_norm, batch, seq, layer, tq=1024, chunk=1024):
    heads = pc.shape[1] // (3 * LANES)
    tq = min(tq, seq)
    chunk = min(chunk, seq)
    lam_init = 0.8 - 0.6 * math.exp(-0.3 * layer)
    view = pc.reshape(batch, seq, pc.shape[1])
    return pl.pallas_call(
        functools.partial(_diff_kernel, lam_init=lam_init, chunk=chunk),
        grid=(batch, heads, seq // tq),
        in_specs=[pl.BlockSpec((None, tq, LANES), lambda b, h, t: (b, t, 3 * h)),
                  pl.BlockSpec((None, seq, LANES), lambda b, h, t: (b, 0, 3 * h + 1)),
                  pl.BlockSpec((None, seq, LANES), lambda b, h, t: (b, 0, 3 * h + 2)),
                  pl.BlockSpec((4, C_QK_DIM), lambda b, h, t: (0, 0)),
                  pl.BlockSpec((1, LANES), lambda b, h, t: (0, 0))],
        out_specs=pl.BlockSpec((None, tq, LANES), lambda b, h, t: (b, t, h)),
        out_shape=jax.ShapeDtypeStruct((batch, seq, heads * LANES), BF16),
        scratch_shapes=[pltpu.VMEM((seq, 2 * LANES), BF16)],
        compiler_params=_params("arbitrary", "arbitrary", "arbitrary"),
        name="diff_attention",
    )(view, view, view, c_lambda, c_norm.reshape(1, LANES)).reshape(batch * seq, heads * LANES)


def _cross_kernel(q_ref, kv_ref, o_ref, *, heads):
    scale = HEAD_DIM ** -0.5
    width = heads * HEAD_DIM
    for h in range(heads):
        cs = slice(h * HEAD_DIM, (h + 1) * HEAD_DIM)
        k = kv_ref[:, h * HEAD_DIM:(h + 1) * HEAD_DIM]
        v = kv_ref[:, width + h * HEAD_DIM:width + (h + 1) * HEAD_DIM]
        s = lax.dot_general(q_ref[:, cs], k, (((1,), (1,)), ((), ())), preferred_element_type=F32) * scale
        m = jnp.max(s, axis=-1, keepdims=True)
        e = jnp.exp(s - m)
        den = jnp.sum(e, axis=-1, keepdims=True)
        o = jnp.dot(e.astype(BF16), v, preferred_element_type=F32) / den
        o_ref[:, cs] = o.astype(o_ref.dtype)


def _cross_attention(pbdx, kv, batch, seq, col0, width, tq=512):
    mem_len = kv.shape[0] // batch
    tq = min(tq, seq)
    view = pbdx.reshape(batch, seq, pbdx.shape[1])
    return pl.pallas_call(
        functools.partial(_cross_kernel, heads=width // HEAD_DIM),
        grid=(batch, seq // tq),
        in_specs=[pl.BlockSpec((None, tq, width), lambda b, t: (b, t, col0 // width)),
                  pl.BlockSpec((None, mem_len, 2 * width), lambda b, t: (b, 0, 0))],
        out_specs=pl.BlockSpec((None, tq, width), lambda b, t: (b, t, 0)),
        out_shape=jax.ShapeDtypeStruct((batch, seq, width), BF16),
        compiler_params=_params("arbitrary", "arbitrary"),
        name="cross_attention",
    )(view, kv.reshape(batch, mem_len, 2 * width)).reshape(batch * seq, width)


def _merge_kernel(x_ref, ya_ref, yb_ref, yc_ref, yd_ref, yx_ref,
                  g0, g1, g2, g3, g4, p0, p1, p2, p3, p4, b0, b1, b2, b3, b4, o_ref):
    x = x_ref[...]
    merged = None
    for y_ref, g_ref, p_ref, b_ref in ((ya_ref, g0, p0, b0), (yb_ref, g1, p1, b1), (yc_ref, g2, p2, b2),
                                       (yd_ref, g3, p3, b3), (yx_ref, g4, p4, b4)):
        gate = jax.nn.sigmoid(jnp.dot(x, g_ref[...], preferred_element_type=F32) + b_ref[...])
        term = gate * jnp.dot(y_ref[...], p_ref[...], preferred_element_type=F32)
        merged = term if merged is None else merged + term
    o_ref[...] = merged.astype(o_ref.dtype)


def _gated_merge(xb, ys, w_gate, w_branch, w_branch_x, b_gate, tm=512, tn=512):
    m, d = xb.shape
    tm = min(tm, m)
    nb = d // tn
    n_br = 5
    act = [pl.BlockSpec((tm, d), lambda j, i: (i, 0))]
    act += [pl.BlockSpec((tm, y.shape[1]), lambda j, i: (i, 0)) for y in ys]
    gates = [pl.BlockSpec((d, tn), functools.partial(lambda j, i, br: (0, br * nb + j), br=br)) for br in range(n_br)]
    projs = [pl.BlockSpec((None, w_branch.shape[1], tn), functools.partial(lambda j, i, br: (br, 0, j), br=br))
             for br in range(4)]
    projs.append(pl.BlockSpec((w_branch_x.shape[0], tn), lambda j, i: (0, j)))
    biases = [pl.BlockSpec((1, tn), functools.partial(lambda j, i, br: (0, br * nb + j), br=br)) for br in range(n_br)]
    bg2 = b_gate.reshape(1, n_br * d)
    return pl.pallas_call(
        _merge_kernel,
        grid=(nb, m // tm),
        in_specs=act + gates + projs + biases,
        out_specs=pl.BlockSpec((tm, tn), lambda j, i: (i, j)),
        out_shape=jax.ShapeDtypeStruct((m, d), BF16),
        compiler_params=_params("arbitrary", "arbitrary"),
        name="gated_merge",
    )(xb, *ys, *([w_gate] * n_br), *([w_branch] * 4), w_branch_x, *([bg2] * n_br))


def _layer_norm_rows(h, g, b):
    mu = jnp.mean(h, axis=-1, keepdims=True)
    hc = h - mu
    var = jnp.mean(hc * hc, axis=-1, keepdims=True)
    return hc * lax.rsqrt(var + LN_EPS) * g + b


def _out_ln_kernel(mg_ref, w_ref, x_ref, g_ref, b_ref, xo_ref, xb_ref, *, alpha):
    y = jnp.dot(mg_ref[...], w_ref[...], preferred_element_type=F32)
    xn = _layer_norm_rows(alpha * x_ref[...] + y, g_ref[...], b_ref[...])
    xo_ref[...] = xn
    xb_ref[...] = xn.astype(BF16)


def _out_proj_ln(merged, w_out_b, x, g, b, alpha, tm=256):
    m, d = x.shape
    tm = min(tm, m)
    row = pl.BlockSpec((tm, d), lambda i: (i, 0))
    vec = pl.BlockSpec((1, d), lambda i: (0, 0))
    return pl.pallas_call(
        functools.partial(_out_ln_kernel, alpha=alpha),
        grid=(m // tm,),
        in_specs=[row, pl.BlockSpec((d, d), lambda i: (0, 0)), row, vec, vec],
        out_specs=[row, row],
        out_shape=[jax.ShapeDtypeStruct((m, d), F32), jax.ShapeDtypeStruct((m, d), BF16)],
        compiler_params=_params("arbitrary"),
        name="out_proj_ln",
    )(merged, w_out_b, x, g.reshape(1, d), b.reshape(1, d))


def _split_bf16(a):
    hi = a.astype(BF16)
    lo = (a - hi.astype(F32)).astype(BF16)
    return hi, lo


def _router_kernel(x_ref, w_ref, b_ref, idx_ref, wt_ref, cnt_ref):
    @pl.when(pl.program_id(0) == 0)
    def _():
        cnt_ref[...] = jnp.zeros_like(cnt_ref)

    x = x_ref[...]
    w = w_ref[...]
    xh, xl = _split_bf16(x)
    wh, wl = _split_bf16(w)
    logits = (jnp.dot(xh, wh, preferred_element_type=F32) + jnp.dot(xl, wh, preferred_element_type=F32)
              + jnp.dot(xh, wl, preferred_element_type=F32))
    lt = logits.T[:N_EXPERTS, :]
    tm = lt.shape[1]
    scores = jax.nn.sigmoid(lt)
    biased = scores + b_ref[...]
    gsz = N_EXPERTS // N_GROUPS
    sub8 = lax.broadcasted_iota(jnp.int32, (gsz, tm), 0).astype(F32)
    grp_rows = []
    for g in range(N_GROUPS):
        blk = biased[g * gsz:(g + 1) * gsz, :]
        m1 = jnp.max(blk, axis=0, keepdims=True)
        i1 = jnp.min(jnp.where(blk == m1, sub8, float(gsz)), axis=0, keepdims=True)
        m2 = jnp.max(jnp.where(sub8 == i1, -jnp.inf, blk), axis=0, keepdims=True)
        grp_rows.append(m1 + m2)
    grp = jnp.concatenate(grp_rows, axis=0)
    subg = lax.broadcasted_iota(jnp.int32, (N_GROUPS, tm), 0).astype(F32)
    gsel = jnp.zeros((N_GROUPS, tm), F32)
    for _ in range(TOP_GROUPS):
        mg = jnp.max(grp, axis=0, keepdims=True)
        ig = jnp.min(jnp.where(grp == mg, subg, float(N_GROUPS)), axis=0, keepdims=True)
        hit = subg == ig
        gsel = jnp.where(hit, 1.0, gsel)
        grp = jnp.where(hit, -jnp.inf, grp)
    emask = jnp.concatenate([jnp.broadcast_to(gsel[g:g + 1, :], (gsz, tm)) for g in range(N_GROUPS)], axis=0)
    cand = jnp.where(emask > 0.5, biased, -jnp.inf)
    sube = lax.broadcasted_iota(jnp.int32, (N_EXPERTS, tm), 0).astype(F32)
    idx_rows, w_rows = [], []
    chosen = jnp.zeros((N_EXPERTS, tm), F32)
    for _ in range(TOP_K):
        mc = jnp.max(cand, axis=0, keepdims=True)
        ic = jnp.min(jnp.where(cand == mc, sube, float(N_EXPERTS)), axis=0, keepdims=True)
        hit = sube == ic
        idx_rows.append(ic)
        w_rows.append(jnp.sum(jnp.where(hit, scores, 0.0), axis=0, keepdims=True))
        chosen = jnp.where(hit, 1.0, chosen)
        cand = jnp.where(hit, -jnp.inf, cand)
    cnt_ref[...] += jnp.sum(chosen, axis=1, keepdims=True)
    wsel = jnp.concatenate(w_rows, axis=0)
    wsel = wsel / jnp.sum(wsel, axis=0, keepdims=True) * ROUTE_SCALE
    idx_ref[...] = jnp.concatenate(idx_rows, axis=0).astype(jnp.int32)
    wt_ref[...] = wsel


def _router(x, router_w, router_b, tm=512):
    m, d = x.shape
    tm = min(tm, m)
    w_pad = jnp.pad(router_w, ((0, 0), (0, LANES - N_EXPERTS)))
    return pl.pallas_call(
        _router_kernel,
        grid=(m // tm,),
        in_specs=[pl.BlockSpec((tm, d), lambda i: (i, 0)),
                  pl.BlockSpec((d, LANES), lambda i: (0, 0)),
                  pl.BlockSpec((N_EXPERTS, 1), lambda i: (0, 0))],
        out_specs=[pl.BlockSpec((TOP_K, tm), lambda i: (0, i)), pl.BlockSpec((TOP_K, tm), lambda i: (0, i)),
                   pl.BlockSpec((N_EXPERTS, LANES), lambda i: (0, 0))],
        out_shape=[jax.ShapeDtypeStruct((TOP_K, m), jnp.int32), jax.ShapeDtypeStruct((TOP_K, m), F32),
                   jax.ShapeDtypeStruct((N_EXPERTS, LANES), F32)],
        compiler_params=_params("arbitrary"),
        name="router",
    )(x, w_pad, router_b.reshape(N_EXPERTS, 1))


def _slot_kernel(idx_ref, base_ref, dest_ref, tri_ref, run_ref):
    tm = idx_ref.shape[1]

    @pl.when(pl.program_id(0) == 0)
    def _():
        row = lax.broadcasted_iota(jnp.int32, (tm, tm), 0)
        col = lax.broadcasted_iota(jnp.int32, (tm, tm), 1)
        tri_ref[...] = jnp.where(row < col, 1.0, 0.0).astype(BF16)
        run_ref[...] = base_ref[...]

    sube = lax.broadcasted_iota(jnp.int32, (N_EXPERTS, tm), 0)
    idx = idx_ref[...]
    base = run_ref[...]
    hits = [sube == idx[k:k + 1, :] for k in range(TOP_K)]
    chosen = jnp.zeros((N_EXPERTS, tm), F32)
    for hit in hits:
        chosen = jnp.where(hit, 1.0, chosen)
    before = jnp.dot(chosen.astype(BF16), tri_ref[...], preferred_element_type=F32)
    slot = base + before
    rows = [jnp.sum(jnp.where(hit, slot, 0.0), axis=0, keepdims=True) for hit in hits]
    run_ref[...] = base + jnp.sum(chosen, axis=1, keepdims=True)
    dest_ref[...] = jnp.concatenate(rows, axis=0).astype(jnp.int32)


def _assign_slots(idx_t, base, tm=512):
    kk, m = idx_t.shape
    tm = min(tm, m)
    return pl.pallas_call(
        _slot_kernel,
        grid=(m // tm,),
        in_specs=[pl.BlockSpec((kk, tm), lambda i: (0, i)), pl.BlockSpec((N_EXPERTS, 1), lambda i: (0, 0))],
        out_specs=pl.BlockSpec((kk, tm), lambda i: (0, i)),
        out_shape=jax.ShapeDtypeStruct((kk, m), jnp.int32),
        scratch_shapes=[pltpu.VMEM((tm, tm), BF16), pltpu.VMEM((N_EXPERTS, 1), F32)],
        compiler_params=_params("arbitrary"),
        name="assign_slots",
    )(idx_t, base.astype(F32).reshape(N_EXPERTS, 1))


def _expert_kernel(be_ref, nb_ref, x_ref, wg_ref, wu_ref, wd_ref, o_ref):
    del be_ref
    i = pl.program_id(0)

    @pl.when(i < nb_ref[0])
    def _():
        x = x_ref[...]
        y = None
        for c in range(wg_ref.shape[1] // MXU_WIDTH):
            cs = slice(c * MXU_WIDTH, (c + 1) * MXU_WIDTH)
            gate = jnp.dot(x, wg_ref[:, cs].astype(BF16), preferred_element_type=F32)
            up = jnp.dot(x, wu_ref[:, cs].astype(BF16), preferred_element_type=F32)
            act = (jax.nn.silu(gate) * up).astype(BF16)
            part = jnp.dot(act, wd_ref[cs, :].astype(BF16), preferred_element_type=F32)
            y = part if y is None else y + part
        o_ref[...] = y.astype(o_ref.dtype)

    @pl.when(i >= nb_ref[0])
    def _():
        o_ref[...] = jnp.zeros_like(o_ref)


def _expert_ffn(xs, block_e, nblocks, w_gate, w_up, w_down, layer, rows):
    ns, d = xs.shape
    de = w_gate.shape[3]
    grid_spec = pltpu.PrefetchScalarGridSpec(
        num_scalar_prefetch=2,
        grid=(ns // rows,),
        in_specs=[pl.BlockSpec((rows, d), lambda i, be, nb: (i, 0)),
                  pl.BlockSpec((None, None, d, de), lambda i, be, nb: (layer, be[i], 0, 0)),
                  pl.BlockSpec((None, None, d, de), lambda i, be, nb: (layer, be[i], 0, 0)),
                  pl.BlockSpec((None, None, de, d), lambda i, be, nb: (layer, be[i], 0, 0))],
        out_specs=pl.BlockSpec((rows, d), lambda i, be, nb: (i, 0)),
    )
    return pl.pallas_call(
        _expert_kernel,
        grid_spec=grid_spec,
        out_shape=jax.ShapeDtypeStruct((ns, d), BF16),
        compiler_params=_params("arbitrary"),
        name="expert_ffn",
    )(block_e, nblocks, xs, w_gate, w_up, w_down)


def _combine_ln_kernel(*refs, alpha):
    yg_refs = refs[:TOP_K]
    wt_ref, sh_ref, x_ref, g_ref, b_ref, xo_ref, xb_ref = refs[TOP_K:]
    wt = wt_ref[...]
    routed = sh_ref[...].astype(F32)
    for k in range(TOP_K):
        routed = routed + wt[:, k:k + 1] * yg_refs[k][...].astype(F32)
    xn = _layer_norm_rows(alpha * x_ref[...] + routed, g_ref[...], b_ref[...])
    xo_ref[...] = xn
    xb_ref[...] = xn.astype(BF16)


def _combine_ln(yg, wt, shared, x, g, b, alpha, tm=256):
    m, d = x.shape
    tm = min(tm, m)
    nb = m // tm
    row = pl.BlockSpec((tm, d), lambda i: (i, 0))
    vec = pl.BlockSpec((1, d), lambda i: (0, 0))
    yg_specs = [pl.BlockSpec((tm, d), functools.partial(lambda i, k: (k * nb + i, 0), k=k)) for k in range(TOP_K)]
    return pl.pallas_call(
        functools.partial(_combine_ln_kernel, alpha=alpha),
        grid=(nb,),
        in_specs=yg_specs + [pl.BlockSpec((tm, TOP_K), lambda i: (i, 0)), row, row, vec, vec],
        out_specs=[row, row],
        out_shape=[jax.ShapeDtypeStruct((m, d), F32), jax.ShapeDtypeStruct((m, d), BF16)],
        compiler_params=_params("arbitrary"),
        name="combine_ln",
    )(*([yg] * TOP_K), wt, shared, x, g.reshape(1, d), b.reshape(1, d))


def _moe_layer(x1, xb1, layer, router_w, router_b, e_gate, e_up, e_down, s_gate, s_up, s_down, g, b, alpha):
    t, d = x1.shape
    rows = min(MOE_ROWS, t)
    idx_t, wt_t, cnt = _router(x1, router_w, router_b)
    n_assign = t * TOP_K
    i32 = jnp.int32
    counts = cnt[:, 0].astype(i32)
    padded = (counts + rows - 1) // rows * rows
    ends_p = jnp.cumsum(padded)
    pad_e = padded - counts
    dest = _assign_slots(idx_t, ends_p - padded).reshape(n_assign)
    nblk = n_assign // rows + N_EXPERTS
    n_pad = nblk * rows - n_assign
    pad_i = jnp.arange(n_pad, dtype=i32)
    pad_before = jnp.cumsum(pad_e) - pad_e
    pad_slot = pad_i + jnp.sum(jnp.where(pad_before[None, :] <= pad_i[:, None], counts[None, :], 0), axis=1)
    keys = jnp.concatenate([dest, pad_slot])
    toks = jnp.concatenate([jnp.arange(n_assign, dtype=i32) % t, pad_i % t])
    _, slot_tok = lax.sort_key_val(keys, toks)
    blk0 = jnp.arange(nblk, dtype=i32) * rows
    block_e = jnp.minimum(jnp.sum((ends_p[None, :] <= blk0[:, None]).astype(i32), axis=1), N_EXPERTS - 1)
    nused = (ends_p[-1] // rows).astype(i32).reshape(1)
    xs = jnp.take(xb1, slot_tok, axis=0, mode='clip')
    ys = _expert_ffn(xs, block_e, nused, e_gate, e_up, e_down, layer, rows)
    yg = jnp.take(ys, dest, axis=0, mode='clip')
    srows = min(2 * MOE_ROWS, t)
    shared = _expert_ffn(xb1, jnp.zeros((t // srows,), i32), jnp.full((1,), t // srows, i32),
                         s_gate[:, None], s_up[:, None], s_down[:, None], layer, srows)
    return _combine_ln(yg, wt_t.T, shared, x1, g, b, alpha)


def _mixer_layer(x, xb, memb, layer, tabs_a, tabs_c, batch, seq, w_in, w_c, pool_w, pool_scale, c_lambda, c_norm,
                 conv_w, w_mem_kv, merge_w, b_gate, w_out_b, g, b, alpha):
    mix = merge_w[1].shape[1]
    xw = merge_w[2].shape[0]
    a_cols, c_cols, d_cols = 3 * mix, 3 * mix, 3 * mix
    c0_b, c0_d = a_cols, a_cols + mix + c_cols
    pa = _proj(xb, w_in, layer, range(3), mix, tabs_a, seq, half=HEAD_DIM // 8,
               rope_blocks=(1,) * (mix // LANES), rope_jmax=2)
    tn_c = 6 * LANES
    pc = _proj(xb, w_c, 0, range(c_cols // tn_c), tn_c, tabs_c, seq,
               half=C_QK_DIM // 8, rope_blocks=(1, 2, 0) * 2)
    tn_p = 512
    blocks = list(range(c0_b // tn_p, c0_b // tn_p + mix // tn_p)) + \
        list(range(c0_d // tn_p, c0_d // tn_p + (d_cols + xw) // tn_p))
    pbdx = _proj(xb, w_in, layer, blocks, tn_p, tabs_a, seq)
    ya = _dilated_attention(pa, batch, seq)
    yb = _pooling_mixer(pbdx, pool_w[layer], pool_scale[layer], batch, seq, 0)
    yc = _diff_attention(pc, c_lambda[layer], c_norm[layer], batch, seq, layer)
    yd = _gated_short_conv(pbdx, conv_w[layer], batch, seq, mix)
    kv = _proj(memb, w_mem_kv, layer, range(2 * xw // tn_p), tn_p, tabs_a, seq)
    yx = _cross_attention(pbdx, kv, batch, seq, mix + d_cols, xw)
    merged = _gated_merge(xb, (ya, yb, yc, yd, yx), *merge_w, b_gate[layer])
    return _out_proj_ln(merged, w_out_b, x, g[layer], b[layer], alpha)


def kernel(x, mem, w_in, pool_w, pool_scale, c_lambda, c_norm, conv_w, w_mem_kv, w_branch, w_branch_x,
           w_gate, b_gate, w_out, ln1_g, ln1_b, router_w, router_b, e_gate, e_up, e_down, s_gate, s_up,
           s_down, ln2_g, ln2_b):
    batch, seq, d = x.shape
    depth = w_in.shape[0]
    alpha = (2.0 * depth) ** 0.25
    tabs_a = _rope_tables(seq, HEAD_DIM // 4, LANES)
    tabs_ck = _rope_tables(seq, C_QK_DIM // 4, C_QK_DIM)
    q_scale = C_QK_DIM ** -0.5 * math.log2(math.e)
    tabs_c = tuple(tab * q_scale for tab in tabs_ck) + tabs_ck
    mix = w_branch.shape[2]
    groups = BATCH_GROUPS if batch % BATCH_GROUPS == 0 else 1
    gb = batch // groups
    state = []
    for gi in range(groups):
        xg = x[gi * gb:(gi + 1) * gb].reshape(gb * seq, d)
        state.append((xg, xg.astype(BF16), mem[gi * gb:(gi + 1) * gb].reshape(-1, d).astype(BF16)))
    for l in range(depth):
        w_c = w_in[l:l + 1, :, 4 * mix:7 * mix]
        merge_w = (w_gate[l].astype(BF16), w_branch[l].astype(BF16), w_branch_x[l].astype(BF16))
        w_out_b = w_out[l].astype(BF16)
        for gi in range(groups):
            xf, xb, memb = state[gi]
            xf, xb = _mixer_layer(xf, xb, memb, l, tabs_a, tabs_c, gb, seq, w_in, w_c, pool_w, pool_scale,
                                  c_lambda, c_norm, conv_w, w_mem_kv, merge_w, b_gate, w_out_b,
                                  ln1_g, ln1_b, alpha)
            xf, xb = _moe_layer(xf, xb, l, router_w[l], router_b[l], e_gate, e_up, e_down, s_gate,
                                s_up, s_down, ln2_g[l], ln2_b[l], alpha)
            state[gi] = (xf, xb, memb)
    return jnp.concatenate([s[0] for s in state], axis=0).reshape(batch, seq, d)
```

```python
import functools
import math

import jax
import jax.numpy as jnp
from jax import lax
from jax.experimental import pallas as pl
from jax.experimental.pallas import tpu as pltpu

F32 = jnp.float32
BF16 = jnp.bfloat16

LANES = 128
MXU_WIDTH = 256
HEAD_DIM = 128
ROPE_THETA = 500000.0
A_PATTERNS = ((128, 1), (512, 4), (2048, 16))
A_HALF = 64
B_WINDOWS = (2, 4, 8, 16)
C_QK_DIM = 64
N_EXPERTS = 64
N_GROUPS = 8
TOP_GROUPS = 4
TOP_K = 8
ROUTE_SCALE = 2.5
LN_EPS = 1e-5
NEG_INF = -1e30
VMEM_LIMIT_BYTES = 56 * 1024 * 1024
MOE_ROWS = 256
BATCH_GROUPS = 1


def _params(*sem):
    return pltpu.CompilerParams(dimension_semantics=sem, vmem_limit_bytes=VMEM_LIMIT_BYTES)


def _rope_tables(seq, rot_dim, period):
    half = rot_dim // 2
    inv = 1.0 / (ROPE_THETA ** (jnp.arange(0, rot_dim, 2, dtype=F32) / rot_dim))
    ang = jnp.arange(seq, dtype=F32)[:, None] * inv[None, :]
    cos, sin = jnp.cos(ang), jnp.sin(ang)
    lane = jnp.arange(LANES) % period
    first = lane < half
    second = (lane >= half) & (lane < 2 * half)
    idx = jnp.where(first, lane, jnp.where(second, lane - half, 0))
    cos_l, sin_l = cos[:, idx], sin[:, idx]
    ct = jnp.where(first | second, cos_l, 1.0)
    s1 = jnp.where(first, -sin_l, 0.0)
    s2 = jnp.where(second, sin_l, 0.0)
    return ct.astype(F32), s1.astype(F32), s2.astype(F32)


def _proj_kernel(x_ref, w_ref, *rest, half, rope_blocks, rope_jmax):
    tab_refs, o_ref, wb_ref = rest[:-2], rest[-2], rest[-1]
    j = pl.program_id(0)
    i = pl.program_id(1)

    @pl.when(i == 0)
    def _():
        wb_ref[...] = w_ref[...].astype(BF16)

    def plain():
        o_ref[...] = jnp.dot(x_ref[...], wb_ref[...], preferred_element_type=F32).astype(o_ref.dtype)

    def roped():
        tabs = [r[...] for r in tab_refs]
        per = MXU_WIDTH // LANES
        for c in range(len(rope_blocks) // per):
            acc = jnp.dot(x_ref[...], wb_ref[:, c * MXU_WIDTH:(c + 1) * MXU_WIDTH], preferred_element_type=F32)
            for b in range(c * per, (c + 1) * per):
                use = rope_blocks[b]
                blk = acc[:, (b - c * per) * LANES:(b - c * per + 1) * LANES]
                if use:
                    ct, s1, s2 = tabs[3 * (use - 1):3 * use]
                    packed = pltpu.bitcast(blk.astype(BF16), jnp.uint32)
                    fwd = pltpu.bitcast(pltpu.roll(packed, LANES - half, 1), BF16).astype(F32)
                    bwd = pltpu.bitcast(pltpu.roll(packed, half, 1), BF16).astype(F32)
                    blk = blk * ct + fwd * s1 + bwd * s2
                o_ref[:, b * LANES:(b + 1) * LANES] = blk.astype(o_ref.dtype)

    if not any(rope_blocks):
        plain()
    elif rope_jmax is None:
        roped()
    else:
        pl.when(j < rope_jmax)(roped)
        pl.when(j >= rope_jmax)(plain)


def _proj(xb, w, layer, col_blocks, tn, tables, seq, *, half=0, rope_blocks=None, rope_jmax=None, tm=512):
    m, k = xb.shape
    nj = len(col_blocks)
    tm = min(tm, m)
    rope_blocks = tuple(int(u) for u in rope_blocks) if rope_blocks is not None else (0,) * (tn // LANES)
    col_blocks = tuple(col_blocks)
    first, contiguous = col_blocks[0], all(col_blocks[a] == col_blocks[0] + a for a in range(nj))
    if contiguous:
        w_map = lambda j, i: (layer, 0, first + j)
    else:
        n0 = next(a for a in range(1, nj) if col_blocks[a] != col_blocks[0] + a)
        second = col_blocks[n0]
        w_map = lambda j, i: (layer, 0, jnp.where(j < n0, first + j, second + j - n0))
    sblocks = seq // tm if seq >= tm else 1
    tab_spec = pl.BlockSpec((tm, LANES), lambda j, i: (i % sblocks, 0))
    kern = functools.partial(_proj_kernel, half=half, rope_blocks=rope_blocks, rope_jmax=rope_jmax)
    return pl.pallas_call(
        kern,
        grid=(nj, m // tm),
        in_specs=[pl.BlockSpec((tm, k), lambda j, i: (i, 0)),
                  pl.BlockSpec((None, k, tn), w_map)] + [tab_spec] * len(tables),
        out_specs=pl.BlockSpec((tm, tn), lambda j, i: (i, j)),
        out_shape=jax.ShapeDtypeStruct((m, nj * tn), BF16),
        scratch_shapes=[pltpu.VMEM((k, tn), BF16)],
        compiler_params=_params("arbitrary", "arbitrary"),
        name="proj",
    )(xb, w, *tables)


def _dilated_kernel(q_ref, kp_ref, km_ref, kn_ref, vp_ref, vm_ref, vn_ref, o_ref, lse_ref, *, tl, seq_len, heads):
    t = pl.program_id(2)
    l0 = t * tl
    sub = 128
    nsub = tl // sub
    scale = HEAD_DIM ** -0.5
    iq = lax.broadcasted_iota(jnp.int32, (sub, sub + 2 * A_HALF), 0)
    ik = lax.broadcasted_iota(jnp.int32, (sub, sub + 2 * A_HALF), 1)
    band = jnp.abs(ik - A_HALF - iq) <= A_HALF
    lane = lax.broadcasted_iota(jnp.int32, (sub, LANES), 1)
    valid = []
    for a in range(nsub):
        kk = l0 + a * sub - A_HALF + ik
        valid.append(band & (kk >= 0) & (kk < seq_len))
    lse_rows = [jnp.zeros((sub, LANES), F32) for _ in range(nsub)]
    for h in range(heads):
        cs = slice(h * HEAD_DIM, (h + 1) * HEAD_DIM)
        kcat = jnp.concatenate([kp_ref[:, cs], km_ref[:, cs], kn_ref[:, cs]], axis=0)
        vcat = jnp.concatenate([vp_ref[:, cs], vm_ref[:, cs], vn_ref[:, cs]], axis=0)
        for a in range(nsub):
            q = q_ref[a * sub:(a + 1) * sub, cs]
            kw = kcat[a * sub:a * sub + sub + 2 * A_HALF]
            vw = vcat[a * sub:a * sub + sub + 2 * A_HALF]
            s = lax.dot_general(q, kw, (((1,), (1,)), ((), ())), preferred_element_type=F32) * scale
            s = jnp.where(valid[a], s, NEG_INF)
            m = jnp.max(s, axis=-1, keepdims=True)
            p = jnp.exp(s - m)
            den = jnp.sum(p, axis=-1, keepdims=True)
            o = jnp.dot(p.astype(BF16), vw, preferred_element_type=F32) / den
            o_ref[a * sub:(a + 1) * sub, cs] = o.astype(o_ref.dtype)
            lse_rows[a] = jnp.where(lane == h, m + jnp.log(den), lse_rows[a])
    for a in range(nsub):
        lse_ref[a * sub:(a + 1) * sub, :] = lse_rows[a]


def _dilated_pattern(pa, batch, seq, dil):
    width = pa.shape[1] // 3
    heads = width // HEAD_DIM
    sl = seq // dil
    tl = min(256, sl)
    nt = sl // tl
    hb = tl // A_HALF
    last_halo = sl // A_HALF - 1
    view = pa.reshape(batch, sl, dil * 3 * width)

    def main(c):
        return pl.BlockSpec((None, tl, width), lambda b, r, t: (b, t, r * 3 + c))

    def prev(c):
        return pl.BlockSpec((None, A_HALF, width), lambda b, r, t: (b, jnp.maximum(t * hb - 1, 0), r * 3 + c))

    def nxt(c):
        return pl.BlockSpec((None, A_HALF, width),
                            lambda b, r, t: (b, jnp.minimum((t + 1) * hb, last_halo), r * 3 + c))

    kern = functools.partial(_dilated_kernel, tl=tl, seq_len=sl, heads=heads)
    o, lse = pl.pallas_call(
        kern,
        grid=(batch, dil, nt),
        in_specs=[main(0), prev(1), main(1), nxt(1), prev(2), main(2), nxt(2)],
        out_specs=[pl.BlockSpec((None, tl, width), lambda b, r, t: (b, t, r)),
                   pl.BlockSpec((None, tl, LANES), lambda b, r, t: (b, t, r))],
        out_shape=[jax.ShapeDtypeStruct((batch, sl, dil * width), BF16),
                   jax.ShapeDtypeStruct((batch, sl, dil * LANES), F32)],
        compiler_params=_params("arbitrary", "arbitrary", "arbitrary"),
        name=f"dilated_d{dil}",
    )(view, view, view, view, view, view, view)
    return o.reshape(batch * seq, width), lse.reshape(batch * seq, LANES)


def _dilated_combine_kernel(o0_ref, o1_ref, o2_ref, l0_ref, l1_ref, l2_ref, y_ref, *, heads):
    l0, l1, l2 = l0_ref[...], l1_ref[...], l2_ref[...]
    m = jnp.maximum(jnp.maximum(l0, l1), l2)
    e0, e1, e2 = jnp.exp(l0 - m), jnp.exp(l1 - m), jnp.exp(l2 - m)
    inv = 1.0 / (e0 + e1 + e2)
    w0, w1, w2 = e0 * inv, e1 * inv, e2 * inv
    for h in range(heads):
        cs = slice(h * HEAD_DIM, (h + 1) * HEAD_DIM)
        y = (w0[:, h:h + 1] * o0_ref[:, cs].astype(F32) + w1[:, h:h + 1] * o1_ref[:, cs].astype(F32)
             + w2[:, h:h + 1] * o2_ref[:, cs].astype(F32))
        y_ref[:, cs] = y.astype(y_ref.dtype)


def _dilated_attention(pa, batch, seq):
    outs, lses = zip(*[_dilated_pattern(pa, batch, seq, d) for _, d in A_PATTERNS])
    m, width = outs[0].shape
    tm = min(512, m)
    ospec = pl.BlockSpec((tm, width), lambda i: (i, 0))
    lspec = pl.BlockSpec((tm, LANES), lambda i: (i, 0))
    return pl.pallas_call(
        functools.partial(_dilated_combine_kernel, heads=width // HEAD_DIM),
        grid=(m // tm,),
        in_specs=[ospec, ospec, ospec, lspec, lspec, lspec],
        out_specs=ospec,
        out_shape=jax.ShapeDtypeStruct((m, width), BF16),
        compiler_params=_params("arbitrary"),
        name="dilated_combine",
    )(*outs, *lses)


def _pool_kernel(u_ref, w_ref, sc_ref, o_ref, pad_ref, *, seq):
    g = pl.program_id(1)
    pad = 8
    uf = u_ref[...].astype(F32)
    pad_ref[0:pad, :] = jnp.zeros((pad, uf.shape[1]), F32)
    pad_ref[pad + seq:pad + seq + pad, :] = jnp.zeros((pad, uf.shape[1]), F32)
    pad_ref[pad:pad + seq, :] = uf
    pos = lax.broadcasted_iota(jnp.int32, (seq, 1), 0)
    for gi, win in enumerate(B_WINDOWS):
        @pl.when(g == gi)
        def _(win=win):
            before, after = win // 2, win - win // 2
            tot = pad_ref[pad - before:pad - before + seq, :]
            for off in range(-before + 1, after):
                tot = tot + pad_ref[pad + off:pad + off + seq, :]
            cnt = (jnp.minimum(pos + after, seq) - jnp.maximum(pos - before, 0)).astype(F32)
            pooled = tot / cnt - uf
            mixed = jnp.dot(pooled.astype(BF16), w_ref[...].astype(BF16), preferred_element_type=F32)
            o_ref[...] = (mixed * sc_ref[...]).astype(o_ref.dtype)


def _pooling_mixer(pbdx, pool_w, pool_scale, batch, seq, col0):
    ng, cg = pool_w.shape[0], pool_w.shape[1]
    view = pbdx.reshape(batch, seq, pbdx.shape[1])
    cb = col0 // cg
    return pl.pallas_call(
        functools.partial(_pool_kernel, seq=seq),
        grid=(batch, ng),
        in_specs=[pl.BlockSpec((None, seq, cg), lambda b, g: (b, 0, cb + g)),
                  pl.BlockSpec((None, cg, cg), lambda b, g: (g, 0, 0)),
                  pl.BlockSpec((1, cg), lambda b, g: (0, g))],
        out_specs=pl.BlockSpec((None, seq, cg), lambda b, g: (b, 0, g)),
        out_shape=jax.ShapeDtypeStruct((batch, seq, ng * cg), BF16),
        scratch_shapes=[pltpu.VMEM((seq + 16, cg), F32)],
        compiler_params=_params("arbitrary", "arbitrary"),
        name="pooling",
    )(view, pool_w, pool_scale.reshape(1, ng * cg)).reshape(batch * seq, ng * cg)


def _conv_kernel(bg_ref, cg_ref, h_ref, w_ref, o_ref, pad_ref, *, seq):
    pad = 8
    u = cg_ref[...].astype(F32) * h_ref[...].astype(F32)
    cols = u.shape[1]
    pad_ref[0:pad, :] = jnp.zeros((pad, cols), F32)
    pad_ref[pad + seq:pad + seq + pad, :] = jnp.zeros((pad, cols), F32)
    pad_ref[pad:pad + seq, :] = u
    w = w_ref[...]
    conv = (pad_ref[pad - 1:pad - 1 + seq, :] * w[0:1, :] + u * w[1:2, :]
            + pad_ref[pad + 1:pad + 1 + seq, :] * w[2:3, :])
    o_ref[...] = (bg_ref[...].astype(F32) * conv).astype(o_ref.dtype)


def _gated_short_conv(pbdx, conv_w, batch, seq, col0):
    width = conv_w.shape[1]
    cb = 256
    nb = width // cb
    view = pbdx.reshape(batch, seq, pbdx.shape[1])
    base = col0 // cb

    def spec(part):
        return pl.BlockSpec((None, seq, cb), lambda b, c: (b, 0, base + part * nb + c))

    return pl.pallas_call(
        functools.partial(_conv_kernel, seq=seq),
        grid=(batch, nb),
        in_specs=[spec(0), spec(1), spec(2), pl.BlockSpec((3, cb), lambda b, c: (0, c))],
        out_specs=pl.BlockSpec((None, seq, cb), lambda b, c: (b, 0, c)),
        out_shape=jax.ShapeDtypeStruct((batch, seq, width), BF16),
        scratch_shapes=[pltpu.VMEM((seq + 16, cb), F32)],
        compiler_params=_params("arbitrary", "arbitrary"),
        name="short_conv",
    )(view, view, view, conv_w).reshape(batch * seq, width)


def _diff_kernel(q_ref, k_ref, v_ref, lam_ref, nrm_ref, o_ref, vone_ref, *, lam_init, chunk):
    t = pl.program_id(2)
    seq = k_ref.shape[0]
    tq = q_ref.shape[0]

    @pl.when(t == 0)
    def _():
        vone_ref[:, :LANES] = v_ref[...]
        vone_ref[:, LANES:] = jnp.ones((seq, LANES), BF16)

    lq = lam_ref[...].astype(F32)
    lam = (jnp.exp(jnp.sum(lq[0:1] * lq[1:2], axis=-1, keepdims=True))
           - jnp.exp(jnp.sum(lq[2:3] * lq[3:4], axis=-1, keepdims=True)) + lam_init)
    q = q_ref[...]
    lane = lax.broadcasted_iota(jnp.int32, q.shape, 1)
    zero = jnp.zeros_like(q)
    q2 = jnp.concatenate([jnp.where(lane < C_QK_DIM, q, zero), jnp.where(lane >= C_QK_DIM, q, zero)], axis=0)
    m = acc = None
    for c in range(seq // chunk):
        ks = slice(c * chunk, (c + 1) * chunk)
        s = lax.dot_general(q2, k_ref[ks, :], (((1,), (1,)), ((), ())), preferred_element_type=F32)
        mc = jnp.max(s, axis=-1, keepdims=True)
        m_new = mc if m is None else jnp.maximum(m, mc)
        e = jnp.exp2((s - m_new).astype(BF16))
        pv = jnp.dot(e, vone_ref[ks, :], preferred_element_type=F32)
        acc = pv if acc is None else acc * jnp.exp2(m - m_new) + pv
        m = m_new
    out = acc[:, :LANES] / acc[:, LANES:]
    o = out[:tq] - lam * out[tq:]
    o = o * lax.rsqrt(jnp.mean(o * o, axis=-1, keepdims=True) + LN_EPS) * nrm_ref[...] * (1.0 - lam_init)
    o_ref[...] = o.astype(o_ref.dtype)


def _diff_attention(pc, c_lambda, c_norm, batch, seq, layer, tq=1024, chunk=1024):
    heads = pc.shape[1] // (3 * LANES)
    tq = min(tq, seq)
    chunk = min(chunk, seq)
    lam_init = 0.8 - 0.6 * math.exp(-0.3 * layer)
    view = pc.reshape(batch, seq, pc.shape[1])
    return pl.pallas_call(
        functools.partial(_diff_kernel, lam_init=lam_init, chunk=chunk),
        grid=(batch, heads, seq // tq),
        in_specs=[pl.BlockSpec((None, tq, LANES), lambda b, h, t: (b, t, 3 * h)),
                  pl.BlockSpec((None, seq, LANES), lambda b, h, t: (b, 0, 3 * h + 1)),
                  pl.BlockSpec((None, seq, LANES), lambda b, h, t: (b, 0, 3 * h + 2)),
                  pl.BlockSpec((4, C_QK_DIM), lambda b, h, t: (0, 0)),
                  pl.BlockSpec((1, LANES), lambda b, h, t: (0, 0))],
        out_specs=pl.BlockSpec((None, tq, LANES), lambda b, h, t: (b, t, h)),
        out_shape=jax.ShapeDtypeStruct((batch, seq, heads * LANES), BF16),
        scratch_shapes=[pltpu.VMEM((seq, 2 * LANES), BF16)],
        compiler_params=_params("arbitrary", "arbitrary", "arbitrary"),
        name="diff_attention",
    )(view, view, view, c_lambda, c_norm.reshape(1, LANES)).reshape(batch * seq, heads * LANES)


def _cross_kernel(q_ref, kv_ref, o_ref, *, heads):
    scale = HEAD_DIM ** -0.5
    width = heads * HEAD_DIM
    for h in range(heads):
        cs = slice(h * HEAD_DIM, (h + 1) * HEAD_DIM)
        k = kv_ref[:, h * HEAD_DIM:(h + 1) * HEAD_DIM]
        v = kv_ref[:, width + h * HEAD_DIM:width + (h + 1) * HEAD_DIM]
        s = lax.dot_general(q_ref[:, cs], k, (((1,), (1,)), ((), ())), preferred_element_type=F32) * scale
        m = jnp.max(s, axis=-1, keepdims=True)
        e = jnp.exp(s - m)
        den = jnp.sum(e, axis=-1, keepdims=True)
        o = jnp.dot(e.astype(BF16), v, preferred_element_type=F32) / den
        o_ref[:, cs] = o.astype(o_ref.dtype)


def _cross_attention(pbdx, kv, batch, seq, col0, width, tq=512):
    mem_len = kv.shape[0] // batch
    tq = min(tq, seq)
    view = pbdx.reshape(batch, seq, pbdx.shape[1])
    return pl.pallas_call(
        functools.partial(_cross_kernel, heads=width // HEAD_DIM),
        grid=(batch, seq // tq),
        in_specs=[pl.BlockSpec((None, tq, width), lambda b, t: (b, t, col0 // width)),
                  pl.BlockSpec((None, mem_len, 2 * width), lambda b, t: (b, 0, 0))],
        out_specs=pl.BlockSpec((None, tq, width), lambda b, t: (b, t, 0)),
        out_shape=jax.ShapeDtypeStruct((batch, seq, width), BF16),
        compiler_params=_params("arbitrary", "arbitrary"),
        name="cross_attention",
    )(view, kv.reshape(batch, mem_len, 2 * width)).reshape(batch * seq, width)


def _merge_kernel(x_ref, ya_ref, yb_ref, yc_ref, yd_ref, yx_ref,
                  g0, g1, g2, g3, g4, p0, p1, p2, p3, p4, b0, b1, b2, b3, b4, o_ref):
    x = x_ref[...]
    merged = None
    for y_ref, g_ref, p_ref, b_ref in ((ya_ref, g0, p0, b0), (yb_ref, g1, p1, b1), (yc_ref, g2, p2, b2),
                                       (yd_ref, g3, p3, b3), (yx_ref, g4, p4, b4)):
        gate = jax.nn.sigmoid(jnp.dot(x, g_ref[...], preferred_element_type=F32) + b_ref[...])
        term = gate * jnp.dot(y_ref[...], p_ref[...], preferred_element_type=F32)
        merged = term if merged is None else merged + term
    o_ref[...] = merged.astype(o_ref.dtype)


def _gated_merge(xb, ys, w_gate, w_branch, w_branch_x, b_gate, tm=512, tn=512):
    m, d = xb.shape
    tm = min(tm, m)
    nb = d // tn
    n_br = 5
    act = [pl.BlockSpec((tm, d), lambda j, i: (i, 0))]
    act += [pl.BlockSpec((tm, y.shape[1]), lambda j, i: (i, 0)) for y in ys]
    gates = [pl.BlockSpec((d, tn), functools.partial(lambda j, i, br: (0, br * nb + j), br=br)) for br in range(n_br)]
    projs = [pl.BlockSpec((None, w_branch.shape[1], tn), functools.partial(lambda j, i, br: (br, 0, j), br=br))
             for br in range(4)]
    projs.append(pl.BlockSpec((w_branch_x.shape[0], tn), lambda j, i: (0, j)))
    biases = [pl.BlockSpec((1, tn), functools.partial(lambda j, i, br: (0, br * nb + j), br=br)) for br in range(n_br)]
    bg2 = b_gate.reshape(1, n_br * d)
    return pl.pallas_call(
        _merge_kernel,
        grid=(nb, m // tm),
        in_specs=act + gates + projs + biases,
        out_specs=pl.BlockSpec((tm, tn), lambda j, i: (i, j)),
        out_shape=jax.ShapeDtypeStruct((m, d), BF16),
        compiler_params=_params("arbitrary", "arbitrary"),
        name="gated_merge",
    )(xb, *ys, *([w_gate] * n_br), *([w_branch] * 4), w_branch_x, *([bg2] * n_br))


def _layer_norm_rows(h, g, b):
    mu = jnp.mean(h, axis=-1, keepdims=True)
    hc = h - mu
    var = jnp.mean(hc * hc, axis=-1, keepdims=True)
    return hc * lax.rsqrt(var + LN_EPS) * g + b


def _out_ln_kernel(mg_ref, w_ref, x_ref, g_ref, b_ref, xo_ref, xb_ref, *, alpha):
    y = jnp.dot(mg_ref[...], w_ref[...], preferred_element_type=F32)
    xn = _layer_norm_rows(alpha * x_ref[...] + y, g_ref[...], b_ref[...])
    xo_ref[...] = xn
    xb_ref[...] = xn.astype(BF16)


def _out_proj_ln(merged, w_out_b, x, g, b, alpha, tm=512):
    m, d = x.shape
    tm = min(tm, m)
    row = pl.BlockSpec((tm, d), lambda i: (i, 0))
    vec = pl.BlockSpec((1, d), lambda i: (0, 0))
    return pl.pallas_call(
        functools.partial(_out_ln_kernel, alpha=alpha),
        grid=(m // tm,),
        in_specs=[row, pl.BlockSpec((d, d), lambda i: (0, 0)), row, vec, vec],
        out_specs=[row, row],
        out_shape=[jax.ShapeDtypeStruct((m, d), F32), jax.ShapeDtypeStruct((m, d), BF16)],
        compiler_params=_params("arbitrary"),
        name="out_proj_ln",
    )(merged, w_out_b, x, g.reshape(1, d), b.reshape(1, d))


def _split_bf16(a):
    hi = a.astype(BF16)
    lo = (a - hi.astype(F32)).astype(BF16)
    return hi, lo


def _router_kernel(x_ref, w_ref, b_ref, idx_ref, wt_ref, cnt_ref):
    @pl.when(pl.program_id(0) == 0)
    def _():
        cnt_ref[...] = jnp.zeros_like(cnt_ref)

    x = x_ref[...]
    w = w_ref[...]
    xh, xl = _split_bf16(x)
    wh, wl = _split_bf16(w)
    logits = (jnp.dot(xh, wh, preferred_element_type=F32) + jnp.dot(xl, wh, preferred_element_type=F32)
              + jnp.dot(xh, wl, preferred_element_type=F32))
    lt = logits.T[:N_EXPERTS, :]
    tm = lt.shape[1]
    scores = jax.nn.sigmoid(lt)
    biased = scores + b_ref[...]
    gsz = N_EXPERTS // N_GROUPS
    sub8 = lax.broadcasted_iota(jnp.int32, (gsz, tm), 0).astype(F32)
    grp_rows = []
    for g in range(N_GROUPS):
        blk = biased[g * gsz:(g + 1) * gsz, :]
        m1 = jnp.max(blk, axis=0, keepdims=True)
        i1 = jnp.min(jnp.where(blk == m1, sub8, float(gsz)), axis=0, keepdims=True)
        m2 = jnp.max(jnp.where(sub8 == i1, -jnp.inf, blk), axis=0, keepdims=True)
        grp_rows.append(m1 + m2)
    grp = jnp.concatenate(grp_rows, axis=0)
    subg = lax.broadcasted_iota(jnp.int32, (N_GROUPS, tm), 0).astype(F32)
    gsel = jnp.zeros((N_GROUPS, tm), F32)
    for _ in range(TOP_GROUPS):
        mg = jnp.max(grp, axis=0, keepdims=True)
        ig = jnp.min(jnp.where(grp == mg, subg, float(N_GROUPS)), axis=0, keepdims=True)
        hit = subg == ig
        gsel = jnp.where(hit, 1.0, gsel)
        grp = jnp.where(hit, -jnp.inf, grp)
    emask = jnp.concatenate([jnp.broadcast_to(gsel[g:g + 1, :], (gsz, tm)) for g in range(N_GROUPS)], axis=0)
    cand = jnp.where(emask > 0.5, biased, -jnp.inf)
    sube = lax.broadcasted_iota(jnp.int32, (N_EXPERTS, tm), 0).astype(F32)
    idx_rows, w_rows = [], []
    chosen = jnp.zeros((N_EXPERTS, tm), F32)
    for _ in range(TOP_K):
        mc = jnp.max(cand, axis=0, keepdims=True)
        ic = jnp.min(jnp.where(cand == mc, sube, float(N_EXPERTS)), axis=0, keepdims=True)
        hit = sube == ic
        idx_rows.append(ic)
        w_rows.append(jnp.sum(jnp.where(hit, scores, 0.0), axis=0, keepdims=True))
        chosen = jnp.where(hit, 1.0, chosen)
        cand = jnp.where(hit, -jnp.inf, cand)
    cnt_ref[...] += jnp.sum(chosen, axis=1, keepdims=True)
    wsel = jnp.concatenate(w_rows, axis=0)
    wsel = wsel / jnp.sum(wsel, axis=0, keepdims=True) * ROUTE_SCALE
    idx_ref[...] = jnp.concatenate(idx_rows, axis=0).astype(jnp.int32)
    wt_ref[...] = wsel


def _router(x, router_w, router_b, tm=512):
    m, d = x.shape
    tm = min(tm, m)
    w_pad = jnp.pad(router_w, ((0, 0), (0, LANES - N_EXPERTS)))
    return pl.pallas_call(
        _router_kernel,
        grid=(m // tm,),
        in_specs=[pl.BlockSpec((tm, d), lambda i: (i, 0)),
                  pl.BlockSpec((d, LANES), lambda i: (0, 0)),
                  pl.BlockSpec((N_EXPERTS, 1), lambda i: (0, 0))],
        out_specs=[pl.BlockSpec((TOP_K, tm), lambda i: (0, i)), pl.BlockSpec((TOP_K, tm), lambda i: (0, i)),
                   pl.BlockSpec((N_EXPERTS, LANES), lambda i: (0, 0))],
        out_shape=[jax.ShapeDtypeStruct((TOP_K, m), jnp.int32), jax.ShapeDtypeStruct((TOP_K, m), F32),
                   jax.ShapeDtypeStruct((N_EXPERTS, LANES), F32)],
        compiler_params=_params("arbitrary"),
        name="router",
    )(x, w_pad, router_b.reshape(N_EXPERTS, 1))


def _slot_kernel(idx_ref, base_ref, dest_ref, tri_ref, run_ref):
    tm = idx_ref.shape[1]

    @pl.when(pl.program_id(0) == 0)
    def _():
        row = lax.broadcasted_iota(jnp.int32, (tm, tm), 0)
        col = lax.broadcasted_iota(jnp.int32, (tm, tm), 1)
        tri_ref[...] = jnp.where(row < col, 1.0, 0.0).astype(BF16)
        run_ref[...] = base_ref[...]

    sube = lax.broadcasted_iota(jnp.int32, (N_EXPERTS, tm), 0)
    idx = idx_ref[...]
    base = run_ref[...]
    hits = [sube == idx[k:k + 1, :] for k in range(TOP_K)]
    chosen = jnp.zeros((N_EXPERTS, tm), F32)
    for hit in hits:
        chosen = jnp.where(hit, 1.0, chosen)
    before = jnp.dot(chosen.astype(BF16), tri_ref[...], preferred_element_type=F32)
    slot = base + before
    rows = [jnp.sum(jnp.where(hit, slot, 0.0), axis=0, keepdims=True) for hit in hits]
    run_ref[...] = base + jnp.sum(chosen, axis=1, keepdims=True)
    dest_ref[...] = jnp.concatenate(rows, axis=0).astype(jnp.int32)


def _assign_slots(idx_t, base, tm=512):
    kk, m = idx_t.shape
    tm = min(tm, m)
    return pl.pallas_call(
        _slot_kernel,
        grid=(m // tm,),
        in_specs=[pl.BlockSpec((kk, tm), lambda i: (0, i)), pl.BlockSpec((N_EXPERTS, 1), lambda i: (0, 0))],
        out_specs=pl.BlockSpec((kk, tm), lambda i: (0, i)),
        out_shape=jax.ShapeDtypeStruct((kk, m), jnp.int32),
        scratch_shapes=[pltpu.VMEM((tm, tm), BF16), pltpu.VMEM((N_EXPERTS, 1), F32)],
        compiler_params=_params("arbitrary"),
        name="assign_slots",
    )(idx_t, base.astype(F32).reshape(N_EXPERTS, 1))


def _expert_kernel(be_ref, nb_ref, x_ref, wg_ref, wu_ref, wd_ref, o_ref):
    del be_ref
    i = pl.program_id(0)

    @pl.when(i < nb_ref[0])
    def _():
        x = x_ref[...]
        y = None
        for c in range(wg_ref.shape[1] // MXU_WIDTH):
            cs = slice(c * MXU_WIDTH, (c + 1) * MXU_WIDTH)
            gate = jnp.dot(x, wg_ref[:, cs].astype(BF16), preferred_element_type=F32)
            up = jnp.dot(x, wu_ref[:, cs].astype(BF16), preferred_element_type=F32)
            act = (jax.nn.silu(gate) * up).astype(BF16)
            part = jnp.dot(act, wd_ref[cs, :].astype(BF16), preferred_element_type=F32)
            y = part if y is None else y + part
        o_ref[...] = y.astype(o_ref.dtype)

    @pl.when(i >= nb_ref[0])
    def _():
        o_ref[...] = jnp.zeros_like(o_ref)


def _expert_ffn(xs, block_e, nblocks, w_gate, w_up, w_down, layer, rows):
    ns, d = xs.shape
    de = w_gate.shape[3]
    grid_spec = pltpu.PrefetchScalarGridSpec(
        num_scalar_prefetch=2,
        grid=(ns // rows,),
        in_specs=[pl.BlockSpec((rows, d), lambda i, be, nb: (i, 0)),
                  pl.BlockSpec((None, None, d, de), lambda i, be, nb: (layer, be[i], 0, 0)),
                  pl.BlockSpec((None, None, d, de), lambda i, be, nb: (layer, be[i], 0, 0)),
                  pl.BlockSpec((None, None, de, d), lambda i, be, nb: (layer, be[i], 0, 0))],
        out_specs=pl.BlockSpec((rows, d), lambda i, be, nb: (i, 0)),
    )
    return pl.pallas_call(
        _expert_kernel,
        grid_spec=grid_spec,
        out_shape=jax.ShapeDtypeStruct((ns, d), BF16),
        compiler_params=_params("arbitrary"),
        name="expert_ffn",
    )(block_e, nblocks, xs, w_gate, w_up, w_down)


def _combine_ln_kernel(*refs, alpha):
    yg_refs = refs[:TOP_K]
    wt_ref, sh_ref, x_ref, g_ref, b_ref, xo_ref, xb_ref = refs[TOP_K:]
    wt = wt_ref[...]
    routed = sh_ref[...].astype(F32)
    for k in range(TOP_K):
        routed = routed + wt[:, k:k + 1] * yg_refs[k][...].astype(F32)
    xn = _layer_norm_rows(alpha * x_ref[...] + routed, g_ref[...], b_ref[...])
    xo_ref[...] = xn
    xb_ref[...] = xn.astype(BF16)


def _combine_ln(yg, wt, shared, x, g, b, alpha, tm=256):
    m, d = x.shape
    tm = min(tm, m)
    nb = m // tm
    row = pl.BlockSpec((tm, d), lambda i: (i, 0))
    vec = pl.BlockSpec((1, d), lambda i: (0, 0))
    yg_specs = [pl.BlockSpec((tm, d), functools.partial(lambda i, k: (k * nb + i, 0), k=k)) for k in range(TOP_K)]
    return pl.pallas_call(
        functools.partial(_combine_ln_kernel, alpha=alpha),
        grid=(nb,),
        in_specs=yg_specs + [pl.BlockSpec((tm, TOP_K), lambda i: (i, 0)), row, row, vec, vec],
        out_specs=[row, row],
        out_shape=[jax.ShapeDtypeStruct((m, d), F32), jax.ShapeDtypeStruct((m, d), BF16)],
        compiler_params=_params("arbitrary"),
        name="combine_ln",
    )(*([yg] * TOP_K), wt, shared, x, g.reshape(1, d), b.reshape(1, d))


def _moe_layer(x1, xb1, layer, router_w, router_b, e_gate, e_up, e_down, s_gate, s_up, s_down, g, b, alpha):
    t, d = x1.shape
    rows = min(MOE_ROWS, t)
    idx_t, wt_t, cnt = _router(x1, router_w, router_b)
    n_assign = t * TOP_K
    i32 = jnp.int32
    counts = cnt[:, 0].astype(i32)
    padded = (counts + rows - 1) // rows * rows
    ends_p = jnp.cumsum(padded)
    pad_e = padded - counts
    dest = _assign_slots(idx_t, ends_p - padded).reshape(n_assign)
    nblk = n_assign // rows + N_EXPERTS
    n_pad = nblk * rows - n_assign
    pad_i = jnp.arange(n_pad, dtype=i32)
    pad_before = jnp.cumsum(pad_e) - pad_e
    pad_slot = pad_i + jnp.sum(jnp.where(pad_before[None, :] <= pad_i[:, None], counts[None, :], 0), axis=1)
    keys = jnp.concatenate([dest, pad_slot])
    toks = jnp.concatenate([jnp.arange(n_assign, dtype=i32) % t, pad_i % t])
    _, slot_tok = lax.sort_key_val(keys, toks)
    blk0 = jnp.arange(nblk, dtype=i32) * rows
    block_e = jnp.minimum(jnp.sum((ends_p[None, :] <= blk0[:, None]).astype(i32), axis=1), N_EXPERTS - 1)
    nused = (ends_p[-1] // rows).astype(i32).reshape(1)
    xs = jnp.take(xb1, slot_tok, axis=0, mode='clip')
    ys = _expert_ffn(xs, block_e, nused, e_gate, e_up, e_down, layer, rows)
    yg = jnp.take(ys, dest, axis=0, mode='clip')
    srows = min(2 * MOE_ROWS, t)
    shared = _expert_ffn(xb1, jnp.zeros((t // srows,), i32), jnp.full((1,), t // srows, i32),
                         s_gate[:, None], s_up[:, None], s_down[:, None], layer, srows)
    return _combine_ln(yg, wt_t.T, shared, x1, g, b, alpha)


def _mixer_layer(x, xb, memb, layer, tabs_a, tabs_c, batch, seq, w_in, w_c, pool_w, pool_scale, c_lambda, c_norm,
                 conv_w, w_mem_kv, merge_w, b_gate, w_out_b, g, b, alpha):
    mix = merge_w[1].shape[1]
    xw = merge_w[2].shape[0]
    a_cols, c_cols, d_cols = 3 * mix, 3 * mix, 3 * mix
    c0_b, c0_d = a_cols, a_cols + mix + c_cols
    pa = _proj(xb, w_in, layer, range(3), mix, tabs_a, seq, half=HEAD_DIM // 8,
               rope_blocks=(1,) * (mix // LANES), rope_jmax=2)
    tn_c = 6 * LANES
    pc = _proj(xb, w_c, 0, range(c_cols // tn_c), tn_c, tabs_c, seq,
               half=C_QK_DIM // 8, rope_blocks=(1, 2, 0) * 2)
    tn_p = 512
    blocks = list(range(c0_b // tn_p, c0_b // tn_p + mix // tn_p)) + \
        list(range(c0_d // tn_p, c0_d // tn_p + (d_cols + xw) // tn_p))
    pbdx = _proj(xb, w_in, layer, blocks, tn_p, tabs_a, seq)
    ya = _dilated_attention(pa, batch, seq)
    yb = _pooling_mixer(pbdx, pool_w[layer], pool_scale[layer], batch, seq, 0)
    yc = _diff_attention(pc, c_lambda[layer], c_norm[layer], batch, seq, layer)
    yd = _gated_short_conv(pbdx, conv_w[layer], batch, seq, mix)
    kv = _proj(memb, w_mem_kv, layer, range(2 * xw // tn_p), tn_p, tabs_a, seq)
    yx = _cross_attention(pbdx, kv, batch, seq, mix + d_cols, xw)
    merged = _gated_merge(xb, (ya, yb, yc, yd, yx), *merge_w, b_gate[layer])
    return _out_proj_ln(merged, w_out_b, x, g[layer], b[layer], alpha)


def kernel(x, mem, w_in, pool_w, pool_scale, c_lambda, c_norm, conv_w, w_mem_kv, w_branch, w_branch_x,
           w_gate, b_gate, w_out, ln1_g, ln1_b, router_w, router_b, e_gate, e_up, e_down, s_gate, s_up,
           s_down, ln2_g, ln2_b):
    batch, seq, d = x.shape
    depth = w_in.shape[0]
    alpha = (2.0 * depth) ** 0.25
    tabs_a = _rope_tables(seq, HEAD_DIM // 4, LANES)
    tabs_ck = _rope_tables(seq, C_QK_DIM // 4, C_QK_DIM)
    q_scale = C_QK_DIM ** -0.5 * math.log2(math.e)
    tabs_c = tuple(tab * q_scale for tab in tabs_ck) + tabs_ck
    mix = w_branch.shape[2]
    groups = BATCH_GROUPS if batch % BATCH_GROUPS == 0 else 1
    gb = batch // groups
    state = []
    for gi in range(groups):
        xg = x[gi * gb:(gi + 1) * gb].reshape(gb * seq, d)
        state.append((xg, xg.astype(BF16), mem[gi * gb:(gi + 1) * gb].reshape(-1, d).astype(BF16)))
    for l in range(depth):
        w_c = w_in[l:l + 1, :, 4 * mix:7 * mix]
        merge_w = (w_gate[l].astype(BF16), w_branch[l].astype(BF16), w_branch_x[l].astype(BF16))
        w_out_b = w_out[l].astype(BF16)
        for gi in range(groups):
            xf, xb, memb = state[gi]
            xf, xb = _mixer_layer(xf, xb, memb, l, tabs_a, tabs_c, gb, seq, w_in, w_c, pool_w, pool_scale,
                                  c_lambda, c_norm, conv_w, w_mem_kv, merge_w, b_gate, w_out_b,
                                  ln1_g, ln1_b, alpha)
            xf, xb = _moe_layer(xf, xb, l, router_w[l], router_b[l], e_gate, e_up, e_down, s_gate,
                                s_up, s_down, ln2_g[l], ln2_b[l], alpha)
            state[gi] = (xf, xb, memb)
    return jnp.concatenate([s[0] for s in state], axis=0).reshape(batch, seq, d)
```

```python
import functools
import math

import jax
import jax.numpy as jnp
from jax import lax
from jax.experimental import pallas as pl
from jax.experimental.pallas import tpu as pltpu

F32 = jnp.float32
BF16 = jnp.bfloat16

LANES = 128
MXU_WIDTH = 256
HEAD_DIM = 128
ROPE_THETA = 500000.0
A_PATTERNS = ((128, 1), (512, 4), (2048, 16))
A_HALF = 64
B_WINDOWS = (2, 4, 8, 16)
C_QK_DIM = 64
N_EXPERTS = 64
N_GROUPS = 8
TOP_GROUPS = 4
TOP_K = 8
ROUTE_SCALE = 2.5
LN_EPS = 1e-5
NEG_INF = -1e30
VMEM_LIMIT_BYTES = 56 * 1024 * 1024
MOE_ROWS = 256
BATCH_GROUPS = 1


def _params(*sem):
    return pltpu.CompilerParams(dimension_semantics=sem, vmem_limit_bytes=VMEM_LIMIT_BYTES)


def _rope_tables(seq, rot_dim, period):
    half = rot_dim // 2
    inv = 1.0 / (ROPE_THETA ** (jnp.arange(0, rot_dim, 2, dtype=F32) / rot_dim))
    ang = jnp.arange(seq, dtype=F32)[:, None] * inv[None, :]
    cos, sin = jnp.cos(ang), jnp.sin(ang)
    lane = jnp.arange(LANES) % period
    first = lane < half
    second = (lane >= half) & (lane < 2 * half)
    idx = jnp.where(first, lane, jnp.where(second, lane - half, 0))
    cos_l, sin_l = cos[:, idx], sin[:, idx]
    ct = jnp.where(first | second, cos_l, 1.0)
    s1 = jnp.where(first, -sin_l, 0.0)
    s2 = jnp.where(second, sin_l, 0.0)
    return ct.astype(F32), s1.astype(F32), s2.astype(F32)


def _proj_kernel(x_ref, w_ref, *rest, half, rope_blocks, rope_jmax):
    tab_refs, o_ref, wb_ref = rest[:-2], rest[-2], rest[-1]
    j = pl.program_id(0)
    i = pl.program_id(1)

    @pl.when(i == 0)
    def _():
        wb_ref[...] = w_ref[...].astype(BF16)

    def plain():
        o_ref[...] = jnp.dot(x_ref[...], wb_ref[...], preferred_element_type=F32).astype(o_ref.dtype)

    def roped():
        tabs = [r[...] for r in tab_refs]
        per = MXU_WIDTH // LANES
        for c in range(len(rope_blocks) // per):
            acc = jnp.dot(x_ref[...], wb_ref[:, c * MXU_WIDTH:(c + 1) * MXU_WIDTH], preferred_element_type=F32)
            for b in range(c * per, (c + 1) * per):
                use = rope_blocks[b]
                blk = acc[:, (b - c * per) * LANES:(b - c * per + 1) * LANES]
                if use:
                    ct, s1, s2 = tabs[3 * (use - 1):3 * use]
                    packed = pltpu.bitcast(blk.astype(BF16), jnp.uint32)
                    fwd = pltpu.bitcast(pltpu.roll(packed, LANES - half, 1), BF16).astype(F32)
                    bwd = pltpu.bitcast(pltpu.roll(packed, half, 1), BF16).astype(F32)
                    blk = blk * ct + fwd * s1 + bwd * s2
                o_ref[:, b * LANES:(b + 1) * LANES] = blk.astype(o_ref.dtype)

    if not any(rope_blocks):
        plain()
    elif rope_jmax is None:
        roped()
    else:
        pl.when(j < rope_jmax)(roped)
        pl.when(j >= rope_jmax)(plain)


def _proj(xb, w, layer, col_blocks, tn, tables, seq, *, half=0, rope_blocks=None, rope_jmax=None, tm=512):
    m, k = xb.shape
    nj = len(col_blocks)
    tm = min(tm, m)
    rope_blocks = tuple(int(u) for u in rope_blocks) if rope_blocks is not None else (0,) * (tn // LANES)
    col_blocks = tuple(col_blocks)
    first, contiguous = col_blocks[0], all(col_blocks[a] == col_blocks[0] + a for a in range(nj))
    if contiguous:
        w_map = lambda j, i: (layer, 0, first + j)
    else:
        n0 = next(a for a in range(1, nj) if col_blocks[a] != col_blocks[0] + a)
        second = col_blocks[n0]
        w_map = lambda j, i: (layer, 0, jnp.where(j < n0, first + j, second + j - n0))
    sblocks = seq // tm if seq >= tm else 1
    tab_spec = pl.BlockSpec((tm, LANES), lambda j, i: (i % sblocks, 0))
    kern = functools.partial(_proj_kernel, half=half, rope_blocks=rope_blocks, rope_jmax=rope_jmax)
    return pl.pallas_call(
        kern,
        grid=(nj, m // tm),
        in_specs=[pl.BlockSpec((tm, k), lambda j, i: (i, 0)),
                  pl.BlockSpec((None, k, tn), w_map)] + [tab_spec] * len(tables),
        out_specs=pl.BlockSpec((tm, tn), lambda j, i: (i, j)),
        out_shape=jax.ShapeDtypeStruct((m, nj * tn), BF16),
        scratch_shapes=[pltpu.VMEM((k, tn), BF16)],
        compiler_params=_params("arbitrary", "arbitrary"),
        name="proj",
    )(xb, w, *tables)


def _dilated_kernel(q_ref, kp_ref, km_ref, kn_ref, vp_ref, vm_ref, vn_ref, o_ref, lse_ref, *, tl, seq_len, heads):
    t = pl.program_id(2)
    l0 = t * tl
    sub = 128
    nsub = tl // sub
    scale = HEAD_DIM ** -0.5
    iq = lax.broadcasted_iota(jnp.int32, (sub, sub + 2 * A_HALF), 0)
    ik = lax.broadcasted_iota(jnp.int32, (sub, sub + 2 * A_HALF), 1)
    band = jnp.abs(ik - A_HALF - iq) <= A_HALF
    lane = lax.broadcasted_iota(jnp.int32, (sub, LANES), 1)
    valid = []
    for a in range(nsub):
        kk = l0 + a * sub - A_HALF + ik
        valid.append(band & (kk >= 0) & (kk < seq_len))
    lse_rows = [jnp.zeros((sub, LANES), F32) for _ in range(nsub)]
    for h in range(heads):
        cs = slice(h * HEAD_DIM, (h + 1) * HEAD_DIM)
        kcat = jnp.concatenate([kp_ref[:, cs], km_ref[:, cs], kn_ref[:, cs]], axis=0)
        vcat = jnp.concatenate([vp_ref[:, cs], vm_ref[:, cs], vn_ref[:, cs]], axis=0)
        for a in range(nsub):
            q = q_ref[a * sub:(a + 1) * sub, cs]
            kw = kcat[a * sub:a * sub + sub + 2 * A_HALF]
            vw = vcat[a * sub:a * sub + sub + 2 * A_HALF]
            s = lax.dot_general(q, kw, (((1,), (1,)), ((), ())), preferred_element_type=F32) * scale
            s = jnp.where(valid[a], s, NEG_INF)
            m = jnp.max(s, axis=-1, keepdims=True)
            p = jnp.exp(s - m)
            den = jnp.sum(p, axis=-1, keepdims=True)
            o = jnp.dot(p.astype(BF16), vw, preferred_element_type=F32) / den
            o_ref[a * sub:(a + 1) * sub, cs] = o.astype(o_ref.dtype)
            lse_rows[a] = jnp.where(lane == h, m + jnp.log(den), lse_rows[a])
    for a in range(nsub):
        lse_ref[a * sub:(a + 1) * sub, :] = lse_rows[a]


def _dilated_pattern(pa, batch, seq, dil):
    width = pa.shape[1] // 3
    heads = width // HEAD_DIM
    sl = seq // dil
    tl = min(256, sl)
    nt = sl // tl
    hb = tl // A_HALF
    last_halo = sl // A_HALF - 1
    view = pa.reshape(batch, sl, dil * 3 * width)

    def main(c):
        return pl.BlockSpec((None, tl, width), lambda b, r, t: (b, t, r * 3 + c))

    def prev(c):
        return pl.BlockSpec((None, A_HALF, width), lambda b, r, t: (b, jnp.maximum(t * hb - 1, 0), r * 3 + c))

    def nxt(c):
        return pl.BlockSpec((None, A_HALF, width),
                            lambda b, r, t: (b, jnp.minimum((t + 1) * hb, last_halo), r * 3 + c))

    kern = functools.partial(_dilated_kernel, tl=tl, seq_len=sl, heads=heads)
    o, lse = pl.pallas_call(
        kern,
        grid=(batch, dil, nt),
        in_specs=[main(0), prev(1), main(1), nxt(1), prev(2), main(2), nxt(2)],
        out_specs=[pl.BlockSpec((None, tl, width), lambda b, r, t: (b, t, r)),
                   pl.BlockSpec((None, tl, LANES), lambda b, r, t: (b, t, r))],
        out_shape=[jax.ShapeDtypeStruct((batch, sl, dil * width), BF16),
                   jax.ShapeDtypeStruct((batch, sl, dil * LANES), F32)],
        compiler_params=_params("arbitrary", "arbitrary", "arbitrary"),
        name=f"dilated_d{dil}",
    )(view, view, view, view, view, view, view)
    return o.reshape(batch * seq, width), lse.reshape(batch * seq, LANES)


def _dilated_combine_kernel(o0_ref, o1_ref, o2_ref, l0_ref, l1_ref, l2_ref, y_ref, *, heads):
    l0, l1, l2 = l0_ref[...], l1_ref[...], l2_ref[...]
    m = jnp.maximum(jnp.maximum(l0, l1), l2)
    e0, e1, e2 = jnp.exp(l0 - m), jnp.exp(l1 - m), jnp.exp(l2 - m)
    inv = 1.0 / (e0 + e1 + e2)
    w0, w1, w2 = e0 * inv, e1 * inv, e2 * inv
    for h in range(heads):
        cs = slice(h * HEAD_DIM, (h + 1) * HEAD_DIM)
        y = (w0[:, h:h + 1] * o0_ref[:, cs].astype(F32) + w1[:, h:h + 1] * o1_ref[:, cs].astype(F32)
             + w2[:, h:h + 1] * o2_ref[:, cs].astype(F32))
        y_ref[:, cs] = y.astype(y_ref.dtype)


def _dilated_attention(pa, batch, seq):
    outs, lses = zip(*[_dilated_pattern(pa, batch, seq, d) for _, d in A_PATTERNS])
    m, width = outs[0].shape
    tm = min(512, m)
    ospec = pl.BlockSpec((tm, width), lambda i: (i, 0))
    lspec = pl.BlockSpec((tm, LANES), lambda i: (i, 0))
    return pl.pallas_call(
        functools.partial(_dilated_combine_kernel, heads=width // HEAD_DIM),
        grid=(m // tm,),
        in_specs=[ospec, ospec, ospec, lspec, lspec, lspec],
        out_specs=ospec,
        out_shape=jax.ShapeDtypeStruct((m, width), BF16),
        compiler_params=_params("arbitrary"),
        name="dilated_combine",
    )(*outs, *lses)


def _pool_kernel(u_ref, w_ref, sc_ref, o_ref, pad_ref, *, seq):
    g = pl.program_id(1)
    pad = 8
    uf = u_ref[...].astype(F32)
    pad_ref[0:pad, :] = jnp.zeros((pad, uf.shape[1]), F32)
    pad_ref[pad + seq:pad + seq + pad, :] = jnp.zeros((pad, uf.shape[1]), F32)
    pad_ref[pad:pad + seq, :] = uf
    pos = lax.broadcasted_iota(jnp.int32, (seq, 1), 0)
    for gi, win in enumerate(B_WINDOWS):
        @pl.when(g == gi)
        def _(win=win):
            before, after = win // 2, win - win // 2
            tot = pad_ref[pad - before:pad - before + seq, :]
            for off in range(-before + 1, after):
                tot = tot + pad_ref[pad + off:pad + off + seq, :]
            cnt = (jnp.minimum(pos + after, seq) - jnp.maximum(pos - before, 0)).astype(F32)
            pooled = tot / cnt - uf
            mixed = jnp.dot(pooled.astype(BF16), w_ref[...].astype(BF16), preferred_element_type=F32)
            o_ref[...] = (mixed * sc_ref[...]).astype(o_ref.dtype)


def _pooling_mixer(pbdx, pool_w, pool_scale, batch, seq, col0):
    ng, cg = pool_w.shape[0], pool_w.shape[1]
    view = pbdx.reshape(batch, seq, pbdx.shape[1])
    cb = col0 // cg
    return pl.pallas_call(
        functools.partial(_pool_kernel, seq=seq),
        grid=(batch, ng),
        in_specs=[pl.BlockSpec((None, seq, cg), lambda b, g: (b, 0, cb + g)),
                  pl.BlockSpec((None, cg, cg), lambda b, g: (g, 0, 0)),
                  pl.BlockSpec((1, cg), lambda b, g: (0, g))],
        out_specs=pl.BlockSpec((None, seq, cg), lambda b, g: (b, 0, g)),
        out_shape=jax.ShapeDtypeStruct((batch, seq, ng * cg), BF16),
        scratch_shapes=[pltpu.VMEM((seq + 16, cg), F32)],
        compiler_params=_params("arbitrary", "arbitrary"),
        name="pooling",
    )(view, pool_w, pool_scale.reshape(1, ng * cg)).reshape(batch * seq, ng * cg)


def _conv_kernel(bg_ref, cg_ref, h_ref, w_ref, o_ref, pad_ref, *, seq):
    pad = 8
    u = cg_ref[...].astype(F32) * h_ref[...].astype(F32)
    cols = u.shape[1]
    pad_ref[0:pad, :] = jnp.zeros((pad, cols), F32)
    pad_ref[pad + seq:pad + seq + pad, :] = jnp.zeros((pad, cols), F32)
    pad_ref[pad:pad + seq, :] = u
    w = w_ref[...]
    conv = (pad_ref[pad - 1:pad - 1 + seq, :] * w[0:1, :] + u * w[1:2, :]
            + pad_ref[pad + 1:pad + 1 + seq, :] * w[2:3, :])
    o_ref[...] = (bg_ref[...].astype(F32) * conv).astype(o_ref.dtype)


def _gated_short_conv(pbdx, conv_w, batch, seq, col0):
    width = conv_w.shape[1]
    cb = 256
    nb = width // cb
    view = pbdx.reshape(batch, seq, pbdx.shape[1])
    base = col0 // cb

    def spec(part):
        return pl.BlockSpec((None, seq, cb), lambda b, c: (b, 0, base + part * nb + c))

    return pl.pallas_call(
        functools.partial(_conv_kernel, seq=seq),
        grid=(batch, nb),
        in_specs=[spec(0), spec(1), spec(2), pl.BlockSpec((3, cb), lambda b, c: (0, c))],
        out_specs=pl.BlockSpec((None, seq, cb), lambda b, c: (b, 0, c)),
        out_shape=jax.ShapeDtypeStruct((batch, seq, width), BF16),
        scratch_shapes=[pltpu.VMEM((seq + 16, cb), F32)],
        compiler_params=_params("arbitrary", "arbitrary"),
        name="short_conv",
    )(view, view, view, conv_w).reshape(batch * seq, width)


def _diff_kernel(q_ref, k_ref, v_ref, lam_ref, nrm_ref, o_ref, vone_ref, *, lam_init, chunk):
    t = pl.program_id(2)
    seq = k_ref.shape[0]
    tq = q_ref.shape[0]

    @pl.when(t == 0)
    def _():
        vone_ref[:, :LANES] = v_ref[...]
        vone_ref[:, LANES:] = jnp.ones((seq, LANES), BF16)

    lq = lam_ref[...].astype(F32)
    lam = (jnp.exp(jnp.sum(lq[0:1] * lq[1:2], axis=-1, keepdims=True))
           - jnp.exp(jnp.sum(lq[2:3] * lq[3:4], axis=-1, keepdims=True)) + lam_init)
    q = q_ref[...]
    lane = lax.broadcasted_iota(jnp.int32, q.shape, 1)
    zero = jnp.zeros_like(q)
    q2 = jnp.concatenate([jnp.where(lane < C_QK_DIM, q, zero), jnp.where(lane >= C_QK_DIM, q, zero)], axis=0)
    m = acc = None
    for c in range(seq // chunk):
        ks = slice(c * chunk, (c + 1) * chunk)
        s = lax.dot_general(q2, k_ref[ks, :], (((1,), (1,)), ((), ())), preferred_element_type=F32)
        mc = jnp.max(s, axis=-1, keepdims=True)
        m_new = mc if m is None else jnp.maximum(m, mc)
        e = jnp.exp2((s - m_new).astype(BF16))
        pv = jnp.dot(e, vone_ref[ks, :], preferred_element_type=F32)
        acc = pv if acc is None else acc * jnp.exp2(m - m_new) + pv
        m = m_new
    out = acc[:, :LANES] / acc[:, LANES:]
    o = out[:tq] - lam * out[tq:]
    o = o * lax.rsqrt(jnp.mean(o * o, axis=-1, keepdims=True) + LN_EPS) * nrm_ref[...] * (1.0 - lam_init)
    o_ref[...] = o.astype(o_ref.dtype)


def _diff_attention(pc, c_lambda, c_norm, batch, seq, layer, tq=1024, chunk=1024):
    heads = pc.shape[1] // (3 * LANES)
    tq = min(tq, seq)
    chunk = min(chunk, seq)
    lam_init = 0.8 - 0.6 * math.exp(-0.3 * layer)
    view = pc.reshape(batch, seq, pc.shape[1])
    return pl.pallas_call(
        functools.partial(_diff_kernel, lam_init=lam_init, chunk=chunk),
        grid=(batch, heads, seq // tq),
        in_specs=[pl.BlockSpec((None, tq, LANES), lambda b, h, t: (b, t, 3 * h)),
                  pl.BlockSpec((None, seq, LANES), lambda b, h, t: (b, 0, 3 * h + 1)),
                  pl.BlockSpec((None, seq, LANES), lambda b, h, t: (b, 0, 3 * h + 2)),
                  pl.BlockSpec((4, C_QK_DIM), lambda b, h, t: (0, 0)),
                  pl.BlockSpec((1, LANES), lambda b, h, t: (0, 0))],
        out_specs=pl.BlockSpec((None, tq, LANES), lambda b, h, t: (b, t, h)),
        out_shape=jax.ShapeDtypeStruct((batch, seq, heads * LANES), BF16),
        scratch_shapes=[pltpu.VMEM((seq, 2 * LANES), BF16)],
        compiler_params=_params("arbitrary", "arbitrary", "arbitrary"),
        name="diff_attention",
    )(view, view, view, c_lambda, c_norm.reshape(1, LANES)).reshape(batch * seq, heads * LANES)


def _cross_kernel(q_ref, kv_ref, o_ref, *, heads):
    scale = HEAD_DIM ** -0.5
    width = heads * HEAD_DIM
    for h in range(heads):
        cs = slice(h * HEAD_DIM, (h + 1) * HEAD_DIM)
        k = kv_ref[:, h * HEAD_DIM:(h + 1) * HEAD_DIM]
        v = kv_ref[:, width + h * HEAD_DIM:width + (h + 1) * HEAD_DIM]
        s = lax.dot_general(q_ref[:, cs], k, (((1,), (1,)), ((), ())), preferred_element_type=F32) * scale
        m = jnp.max(s, axis=-1, keepdims=True)
        e = jnp.exp(s - m)
        den = jnp.sum(e, axis=-1, keepdims=True)
        o = jnp.dot(e.astype(BF16), v, preferred_element_type=F32) / den
        o_ref[:, cs] = o.astype(o_ref.dtype)


def _cross_attention(pbdx, kv, batch, seq, col0, width, tq=512):
    mem_len = kv.shape[0] // batch
    tq = min(tq, seq)
    view = pbdx.reshape(batch, seq, pbdx.shape[1])
    return pl.pallas_call(
        functools.partial(_cross_kernel, heads=width // HEAD_DIM),
        grid=(batch, seq // tq),
        in_specs=[pl.BlockSpec((None, tq, width), lambda b, t: (b, t, col0 // width)),
                  pl.BlockSpec((None, mem_len, 2 * width), lambda b, t: (b, 0, 0))],
        out_specs=pl.BlockSpec((None, tq, width), lambda b, t: (b, t, 0)),
        out_shape=jax.ShapeDtypeStruct((batch, seq, width), BF16),
        compiler_params=_params("arbitrary", "arbitrary"),
        name="cross_attention",
    )(view, kv.reshape(batch, mem_len, 2 * width)).reshape(batch * seq, width)


def _merge_kernel(x_ref, ya_ref, yb_ref, yc_ref, yd_ref, yx_ref,
                  g0, g1, g2, g3, g4, p0, p1, p2, p3, p4, b0, b1, b2, b3, b4, o_ref):
    x = x_ref[...]
    merged = None
    for y_ref, g_ref, p_ref, b_ref in ((ya_ref, g0, p0, b0), (yb_ref, g1, p1, b1), (yc_ref, g2, p2, b2),
                                       (yd_ref, g3, p3, b3), (yx_ref, g4, p4, b4)):
        gate = jax.nn.sigmoid(jnp.dot(x, g_ref[...], preferred_element_type=F32) + b_ref[...])
        term = gate * jnp.dot(y_ref[...], p_ref[...], preferred_element_type=F32)
        merged = term if merged is None else merged + term
    o_ref[...] = merged.astype(o_ref.dtype)


def _gated_merge(xb, ys, w_gate, w_branch, w_branch_x, b_gate, tm=512, tn=512):
    m, d = xb.shape
    tm = min(tm, m)
    nb = d // tn
    n_br = 5
    act = [pl.BlockSpec((tm, d), lambda j, i: (i, 0))]
    act += [pl.BlockSpec((tm, y.shape[1]), lambda j, i: (i, 0)) for y in ys]
    gates = [pl.BlockSpec((d, tn), functools.partial(lambda j, i, br: (0, br * nb + j), br=br)) for br in range(n_br)]
    projs = [pl.BlockSpec((None, w_branch.shape[1], tn), functools.partial(lambda j, i, br: (br, 0, j), br=br))
             for br in range(4)]
    projs.append(pl.BlockSpec((w_branch_x.shape[0], tn), lambda j, i: (0, j)))
    biases = [pl.BlockSpec((1, tn), functools.partial(lambda j, i, br: (0, br * nb + j), br=br)) for br in range(n_br)]
    bg2 = b_gate.reshape(1, n_br * d)
    return pl.pallas_call(
        _merge_kernel,
        grid=(nb, m // tm),
        in_specs=act + gates + projs + biases,
        out_specs=pl.BlockSpec((tm, tn), lambda j, i: (i, j)),
        out_shape=jax.ShapeDtypeStruct((m, d), BF16),
        compiler_params=_params("arbitrary", "arbitrary"),
        name="gated_merge",
    )(xb, *ys, *([w_gate] * n_br), *([w_branch] * 4), w_branch_x, *([bg2] * n_br))


def _layer_norm_rows(h, g, b):
    mu = jnp.mean(h, axis=-1, keepdims=True)
    hc = h - mu
    var = jnp.mean(hc * hc, axis=-1, keepdims=True)
    return hc * lax.rsqrt(var + LN_EPS) * g + b


def _out_ln_kernel(mg_ref, w_ref, x_ref, g_ref, b_ref, xo_ref, xb_ref, *, alpha):
    y = jnp.dot(mg_ref[...], w_ref[...], preferred_element_type=F32)
    xn = _layer_norm_rows(alpha * x_ref[...] + y, g_ref[...], b_ref[...])
    xo_ref[...] = xn
    xb_ref[...] = xn.astype(BF16)


def _out_proj_ln(merged, w_out_b, x, g, b, alpha, tm=512):
    m, d = x.shape
    tm = min(tm, m)
    row = pl.BlockSpec((tm, d), lambda i: (i, 0))
    vec = pl.BlockSpec((1, d), lambda i: (0, 0))
    return pl.pallas_call(
        functools.partial(_out_ln_kernel, alpha=alpha),
        grid=(m // tm,),
        in_specs=[row, pl.BlockSpec((d, d), lambda i: (0, 0)), row, vec, vec],
        out_specs=[row, row],
        out_shape=[jax.ShapeDtypeStruct((m, d), F32), jax.ShapeDtypeStruct((m, d), BF16)],
        compiler_params=_params("arbitrary"),
        name="out_proj_ln",
    )(merged, w_out_b, x, g.reshape(1, d), b.reshape(1, d))


def _split_bf16(a):
    hi = a.astype(BF16)
    lo = (a - hi.astype(F32)).astype(BF16)
    return hi, lo


def _router_kernel(x_ref, w_ref, b_ref, idx_ref, wt_ref, cnt_ref):
    @pl.when(pl.program_id(0) == 0)
    def _():
        cnt_ref[...] = jnp.zeros_like(cnt_ref)

    x = x_ref[...]
    w = w_ref[...]
    xh, xl = _split_bf16(x)
    wh, wl = _split_bf16(w)
    logits = (jnp.dot(xh, wh, preferred_element_type=F32) + jnp.dot(xl, wh, preferred_element_type=F32)
              + jnp.dot(xh, wl, preferred_element_type=F32))
    lt = logits.T[:N_EXPERTS, :]
    tm = lt.shape[1]
    scores = jax.nn.sigmoid(lt)
    biased = scores + b_ref[...]
    gsz = N_EXPERTS // N_GROUPS
    sub8 = lax.broadcasted_iota(jnp.int32, (gsz, tm), 0).astype(F32)
    grp_rows = []
    for g in range(N_GROUPS):
        blk = biased[g * gsz:(g + 1) * gsz, :]
        m1 = jnp.max(blk, axis=0, keepdims=True)
        i1 = jnp.min(jnp.where(blk == m1, sub8, float(gsz)), axis=0, keepdims=True)
        m2 = jnp.max(jnp.where(sub8 == i1, -jnp.inf, blk), axis=0, keepdims=True)
        grp_rows.append(m1 + m2)
    grp = jnp.concatenate(grp_rows, axis=0)
    subg = lax.broadcasted_iota(jnp.int32, (N_GROUPS, tm), 0).astype(F32)
    gsel = jnp.zeros((N_GROUPS, tm), F32)
    for _ in range(TOP_GROUPS):
        mg = jnp.max(grp, axis=0, keepdims=True)
        ig = jnp.min(jnp.where(grp == mg, subg, float(N_GROUPS)), axis=0, keepdims=True)
        hit = subg == ig
        gsel = jnp.where(hit, 1.0, gsel)
        grp = jnp.where(hit, -jnp.inf, grp)
    emask = jnp.concatenate([jnp.broadcast_to(gsel[g:g + 1, :], (gsz, tm)) for g in range(N_GROUPS)], axis=0)
    cand = jnp.where(emask > 0.5, biased, -jnp.inf)
    sube = lax.broadcasted_iota(jnp.int32, (N_EXPERTS, tm), 0).astype(F32)
    idx_rows, w_rows = [], []
    chosen = jnp.zeros((N_EXPERTS, tm), F32)
    for _ in range(TOP_K):
        mc = jnp.max(cand, axis=0, keepdims=True)
        ic = jnp.min(jnp.where(cand == mc, sube, float(N_EXPERTS)), axis=0, keepdims=True)
        hit = sube == ic
        idx_rows.append(ic)
        w_rows.append(jnp.sum(jnp.where(hit, scores, 0.0), axis=0, keepdims=True))
        chosen = jnp.where(hit, 1.0, chosen)
        cand = jnp.where(hit, -jnp.inf, cand)
    cnt_ref[...] += jnp.sum(chosen, axis=1, keepdims=True)
    wsel = jnp.concatenate(w_rows, axis=0)
    wsel = wsel / jnp.sum(wsel, axis=0, keepdims=True) * ROUTE_SCALE
    idx_ref[...] = jnp.concatenate(idx_rows, axis=0).astype(jnp.int32)
    wt_ref[...] = wsel


def _router(x, router_w, router_b, tm=512):
    m, d = x.shape
    tm = min(tm, m)
    w_pad = jnp.pad(router_w, ((0, 0), (0, LANES - N_EXPERTS)))
    return pl.pallas_call(
        _router_kernel,
        grid=(m // tm,),
        in_specs=[pl.BlockSpec((tm, d), lambda i: (i, 0)),
                  pl.BlockSpec((d, LANES), lambda i: (0, 0)),
                  pl.BlockSpec((N_EXPERTS, 1), lambda i: (0, 0))],
        out_specs=[pl.BlockSpec((TOP_K, tm), lambda i: (0, i)), pl.BlockSpec((TOP_K, tm), lambda i: (0, i)),
                   pl.BlockSpec((N_EXPERTS, LANES), lambda i: (0, 0))],
        out_shape=[jax.ShapeDtypeStruct((TOP_K, m), jnp.int32), jax.ShapeDtypeStruct((TOP_K, m), F32),
                   jax.ShapeDtypeStruct((N_EXPERTS, LANES), F32)],
        compiler_params=_params("arbitrary"),
        name="router",
    )(x, w_pad, router_b.reshape(N_EXPERTS, 1))


def _slot_kernel(idx_ref, base_ref, dest_ref, tri_ref, run_ref):
    tm = idx_ref.shape[1]

    @pl.when(pl.program_id(0) == 0)
    def _():
        row = lax.broadcasted_iota(jnp.int32, (tm, tm), 0)
        col = lax.broadcasted_iota(jnp.int32, (tm, tm), 1)
        tri_ref[...] = jnp.where(row < col, 1.0, 0.0).astype(BF16)
        run_ref[...] = base_ref[...]

    sube = lax.broadcasted_iota(jnp.int32, (N_EXPERTS, tm), 0)
    idx = idx_ref[...]
    base = run_ref[...]
    hits = [sube == idx[k:k + 1, :] for k in range(TOP_K)]
    chosen = jnp.zeros((N_EXPERTS, tm), F32)
    for hit in hits:
        chosen = jnp.where(hit, 1.0, chosen)
    before = jnp.dot(chosen.astype(BF16), tri_ref[...], preferred_element_type=F32)
    slot = base + before
    rows = [jnp.sum(jnp.where(hit, slot, 0.0), axis=0, keepdims=True) for hit in hits]
    run_ref[...] = base + jnp.sum(chosen, axis=1, keepdims=True)
    dest_ref[...] = jnp.concatenate(rows, axis=0).astype(jnp.int32)


def _assign_slots(idx_t, base, tm=512):
    kk, m = idx_t.shape
    tm = min(tm, m)
    return pl.pallas_call(
        _slot_kernel,
        grid=(m // tm,),
        in_specs=[pl.BlockSpec((kk, tm), lambda i: (0, i)), pl.BlockSpec((N_EXPERTS, 1), lambda i: (0, 0))],
        out_specs=pl.BlockSpec((kk, tm), lambda i: (0, i)),
        out_shape=jax.ShapeDtypeStruct((kk, m), jnp.int32),
        scratch_shapes=[pltpu.VMEM((tm, tm), BF16), pltpu.VMEM((N_EXPERTS, 1), F32)],
        compiler_params=_params("arbitrary"),
        name="assign_slots",
    )(idx_t, base.astype(F32).reshape(N_EXPERTS, 1))


def _expert_kernel(be_ref, nb_ref, x_ref, wg_ref, wu_ref, wd_ref, o_ref):
    del be_ref
    i = pl.program_id(0)

    @pl.when(i < nb_ref[0])
    def _():
        x = x_ref[...]
        y = None
        for c in range(wg_ref.shape[1] // MXU_WIDTH):
            cs = slice(c * MXU_WIDTH, (c + 1) * MXU_WIDTH)
            gate = jnp.dot(x, wg_ref[:, cs].astype(BF16), preferred_element_type=F32)
            up = jnp.dot(x, wu_ref[:, cs].astype(BF16), preferred_element_type=F32)
            act = (jax.nn.silu(gate) * up).astype(BF16)
            part = jnp.dot(act, wd_ref[cs, :].astype(BF16), preferred_element_type=F32)
            y = part if y is None else y + part
        o_ref[...] = y.astype(o_ref.dtype)

    @pl.when(i >= nb_ref[0])
    def _():
        o_ref[...] = jnp.zeros_like(o_ref)


def _expert_ffn(xs, block_e, nblocks, w_gate, w_up, w_down, layer, rows):
    ns, d = xs.shape
    de = w_gate.shape[3]
    grid_spec = pltpu.PrefetchScalarGridSpec(
        num_scalar_prefetch=2,
        grid=(ns // rows,),
        in_specs=[pl.BlockSpec((rows, d), lambda i, be, nb: (i, 0)),
                  pl.BlockSpec((None, None, d, de), lambda i, be, nb: (layer, be[i], 0, 0)),
                  pl.BlockSpec((None, None, d, de), lambda i, be, nb: (layer, be[i], 0, 0)),
                  pl.BlockSpec((None, None, de, d), lambda i, be, nb: (layer, be[i], 0, 0))],
        out_specs=pl.BlockSpec((rows, d), lambda i, be, nb: (i, 0)),
    )
    return pl.pallas_call(
        _expert_kernel,
        grid_spec=grid_spec,
        out_shape=jax.ShapeDtypeStruct((ns, d), BF16),
        compiler_params=_params("arbitrary"),
        name="expert_ffn",
    )(block_e, nblocks, xs, w_gate, w_up, w_down)


def _combine_ln_kernel(*refs, alpha):
    yg_refs = refs[:TOP_K]
    wt_ref, sh_ref, x_ref, g_ref, b_ref, xo_ref, xb_ref = refs[TOP_K:]
    wt = wt_ref[...]
    routed = sh_ref[...].astype(F32)
    for k in range(TOP_K):
        routed = routed + wt[:, k:k + 1] * yg_refs[k][...].astype(F32)
    xn = _layer_norm_rows(alpha * x_ref[...] + routed, g_ref[...], b_ref[...])
    xo_ref[...] = xn
    xb_ref[...] = xn.astype(BF16)


def _combine_ln(yg, wt, shared, x, g, b, alpha, tm=256):
    m, d = x.shape
    tm = min(tm, m)
    nb = m // tm
    row = pl.BlockSpec((tm, d), lambda i: (i, 0))
    vec = pl.BlockSpec((1, d), lambda i: (0, 0))
    yg_specs = [pl.BlockSpec((tm, d), functools.partial(lambda i, k: (k * nb + i, 0), k=k)) for k in range(TOP_K)]
    return pl.pallas_call(
        functools.partial(_combine_ln_kernel, alpha=alpha),
        grid=(nb,),
        in_specs=yg_specs + [pl.BlockSpec((tm, TOP_K), lambda i: (i, 0)), row, row, vec, vec],
        out_specs=[row, row],
        out_shape=[jax.ShapeDtypeStruct((m, d), F32), jax.ShapeDtypeStruct((m, d), BF16)],
        compiler_params=_params("arbitrary"),
        name="combine_ln",
    )(*([yg] * TOP_K), wt, shared, x, g.reshape(1, d), b.reshape(1, d))


def _moe_layer(x1, xb1, layer, router_w, router_b, e_gate, e_up, e_down, s_gate, s_up, s_down, g, b, alpha):
    t, d = x1.shape
    rows = min(MOE_ROWS, t)
    idx_t, wt_t, cnt = _router(x1, router_w, router_b)
    n_assign = t * TOP_K
    i32 = jnp.int32
    counts = cnt[:, 0].astype(i32)
    padded = (counts + rows - 1) // rows * rows
    ends_p = jnp.cumsum(padded)
    pad_e = padded - counts
    dest = _assign_slots(idx_t, ends_p - padded).reshape(n_assign)
    nblk = n_assign // rows + N_EXPERTS
    n_pad = nblk * rows - n_assign
    pad_i = jnp.arange(n_pad, dtype=i32)
    pad_before = jnp.cumsum(pad_e) - pad_e
    pad_slot = pad_i + jnp.sum(jnp.where(pad_before[None, :] <= pad_i[:, None], counts[None, :], 0), axis=1)
    keys = jnp.concatenate([dest, pad_slot])
    toks = jnp.concatenate([jnp.arange(n_assign, dtype=i32) % t, pad_i % t])
    _, slot_tok = lax.sort_key_val(keys, toks)
    blk0 = jnp.arange(nblk, dtype=i32) * rows
    block_e = jnp.minimum(jnp.sum((ends_p[None, :] <= blk0[:, None]).astype(i32), axis=1), N_EXPERTS - 1)
    nused = (ends_p[-1] // rows).astype(i32).reshape(1)
    xs = jnp.take(xb1, slot_tok, axis=0, mode='clip')
    ys = _expert_ffn(xs, block_e, nused, e_gate, e_up, e_down, layer, rows)
    yg = jnp.take(ys, dest, axis=0, mode='clip')
    srows = min(2 * MOE_ROWS, t)
    shared = _expert_ffn(xb1, jnp.zeros((t // srows,), i32), jnp.full((1,), t // srows, i32),
                         s_gate[:, None], s_up[:, None], s_down[:, None], layer, srows)
    return _combine_ln(yg, wt_t.T, shared, x1, g, b, alpha)


def _mixer_layer(x, xb, memb, layer, tabs_a, tabs_c, batch, seq, w_in, w_c, pool_w, pool_scale, c_lambda, c_norm,
                 conv_w, w_mem_kv, merge_w, b_gate, w_out_b, g, b, alpha):
    mix = merge_w[1].shape[1]
    xw = merge_w[2].shape[0]
    a_cols, c_cols, d_cols = 3 * mix, 3 * mix, 3 * mix
    c0_b, c0_d = a_cols, a_cols + mix + c_cols
    pa = _proj(xb, w_in, layer, range(3), mix, tabs_a, seq, half=HEAD_DIM // 8,
               rope_blocks=(1,) * (mix // LANES), rope_jmax=2)
    tn_c = 12 * LANES
    pc = _proj(xb, w_c, 0, range(c_cols // tn_c), tn_c, tabs_c, seq,
               half=C_QK_DIM // 8, rope_blocks=(1, 2, 0) * 4)
    blocks = list(range(c0_b // mix, c0_b // mix + 1)) + list(range(c0_d // mix, c0_d // mix + d_cols // mix))
    pbd = _proj(xb, w_in, layer, blocks, mix, tabs_a, seq)
    px = _proj(xb, w_in, layer, [(c0_d + d_cols) // xw], xw, tabs_a, seq)
    ya = _dilated_attention(pa, batch, seq)
    yb = _pooling_mixer(pbd, pool_w[layer], pool_scale[layer], batch, seq, 0)
    yc = _diff_attention(pc, c_lambda[layer], c_norm[layer], batch, seq, layer)
    yd = _gated_short_conv(pbd, conv_w[layer], batch, seq, mix)
    kv = _proj(memb, w_mem_kv, layer, range(2), xw, tabs_a, seq)
    yx = _cross_attention(px, kv, batch, seq, 0, xw)
    merged = _gated_merge(xb, (ya, yb, yc, yd, yx), *merge_w, b_gate[layer])
    return _out_proj_ln(merged, w_out_b, x, g[layer], b[layer], alpha)


def kernel(x, mem, w_in, pool_w, pool_scale, c_lambda, c_norm, conv_w, w_mem_kv, w_branch, w_branch_x,
           w_gate, b_gate, w_out, ln1_g, ln1_b, router_w, router_b, e_gate, e_up, e_down, s_gate, s_up,
           s_down, ln2_g, ln2_b):
    batch, seq, d = x.shape
    depth = w_in.shape[0]
    alpha = (2.0 * depth) ** 0.25
    tabs_a = _rope_tables(seq, HEAD_DIM // 4, LANES)
    tabs_ck = _rope_tables(seq, C_QK_DIM // 4, C_QK_DIM)
    q_scale = C_QK_DIM ** -0.5 * math.log2(math.e)
    tabs_c = tuple(tab * q_scale for tab in tabs_ck) + tabs_ck
    mix = w_branch.shape[2]
    groups = BATCH_GROUPS if batch % BATCH_GROUPS == 0 else 1
    gb = batch // groups
    state = []
    for gi in range(groups):
        xg = x[gi * gb:(gi + 1) * gb].reshape(gb * seq, d)
        state.append((xg, xg.astype(BF16), mem[gi * gb:(gi + 1) * gb].reshape(-1, d).astype(BF16)))
    for l in range(depth):
        w_c = w_in[l:l + 1, :, 4 * mix:7 * mix]
        merge_w = (w_gate[l].astype(BF16), w_branch[l].astype(BF16), w_branch_x[l].astype(BF16))
        w_out_b = w_out[l].astype(BF16)
        for gi in range(groups):
            xf, xb, memb = state[gi]
            xf, xb = _mixer_layer(xf, xb, memb, l, tabs_a, tabs_c, gb, seq, w_in, w_c, pool_w, pool_scale,
                                  c_lambda, c_norm, conv_w, w_mem_kv, merge_w, b_gate, w_out_b,
                                  ln1_g, ln1_b, alpha)
            xf, xb = _moe_layer(xf, xb, l, router_w[l], router_b[l], e_gate, e_up, e_down, s_gate,
                                s_up, s_down, ln2_g[l], ln2_b[l], alpha)
            state[gi] = (xf, xb, memb)
    return jnp.concatenate([s[0] for s in state], axis=0).reshape(batch, seq, d)
```

```python
import functools
import math

import jax
import jax.numpy as jnp
from jax import lax
from jax.experimental import pallas as pl
from jax.experimental.pallas import tpu as pltpu

F32 = jnp.float32
BF16 = jnp.bfloat16

LANES = 128
MXU_WIDTH = 256
HEAD_DIM = 128
ROPE_THETA = 500000.0
A_PATTERNS = ((128, 1), (512, 4), (2048, 16))
A_HALF = 64
B_WINDOWS = (2, 4, 8, 16)
C_QK_DIM = 64
N_EXPERTS = 64
N_GROUPS = 8
TOP_GROUPS = 4
TOP_K = 8
ROUTE_SCALE = 2.5
LN_EPS = 1e-5
NEG_INF = -1e30
VMEM_LIMIT_BYTES = 56 * 1024 * 1024
MOE_ROWS = 256
BATCH_GROUPS = 1


def _params(*sem):
    return pltpu.CompilerParams(dimension_semantics=sem, vmem_limit_bytes=VMEM_LIMIT_BYTES)


def _rope_tables(seq, rot_dim, period):
    half = rot_dim // 2
    inv = 1.0 / (ROPE_THETA ** (jnp.arange(0, rot_dim, 2, dtype=F32) / rot_dim))
    ang = jnp.arange(seq, dtype=F32)[:, None] * inv[None, :]
    cos, sin = jnp.cos(ang), jnp.sin(ang)
    lane = jnp.arange(LANES) % period
    first = lane < half
    second = (lane >= half) & (lane < 2 * half)
    idx = jnp.where(first, lane, jnp.where(second, lane - half, 0))
    cos_l, sin_l = cos[:, idx], sin[:, idx]
    ct = jnp.where(first | second, cos_l, 1.0)
    s1 = jnp.where(first, -sin_l, 0.0)
    s2 = jnp.where(second, sin_l, 0.0)
    return ct.astype(F32), s1.astype(F32), s2.astype(F32)


def _proj_kernel(x_ref, w_ref, *rest, half, rope_blocks, rope_jmax):
    tab_refs, o_ref, wb_ref = rest[:-2], rest[-2], rest[-1]
    j = pl.program_id(0)
    i = pl.program_id(1)

    @pl.when(i == 0)
    def _():
        wb_ref[...] = w_ref[...].astype(BF16)

    def plain():
        o_ref[...] = jnp.dot(x_ref[...], wb_ref[...], preferred_element_type=F32).astype(o_ref.dtype)

    def roped():
        tabs = [r[...] for r in tab_refs]
        per = MXU_WIDTH // LANES
        for c in range(len(rope_blocks) // per):
            acc = jnp.dot(x_ref[...], wb_ref[:, c * MXU_WIDTH:(c + 1) * MXU_WIDTH], preferred_element_type=F32)
            for b in range(c * per, (c + 1) * per):
                use = rope_blocks[b]
                blk = acc[:, (b - c * per) * LANES:(b - c * per + 1) * LANES]
                if use:
                    ct, s1, s2 = tabs[3 * (use - 1):3 * use]
                    packed = pltpu.bitcast(blk.astype(BF16), jnp.uint32)
                    fwd = pltpu.bitcast(pltpu.roll(packed, LANES - half, 1), BF16).astype(F32)
                    bwd = pltpu.bitcast(pltpu.roll(packed, half, 1), BF16).astype(F32)
                    blk = blk * ct + fwd * s1 + bwd * s2
                o_ref[:, b * LANES:(b + 1) * LANES] = blk.astype(o_ref.dtype)

    if not any(rope_blocks):
        plain()
    elif rope_jmax is None:
        roped()
    else:
        pl.when(j < rope_jmax)(roped)
        pl.when(j >= rope_jmax)(plain)


def _proj(xb, w, layer, col_blocks, tn, tables, seq, *, half=0, rope_blocks=None, rope_jmax=None, tm=1024):
    m, k = xb.shape
    nj = len(col_blocks)
    tm = min(tm, m)
    rope_blocks = tuple(int(u) for u in rope_blocks) if rope_blocks is not None else (0,) * (tn // LANES)
    col_blocks = tuple(col_blocks)
    first, contiguous = col_blocks[0], all(col_blocks[a] == col_blocks[0] + a for a in range(nj))
    if contiguous:
        w_map = lambda j, i: (layer, 0, first + j)
    else:
        n0 = next(a for a in range(1, nj) if col_blocks[a] != col_blocks[0] + a)
        second = col_blocks[n0]
        w_map = lambda j, i: (layer, 0, jnp.where(j < n0, first + j, second + j - n0))
    sblocks = seq // tm if seq >= tm else 1
    tab_spec = pl.BlockSpec((tm, LANES), lambda j, i: (i % sblocks, 0))
    kern = functools.partial(_proj_kernel, half=half, rope_blocks=rope_blocks, rope_jmax=rope_jmax)
    return pl.pallas_call(
        kern,
        grid=(nj, m // tm),
        in_specs=[pl.BlockSpec((tm, k), lambda j, i: (i, 0)),
                  pl.BlockSpec((None, k, tn), w_map)] + [tab_spec] * len(tables),
        out_specs=pl.BlockSpec((tm, tn), lambda j, i: (i, j)),
        out_shape=jax.ShapeDtypeStruct((m, nj * tn), BF16),
        scratch_shapes=[pltpu.VMEM((k, tn), BF16)],
        compiler_params=_params("arbitrary", "arbitrary"),
        name="proj",
    )(xb, w, *tables)


def _dilated_kernel(q_ref, kp_ref, km_ref, kn_ref, vp_ref, vm_ref, vn_ref, o_ref, lse_ref, *, tl, seq_len, heads):
    t = pl.program_id(2)
    l0 = t * tl
    sub = 128
    nsub = tl // sub
    scale = HEAD_DIM ** -0.5
    iq = lax.broadcasted_iota(jnp.int32, (sub, sub + 2 * A_HALF), 0)
    ik = lax.broadcasted_iota(jnp.int32, (sub, sub + 2 * A_HALF), 1)
    band = jnp.abs(ik - A_HALF - iq) <= A_HALF
    lane = lax.broadcasted_iota(jnp.int32, (sub, LANES), 1)
    valid = []
    for a in range(nsub):
        kk = l0 + a * sub - A_HALF + ik
        valid.append(band & (kk >= 0) & (kk < seq_len))
    lse_rows = [jnp.zeros((sub, LANES), F32) for _ in range(nsub)]
    for h in range(heads):
        cs = slice(h * HEAD_DIM, (h + 1) * HEAD_DIM)
        kcat = jnp.concatenate([kp_ref[:, cs], km_ref[:, cs], kn_ref[:, cs]], axis=0)
        vcat = jnp.concatenate([vp_ref[:, cs], vm_ref[:, cs], vn_ref[:, cs]], axis=0)
        for a in range(nsub):
            q = q_ref[a * sub:(a + 1) * sub, cs]
            kw = kcat[a * sub:a * sub + sub + 2 * A_HALF]
            vw = vcat[a * sub:a * sub + sub + 2 * A_HALF]
            s = lax.dot_general(q, kw, (((1,), (1,)), ((), ())), preferred_element_type=F32) * scale
            s = jnp.where(valid[a], s, NEG_INF)
            m = jnp.max(s, axis=-1, keepdims=True)
            p = jnp.exp(s - m)
            den = jnp.sum(p, axis=-1, keepdims=True)
            o = jnp.dot(p.astype(BF16), vw, preferred_element_type=F32) / den
            o_ref[a * sub:(a + 1) * sub, cs] = o.astype(o_ref.dtype)
            lse_rows[a] = jnp.where(lane == h, m + jnp.log(den), lse_rows[a])
    for a in range(nsub):
        lse_ref[a * sub:(a + 1) * sub, :] = lse_rows[a]


def _dilated_pattern(pa, batch, seq, dil):
    width = pa.shape[1] // 3
    heads = width // HEAD_DIM
    sl = seq // dil
    tl = min(256, sl)
    nt = sl // tl
    hb = tl // A_HALF
    last_halo = sl // A_HALF - 1
    view = pa.reshape(batch, sl, dil * 3 * width)

    def main(c):
        return pl.BlockSpec((None, tl, width), lambda b, r, t: (b, t, r * 3 + c))

    def prev(c):
        return pl.BlockSpec((None, A_HALF, width), lambda b, r, t: (b, jnp.maximum(t * hb - 1, 0), r * 3 + c))

    def nxt(c):
        return pl.BlockSpec((None, A_HALF, width),
                            lambda b, r, t: (b, jnp.minimum((t + 1) * hb, last_halo), r * 3 + c))

    kern = functools.partial(_dilated_kernel, tl=tl, seq_len=sl, heads=heads)
    o, lse = pl.pallas_call(
        kern,
        grid=(batch, dil, nt),
        in_specs=[main(0), prev(1), main(1), nxt(1), prev(2), main(2), nxt(2)],
        out_specs=[pl.BlockSpec((None, tl, width), lambda b, r, t: (b, t, r)),
                   pl.BlockSpec((None, tl, LANES), lambda b, r, t: (b, t, r))],
        out_shape=[jax.ShapeDtypeStruct((batch, sl, dil * width), BF16),
                   jax.ShapeDtypeStruct((batch, sl, dil * LANES), F32)],
        compiler_params=_params("arbitrary", "arbitrary", "arbitrary"),
        name=f"dilated_d{dil}",
    )(view, view, view, view, view, view, view)
    return o.reshape(batch * seq, width), lse.reshape(batch * seq, LANES)


def _dilated_combine_kernel(o0_ref, o1_ref, o2_ref, l0_ref, l1_ref, l2_ref, y_ref, *, heads):
    l0, l1, l2 = l0_ref[...], l1_ref[...], l2_ref[...]
    m = jnp.maximum(jnp.maximum(l0, l1), l2)
    e0, e1, e2 = jnp.exp(l0 - m), jnp.exp(l1 - m), jnp.exp(l2 - m)
    inv = 1.0 / (e0 + e1 + e2)
    w0, w1, w2 = e0 * inv, e1 * inv, e2 * inv
    for h in range(heads):
        cs = slice(h * HEAD_DIM, (h + 1) * HEAD_DIM)
        y = (w0[:, h:h + 1] * o0_ref[:, cs].astype(F32) + w1[:, h:h + 1] * o1_ref[:, cs].astype(F32)
             + w2[:, h:h + 1] * o2_ref[:, cs].astype(F32))
        y_ref[:, cs] = y.astype(y_ref.dtype)


def _dilated_attention(pa, batch, seq):
    outs, lses = zip(*[_dilated_pattern(pa, batch, seq, d) for _, d in A_PATTERNS])
    m, width = outs[0].shape
    tm = min(512, m)
    ospec = pl.BlockSpec((tm, width), lambda i: (i, 0))
    lspec = pl.BlockSpec((tm, LANES), lambda i: (i, 0))
    return pl.pallas_call(
        functools.partial(_dilated_combine_kernel, heads=width // HEAD_DIM),
        grid=(m // tm,),
        in_specs=[ospec, ospec, ospec, lspec, lspec, lspec],
        out_specs=ospec,
        out_shape=jax.ShapeDtypeStruct((m, width), BF16),
        compiler_params=_params("arbitrary"),
        name="dilated_combine",
    )(*outs, *lses)


def _pool_kernel(u_ref, w_ref, sc_ref, o_ref, pad_ref, *, seq):
    g = pl.program_id(1)
    pad = 8
    uf = u_ref[...].astype(F32)
    pad_ref[0:pad, :] = jnp.zeros((pad, uf.shape[1]), F32)
    pad_ref[pad + seq:pad + seq + pad, :] = jnp.zeros((pad, uf.shape[1]), F32)
    pad_ref[pad:pad + seq, :] = uf
    pos = lax.broadcasted_iota(jnp.int32, (seq, 1), 0)
    for gi, win in enumerate(B_WINDOWS):
        @pl.when(g == gi)
        def _(win=win):
            before, after = win // 2, win - win // 2
            tot = pad_ref[pad - before:pad - before + seq, :]
            for off in range(-before + 1, after):
                tot = tot + pad_ref[pad + off:pad + off + seq, :]
            cnt = (jnp.minimum(pos + after, seq) - jnp.maximum(pos - before, 0)).astype(F32)
            pooled = tot / cnt - uf
            mixed = jnp.dot(pooled.astype(BF16), w_ref[...].astype(BF16), preferred_element_type=F32)
            o_ref[...] = (mixed * sc_ref[...]).astype(o_ref.dtype)


def _pooling_mixer(pbdx, pool_w, pool_scale, batch, seq, col0):
    ng, cg = pool_w.shape[0], pool_w.shape[1]
    view = pbdx.reshape(batch, seq, pbdx.shape[1])
    cb = col0 // cg
    return pl.pallas_call(
        functools.partial(_pool_kernel, seq=seq),
        grid=(batch, ng),
        in_specs=[pl.BlockSpec((None, seq, cg), lambda b, g: (b, 0, cb + g)),
                  pl.BlockSpec((None, cg, cg), lambda b, g: (g, 0, 0)),
                  pl.BlockSpec((1, cg), lambda b, g: (0, g))],
        out_specs=pl.BlockSpec((None, seq, cg), lambda b, g: (b, 0, g)),
        out_shape=jax.ShapeDtypeStruct((batch, seq, ng * cg), BF16),
        scratch_shapes=[pltpu.VMEM((seq + 16, cg), F32)],
        compiler_params=_params("arbitrary", "arbitrary"),
        name="pooling",
    )(view, pool_w, pool_scale.reshape(1, ng * cg)).reshape(batch * seq, ng * cg)


def _conv_kernel(bg_ref, cg_ref, h_ref, w_ref, o_ref, pad_ref, *, seq):
    pad = 8
    u = cg_ref[...].astype(F32) * h_ref[...].astype(F32)
    cols = u.shape[1]
    pad_ref[0:pad, :] = jnp.zeros((pad, cols), F32)
    pad_ref[pad + seq:pad + seq + pad, :] = jnp.zeros((pad, cols), F32)
    pad_ref[pad:pad + seq, :] = u
    w = w_ref[...]
    conv = (pad_ref[pad - 1:pad - 1 + seq, :] * w[0:1, :] + u * w[1:2, :]
            + pad_ref[pad + 1:pad + 1 + seq, :] * w[2:3, :])
    o_ref[...] = (bg_ref[...].astype(F32) * conv).astype(o_ref.dtype)


def _gated_short_conv(pbdx, conv_w, batch, seq, col0):
    width = conv_w.shape[1]
    cb = 256
    nb = width // cb
    view = pbdx.reshape(batch, seq, pbdx.shape[1])
    base = col0 // cb

    def spec(part):
        return pl.BlockSpec((None, seq, cb), lambda b, c: (b, 0, base + part * nb + c))

    return pl.pallas_call(
        functools.partial(_conv_kernel, seq=seq),
        grid=(batch, nb),
        in_specs=[spec(0), spec(1), spec(2), pl.BlockSpec((3, cb), lambda b, c: (0, c))],
        out_specs=pl.BlockSpec((None, seq, cb), lambda b, c: (b, 0, c)),
        out_shape=jax.ShapeDtypeStruct((batch, seq, width), BF16),
        scratch_shapes=[pltpu.VMEM((seq + 16, cb), F32)],
        compiler_params=_params("arbitrary", "arbitrary"),
        name="short_conv",
    )(view, view, view, conv_w).reshape(batch * seq, width)


def _diff_kernel(q_ref, k_ref, v_ref, lam_ref, nrm_ref, o_ref, vone_ref, *, lam_init, chunk):
    t = pl.program_id(2)
    seq = k_ref.shape[0]
    tq = q_ref.shape[0]

    @pl.when(t == 0)
    def _():
        vone_ref[:, :LANES] = v_ref[...]
        vone_ref[:, LANES:] = jnp.ones((seq, LANES), BF16)

    lq = lam_ref[...].astype(F32)
    lam = (jnp.exp(jnp.sum(lq[0:1] * lq[1:2], axis=-1, keepdims=True))
           - jnp.exp(jnp.sum(lq[2:3] * lq[3:4], axis=-1, keepdims=True)) + lam_init)
    q = q_ref[...]
    lane = lax.broadcasted_iota(jnp.int32, q.shape, 1)
    zero = jnp.zeros_like(q)
    q2 = jnp.concatenate([jnp.where(lane < C_QK_DIM, q, zero), jnp.where(lane >= C_QK_DIM, q, zero)], axis=0)
    m = acc = None
    for c in range(seq // chunk):
        ks = slice(c * chunk, (c + 1) * chunk)
        s = lax.dot_general(q2, k_ref[ks, :], (((1,), (1,)), ((), ())), preferred_element_type=F32)
        mc = jnp.max(s, axis=-1, keepdims=True)
        m_new = mc if m is None else jnp.maximum(m, mc)
        e = jnp.exp2((s - m_new).astype(BF16))
        pv = jnp.dot(e, vone_ref[ks, :], preferred_element_type=F32)
        acc = pv if acc is None else acc * jnp.exp2(m - m_new) + pv
        m = m_new
    out = acc[:, :LANES] / acc[:, LANES:]
    o = out[:tq] - lam * out[tq:]
    o = o * lax.rsqrt(jnp.mean(o * o, axis=-1, keepdims=True) + LN_EPS) * nrm_ref[...] * (1.0 - lam_init)
    o_ref[...] = o.astype(o_ref.dtype)


def _diff_attention(pc, c_lambda, c_norm, batch, seq, layer, tq=1024, chunk=1024):
    heads = pc.shape[1] // (3 * LANES)
    tq = min(tq, seq)
    chunk = min(chunk, seq)
    lam_init = 0.8 - 0.6 * math.exp(-0.3 * layer)
    view = pc.reshape(batch, seq, pc.shape[1])
    return pl.pallas_call(
        functools.partial(_diff_kernel, lam_init=lam_init, chunk=chunk),
        grid=(batch, heads, seq // tq),
        in_specs=[pl.BlockSpec((None, tq, LANES), lambda b, h, t: (b, t, 3 * h)),
                  pl.BlockSpec((None, seq, LANES), lambda b, h, t: (b, 0, 3 * h + 1)),
                  pl.BlockSpec((None, seq, LANES), lambda b, h, t: (b, 0, 3 * h + 2)),
                  pl.BlockSpec((4, C_QK_DIM), lambda b, h, t: (0, 0)),
                  pl.BlockSpec((1, LANES), lambda b, h, t: (0, 0))],
        out_specs=pl.BlockSpec((None, tq, LANES), lambda b, h, t: (b, t, h)),
        out_shape=jax.ShapeDtypeStruct((batch, seq, heads * LANES), BF16),
        scratch_shapes=[pltpu.VMEM((seq, 2 * LANES), BF16)],
        compiler_params=_params("arbitrary", "arbitrary", "arbitrary"),
        name="diff_attention",
    )(view, view, view, c_lambda, c_norm.reshape(1, LANES)).reshape(batch * seq, heads * LANES)


def _cross_kernel(q_ref, kv_ref, o_ref, *, heads):
    scale = HEAD_DIM ** -0.5
    width = heads * HEAD_DIM
    for h in range(heads):
        cs = slice(h * HEAD_DIM, (h + 1) * HEAD_DIM)
        k = kv_ref[:, h * HEAD_DIM:(h + 1) * HEAD_DIM]
        v = kv_ref[:, width + h * HEAD_DIM:width + (h + 1) * HEAD_DIM]
        s = lax.dot_general(q_ref[:, cs], k, (((1,), (1,)), ((), ())), preferred_element_type=F32) * scale
        m = jnp.max(s, axis=-1, keepdims=True)
        e = jnp.exp(s - m)
        den = jnp.sum(e, axis=-1, keepdims=True)
        o = jnp.dot(e.astype(BF16), v, preferred_element_type=F32) / den
        o_ref[:, cs] = o.astype(o_ref.dtype)


def _cross_attention(pbdx, kv, batch, seq, col0, width, tq=512):
    mem_len = kv.shape[0] // batch
    tq = min(tq, seq)
    view = pbdx.reshape(batch, seq, pbdx.shape[1])
    return pl.pallas_call(
        functools.partial(_cross_kernel, heads=width // HEAD_DIM),
        grid=(batch, seq // tq),
        in_specs=[pl.BlockSpec((None, tq, width), lambda b, t: (b, t, col0 // width)),
                  pl.BlockSpec((None, mem_len, 2 * width), lambda b, t: (b, 0, 0))],
        out_specs=pl.BlockSpec((None, tq, width), lambda b, t: (b, t, 0)),
        out_shape=jax.ShapeDtypeStruct((batch, seq, width), BF16),
        compiler_params=_params("arbitrary", "arbitrary"),
        name="cross_attention",
    )(view, kv.reshape(batch, mem_len, 2 * width)).reshape(batch * seq, width)


def _merge_kernel(x_ref, ya_ref, yb_ref, yc_ref, yd_ref, yx_ref,
                  g0, g1, g2, g3, g4, p0, p1, p2, p3, p4, b0, b1, b2, b3, b4, o_ref):
    x = x_ref[...]
    merged = None
    for y_ref, g_ref, p_ref, b_ref in ((ya_ref, g0, p0, b0), (yb_ref, g1, p1, b1), (yc_ref, g2, p2, b2),
                                       (yd_ref, g3, p3, b3), (yx_ref, g4, p4, b4)):
        gate = jax.nn.sigmoid(jnp.dot(x, g_ref[...], preferred_element_type=F32) + b_ref[...])
        term = gate * jnp.dot(y_ref[...], p_ref[...], preferred_element_type=F32)
        merged = term if merged is None else merged + term
    o_ref[...] = merged.astype(o_ref.dtype)


def _gated_merge(xb, ys, w_gate, w_branch, w_branch_x, b_gate, tm=512, tn=512):
    m, d = xb.shape
    tm = min(tm, m)
    nb = d // tn
    n_br = 5
    act = [pl.BlockSpec((tm, d), lambda j, i: (i, 0))]
    act += [pl.BlockSpec((tm, y.shape[1]), lambda j, i: (i, 0)) for y in ys]
    gates = [pl.BlockSpec((d, tn), functools.partial(lambda j, i, br: (0, br * nb + j), br=br)) for br in range(n_br)]
    projs = [pl.BlockSpec((None, w_branch.shape[1], tn), functools.partial(lambda j, i, br: (br, 0, j), br=br))
             for br in range(4)]
    projs.append(pl.BlockSpec((w_branch_x.shape[0], tn), lambda j, i: (0, j)))
    biases = [pl.BlockSpec((1, tn), functools.partial(lambda j, i, br: (0, br * nb + j), br=br)) for br in range(n_br)]
    bg2 = b_gate.reshape(1, n_br * d)
    return pl.pallas_call(
        _merge_kernel,
        grid=(nb, m // tm),
        in_specs=act + gates + projs + biases,
        out_specs=pl.BlockSpec((tm, tn), lambda j, i: (i, j)),
        out_shape=jax.ShapeDtypeStruct((m, d), BF16),
        compiler_params=_params("arbitrary", "arbitrary"),
        name="gated_merge",
    )(xb, *ys, *([w_gate] * n_br), *([w_branch] * 4), w_branch_x, *([bg2] * n_br))


def _layer_norm_rows(h, g, b):
    mu = jnp.mean(h, axis=-1, keepdims=True)
    hc = h - mu
    var = jnp.mean(hc * hc, axis=-1, keepdims=True)
    return hc * lax.rsqrt(var + LN_EPS) * g + b


def _out_ln_kernel(mg_ref, w_ref, x_ref, g_ref, b_ref, xo_ref, xb_ref, *, alpha):
    y = jnp.dot(mg_ref[...], w_ref[...], preferred_element_type=F32)
    xn = _layer_norm_rows(alpha * x_ref[...] + y, g_ref[...], b_ref[...])
    xo_ref[...] = xn
    xb_ref[...] = xn.astype(BF16)


def _out_proj_ln(merged, w_out_b, x, g, b, alpha, tm=512):
    m, d = x.shape
    tm = min(tm, m)
    row = pl.BlockSpec((tm, d), lambda i: (i, 0))
    vec = pl.BlockSpec((1, d), lambda i: (0, 0))
    return pl.pallas_call(
        functools.partial(_out_ln_kernel, alpha=alpha),
        grid=(m // tm,),
        in_specs=[row, pl.BlockSpec((d, d), lambda i: (0, 0)), row, vec, vec],
        out_specs=[row, row],
        out_shape=[jax.ShapeDtypeStruct((m, d), F32), jax.ShapeDtypeStruct((m, d), BF16)],
        compiler_params=_params("arbitrary"),
        name="out_proj_ln",
    )(merged, w_out_b, x, g.reshape(1, d), b.reshape(1, d))


def _split_bf16(a):
    hi = a.astype(BF16)
    lo = (a - hi.astype(F32)).astype(BF16)
    return hi, lo


def _router_kernel(x_ref, w_ref, b_ref, idx_ref, wt_ref, cnt_ref):
    @pl.when(pl.program_id(0) == 0)
    def _():
        cnt_ref[...] = jnp.zeros_like(cnt_ref)

    x = x_ref[...]
    w = w_ref[...]
    xh, xl = _split_bf16(x)
    wh, wl = _split_bf16(w)
    logits = (jnp.dot(xh, wh, preferred_element_type=F32) + jnp.dot(xl, wh, preferred_element_type=F32)
              + jnp.dot(xh, wl, preferred_element_type=F32))
    lt = logits.T[:N_EXPERTS, :]
    tm = lt.shape[1]
    scores = jax.nn.sigmoid(lt)
    biased = scores + b_ref[...]
    gsz = N_EXPERTS // N_GROUPS
    sub8 = lax.broadcasted_iota(jnp.int32, (gsz, tm), 0).astype(F32)
    grp_rows = []
    for g in range(N_GROUPS):
        blk = biased[g * gsz:(g + 1) * gsz, :]
        m1 = jnp.max(blk, axis=0, keepdims=True)
        i1 = jnp.min(jnp.where(blk == m1, sub8, float(gsz)), axis=0, keepdims=True)
        m2 = jnp.max(jnp.where(sub8 == i1, -jnp.inf, blk), axis=0, keepdims=True)
        grp_rows.append(m1 + m2)
    grp = jnp.concatenate(grp_rows, axis=0)
    subg = lax.broadcasted_iota(jnp.int32, (N_GROUPS, tm), 0).astype(F32)
    gsel = jnp.zeros((N_GROUPS, tm), F32)
    for _ in range(TOP_GROUPS):
        mg = jnp.max(grp, axis=0, keepdims=True)
        ig = jnp.min(jnp.where(grp == mg, subg, float(N_GROUPS)), axis=0, keepdims=True)
        hit = subg == ig
        gsel = jnp.where(hit, 1.0, gsel)
        grp = jnp.where(hit, -jnp.inf, grp)
    emask = jnp.concatenate([jnp.broadcast_to(gsel[g:g + 1, :], (gsz, tm)) for g in range(N_GROUPS)], axis=0)
    cand = jnp.where(emask > 0.5, biased, -jnp.inf)
    sube = lax.broadcasted_iota(jnp.int32, (N_EXPERTS, tm), 0).astype(F32)
    idx_rows, w_rows = [], []
    chosen = jnp.zeros((N_EXPERTS, tm), F32)
    for _ in range(TOP_K):
        mc = jnp.max(cand, axis=0, keepdims=True)
        ic = jnp.min(jnp.where(cand == mc, sube, float(N_EXPERTS)), axis=0, keepdims=True)
        hit = sube == ic
        idx_rows.append(ic)
        w_rows.append(jnp.sum(jnp.where(hit, scores, 0.0), axis=0, keepdims=True))
        chosen = jnp.where(hit, 1.0, chosen)
        cand = jnp.where(hit, -jnp.inf, cand)
    cnt_ref[...] += jnp.sum(chosen, axis=1, keepdims=True)
    wsel = jnp.concatenate(w_rows, axis=0)
    wsel = wsel / jnp.sum(wsel, axis=0, keepdims=True) * ROUTE_SCALE
    idx_ref[...] = jnp.concatenate(idx_rows, axis=0).astype(jnp.int32)
    wt_ref[...] = wsel


def _router(x, router_w, router_b, tm=512):
    m, d = x.shape
    tm = min(tm, m)
    w_pad = jnp.pad(router_w, ((0, 0), (0, LANES - N_EXPERTS)))
    return pl.pallas_call(
        _router_kernel,
        grid=(m // tm,),
        in_specs=[pl.BlockSpec((tm, d), lambda i: (i, 0)),
                  pl.BlockSpec((d, LANES), lambda i: (0, 0)),
                  pl.BlockSpec((N_EXPERTS, 1), lambda i: (0, 0))],
        out_specs=[pl.BlockSpec((TOP_K, tm), lambda i: (0, i)), pl.BlockSpec((TOP_K, tm), lambda i: (0, i)),
                   pl.BlockSpec((N_EXPERTS, LANES), lambda i: (0, 0))],
        out_shape=[jax.ShapeDtypeStruct((TOP_K, m), jnp.int32), jax.ShapeDtypeStruct((TOP_K, m), F32),
                   jax.ShapeDtypeStruct((N_EXPERTS, LANES), F32)],
        compiler_params=_params("arbitrary"),
        name="router",
    )(x, w_pad, router_b.reshape(N_EXPERTS, 1))


def _slot_kernel(idx_ref, base_ref, dest_ref, tri_ref, run_ref):
    tm = idx_ref.shape[1]

    @pl.when(pl.program_id(0) == 0)
    def _():
        row = lax.broadcasted_iota(jnp.int32, (tm, tm), 0)
        col = lax.broadcasted_iota(jnp.int32, (tm, tm), 1)
        tri_ref[...] = jnp.where(row < col, 1.0, 0.0).astype(BF16)
        run_ref[...] = base_ref[...]

    sube = lax.broadcasted_iota(jnp.int32, (N_EXPERTS, tm), 0)
    idx = idx_ref[...]
    base = run_ref[...]
    hits = [sube == idx[k:k + 1, :] for k in range(TOP_K)]
    chosen = jnp.zeros((N_EXPERTS, tm), F32)
    for hit in hits:
        chosen = jnp.where(hit, 1.0, chosen)
    before = jnp.dot(chosen.astype(BF16), tri_ref[...], preferred_element_type=F32)
    slot = base + before
    rows = [jnp.sum(jnp.where(hit, slot, 0.0), axis=0, keepdims=True) for hit in hits]
    run_ref[...] = base + jnp.sum(chosen, axis=1, keepdims=True)
    dest_ref[...] = jnp.concatenate(rows, axis=0).astype(jnp.int32)


def _assign_slots(idx_t, base, tm=512):
    kk, m = idx_t.shape
    tm = min(tm, m)
    return pl.pallas_call(
        _slot_kernel,
        grid=(m // tm,),
        in_specs=[pl.BlockSpec((kk, tm), lambda i: (0, i)), pl.BlockSpec((N_EXPERTS, 1), lambda i: (0, 0))],
        out_specs=pl.BlockSpec((kk, tm), lambda i: (0, i)),
        out_shape=jax.ShapeDtypeStruct((kk, m), jnp.int32),
        scratch_shapes=[pltpu.VMEM((tm, tm), BF16), pltpu.VMEM((N_EXPERTS, 1), F32)],
        compiler_params=_params("arbitrary"),
        name="assign_slots",
    )(idx_t, base.astype(F32).reshape(N_EXPERTS, 1))


def _expert_kernel(be_ref, nb_ref, x_ref, wg_ref, wu_ref, wd_ref, *rest):
    o_ref = rest[-1]
    del be_ref
    i = pl.program_id(0)

    @pl.when(i < nb_ref[0])
    def _():
        x = x_ref[...]
        y = None
        for c in range(wg_ref.shape[1] // MXU_WIDTH):
            cs = slice(c * MXU_WIDTH, (c + 1) * MXU_WIDTH)
            gate = jnp.dot(x, wg_ref[:, cs].astype(BF16), preferred_element_type=F32)
            up = jnp.dot(x, wu_ref[:, cs].astype(BF16), preferred_element_type=F32)
            act = (jax.nn.silu(gate) * up).astype(BF16)
            part = jnp.dot(act, wd_ref[cs, :].astype(BF16), preferred_element_type=F32)
            y = part if y is None else y + part
        o_ref[...] = y.astype(o_ref.dtype)

    @pl.when(i >= nb_ref[0])
    def _():
        o_ref[...] = jnp.zeros_like(o_ref)


def _expert_ffn(xs, block_e, nblocks, w_gate, w_up, w_down, layer, rows, run_after=()):
    ns, d = xs.shape
    de = w_gate.shape[3]
    grid_spec = pltpu.PrefetchScalarGridSpec(
        num_scalar_prefetch=2,
        grid=(ns // rows,),
        in_specs=[pl.BlockSpec((rows, d), lambda i, be, nb: (i, 0)),
                  pl.BlockSpec((None, None, d, de), lambda i, be, nb: (layer, be[i], 0, 0)),
                  pl.BlockSpec((None, None, d, de), lambda i, be, nb: (layer, be[i], 0, 0)),
                  pl.BlockSpec((None, None, de, d), lambda i, be, nb: (layer, be[i], 0, 0))]
        + [pl.BlockSpec(memory_space=pl.ANY)] * len(run_after),
        out_specs=pl.BlockSpec((rows, d), lambda i, be, nb: (i, 0)),
    )
    return pl.pallas_call(
        _expert_kernel,
        grid_spec=grid_spec,
        out_shape=jax.ShapeDtypeStruct((ns, d), BF16),
        compiler_params=_params("arbitrary"),
        name="expert_ffn",
    )(block_e, nblocks, xs, w_gate, w_up, w_down, *run_after)


def _combine_ln_kernel(*refs, alpha):
    yg_refs = refs[:TOP_K]
    wt_ref, sh_ref, x_ref, g_ref, b_ref, xo_ref, xb_ref = refs[TOP_K:]
    wt = wt_ref[...]
    routed = sh_ref[...].astype(F32)
    for k in range(TOP_K):
        routed = routed + wt[:, k:k + 1] * yg_refs[k][...].astype(F32)
    xn = _layer_norm_rows(alpha * x_ref[...] + routed, g_ref[...], b_ref[...])
    xo_ref[...] = xn
    xb_ref[...] = xn.astype(BF16)


def _combine_ln(yg, wt, shared, x, g, b, alpha, tm=256):
    m, d = x.shape
    tm = min(tm, m)
    nb = m // tm
    row = pl.BlockSpec((tm, d), lambda i: (i, 0))
    vec = pl.BlockSpec((1, d), lambda i: (0, 0))
    yg_specs = [pl.BlockSpec((tm, d), functools.partial(lambda i, k: (k * nb + i, 0), k=k)) for k in range(TOP_K)]
    return pl.pallas_call(
        functools.partial(_combine_ln_kernel, alpha=alpha),
        grid=(nb,),
        in_specs=yg_specs + [pl.BlockSpec((tm, TOP_K), lambda i: (i, 0)), row, row, vec, vec],
        out_specs=[row, row],
        out_shape=[jax.ShapeDtypeStruct((m, d), F32), jax.ShapeDtypeStruct((m, d), BF16)],
        compiler_params=_params("arbitrary"),
        name="combine_ln",
    )(*([yg] * TOP_K), wt, shared, x, g.reshape(1, d), b.reshape(1, d))


def _moe_layer(x1, xb1, layer, router_w, router_b, e_gate, e_up, e_down, s_gate, s_up, s_down, g, b, alpha):
    t, d = x1.shape
    rows = min(MOE_ROWS, t)
    idx_t, wt_t, cnt = _router(x1, router_w, router_b)
    n_assign = t * TOP_K
    i32 = jnp.int32
    counts = cnt[:, 0].astype(i32)
    padded = (counts + rows - 1) // rows * rows
    ends_p = jnp.cumsum(padded)
    pad_e = padded - counts
    dest = _assign_slots(idx_t, ends_p - padded).reshape(n_assign)
    nblk = n_assign // rows + N_EXPERTS
    n_pad = nblk * rows - n_assign
    pad_i = jnp.arange(n_pad, dtype=i32)
    pad_before = jnp.cumsum(pad_e) - pad_e
    pad_slot = pad_i + jnp.sum(jnp.where(pad_before[None, :] <= pad_i[:, None], counts[None, :], 0), axis=1)
    keys = jnp.concatenate([dest, pad_slot])
    toks = jnp.concatenate([jnp.arange(n_assign, dtype=i32) % t, pad_i % t])
    _, slot_tok = lax.sort_key_val(keys, toks)
    blk0 = jnp.arange(nblk, dtype=i32) * rows
    block_e = jnp.minimum(jnp.sum((ends_p[None, :] <= blk0[:, None]).astype(i32), axis=1), N_EXPERTS - 1)
    nused = (ends_p[-1] // rows).astype(i32).reshape(1)
    xs = jnp.take(xb1, slot_tok, axis=0, mode='clip')
    srows = min(2 * MOE_ROWS, t)
    shared = _expert_ffn(xb1, jnp.zeros((t // srows,), i32), jnp.full((1,), t // srows, i32),
                         s_gate[:, None], s_up[:, None], s_down[:, None], layer, srows)
    ys = _expert_ffn(xs, block_e, nused, e_gate, e_up, e_down, layer, rows, run_after=(shared,))
    yg = jnp.take(ys, dest, axis=0, mode='clip')
    return _combine_ln(yg, wt_t.T, shared, x1, g, b, alpha)


def _mixer_layer(x, xb, memb, layer, tabs_a, tabs_c, batch, seq, w_in, w_c, pool_w, pool_scale, c_lambda, c_norm,
                 conv_w, w_mem_kv, merge_w, b_gate, w_out_b, g, b, alpha):
    mix = merge_w[1].shape[1]
    xw = merge_w[2].shape[0]
    a_cols, c_cols, d_cols = 3 * mix, 3 * mix, 3 * mix
    c0_b, c0_d = a_cols, a_cols + mix + c_cols
    pa = _proj(xb, w_in, layer, range(3), mix, tabs_a, seq, half=HEAD_DIM // 8,
               rope_blocks=(1,) * (mix // LANES), rope_jmax=2)
    tn_c = 12 * LANES
    pc = _proj(xb, w_c, 0, range(c_cols // tn_c), tn_c, tabs_c, seq,
               half=C_QK_DIM // 8, rope_blocks=(1, 2, 0) * 4, tm=512)
    blocks = list(range(c0_b // mix, c0_b // mix + 1)) + list(range(c0_d // mix, c0_d // mix + d_cols // mix))
    pbd = _proj(xb, w_in, layer, blocks, mix, tabs_a, seq)
    px = _proj(xb, w_in, layer, [(c0_d + d_cols) // xw], xw, tabs_a, seq)
    ya = _dilated_attention(pa, batch, seq)
    yb = _pooling_mixer(pbd, pool_w[layer], pool_scale[layer], batch, seq, 0)
    yc = _diff_attention(pc, c_lambda[layer], c_norm[layer], batch, seq, layer)
    yd = _gated_short_conv(pbd, conv_w[layer], batch, seq, mix)
    kv = _proj(memb, w_mem_kv, layer, range(2), xw, tabs_a, seq)
    yx = _cross_attention(px, kv, batch, seq, 0, xw)
    merged = _gated_merge(xb, (ya, yb, yc, yd, yx), *merge_w, b_gate[layer])
    return _out_proj_ln(merged, w_out_b, x, g[layer], b[layer], alpha)


def kernel(x, mem, w_in, pool_w, pool_scale, c_lambda, c_norm, conv_w, w_mem_kv, w_branch, w_branch_x,
           w_gate, b_gate, w_out, ln1_g, ln1_b, router_w, router_b, e_gate, e_up, e_down, s_gate, s_up,
           s_down, ln2_g, ln2_b):
    batch, seq, d = x.shape
    depth = w_in.shape[0]
    alpha = (2.0 * depth) ** 0.25
    tabs_a = _rope_tables(seq, HEAD_DIM // 4, LANES)
    tabs_ck = _rope_tables(seq, C_QK_DIM // 4, C_QK_DIM)
    q_scale = C_QK_DIM ** -0.5 * math.log2(math.e)
    tabs_c = tuple(tab * q_scale for tab in tabs_ck) + tabs_ck
    mix = w_branch.shape[2]
    groups = BATCH_GROUPS if batch % BATCH_GROUPS == 0 else 1
    gb = batch // groups
    state = []
    for gi in range(groups):
        xg = x[gi * gb:(gi + 1) * gb].reshape(gb * seq, d)
        state.append((xg, xg.astype(BF16), mem[gi * gb:(gi + 1) * gb].reshape(-1, d).astype(BF16)))
    for l in range(depth):
        w_c = w_in[l:l + 1, :, 4 * mix:7 * mix]
        merge_w = (w_gate[l].astype(BF16), w_branch[l].astype(BF16), w_branch_x[l].astype(BF16))
        w_out_b = w_out[l].astype(BF16)
        for gi in range(groups):
            xf, xb, memb = state[gi]
            xf, xb = _mixer_layer(xf, xb, memb, l, tabs_a, tabs_c, gb, seq, w_in, w_c, pool_w, pool_scale,
                                  c_lambda, c_norm, conv_w, w_mem_kv, merge_w, b_gate, w_out_b,
                                  ln1_g, ln1_b, alpha)
            xf, xb = _moe_layer(xf, xb, l, router_w[l], router_b[l], e_gate, e_up, e_down, s_gate,
                                s_up, s_down, ln2_g[l], ln2_b[l], alpha)
            state[gi] = (xf, xb, memb)
    return jnp.concatenate([s[0] for s in state], axis=0).reshape(batch, seq, d)
```

```python
import functools
import math

import jax
import jax.numpy as jnp
from jax import lax
from jax.experimental import pallas as pl
from jax.experimental.pallas import tpu as pltpu

F32 = jnp.float32
BF16 = jnp.bfloat16

LANES = 128
MXU_WIDTH = 256
HEAD_DIM = 128
ROPE_THETA = 500000.0
A_PATTERNS = ((128, 1), (512, 4), (2048, 16))
A_HALF = 64
B_WINDOWS = (2, 4, 8, 16)
C_QK_DIM = 64
N_EXPERTS = 64
N_GROUPS = 8
TOP_GROUPS = 4
TOP_K = 8
ROUTE_SCALE = 2.5
LN_EPS = 1e-5
NEG_INF = -1e30
VMEM_LIMIT_BYTES = 56 * 1024 * 1024
MOE_ROWS = 256
BATCH_GROUPS = 1


def _params(*sem):
    return pltpu.CompilerParams(dimension_semantics=sem, vmem_limit_bytes=VMEM_LIMIT_BYTES)


def _rope_tables(seq, rot_dim, period):
    half = rot_dim // 2
    inv = 1.0 / (ROPE_THETA ** (jnp.arange(0, rot_dim, 2, dtype=F32) / rot_dim))
    ang = jnp.arange(seq, dtype=F32)[:, None] * inv[None, :]
    cos, sin = jnp.cos(ang), jnp.sin(ang)
    lane = jnp.arange(LANES) % period
    first = lane < half
    second = (lane >= half) & (lane < 2 * half)
    idx = jnp.where(first, lane, jnp.where(second, lane - half, 0))
    cos_l, sin_l = cos[:, idx], sin[:, idx]
    ct = jnp.where(first | second, cos_l, 1.0)
    s1 = jnp.where(first, -sin_l, 0.0)
    s2 = jnp.where(second, sin_l, 0.0)
    return ct.astype(F32), s1.astype(F32), s2.astype(F32)


def _proj_kernel(x_ref, w_ref, *rest, half, rope_blocks, rope_jmax):
    tab_refs, o_ref, wb_ref = rest[:-2], rest[-2], rest[-1]
    j = pl.program_id(0)
    i = pl.program_id(1)

    @pl.when(i == 0)
    def _():
        wb_ref[...] = w_ref[...].astype(BF16)

    def plain():
        o_ref[...] = jnp.dot(x_ref[...], wb_ref[...], preferred_element_type=F32).astype(o_ref.dtype)

    def roped():
        tabs = [r[...] for r in tab_refs]
        per = MXU_WIDTH // LANES
        for c in range(len(rope_blocks) // per):
            acc = jnp.dot(x_ref[...], wb_ref[:, c * MXU_WIDTH:(c + 1) * MXU_WIDTH], preferred_element_type=F32)
            for b in range(c * per, (c + 1) * per):
                use = rope_blocks[b]
                blk = acc[:, (b - c * per) * LANES:(b - c * per + 1) * LANES]
                if use:
                    ct, s1, s2 = tabs[3 * (use - 1):3 * use]
                    packed = pltpu.bitcast(blk.astype(BF16), jnp.uint32)
                    fwd = pltpu.bitcast(pltpu.roll(packed, LANES - half, 1), BF16).astype(F32)
                    bwd = pltpu.bitcast(pltpu.roll(packed, half, 1), BF16).astype(F32)
                    blk = blk * ct + fwd * s1 + bwd * s2
                o_ref[:, b * LANES:(b + 1) * LANES] = blk.astype(o_ref.dtype)

    if not any(rope_blocks):
        plain()
    elif rope_jmax is None:
        roped()
    else:
        pl.when(j < rope_jmax)(roped)
        pl.when(j >= rope_jmax)(plain)


def _proj(xb, w, layer, col_blocks, tn, tables, seq, *, half=0, rope_blocks=None, rope_jmax=None, tm=1024):
    m, k = xb.shape
    nj = len(col_blocks)
    tm = min(tm, m)
    rope_blocks = tuple(int(u) for u in rope_blocks) if rope_blocks is not None else (0,) * (tn // LANES)
    col_blocks = tuple(col_blocks)
    first, contiguous = col_blocks[0], all(col_blocks[a] == col_blocks[0] + a for a in range(nj))
    if contiguous:
        w_map = lambda j, i: (layer, 0, first + j)
    else:
        n0 = next(a for a in range(1, nj) if col_blocks[a] != col_blocks[0] + a)
        second = col_blocks[n0]
        w_map = lambda j, i: (layer, 0, jnp.where(j < n0, first + j, second + j - n0))
    sblocks = seq // tm if seq >= tm else 1
    tab_spec = pl.BlockSpec((tm, LANES), lambda j, i: (i % sblocks, 0))
    kern = functools.partial(_proj_kernel, half=half, rope_blocks=rope_blocks, rope_jmax=rope_jmax)
    return pl.pallas_call(
        kern,
        grid=(nj, m // tm),
        in_specs=[pl.BlockSpec((tm, k), lambda j, i: (i, 0)),
                  pl.BlockSpec((None, k, tn), w_map)] + [tab_spec] * len(tables),
        out_specs=pl.BlockSpec((tm, tn), lambda j, i: (i, j)),
        out_shape=jax.ShapeDtypeStruct((m, nj * tn), BF16),
        scratch_shapes=[pltpu.VMEM((k, tn), BF16)],
        compiler_params=_params("arbitrary", "arbitrary"),
        name="proj",
    )(xb, w, *tables)


def _dilated_kernel(q_ref, kp_ref, km_ref, kn_ref, vp_ref, vm_ref, vn_ref, o_ref, lse_ref, *, tl, seq_len, heads):
    t = pl.program_id(2)
    l0 = t * tl
    sub = 128
    nsub = tl // sub
    scale = HEAD_DIM ** -0.5
    iq = lax.broadcasted_iota(jnp.int32, (sub, sub + 2 * A_HALF), 0)
    ik = lax.broadcasted_iota(jnp.int32, (sub, sub + 2 * A_HALF), 1)
    band = jnp.abs(ik - A_HALF - iq) <= A_HALF
    lane = lax.broadcasted_iota(jnp.int32, (sub, LANES), 1)
    valid = []
    for a in range(nsub):
        kk = l0 + a * sub - A_HALF + ik
        valid.append(band & (kk >= 0) & (kk < seq_len))
    lse_rows = [jnp.zeros((sub, LANES), F32) for _ in range(nsub)]
    for h in range(heads):
        cs = slice(h * HEAD_DIM, (h + 1) * HEAD_DIM)
        kcat = jnp.concatenate([kp_ref[:, cs], km_ref[:, cs], kn_ref[:, cs]], axis=0)
        vcat = jnp.concatenate([vp_ref[:, cs], vm_ref[:, cs], vn_ref[:, cs]], axis=0)
        for a in range(nsub):
            q = q_ref[a * sub:(a + 1) * sub, cs]
            kw = kcat[a * sub:a * sub + sub + 2 * A_HALF]
            vw = vcat[a * sub:a * sub + sub + 2 * A_HALF]
            s = lax.dot_general(q, kw, (((1,), (1,)), ((), ())), preferred_element_type=F32) * scale
            s = jnp.where(valid[a], s, NEG_INF)
            m = jnp.max(s, axis=-1, keepdims=True)
            p = jnp.exp(s - m)
            den = jnp.sum(p, axis=-1, keepdims=True)
            o = jnp.dot(p.astype(BF16), vw, preferred_element_type=F32) / den
            o_ref[a * sub:(a + 1) * sub, cs] = o.astype(o_ref.dtype)
            lse_rows[a] = jnp.where(lane == h, m + jnp.log(den), lse_rows[a])
    for a in range(nsub):
        lse_ref[a * sub:(a + 1) * sub, :] = lse_rows[a]


def _dilated_pattern(pa, batch, seq, dil):
    width = pa.shape[1] // 3
    heads = width // HEAD_DIM
    sl = seq // dil
    tl = min(512, sl)
    nt = sl // tl
    hb = tl // A_HALF
    last_halo = sl // A_HALF - 1
    view = pa.reshape(batch, sl, dil * 3 * width)

    def main(c):
        return pl.BlockSpec((None, tl, width), lambda b, r, t: (b, t, r * 3 + c))

    def prev(c):
        return pl.BlockSpec((None, A_HALF, width), lambda b, r, t: (b, jnp.maximum(t * hb - 1, 0), r * 3 + c))

    def nxt(c):
        return pl.BlockSpec((None, A_HALF, width),
                            lambda b, r, t: (b, jnp.minimum((t + 1) * hb, last_halo), r * 3 + c))

    kern = functools.partial(_dilated_kernel, tl=tl, seq_len=sl, heads=heads)
    o, lse = pl.pallas_call(
        kern,
        grid=(batch, dil, nt),
        in_specs=[main(0), prev(1), main(1), nxt(1), prev(2), main(2), nxt(2)],
        out_specs=[pl.BlockSpec((None, tl, width), lambda b, r, t: (b, t, r)),
                   pl.BlockSpec((None, tl, LANES), lambda b, r, t: (b, t, r))],
        out_shape=[jax.ShapeDtypeStruct((batch, sl, dil * width), BF16),
                   jax.ShapeDtypeStruct((batch, sl, dil * LANES), F32)],
        compiler_params=_params("arbitrary", "arbitrary", "arbitrary"),
        name=f"dilated_d{dil}",
    )(view, view, view, view, view, view, view)
    return o.reshape(batch * seq, width), lse.reshape(batch * seq, LANES)


def _dilated_combine_kernel(o0_ref, o1_ref, o2_ref, l0_ref, l1_ref, l2_ref, y_ref, *, heads):
    l0, l1, l2 = l0_ref[...], l1_ref[...], l2_ref[...]
    m = jnp.maximum(jnp.maximum(l0, l1), l2)
    e0, e1, e2 = jnp.exp(l0 - m), jnp.exp(l1 - m), jnp.exp(l2 - m)
    inv = 1.0 / (e0 + e1 + e2)
    w0, w1, w2 = e0 * inv, e1 * inv, e2 * inv
    for h in range(heads):
        cs = slice(h * HEAD_DIM, (h + 1) * HEAD_DIM)
        y = (w0[:, h:h + 1] * o0_ref[:, cs].astype(F32) + w1[:, h:h + 1] * o1_ref[:, cs].astype(F32)
             + w2[:, h:h + 1] * o2_ref[:, cs].astype(F32))
        y_ref[:, cs] = y.astype(y_ref.dtype)


def _dilated_attention(pa, batch, seq):
    outs, lses = zip(*[_dilated_pattern(pa, batch, seq, d) for _, d in A_PATTERNS])
    m, width = outs[0].shape
    tm = min(512, m)
    ospec = pl.BlockSpec((tm, width), lambda i: (i, 0))
    lspec = pl.BlockSpec((tm, LANES), lambda i: (i, 0))
    return pl.pallas_call(
        functools.partial(_dilated_combine_kernel, heads=width // HEAD_DIM),
        grid=(m // tm,),
        in_specs=[ospec, ospec, ospec, lspec, lspec, lspec],
        out_specs=ospec,
        out_shape=jax.ShapeDtypeStruct((m, width), BF16),
        compiler_params=_params("arbitrary"),
        name="dilated_combine",
    )(*outs, *lses)


def _pool_kernel(u_ref, w_ref, sc_ref, o_ref, pad_ref, *, seq):
    g = pl.program_id(1)
    pad = 8
    uf = u_ref[...].astype(F32)
    pad_ref[0:pad, :] = jnp.zeros((pad, uf.shape[1]), F32)
    pad_ref[pad + seq:pad + seq + pad, :] = jnp.zeros((pad, uf.shape[1]), F32)
    pad_ref[pad:pad + seq, :] = uf
    pos = lax.broadcasted_iota(jnp.int32, (seq, 1), 0)
    for gi, win in enumerate(B_WINDOWS):
        @pl.when(g == gi)
        def _(win=win):
            before, after = win // 2, win - win // 2
            tot = pad_ref[pad - before:pad - before + seq, :]
            for off in range(-before + 1, after):
                tot = tot + pad_ref[pad + off:pad + off + seq, :]
            cnt = (jnp.minimum(pos + after, seq) - jnp.maximum(pos - before, 0)).astype(F32)
            pooled = tot / cnt - uf
            mixed = jnp.dot(pooled.astype(BF16), w_ref[...].astype(BF16), preferred_element_type=F32)
            o_ref[...] = (mixed * sc_ref[...]).astype(o_ref.dtype)


def _pooling_mixer(pbdx, pool_w, pool_scale, batch, seq, col0):
    ng, cg = pool_w.shape[0], pool_w.shape[1]
    view = pbdx.reshape(batch, seq, pbdx.shape[1])
    cb = col0 // cg
    return pl.pallas_call(
        functools.partial(_pool_kernel, seq=seq),
        grid=(batch, ng),
        in_specs=[pl.BlockSpec((None, seq, cg), lambda b, g: (b, 0, cb + g)),
                  pl.BlockSpec((None, cg, cg), lambda b, g: (g, 0, 0)),
                  pl.BlockSpec((1, cg), lambda b, g: (0, g))],
        out_specs=pl.BlockSpec((None, seq, cg), lambda b, g: (b, 0, g)),
        out_shape=jax.ShapeDtypeStruct((batch, seq, ng * cg), BF16),
        scratch_shapes=[pltpu.VMEM((seq + 16, cg), F32)],
        compiler_params=_params("arbitrary", "arbitrary"),
        name="pooling",
    )(view, pool_w, pool_scale.reshape(1, ng * cg)).reshape(batch * seq, ng * cg)


def _conv_kernel(bg_ref, cg_ref, h_ref, w_ref, o_ref, pad_ref, *, seq):
    pad = 8
    u = cg_ref[...].astype(F32) * h_ref[...].astype(F32)
    cols = u.shape[1]
    pad_ref[0:pad, :] = jnp.zeros((pad, cols), F32)
    pad_ref[pad + seq:pad + seq + pad, :] = jnp.zeros((pad, cols), F32)
    pad_ref[pad:pad + seq, :] = u
    w = w_ref[...]
    conv = (pad_ref[pad - 1:pad - 1 + seq, :] * w[0:1, :] + u * w[1:2, :]
            + pad_ref[pad + 1:pad + 1 + seq, :] * w[2:3, :])
    o_ref[...] = (bg_ref[...].astype(F32) * conv).astype(o_ref.dtype)


def _gated_short_conv(pbdx, conv_w, batch, seq, col0):
    width = conv_w.shape[1]
    cb = 256
    nb = width // cb
    view = pbdx.reshape(batch, seq, pbdx.shape[1])
    base = col0 // cb

    def spec(part):
        return pl.BlockSpec((None, seq, cb), lambda b, c: (b, 0, base + part * nb + c))

    return pl.pallas_call(
        functools.partial(_conv_kernel, seq=seq),
        grid=(batch, nb),
        in_specs=[spec(0), spec(1), spec(2), pl.BlockSpec((3, cb), lambda b, c: (0, c))],
        out_specs=pl.BlockSpec((None, seq, cb), lambda b, c: (b, 0, c)),
        out_shape=jax.ShapeDtypeStruct((batch, seq, width), BF16),
        scratch_shapes=[pltpu.VMEM((seq + 16, cb), F32)],
        compiler_params=_params("arbitrary", "arbitrary"),
        name="short_conv",
    )(view, view, view, conv_w).reshape(batch * seq, width)


def _diff_kernel(q_ref, k_ref, v_ref, lam_ref, nrm_ref, o_ref, vone_ref, *, lam_init, chunk):
    t = pl.program_id(2)
    seq = k_ref.shape[0]
    tq = q_ref.shape[0]

    @pl.when(t == 0)
    def _():
        vone_ref[:, :LANES] = v_ref[...]
        vone_ref[:, LANES:] = jnp.ones((seq, LANES), BF16)

    lq = lam_ref[...].astype(F32)
    lam = (jnp.exp(jnp.sum(lq[0:1] * lq[1:2], axis=-1, keepdims=True))
           - jnp.exp(jnp.sum(lq[2:3] * lq[3:4], axis=-1, keepdims=True)) + lam_init)
    q = q_ref[...]
    lane = lax.broadcasted_iota(jnp.int32, q.shape, 1)
    zero = jnp.zeros_like(q)
    q2 = jnp.concatenate([jnp.where(lane < C_QK_DIM, q, zero), jnp.where(lane >= C_QK_DIM, q, zero)], axis=0)
    m = acc = None
    for c in range(seq // chunk):
        ks = slice(c * chunk, (c + 1) * chunk)
        s = lax.dot_general(q2, k_ref[ks, :], (((1,), (1,)), ((), ())), preferred_element_type=F32)
        mc = jnp.max(s, axis=-1, keepdims=True)
        m_new = mc if m is None else jnp.maximum(m, mc)
        e = jnp.exp2((s - m_new).astype(BF16))
        pv = jnp.dot(e, vone_ref[ks, :], preferred_element_type=F32)
        acc = pv if acc is None else acc * jnp.exp2(m - m_new) + pv
        m = m_new
    out = acc[:, :LANES] / acc[:, LANES:]
    o = out[:tq] - lam * out[tq:]
    o = o * lax.rsqrt(jnp.mean(o * o, axis=-1, keepdims=True) + LN_EPS) * nrm_ref[...] * (1.0 - lam_init)
    o_ref[...] = o.astype(o_ref.dtype)


def _diff_attention(pc, c_lambda, c_norm, batch, seq, layer, tq=1024, chunk=512):
    heads = pc.shape[1] // (3 * LANES)
    tq = min(tq, seq)
    chunk = min(chunk, seq)
    lam_init = 0.8 - 0.6 * math.exp(-0.3 * layer)
    view = pc.reshape(batch, seq, pc.shape[1])
    return pl.pallas_call(
        functools.partial(_diff_kernel, lam_init=lam_init, chunk=chunk),
        grid=(batch, heads, seq // tq),
        in_specs=[pl.BlockSpec((None, tq, LANES), lambda b, h, t: (b, t, 3 * h)),
                  pl.BlockSpec((None, seq, LANES), lambda b, h, t: (b, 0, 3 * h + 1)),
                  pl.BlockSpec((None, seq, LANES), lambda b, h, t: (b, 0, 3 * h + 2)),
                  pl.BlockSpec((4, C_QK_DIM), lambda b, h, t: (0, 0)),
                  pl.BlockSpec((1, LANES), lambda b, h, t: (0, 0))],
        out_specs=pl.BlockSpec((None, tq, LANES), lambda b, h, t: (b, t, h)),
        out_shape=jax.ShapeDtypeStruct((batch, seq, heads * LANES), BF16),
        scratch_shapes=[pltpu.VMEM((seq, 2 * LANES), BF16)],
        compiler_params=_params("arbitrary", "arbitrary", "arbitrary"),
        name="diff_attention",
    )(view, view, view, c_lambda, c_norm.reshape(1, LANES)).reshape(batch * seq, heads * LANES)


def _cross_kernel(q_ref, kv_ref, o_ref, *, heads):
    scale = HEAD_DIM ** -0.5
    width = heads * HEAD_DIM
    for h in range(heads):
        cs = slice(h * HEAD_DIM, (h + 1) * HEAD_DIM)
        k = kv_ref[:, h * HEAD_DIM:(h + 1) * HEAD_DIM]
        v = kv_ref[:, width + h * HEAD_DIM:width + (h + 1) * HEAD_DIM]
        s = lax.dot_general(q_ref[:, cs], k, (((1,), (1,)), ((), ())), preferred_element_type=F32) * scale
        m = jnp.max(s, axis=-1, keepdims=True)
        e = jnp.exp(s - m)
        den = jnp.sum(e, axis=-1, keepdims=True)
        o = jnp.dot(e.astype(BF16), v, preferred_element_type=F32) / den
        o_ref[:, cs] = o.astype(o_ref.dtype)


def _cross_attention(pbdx, kv, batch, seq, col0, width, tq=512):
    mem_len = kv.shape[0] // batch
    tq = min(tq, seq)
    view = pbdx.reshape(batch, seq, pbdx.shape[1])
    return pl.pallas_call(
        functools.partial(_cross_kernel, heads=width // HEAD_DIM),
        grid=(batch, seq // tq),
        in_specs=[pl.BlockSpec((None, tq, width), lambda b, t: (b, t, col0 // width)),
                  pl.BlockSpec((None, mem_len, 2 * width), lambda b, t: (b, 0, 0))],
        out_specs=pl.BlockSpec((None, tq, width), lambda b, t: (b, t, 0)),
        out_shape=jax.ShapeDtypeStruct((batch, seq, width), BF16),
        compiler_params=_params("arbitrary", "arbitrary"),
        name="cross_attention",
    )(view, kv.reshape(batch, mem_len, 2 * width)).reshape(batch * seq, width)


def _merge_kernel(x_ref, ya_ref, yb_ref, yc_ref, yd_ref, yx_ref,
                  g0, g1, g2, g3, g4, p0, p1, p2, p3, p4, b0, b1, b2, b3, b4, o_ref):
    x = x_ref[...]
    merged = None
    for y_ref, g_ref, p_ref, b_ref in ((ya_ref, g0, p0, b0), (yb_ref, g1, p1, b1), (yc_ref, g2, p2, b2),
                                       (yd_ref, g3, p3, b3), (yx_ref, g4, p4, b4)):
        gate = jax.nn.sigmoid(jnp.dot(x, g_ref[...], preferred_element_type=F32) + b_ref[...])
        term = gate * jnp.dot(y_ref[...], p_ref[...], preferred_element_type=F32)
        merged = term if merged is None else merged + term
    o_ref[...] = merged.astype(o_ref.dtype)


def _gated_merge(xb, ys, w_gate, w_branch, w_branch_x, b_gate, tm=512, tn=512):
    m, d = xb.shape
    tm = min(tm, m)
    nb = d // tn
    n_br = 5
    act = [pl.BlockSpec((tm, d), lambda j, i: (i, 0))]
    act += [pl.BlockSpec((tm, y.shape[1]), lambda j, i: (i, 0)) for y in ys]
    gates = [pl.BlockSpec((d, tn), functools.partial(lambda j, i, br: (0, br * nb + j), br=br)) for br in range(n_br)]
    projs = [pl.BlockSpec((None, w_branch.shape[1], tn), functools.partial(lambda j, i, br: (br, 0, j), br=br))
             for br in range(4)]
    projs.append(pl.BlockSpec((w_branch_x.shape[0], tn), lambda j, i: (0, j)))
    biases = [pl.BlockSpec((1, tn), functools.partial(lambda j, i, br: (0, br * nb + j), br=br)) for br in range(n_br)]
    bg2 = b_gate.reshape(1, n_br * d)
    return pl.pallas_call(
        _merge_kernel,
        grid=(nb, m // tm),
        in_specs=act + gates + projs + biases,
        out_specs=pl.BlockSpec((tm, tn), lambda j, i: (i, j)),
        out_shape=jax.ShapeDtypeStruct((m, d), BF16),
        compiler_params=_params("arbitrary", "arbitrary"),
        name="gated_merge",
    )(xb, *ys, *([w_gate] * n_br), *([w_branch] * 4), w_branch_x, *([bg2] * n_br))


def _layer_norm_rows(h, g, b):
    mu = jnp.mean(h, axis=-1, keepdims=True)
    hc = h - mu
    var = jnp.mean(hc * hc, axis=-1, keepdims=True)
    return hc * lax.rsqrt(var + LN_EPS) * g + b


def _out_ln_kernel(mg_ref, w_ref, x_ref, g_ref, b_ref, xo_ref, xb_ref, *, alpha):
    y = jnp.dot(mg_ref[...], w_ref[...], preferred_element_type=F32)
    xn = _layer_norm_rows(alpha * x_ref[...] + y, g_ref[...], b_ref[...])
    xo_ref[...] = xn
    xb_ref[...] = xn.astype(BF16)


def _out_proj_ln(merged, w_out_b, x, g, b, alpha, tm=512):
    m, d = x.shape
    tm = min(tm, m)
    row = pl.BlockSpec((tm, d), lambda i: (i, 0))
    vec = pl.BlockSpec((1, d), lambda i: (0, 0))
    return pl.pallas_call(
        functools.partial(_out_ln_kernel, alpha=alpha),
        grid=(m // tm,),
        in_specs=[row, pl.BlockSpec((d, d), lambda i: (0, 0)), row, vec, vec],
        out_specs=[row, row],
        out_shape=[jax.ShapeDtypeStruct((m, d), F32), jax.ShapeDtypeStruct((m, d), BF16)],
        compiler_params=_params("arbitrary"),
        name="out_proj_ln",
    )(merged, w_out_b, x, g.reshape(1, d), b.reshape(1, d))


def _split_bf16(a):
    hi = a.astype(BF16)
    lo = (a - hi.astype(F32)).astype(BF16)
    return hi, lo


def _router_kernel(x_ref, w_ref, b_ref, idx_ref, wt_ref, cnt_ref):
    @pl.when(pl.program_id(0) == 0)
    def _():
        cnt_ref[...] = jnp.zeros_like(cnt_ref)

    x = x_ref[...]
    w = w_ref[...]
    xh, xl = _split_bf16(x)
    wh, wl = _split_bf16(w)
    logits = (jnp.dot(xh, wh, preferred_element_type=F32) + jnp.dot(xl, wh, preferred_element_type=F32)
              + jnp.dot(xh, wl, preferred_element_type=F32))
    lt = logits.T[:N_EXPERTS, :]
    tm = lt.shape[1]
    scores = jax.nn.sigmoid(lt)
    biased = scores + b_ref[...]
    gsz = N_EXPERTS // N_GROUPS
    sub8 = lax.broadcasted_iota(jnp.int32, (gsz, tm), 0).astype(F32)
    grp_rows = []
    for g in range(N_GROUPS):
        blk = biased[g * gsz:(g + 1) * gsz, :]
        m1 = jnp.max(blk, axis=0, keepdims=True)
        i1 = jnp.min(jnp.where(blk == m1, sub8, float(gsz)), axis=0, keepdims=True)
        m2 = jnp.max(jnp.where(sub8 == i1, -jnp.inf, blk), axis=0, keepdims=True)
        grp_rows.append(m1 + m2)
    grp = jnp.concatenate(grp_rows, axis=0)
    subg = lax.broadcasted_iota(jnp.int32, (N_GROUPS, tm), 0).astype(F32)
    gsel = jnp.zeros((N_GROUPS, tm), F32)
    for _ in range(TOP_GROUPS):
        mg = jnp.max(grp, axis=0, keepdims=True)
        ig = jnp.min(jnp.where(grp == mg, subg, float(N_GROUPS)), axis=0, keepdims=True)
        hit = subg == ig
        gsel = jnp.where(hit, 1.0, gsel)
        grp = jnp.where(hit, -jnp.inf, grp)
    emask = jnp.concatenate([jnp.broadcast_to(gsel[g:g + 1, :], (gsz, tm)) for g in range(N_GROUPS)], axis=0)
    cand = jnp.where(emask > 0.5, biased, -jnp.inf)
    sube = lax.broadcasted_iota(jnp.int32, (N_EXPERTS, tm), 0).astype(F32)
    idx_rows, w_rows = [], []
    chosen = jnp.zeros((N_EXPERTS, tm), F32)
    for _ in range(TOP_K):
        mc = jnp.max(cand, axis=0, keepdims=True)
        ic = jnp.min(jnp.where(cand == mc, sube, float(N_EXPERTS)), axis=0, keepdims=True)
        hit = sube == ic
        idx_rows.append(ic)
        w_rows.append(jnp.sum(jnp.where(hit, scores, 0.0), axis=0, keepdims=True))
        chosen = jnp.where(hit, 1.0, chosen)
        cand = jnp.where(hit, -jnp.inf, cand)
    cnt_ref[...] += jnp.sum(chosen, axis=1, keepdims=True)
    wsel = jnp.concatenate(w_rows, axis=0)
    wsel = wsel / jnp.sum(wsel, axis=0, keepdims=True) * ROUTE_SCALE
    idx_ref[...] = jnp.concatenate(idx_rows, axis=0).astype(jnp.int32)
    wt_ref[...] = wsel


def _router(x, router_w, router_b, tm=512):
    m, d = x.shape
    tm = min(tm, m)
    w_pad = jnp.pad(router_w, ((0, 0), (0, LANES - N_EXPERTS)))
    return pl.pallas_call(
        _router_kernel,
        grid=(m // tm,),
        in_specs=[pl.BlockSpec((tm, d), lambda i: (i, 0)),
                  pl.BlockSpec((d, LANES), lambda i: (0, 0)),
                  pl.BlockSpec((N_EXPERTS, 1), lambda i: (0, 0))],
        out_specs=[pl.BlockSpec((TOP_K, tm), lambda i: (0, i)), pl.BlockSpec((TOP_K, tm), lambda i: (0, i)),
                   pl.BlockSpec((N_EXPERTS, LANES), lambda i: (0, 0))],
        out_shape=[jax.ShapeDtypeStruct((TOP_K, m), jnp.int32), jax.ShapeDtypeStruct((TOP_K, m), F32),
                   jax.ShapeDtypeStruct((N_EXPERTS, LANES), F32)],
        compiler_params=_params("arbitrary"),
        name="router",
    )(x, w_pad, router_b.reshape(N_EXPERTS, 1))


def _slot_kernel(idx_ref, base_ref, dest_ref, tri_ref, run_ref):
    tm = idx_ref.shape[1]

    @pl.when(pl.program_id(0) == 0)
    def _():
        row = lax.broadcasted_iota(jnp.int32, (tm, tm), 0)
        col = lax.broadcasted_iota(jnp.int32, (tm, tm), 1)
        tri_ref[...] = jnp.where(row < col, 1.0, 0.0).astype(BF16)
        run_ref[...] = base_ref[...]

    sube = lax.broadcasted_iota(jnp.int32, (N_EXPERTS, tm), 0)
    idx = idx_ref[...]
    base = run_ref[...]
    hits = [sube == idx[k:k + 1, :] for k in range(TOP_K)]
    chosen = jnp.zeros((N_EXPERTS, tm), F32)
    for hit in hits:
        chosen = jnp.where(hit, 1.0, chosen)
    before = jnp.dot(chosen.astype(BF16), tri_ref[...], preferred_element_type=F32)
    slot = base + before
    rows = [jnp.sum(jnp.where(hit, slot, 0.0), axis=0, keepdims=True) for hit in hits]
    run_ref[...] = base + jnp.sum(chosen, axis=1, keepdims=True)
    dest_ref[...] = jnp.concatenate(rows, axis=0).astype(jnp.int32)


def _assign_slots(idx_t, base, tm=512):
    kk, m = idx_t.shape
    tm = min(tm, m)
    return pl.pallas_call(
        _slot_kernel,
        grid=(m // tm,),
        in_specs=[pl.BlockSpec((kk, tm), lambda i: (0, i)), pl.BlockSpec((N_EXPERTS, 1), lambda i: (0, 0))],
        out_specs=pl.BlockSpec((kk, tm), lambda i: (0, i)),
        out_shape=jax.ShapeDtypeStruct((kk, m), jnp.int32),
        scratch_shapes=[pltpu.VMEM((tm, tm), BF16), pltpu.VMEM((N_EXPERTS, 1), F32)],
        compiler_params=_params("arbitrary"),
        name="assign_slots",
    )(idx_t, base.astype(F32).reshape(N_EXPERTS, 1))


def _expert_kernel(be_ref, nb_ref, x_ref, wg_ref, wu_ref, wd_ref, *rest):
    o_ref = rest[-1]
    del be_ref
    i = pl.program_id(0)

    @pl.when(i < nb_ref[0])
    def _():
        x = x_ref[...]
        y = None
        for c in range(wg_ref.shape[1] // MXU_WIDTH):
            cs = slice(c * MXU_WIDTH, (c + 1) * MXU_WIDTH)
            gate = jnp.dot(x, wg_ref[:, cs].astype(BF16), preferred_element_type=F32)
            up = jnp.dot(x, wu_ref[:, cs].astype(BF16), preferred_element_type=F32)
            act = (jax.nn.silu(gate) * up).astype(BF16)
            part = jnp.dot(act, wd_ref[cs, :].astype(BF16), preferred_element_type=F32)
            y = part if y is None else y + part
        o_ref[...] = y.astype(o_ref.dtype)

    @pl.when(i >= nb_ref[0])
    def _():
        o_ref[...] = jnp.zeros_like(o_ref)


def _expert_ffn(xs, block_e, nblocks, w_gate, w_up, w_down, layer, rows, run_after=()):
    ns, d = xs.shape
    de = w_gate.shape[3]
    grid_spec = pltpu.PrefetchScalarGridSpec(
        num_scalar_prefetch=2,
        grid=(ns // rows,),
        in_specs=[pl.BlockSpec((rows, d), lambda i, be, nb: (i, 0)),
                  pl.BlockSpec((None, None, d, de), lambda i, be, nb: (layer, be[i], 0, 0)),
                  pl.BlockSpec((None, None, d, de), lambda i, be, nb: (layer, be[i], 0, 0)),
                  pl.BlockSpec((None, None, de, d), lambda i, be, nb: (layer, be[i], 0, 0))]
        + [pl.BlockSpec(memory_space=pl.ANY)] * len(run_after),
        out_specs=pl.BlockSpec((rows, d), lambda i, be, nb: (i, 0)),
    )
    return pl.pallas_call(
        _expert_kernel,
        grid_spec=grid_spec,
        out_shape=jax.ShapeDtypeStruct((ns, d), BF16),
        compiler_params=_params("arbitrary"),
        name="expert_ffn",
    )(block_e, nblocks, xs, w_gate, w_up, w_down, *run_after)


def _combine_ln_kernel(*refs, alpha):
    yg_refs = refs[:TOP_K]
    wt_ref, sh_ref, x_ref, g_ref, b_ref, xo_ref, xb_ref = refs[TOP_K:]
    wt = wt_ref[...]
    routed = sh_ref[...].astype(F32)
    for k in range(TOP_K):
        routed = routed + wt[:, k:k + 1] * yg_refs[k][...].astype(F32)
    xn = _layer_norm_rows(alpha * x_ref[...] + routed, g_ref[...], b_ref[...])
    xo_ref[...] = xn
    xb_ref[...] = xn.astype(BF16)


def _combine_ln(yg, wt, shared, x, g, b, alpha, tm=256):
    m, d = x.shape
    tm = min(tm, m)
    nb = m // tm
    row = pl.BlockSpec((tm, d), lambda i: (i, 0))
    vec = pl.BlockSpec((1, d), lambda i: (0, 0))
    yg_specs = [pl.BlockSpec((tm, d), functools.partial(lambda i, k: (k * nb + i, 0), k=k)) for k in range(TOP_K)]
    return pl.pallas_call(
        functools.partial(_combine_ln_kernel, alpha=alpha),
        grid=(nb,),
        in_specs=yg_specs + [pl.BlockSpec((tm, TOP_K), lambda i: (i, 0)), row, row, vec, vec],
        out_specs=[row, row],
        out_shape=[jax.ShapeDtypeStruct((m, d), F32), jax.ShapeDtypeStruct((m, d), BF16)],
        compiler_params=_params("arbitrary"),
        name="combine_ln",
    )(*([yg] * TOP_K), wt, shared, x, g.reshape(1, d), b.reshape(1, d))


def _moe_layer(x1, xb1, layer, router_w, router_b, e_gate, e_up, e_down, s_gate, s_up, s_down, g, b, alpha):
    t, d = x1.shape
    rows = min(MOE_ROWS, t)
    idx_t, wt_t, cnt = _router(x1, router_w, router_b)
    n_assign = t * TOP_K
    i32 = jnp.int32
    counts = cnt[:, 0].astype(i32)
    padded = (counts + rows - 1) // rows * rows
    ends_p = jnp.cumsum(padded)
    pad_e = padded - counts
    dest = _assign_slots(idx_t, ends_p - padded).reshape(n_assign)
    nblk = n_assign // rows + N_EXPERTS
    n_pad = nblk * rows - n_assign
    pad_i = jnp.arange(n_pad, dtype=i32)
    pad_before = jnp.cumsum(pad_e) - pad_e
    pad_slot = pad_i + jnp.sum(jnp.where(pad_before[None, :] <= pad_i[:, None], counts[None, :], 0), axis=1)
    keys = jnp.concatenate([dest, pad_slot])
    toks = jnp.concatenate([jnp.arange(n_assign, dtype=i32) % t, pad_i % t])
    _, slot_tok = lax.sort_key_val(keys, toks)
    blk0 = jnp.arange(nblk, dtype=i32) * rows
    block_e = jnp.minimum(jnp.sum((ends_p[None, :] <= blk0[:, None]).astype(i32), axis=1), N_EXPERTS - 1)
    nused = (ends_p[-1] // rows).astype(i32).reshape(1)
    xs = jnp.take(xb1, slot_tok, axis=0, mode='clip')
    srows = min(2 * MOE_ROWS, t)
    shared = _expert_ffn(xb1, jnp.zeros((t // srows,), i32), jnp.full((1,), t // srows, i32),
                         s_gate[:, None], s_up[:, None], s_down[:, None], layer, srows)
    ys = _expert_ffn(xs, block_e, nused, e_gate, e_up, e_down, layer, rows, run_after=(shared,))
    yg = jnp.take(ys, dest, axis=0, mode='clip')
    return _combine_ln(yg, wt_t.T, shared, x1, g, b, alpha)


def _mixer_layer(x, xb, memb, layer, tabs_a, tabs_c, batch, seq, w_in, w_c, pool_w, pool_scale, c_lambda, c_norm,
                 conv_w, w_mem_kv, merge_w, b_gate, w_out_b, g, b, alpha):
    mix = merge_w[1].shape[1]
    xw = merge_w[2].shape[0]
    a_cols, c_cols, d_cols = 3 * mix, 3 * mix, 3 * mix
    c0_b, c0_d = a_cols, a_cols + mix + c_cols
    pa = _proj(xb, w_in, layer, range(3), mix, tabs_a, seq, half=HEAD_DIM // 8,
               rope_blocks=(1,) * (mix // LANES), rope_jmax=2)
    tn_c = 12 * LANES
    pc = _proj(xb, w_c, 0, range(c_cols // tn_c), tn_c, tabs_c, seq,
               half=C_QK_DIM // 8, rope_blocks=(1, 2, 0) * 4, tm=512)
    blocks = list(range(c0_b // mix, c0_b // mix + 1)) + list(range(c0_d // mix, c0_d // mix + d_cols // mix))
    pbd = _proj(xb, w_in, layer, blocks, mix, tabs_a, seq)
    px = _proj(xb, w_in, layer, [(c0_d + d_cols) // xw], xw, tabs_a, seq)
    ya = _dilated_attention(pa, batch, seq)
    yb = _pooling_mixer(pbd, pool_w[layer], pool_scale[layer], batch, seq, 0)
    yc = _diff_attention(pc, c_lambda[layer], c_norm[layer], batch, seq, layer)
    yd = _gated_short_conv(pbd, conv_w[layer], batch, seq, mix)
    kv = _proj(memb, w_mem_kv, layer, range(2), xw, tabs_a, seq)
    yx = _cross_attention(px, kv, batch, seq, 0, xw)
    merged = _gated_merge(xb, (ya, yb, yc, yd, yx), *merge_w, b_gate[layer])
    return _out_proj_ln(merged, w_out_b, x, g[layer], b[layer], alpha)


def kernel(x, mem, w_in, pool_w, pool_scale, c_lambda, c_norm, conv_w, w_mem_kv, w_branch, w_branch_x,
           w_gate, b_gate, w_out, ln1_g, ln1_b, router_w, router_b, e_gate, e_up, e_down, s_gate, s_up,
           s_down, ln2_g, ln2_b):
    batch, seq, d = x.shape
    depth = w_in.shape[0]
    alpha = (2.0 * depth) ** 0.25
    tabs_a = _rope_tables(seq, HEAD_DIM // 4, LANES)
    tabs_ck = _rope_tables(seq, C_QK_DIM // 4, C_QK_DIM)
    q_scale = C_QK_DIM ** -0.5 * math.log2(math.e)
    tabs_c = tuple(tab * q_scale for tab in tabs_ck) + tabs_ck
    mix = w_branch.shape[2]
    groups = BATCH_GROUPS if batch % BATCH_GROUPS == 0 else 1
    gb = batch // groups
    state = []
    for gi in range(groups):
        xg = x[gi * gb:(gi + 1) * gb].reshape(gb * seq, d)
        state.append((xg, xg.astype(BF16), mem[gi * gb:(gi + 1) * gb].reshape(-1, d).astype(BF16)))
    for l in range(depth):
        w_c = w_in[l:l + 1, :, 4 * mix:7 * mix]
        merge_w = (w_gate[l].astype(BF16), w_branch[l].astype(BF16), w_branch_x[l].astype(BF16))
        w_out_b = w_out[l].astype(BF16)
        for gi in range(groups):
            xf, xb, memb = state[gi]
            xf, xb = _mixer_layer(xf, xb, memb, l, tabs_a, tabs_c, gb, seq, w_in, w_c, pool_w, pool_scale,
                                  c_lambda, c_norm, conv_w, w_mem_kv, merge_w, b_gate, w_out_b,
                                  ln1_g, ln1_b, alpha)
            xf, xb = _moe_layer(xf, xb, l, router_w[l], router_b[l], e_gate, e_up, e_down, s_gate,
                                s_up, s_down, ln2_g[l], ln2_b[l], alpha)
            state[gi] = (xf, xb, memb)
    return jnp.concatenate([s[0] for s in state], axis=0).reshape(batch, seq, d)
```

```python
import functools
import math

import jax
import jax.numpy as jnp
from jax import lax
from jax.experimental import pallas as pl
from jax.experimental.pallas import tpu as pltpu

F32 = jnp.float32
BF16 = jnp.bfloat16

LANES = 128
MXU_WIDTH = 256
HEAD_DIM = 128
ROPE_THETA = 500000.0
A_PATTERNS = ((128, 1), (512, 4), (2048, 16))
A_HALF = 64
B_WINDOWS = (2, 4, 8, 16)
C_QK_DIM = 64
N_EXPERTS = 64
N_GROUPS = 8
TOP_GROUPS = 4
TOP_K = 8
ROUTE_SCALE = 2.5
LN_EPS = 1e-5
NEG_INF = -1e30
VMEM_LIMIT_BYTES = 56 * 1024 * 1024
MOE_ROWS = 256
BATCH_GROUPS = 1


def _params(*sem):
    return pltpu.CompilerParams(dimension_semantics=sem, vmem_limit_bytes=VMEM_LIMIT_BYTES)


def _rope_tables(seq, rot_dim, period):
    half = rot_dim // 2
    inv = 1.0 / (ROPE_THETA ** (jnp.arange(0, rot_dim, 2, dtype=F32) / rot_dim))
    ang = jnp.arange(seq, dtype=F32)[:, None] * inv[None, :]
    cos, sin = jnp.cos(ang), jnp.sin(ang)
    lane = jnp.arange(LANES) % period
    first = lane < half
    second = (lane >= half) & (lane < 2 * half)
    idx = jnp.where(first, lane, jnp.where(second, lane - half, 0))
    cos_l, sin_l = cos[:, idx], sin[:, idx]
    ct = jnp.where(first | second, cos_l, 1.0)
    s1 = jnp.where(first, -sin_l, 0.0)
    s2 = jnp.where(second, sin_l, 0.0)
    return ct.astype(F32), s1.astype(F32), s2.astype(F32)


def _proj_kernel(x_ref, w_ref, *rest, half, rope_blocks, rope_jmax):
    tab_refs, o_ref, wb_ref = rest[:-2], rest[-2], rest[-1]
    j = pl.program_id(0)
    i = pl.program_id(1)

    @pl.when(i == 0)
    def _():
        wb_ref[...] = w_ref[...].astype(BF16)

    def plain():
        o_ref[...] = jnp.dot(x_ref[...], wb_ref[...], preferred_element_type=F32).astype(o_ref.dtype)

    def roped():
        tabs = [r[...] for r in tab_refs]
        per = MXU_WIDTH // LANES
        for c in range(len(rope_blocks) // per):
            acc = jnp.dot(x_ref[...], wb_ref[:, c * MXU_WIDTH:(c + 1) * MXU_WIDTH], preferred_element_type=F32)
            for b in range(c * per, (c + 1) * per):
                use = rope_blocks[b]
                blk = acc[:, (b - c * per) * LANES:(b - c * per + 1) * LANES]
                if use:
                    ct, s1, s2 = tabs[3 * (use - 1):3 * use]
                    packed = pltpu.bitcast(blk.astype(BF16), jnp.uint32)
                    fwd = pltpu.bitcast(pltpu.roll(packed, LANES - half, 1), BF16).astype(F32)
                    bwd = pltpu.bitcast(pltpu.roll(packed, half, 1), BF16).astype(F32)
                    blk = blk * ct + fwd * s1 + bwd * s2
                o_ref[:, b * LANES:(b + 1) * LANES] = blk.astype(o_ref.dtype)

    if not any(rope_blocks):
        plain()
    elif rope_jmax is None:
        roped()
    else:
        pl.when(j < rope_jmax)(roped)
        pl.when(j >= rope_jmax)(plain)


def _proj(xb, w, layer, col_blocks, tn, tables, seq, *, half=0, rope_blocks=None, rope_jmax=None, tm=1024):
    m, k = xb.shape
    nj = len(col_blocks)
    tm = min(tm, m)
    rope_blocks = tuple(int(u) for u in rope_blocks) if rope_blocks is not None else (0,) * (tn // LANES)
    col_blocks = tuple(col_blocks)
    first, contiguous = col_blocks[0], all(col_blocks[a] == col_blocks[0] + a for a in range(nj))
    if contiguous:
        w_map = lambda j, i: (layer, 0, first + j)
    else:
        n0 = next(a for a in range(1, nj) if col_blocks[a] != col_blocks[0] + a)
        second = col_blocks[n0]
        w_map = lambda j, i: (layer, 0, jnp.where(j < n0, first + j, second + j - n0))
    sblocks = seq // tm if seq >= tm else 1
    tab_spec = pl.BlockSpec((tm, LANES), lambda j, i: (i % sblocks, 0))
    kern = functools.partial(_proj_kernel, half=half, rope_blocks=rope_blocks, rope_jmax=rope_jmax)
    return pl.pallas_call(
        kern,
        grid=(nj, m // tm),
        in_specs=[pl.BlockSpec((tm, k), lambda j, i: (i, 0)),
                  pl.BlockSpec((None, k, tn), w_map)] + [tab_spec] * len(tables),
        out_specs=pl.BlockSpec((tm, tn), lambda j, i: (i, j)),
        out_shape=jax.ShapeDtypeStruct((m, nj * tn), BF16),
        scratch_shapes=[pltpu.VMEM((k, tn), BF16)],
        compiler_params=_params("arbitrary", "arbitrary"),
        name="proj",
    )(xb, w, *tables)


def _dilated_kernel(q_ref, kp_ref, km_ref, kn_ref, vp_ref, vm_ref, vn_ref, o_ref, lse_ref, *, tl, seq_len, heads):
    t = pl.program_id(2)
    l0 = t * tl
    sub = 128
    nsub = tl // sub
    scale = HEAD_DIM ** -0.5
    iq = lax.broadcasted_iota(jnp.int32, (sub, sub + 2 * A_HALF), 0)
    ik = lax.broadcasted_iota(jnp.int32, (sub, sub + 2 * A_HALF), 1)
    band = jnp.abs(ik - A_HALF - iq) <= A_HALF
    lane = lax.broadcasted_iota(jnp.int32, (sub, LANES), 1)
    valid = []
    for a in range(nsub):
        kk = l0 + a * sub - A_HALF + ik
        valid.append(band & (kk >= 0) & (kk < seq_len))
    lse_rows = [jnp.zeros((sub, LANES), F32) for _ in range(nsub)]
    for h in range(heads):
        cs = slice(h * HEAD_DIM, (h + 1) * HEAD_DIM)
        kcat = jnp.concatenate([kp_ref[:, cs], km_ref[:, cs], kn_ref[:, cs]], axis=0)
        vcat = jnp.concatenate([vp_ref[:, cs], vm_ref[:, cs], vn_ref[:, cs]], axis=0)
        for a in range(nsub):
            q = q_ref[a * sub:(a + 1) * sub, cs]
            kw = kcat[a * sub:a * sub + sub + 2 * A_HALF]
            vw = vcat[a * sub:a * sub + sub + 2 * A_HALF]
            s = lax.dot_general(q, kw, (((1,), (1,)), ((), ())), preferred_element_type=F32) * scale
            s = jnp.where(valid[a], s, NEG_INF)
            m = jnp.max(s, axis=-1, keepdims=True)
            p = jnp.exp(s - m)
            den = jnp.sum(p, axis=-1, keepdims=True)
            o = jnp.dot(p.astype(BF16), vw, preferred_element_type=F32) / den
            o_ref[a * sub:(a + 1) * sub, cs] = o.astype(o_ref.dtype)
            lse_rows[a] = jnp.where(lane == h, m + jnp.log(den), lse_rows[a])
    for a in range(nsub):
        lse_ref[a * sub:(a + 1) * sub, :] = lse_rows[a]


def _dilated_pattern(pa, batch, seq, dil):
    width = pa.shape[1] // 3
    heads = width // HEAD_DIM
    sl = seq // dil
    tl = min(512, sl)
    nt = sl // tl
    hb = tl // A_HALF
    last_halo = sl // A_HALF - 1
    view = pa.reshape(batch, sl, dil * 3 * width)

    def main(c):
        return pl.BlockSpec((None, tl, width), lambda b, r, t: (b, t, r * 3 + c))

    def prev(c):
        return pl.BlockSpec((None, A_HALF, width), lambda b, r, t: (b, jnp.maximum(t * hb - 1, 0), r * 3 + c))

    def nxt(c):
        return pl.BlockSpec((None, A_HALF, width),
                            lambda b, r, t: (b, jnp.minimum((t + 1) * hb, last_halo), r * 3 + c))

    kern = functools.partial(_dilated_kernel, tl=tl, seq_len=sl, heads=heads)
    o, lse = pl.pallas_call(
        kern,
        grid=(batch, dil, nt),
        in_specs=[main(0), prev(1), main(1), nxt(1), prev(2), main(2), nxt(2)],
        out_specs=[pl.BlockSpec((None, tl, width), lambda b, r, t: (b, t, r)),
                   pl.BlockSpec((None, tl, LANES), lambda b, r, t: (b, t, r))],
        out_shape=[jax.ShapeDtypeStruct((batch, sl, dil * width), BF16),
                   jax.ShapeDtypeStruct((batch, sl, dil * LANES), F32)],
        compiler_params=_params("arbitrary", "arbitrary", "arbitrary"),
        name=f"dilated_d{dil}",
    )(view, view, view, view, view, view, view)
    return o.reshape(batch * seq, width), lse.reshape(batch * seq, LANES)


def _dilated_combine_kernel(o0_ref, o1_ref, o2_ref, l0_ref, l1_ref, l2_ref, y_ref, *, heads):
    l0, l1, l2 = l0_ref[...], l1_ref[...], l2_ref[...]
    m = jnp.maximum(jnp.maximum(l0, l1), l2)
    e0, e1, e2 = jnp.exp(l0 - m), jnp.exp(l1 - m), jnp.exp(l2 - m)
    inv = 1.0 / (e0 + e1 + e2)
    w0, w1, w2 = e0 * inv, e1 * inv, e2 * inv
    for h in range(heads):
        cs = slice(h * HEAD_DIM, (h + 1) * HEAD_DIM)
        y = (w0[:, h:h + 1] * o0_ref[:, cs].astype(F32) + w1[:, h:h + 1] * o1_ref[:, cs].astype(F32)
             + w2[:, h:h + 1] * o2_ref[:, cs].astype(F32))
        y_ref[:, cs] = y.astype(y_ref.dtype)


def _dilated_attention(pa, batch, seq):
    outs, lses = zip(*[_dilated_pattern(pa, batch, seq, d) for _, d in A_PATTERNS])
    m, width = outs[0].shape
    tm = min(512, m)
    ospec = pl.BlockSpec((tm, width), lambda i: (i, 0))
    lspec = pl.BlockSpec((tm, LANES), lambda i: (i, 0))
    return pl.pallas_call(
        functools.partial(_dilated_combine_kernel, heads=width // HEAD_DIM),
        grid=(m // tm,),
        in_specs=[ospec, ospec, ospec, lspec, lspec, lspec],
        out_specs=ospec,
        out_shape=jax.ShapeDtypeStruct((m, width), BF16),
        compiler_params=_params("arbitrary"),
        name="dilated_combine",
    )(*outs, *lses)


def _pool_kernel(u_ref, w_ref, sc_ref, o_ref, pad_ref, *, seq):
    g = pl.program_id(1)
    pad = 8
    uf = u_ref[...].astype(F32)
    pad_ref[0:pad, :] = jnp.zeros((pad, uf.shape[1]), F32)
    pad_ref[pad + seq:pad + seq + pad, :] = jnp.zeros((pad, uf.shape[1]), F32)
    pad_ref[pad:pad + seq, :] = uf
    pos = lax.broadcasted_iota(jnp.int32, (seq, 1), 0)
    for gi, win in enumerate(B_WINDOWS):
        @pl.when(g == gi)
        def _(win=win):
            before, after = win // 2, win - win // 2
            tot = pad_ref[pad - before:pad - before + seq, :]
            for off in range(-before + 1, after):
                tot = tot + pad_ref[pad + off:pad + off + seq, :]
            cnt = (jnp.minimum(pos + after, seq) - jnp.maximum(pos - before, 0)).astype(F32)
            pooled = tot / cnt - uf
            mixed = jnp.dot(pooled.astype(BF16), w_ref[...].astype(BF16), preferred_element_type=F32)
            o_ref[...] = (mixed * sc_ref[...]).astype(o_ref.dtype)


def _pooling_mixer(pbdx, pool_w, pool_scale, batch, seq, col0):
    ng, cg = pool_w.shape[0], pool_w.shape[1]
    view = pbdx.reshape(batch, seq, pbdx.shape[1])
    cb = col0 // cg
    return pl.pallas_call(
        functools.partial(_pool_kernel, seq=seq),
        grid=(batch, ng),
        in_specs=[pl.BlockSpec((None, seq, cg), lambda b, g: (b, 0, cb + g)),
                  pl.BlockSpec((None, cg, cg), lambda b, g: (g, 0, 0)),
                  pl.BlockSpec((1, cg), lambda b, g: (0, g))],
        out_specs=pl.BlockSpec((None, seq, cg), lambda b, g: (b, 0, g)),
        out_shape=jax.ShapeDtypeStruct((batch, seq, ng * cg), BF16),
        scratch_shapes=[pltpu.VMEM((seq + 16, cg), F32)],
        compiler_params=_params("arbitrary", "arbitrary"),
        name="pooling",
    )(view, pool_w, pool_scale.reshape(1, ng * cg)).reshape(batch * seq, ng * cg)


def _conv_kernel(bg_ref, cg_ref, h_ref, w_ref, o_ref, pad_ref, *, seq):
    pad = 8
    u = cg_ref[...].astype(F32) * h_ref[...].astype(F32)
    cols = u.shape[1]
    pad_ref[0:pad, :] = jnp.zeros((pad, cols), F32)
    pad_ref[pad + seq:pad + seq + pad, :] = jnp.zeros((pad, cols), F32)
    pad_ref[pad:pad + seq, :] = u
    w = w_ref[...]
    conv = (pad_ref[pad - 1:pad - 1 + seq, :] * w[0:1, :] + u * w[1:2, :]
            + pad_ref[pad + 1:pad + 1 + seq, :] * w[2:3, :])
    o_ref[...] = (bg_ref[...].astype(F32) * conv).astype(o_ref.dtype)


def _gated_short_conv(pbdx, conv_w, batch, seq, col0):
    width = conv_w.shape[1]
    cb = 256
    nb = width // cb
    view = pbdx.reshape(batch, seq, pbdx.shape[1])
    base = col0 // cb

    def spec(part):
        return pl.BlockSpec((None, seq, cb), lambda b, c: (b, 0, base + part * nb + c))

    return pl.pallas_call(
        functools.partial(_conv_kernel, seq=seq),
        grid=(batch, nb),
        in_specs=[spec(0), spec(1), spec(2), pl.BlockSpec((3, cb), lambda b, c: (0, c))],
        out_specs=pl.BlockSpec((None, seq, cb), lambda b, c: (b, 0, c)),
        out_shape=jax.ShapeDtypeStruct((batch, seq, width), BF16),
        scratch_shapes=[pltpu.VMEM((seq + 16, cb), F32)],
        compiler_params=_params("arbitrary", "arbitrary"),
        name="short_conv",
    )(view, view, view, conv_w).reshape(batch * seq, width)


def _diff_kernel(q_ref, k_ref, v_ref, lam_ref, nrm_ref, o_ref, vone_ref, *, lam_init, chunk):
    t = pl.program_id(2)
    seq = k_ref.shape[0]
    tq = q_ref.shape[0]

    @pl.when(t == 0)
    def _():
        vone_ref[:, :LANES] = v_ref[...]
        vone_ref[:, LANES:] = jnp.ones((seq, LANES), BF16)

    lq = lam_ref[...].astype(F32)
    lam = (jnp.exp(jnp.sum(lq[0:1] * lq[1:2], axis=-1, keepdims=True))
           - jnp.exp(jnp.sum(lq[2:3] * lq[3:4], axis=-1, keepdims=True)) + lam_init)
    q = q_ref[...]
    lane = lax.broadcasted_iota(jnp.int32, q.shape, 1)
    zero = jnp.zeros_like(q)
    q2 = jnp.concatenate([jnp.where(lane < C_QK_DIM, q, zero), jnp.where(lane >= C_QK_DIM, q, zero)], axis=0)
    m = acc = None
    for c in range(seq // chunk):
        ks = slice(c * chunk, (c + 1) * chunk)
        s = lax.dot_general(q2, k_ref[ks, :], (((1,), (1,)), ((), ())), preferred_element_type=F32)
        mc = jnp.max(s, axis=-1, keepdims=True)
        m_new = mc if m is None else jnp.maximum(m, mc)
        e = jnp.exp2((s - m_new).astype(BF16))
        pv = jnp.dot(e, vone_ref[ks, :], preferred_element_type=F32)
        acc = pv if acc is None else acc * jnp.exp2(m - m_new) + pv
        m = m_new
    out = acc[:, :LANES] / acc[:, LANES:]
    o = out[:tq] - lam * out[tq:]
    o = o * lax.rsqrt(jnp.mean(o * o, axis=-1, keepdims=True) + LN_EPS) * nrm_ref[...] * (1.0 - lam_init)
    o_ref[...] = o.astype(o_ref.dtype)


def _diff_attention(pc, c_lambda, c_norm, batch, seq, layer, tq=1024, chunk=256):
    heads = pc.shape[1] // (3 * LANES)
    tq = min(tq, seq)
    chunk = min(chunk, seq)
    lam_init = 0.8 - 0.6 * math.exp(-0.3 * layer)
    view = pc.reshape(batch, seq, pc.shape[1])
    return pl.pallas_call(
        functools.partial(_diff_kernel, lam_init=lam_init, chunk=chunk),
        grid=(batch, heads, seq // tq),
        in_specs=[pl.BlockSpec((None, tq, LANES), lambda b, h, t: (b, t, 3 * h)),
                  pl.BlockSpec((None, seq, LANES), lambda b, h, t: (b, 0, 3 * h + 1)),
                  pl.BlockSpec((None, seq, LANES), lambda b, h, t: (b, 0, 3 * h + 2)),
                  pl.BlockSpec((4, C_QK_DIM), lambda b, h, t: (0, 0)),
                  pl.BlockSpec((1, LANES), lambda b, h, t: (0, 0))],
        out_specs=pl.BlockSpec((None, tq, LANES), lambda b, h, t: (b, t, h)),
        out_shape=jax.ShapeDtypeStruct((batch, seq, heads * LANES), BF16),
        scratch_shapes=[pltpu.VMEM((seq, 2 * LANES), BF16)],
        compiler_params=_params("arbitrary", "arbitrary", "arbitrary"),
        name="diff_attention",
    )(view, view, view, c_lambda, c_norm.reshape(1, LANES)).reshape(batch * seq, heads * LANES)


def _cross_kernel(q_ref, kv_ref, o_ref, *, heads):
    scale = HEAD_DIM ** -0.5
    width = heads * HEAD_DIM
    for h in range(heads):
        cs = slice(h * HEAD_DIM, (h + 1) * HEAD_DIM)
        k = kv_ref[:, h * HEAD_DIM:(h + 1) * HEAD_DIM]
        v = kv_ref[:, width + h * HEAD_DIM:width + (h + 1) * HEAD_DIM]
        s = lax.dot_general(q_ref[:, cs], k, (((1,), (1,)), ((), ())), preferred_element_type=F32) * scale
        m = jnp.max(s, axis=-1, keepdims=True)
        e = jnp.exp(s - m)
        den = jnp.sum(e, axis=-1, keepdims=True)
        o = jnp.dot(e.astype(BF16), v, preferred_element_type=F32) / den
        o_ref[:, cs] = o.astype(o_ref.dtype)


def _cross_attention(pbdx, kv, batch, seq, col0, width, tq=512):
    mem_len = kv.shape[0] // batch
    tq = min(tq, seq)
    view = pbdx.reshape(batch, seq, pbdx.shape[1])
    return pl.pallas_call(
        functools.partial(_cross_kernel, heads=width // HEAD_DIM),
        grid=(batch, seq // tq),
        in_specs=[pl.BlockSpec((None, tq, width), lambda b, t: (b, t, col0 // width)),
                  pl.BlockSpec((None, mem_len, 2 * width), lambda b, t: (b, 0, 0))],
        out_specs=pl.BlockSpec((None, tq, width), lambda b, t: (b, t, 0)),
        out_shape=jax.ShapeDtypeStruct((batch, seq, width), BF16),
        compiler_params=_params("arbitrary", "arbitrary"),
        name="cross_attention",
    )(view, kv.reshape(batch, mem_len, 2 * width)).reshape(batch * seq, width)


def _merge_kernel(x_ref, ya_ref, yb_ref, yc_ref, yd_ref, yx_ref,
                  g0, g1, g2, g3, g4, p0, p1, p2, p3, p4, b0, b1, b2, b3, b4, o_ref):
    x = x_ref[...]
    merged = None
    for y_ref, g_ref, p_ref, b_ref in ((ya_ref, g0, p0, b0), (yb_ref, g1, p1, b1), (yc_ref, g2, p2, b2),
                                       (yd_ref, g3, p3, b3), (yx_ref, g4, p4, b4)):
        gate = jax.nn.sigmoid(jnp.dot(x, g_ref[...], preferred_element_type=F32) + b_ref[...])
        term = gate * jnp.dot(y_ref[...], p_ref[...], preferred_element_type=F32)
        merged = term if merged is None else merged + term
    o_ref[...] = merged.astype(o_ref.dtype)


def _gated_merge(xb, ys, w_gate, w_branch, w_branch_x, b_gate, tm=512, tn=512):
    m, d = xb.shape
    tm = min(tm, m)
    nb = d // tn
    n_br = 5
    act = [pl.BlockSpec((tm, d), lambda j, i: (i, 0))]
    act += [pl.BlockSpec((tm, y.shape[1]), lambda j, i: (i, 0)) for y in ys]
    gates = [pl.BlockSpec((d, tn), functools.partial(lambda j, i, br: (0, br * nb + j), br=br)) for br in range(n_br)]
    projs = [pl.BlockSpec((None, w_branch.shape[1], tn), functools.partial(lambda j, i, br: (br, 0, j), br=br))
             for br in range(4)]
    projs.append(pl.BlockSpec((w_branch_x.shape[0], tn), lambda j, i: (0, j)))
    biases = [pl.BlockSpec((1, tn), functools.partial(lambda j, i, br: (0, br * nb + j), br=br)) for br in range(n_br)]
    bg2 = b_gate.reshape(1, n_br * d)
    return pl.pallas_call(
        _merge_kernel,
        grid=(nb, m // tm),
        in_specs=act + gates + projs + biases,
        out_specs=pl.BlockSpec((tm, tn), lambda j, i: (i, j)),
        out_shape=jax.ShapeDtypeStruct((m, d), BF16),
        compiler_params=_params("arbitrary", "arbitrary"),
        name="gated_merge",
    )(xb, *ys, *([w_gate] * n_br), *([w_branch] * 4), w_branch_x, *([bg2] * n_br))


def _layer_norm_rows(h, g, b):
    mu = jnp.mean(h, axis=-1, keepdims=True)
    hc = h - mu
    var = jnp.mean(hc * hc, axis=-1, keepdims=True)
    return hc * lax.rsqrt(var + LN_EPS) * g + b


def _out_ln_kernel(mg_ref, w_ref, x_ref, g_ref, b_ref, xo_ref, xb_ref, *, alpha):
    y = jnp.dot(mg_ref[...], w_ref[...], preferred_element_type=F32)
    xn = _layer_norm_rows(alpha * x_ref[...] + y, g_ref[...], b_ref[...])
    xo_ref[...] = xn
    xb_ref[...] = xn.astype(BF16)


def _out_proj_ln(merged, w_out_b, x, g, b, alpha, tm=512):
    m, d = x.shape
    tm = min(tm, m)
    row = pl.BlockSpec((tm, d), lambda i: (i, 0))
    vec = pl.BlockSpec((1, d), lambda i: (0, 0))
    return pl.pallas_call(
        functools.partial(_out_ln_kernel, alpha=alpha),
        grid=(m // tm,),
        in_specs=[row, pl.BlockSpec((d, d), lambda i: (0, 0)), row, vec, vec],
        out_specs=[row, row],
        out_shape=[jax.ShapeDtypeStruct((m, d), F32), jax.ShapeDtypeStruct((m, d), BF16)],
        compiler_params=_params("arbitrary"),
        name="out_proj_ln",
    )(merged, w_out_b, x, g.reshape(1, d), b.reshape(1, d))


def _split_bf16(a):
    hi = a.astype(BF16)
    lo = (a - hi.astype(F32)).astype(BF16)
    return hi, lo


def _router_kernel(x_ref, w_ref, b_ref, idx_ref, wt_ref, cnt_ref):
    @pl.when(pl.program_id(0) == 0)
    def _():
        cnt_ref[...] = jnp.zeros_like(cnt_ref)

    x = x_ref[...]
    w = w_ref[...]
    xh, xl = _split_bf16(x)
    wh, wl = _split_bf16(w)
    logits = (jnp.dot(xh, wh, preferred_element_type=F32) + jnp.dot(xl, wh, preferred_element_type=F32)
              + jnp.dot(xh, wl, preferred_element_type=F32))
    lt = logits.T[:N_EXPERTS, :]
    tm = lt.shape[1]
    scores = jax.nn.sigmoid(lt)
    biased = scores + b_ref[...]
    gsz = N_EXPERTS // N_GROUPS
    sub8 = lax.broadcasted_iota(jnp.int32, (gsz, tm), 0).astype(F32)
    grp_rows = []
    for g in range(N_GROUPS):
        blk = biased[g * gsz:(g + 1) * gsz, :]
        m1 = jnp.max(blk, axis=0, keepdims=True)
        i1 = jnp.min(jnp.where(blk == m1, sub8, float(gsz)), axis=0, keepdims=True)
        m2 = jnp.max(jnp.where(sub8 == i1, -jnp.inf, blk), axis=0, keepdims=True)
        grp_rows.append(m1 + m2)
    grp = jnp.concatenate(grp_rows, axis=0)
    subg = lax.broadcasted_iota(jnp.int32, (N_GROUPS, tm), 0).astype(F32)
    gsel = jnp.zeros((N_GROUPS, tm), F32)
    for _ in range(TOP_GROUPS):
        mg = jnp.max(grp, axis=0, keepdims=True)
        ig = jnp.min(jnp.where(grp == mg, subg, float(N_GROUPS)), axis=0, keepdims=True)
        hit = subg == ig
        gsel = jnp.where(hit, 1.0, gsel)
        grp = jnp.where(hit, -jnp.inf, grp)
    emask = jnp.concatenate([jnp.broadcast_to(gsel[g:g + 1, :], (gsz, tm)) for g in range(N_GROUPS)], axis=0)
    cand = jnp.where(emask > 0.5, biased, -jnp.inf)
    sube = lax.broadcasted_iota(jnp.int32, (N_EXPERTS, tm), 0).astype(F32)
    idx_rows, w_rows = [], []
    chosen = jnp.zeros((N_EXPERTS, tm), F32)
    for _ in range(TOP_K):
        mc = jnp.max(cand, axis=0, keepdims=True)
        ic = jnp.min(jnp.where(cand == mc, sube, float(N_EXPERTS)), axis=0, keepdims=True)
        hit = sube == ic
        idx_rows.append(ic)
        w_rows.append(jnp.sum(jnp.where(hit, scores, 0.0), axis=0, keepdims=True))
        chosen = jnp.where(hit, 1.0, chosen)
        cand = jnp.where(hit, -jnp.inf, cand)
    cnt_ref[...] += jnp.sum(chosen, axis=1, keepdims=True)
    wsel = jnp.concatenate(w_rows, axis=0)
    wsel = wsel / jnp.sum(wsel, axis=0, keepdims=True) * ROUTE_SCALE
    idx_ref[...] = jnp.concatenate(idx_rows, axis=0).astype(jnp.int32)
    wt_ref[...] = wsel


def _router(x, router_w, router_b, tm=512):
    m, d = x.shape
    tm = min(tm, m)
    w_pad = jnp.pad(router_w, ((0, 0), (0, LANES - N_EXPERTS)))
    return pl.pallas_call(
        _router_kernel,
        grid=(m // tm,),
        in_specs=[pl.BlockSpec((tm, d), lambda i: (i, 0)),
                  pl.BlockSpec((d, LANES), lambda i: (0, 0)),
                  pl.BlockSpec((N_EXPERTS, 1), lambda i: (0, 0))],
        out_specs=[pl.BlockSpec((TOP_K, tm), lambda i: (0, i)), pl.BlockSpec((TOP_K, tm), lambda i: (0, i)),
                   pl.BlockSpec((N_EXPERTS, LANES), lambda i: (0, 0))],
        out_shape=[jax.ShapeDtypeStruct((TOP_K, m), jnp.int32), jax.ShapeDtypeStruct((TOP_K, m), F32),
                   jax.ShapeDtypeStruct((N_EXPERTS, LANES), F32)],
        compiler_params=_params("arbitrary"),
        name="router",
    )(x, w_pad, router_b.reshape(N_EXPERTS, 1))


def _slot_kernel(idx_ref, base_ref, dest_ref, tri_ref, run_ref):
    tm = idx_ref.shape[1]

    @pl.when(pl.program_id(0) == 0)
    def _():
        row = lax.broadcasted_iota(jnp.int32, (tm, tm), 0)
        col = lax.broadcasted_iota(jnp.int32, (tm, tm), 1)
        tri_ref[...] = jnp.where(row < col, 1.0, 0.0).astype(BF16)
        run_ref[...] = base_ref[...]

    sube = lax.broadcasted_iota(jnp.int32, (N_EXPERTS, tm), 0)
    idx = idx_ref[...]
    base = run_ref[...]
    hits = [sube == idx[k:k + 1, :] for k in range(TOP_K)]
    chosen = jnp.zeros((N_EXPERTS, tm), F32)
    for hit in hits:
        chosen = jnp.where(hit, 1.0, chosen)
    before = jnp.dot(chosen.astype(BF16), tri_ref[...], preferred_element_type=F32)
    slot = base + before
    rows = [jnp.sum(jnp.where(hit, slot, 0.0), axis=0, keepdims=True) for hit in hits]
    run_ref[...] = base + jnp.sum(chosen, axis=1, keepdims=True)
    dest_ref[...] = jnp.concatenate(rows, axis=0).astype(jnp.int32)


def _assign_slots(idx_t, base, tm=512):
    kk, m = idx_t.shape
    tm = min(tm, m)
    return pl.pallas_call(
        _slot_kernel,
        grid=(m // tm,),
        in_specs=[pl.BlockSpec((kk, tm), lambda i: (0, i)), pl.BlockSpec((N_EXPERTS, 1), lambda i: (0, 0))],
        out_specs=pl.BlockSpec((kk, tm), lambda i: (0, i)),
        out_shape=jax.ShapeDtypeStruct((kk, m), jnp.int32),
        scratch_shapes=[pltpu.VMEM((tm, tm), BF16), pltpu.VMEM((N_EXPERTS, 1), F32)],
        compiler_params=_params("arbitrary"),
        name="assign_slots",
    )(idx_t, base.astype(F32).reshape(N_EXPERTS, 1))


def _expert_kernel(be_ref, nb_ref, x_ref, wg_ref, wu_ref, wd_ref, *rest):
    o_ref = rest[-1]
    del be_ref
    i = pl.program_id(0)

    @pl.when(i < nb_ref[0])
    def _():
        x = x_ref[...]
        y = None
        for c in range(wg_ref.shape[1] // MXU_WIDTH):
            cs = slice(c * MXU_WIDTH, (c + 1) * MXU_WIDTH)
            gate = jnp.dot(x, wg_ref[:, cs].astype(BF16), preferred_element_type=F32)
            up = jnp.dot(x, wu_ref[:, cs].astype(BF16), preferred_element_type=F32)
            act = (jax.nn.silu(gate) * up).astype(BF16)
            part = jnp.dot(act, wd_ref[cs, :].astype(BF16), preferred_element_type=F32)
            y = part if y is None else y + part
        o_ref[...] = y.astype(o_ref.dtype)

    @pl.when(i >= nb_ref[0])
    def _():
        o_ref[...] = jnp.zeros_like(o_ref)


def _expert_ffn(xs, block_e, nblocks, w_gate, w_up, w_down, layer, rows, run_after=()):
    ns, d = xs.shape
    de = w_gate.shape[3]
    grid_spec = pltpu.PrefetchScalarGridSpec(
        num_scalar_prefetch=2,
        grid=(ns // rows,),
        in_specs=[pl.BlockSpec((rows, d), lambda i, be, nb: (i, 0)),
                  pl.BlockSpec((None, None, d, de), lambda i, be, nb: (layer, be[i], 0, 0)),
                  pl.BlockSpec((None, None, d, de), lambda i, be, nb: (layer, be[i], 0, 0)),
                  pl.BlockSpec((None, None, de, d), lambda i, be, nb: (layer, be[i], 0, 0))]
        + [pl.BlockSpec(memory_space=pl.ANY)] * len(run_after),
        out_specs=pl.BlockSpec((rows, d), lambda i, be, nb: (i, 0)),
    )
    return pl.pallas_call(
        _expert_kernel,
        grid_spec=grid_spec,
        out_shape=jax.ShapeDtypeStruct((ns, d), BF16),
        compiler_params=_params("arbitrary"),
        name="expert_ffn",
    )(block_e, nblocks, xs, w_gate, w_up, w_down, *run_after)


def _combine_ln_kernel(*refs, alpha):
    yg_refs = refs[:TOP_K]
    wt_ref, sh_ref, x_ref, g_ref, b_ref, xo_ref, xb_ref = refs[TOP_K:]
    wt = wt_ref[...]
    routed = sh_ref[...].astype(F32)
    for k in range(TOP_K):
        routed = routed + wt[:, k:k + 1] * yg_refs[k][...].astype(F32)
    xn = _layer_norm_rows(alpha * x_ref[...] + routed, g_ref[...], b_ref[...])
    xo_ref[...] = xn
    xb_ref[...] = xn.astype(BF16)


def _combine_ln(yg, wt, shared, x, g, b, alpha, tm=256):
    m, d = x.shape
    tm = min(tm, m)
    nb = m // tm
    row = pl.BlockSpec((tm, d), lambda i: (i, 0))
    vec = pl.BlockSpec((1, d), lambda i: (0, 0))
    yg_specs = [pl.BlockSpec((tm, d), functools.partial(lambda i, k: (k * nb + i, 0), k=k)) for k in range(TOP_K)]
    return pl.pallas_call(
        functools.partial(_combine_ln_kernel, alpha=alpha),
        grid=(nb,),
        in_specs=yg_specs + [pl.BlockSpec((tm, TOP_K), lambda i: (i, 0)), row, row, vec, vec],
        out_specs=[row, row],
        out_shape=[jax.ShapeDtypeStruct((m, d), F32), jax.ShapeDtypeStruct((m, d), BF16)],
        compiler_params=_params("arbitrary"),
        name="combine_ln",
    )(*([yg] * TOP_K), wt, shared, x, g.reshape(1, d), b.reshape(1, d))


def _moe_layer(x1, xb1, layer, router_w, router_b, e_gate, e_up, e_down, s_gate, s_up, s_down, g, b, alpha):
    t, d = x1.shape
    rows = min(MOE_ROWS, t)
    idx_t, wt_t, cnt = _router(x1, router_w, router_b)
    n_assign = t * TOP_K
    i32 = jnp.int32
    counts = cnt[:, 0].astype(i32)
    padded = (counts + rows - 1) // rows * rows
    ends_p = jnp.cumsum(padded)
    pad_e = padded - counts
    dest = _assign_slots(idx_t, ends_p - padded).reshape(n_assign)
    nblk = n_assign // rows + N_EXPERTS
    n_pad = nblk * rows - n_assign
    pad_i = jnp.arange(n_pad, dtype=i32)
    pad_before = jnp.cumsum(pad_e) - pad_e
    pad_slot = pad_i + jnp.sum(jnp.where(pad_before[None, :] <= pad_i[:, None], counts[None, :], 0), axis=1)
    keys = jnp.concatenate([dest, pad_slot])
    toks = jnp.concatenate([jnp.arange(n_assign, dtype=i32) % t, pad_i % t])
    _, slot_tok = lax.sort_key_val(keys, toks)
    blk0 = jnp.arange(nblk, dtype=i32) * rows
    block_e = jnp.minimum(jnp.sum((ends_p[None, :] <= blk0[:, None]).astype(i32), axis=1), N_EXPERTS - 1)
    nused = (ends_p[-1] // rows).astype(i32).reshape(1)
    xs = jnp.take(xb1, slot_tok, axis=0, mode='clip')
    srows = min(2 * MOE_ROWS, t)
    shared = _expert_ffn(xb1, jnp.zeros((t // srows,), i32), jnp.full((1,), t // srows, i32),
                         s_gate[:, None], s_up[:, None], s_down[:, None], layer, srows)
    ys = _expert_ffn(xs, block_e, nused, e_gate, e_up, e_down, layer, rows, run_after=(shared,))
    yg = jnp.take(ys, dest, axis=0, mode='clip')
    return _combine_ln(yg, wt_t.T, shared, x1, g, b, alpha)


def _mixer_layer(x, xb, memb, layer, tabs_a, tabs_c, batch, seq, w_in, w_c, pool_w, pool_scale, c_lambda, c_norm,
                 conv_w, w_mem_kv, merge_w, b_gate, w_out_b, g, b, alpha):
    mix = merge_w[1].shape[1]
    xw = merge_w[2].shape[0]
    a_cols, c_cols, d_cols = 3 * mix, 3 * mix, 3 * mix
    c0_b, c0_d = a_cols, a_cols + mix + c_cols
    pa = _proj(xb, w_in, layer, range(3), mix, tabs_a, seq, half=HEAD_DIM // 8,
               rope_blocks=(1,) * (mix // LANES), rope_jmax=2)
    tn_c = 12 * LANES
    pc = _proj(xb, w_c, 0, range(c_cols // tn_c), tn_c, tabs_c, seq,
               half=C_QK_DIM // 8, rope_blocks=(1, 2, 0) * 4, tm=512)
    blocks = list(range(c0_b // mix, c0_b // mix + 1)) + list(range(c0_d // mix, c0_d // mix + d_cols // mix))
    pbd = _proj(xb, w_in, layer, blocks, mix, tabs_a, seq)
    px = _proj(xb, w_in, layer, [(c0_d + d_cols) // xw], xw, tabs_a, seq)
    ya = _dilated_attention(pa, batch, seq)
    yb = _pooling_mixer(pbd, pool_w[layer], pool_scale[layer], batch, seq, 0)
    yc = _diff_attention(pc, c_lambda[layer], c_norm[layer], batch, seq, layer)
    yd = _gated_short_conv(pbd, conv_w[layer], batch, seq, mix)
    kv = _proj(memb, w_mem_kv, layer, range(2), xw, tabs_a, seq)
    yx = _cross_attention(px, kv, batch, seq, 0, xw)
    merged = _gated_merge(xb, (ya, yb, yc, yd, yx), *merge_w, b_gate[layer])
    return _out_proj_ln(merged, w_out_b, x, g[layer], b[layer], alpha)


def kernel(x, mem, w_in, pool_w, pool_scale, c_lambda, c_norm, conv_w, w_mem_kv, w_branch, w_branch_x,
           w_gate, b_gate, w_out, ln1_g, ln1_b, router_w, router_b, e_gate, e_up, e_down, s_gate, s_up,
           s_down, ln2_g, ln2_b):
    batch, seq, d = x.shape
    depth = w_in.shape[0]
    alpha = (2.0 * depth) ** 0.25
    tabs_a = _rope_tables(seq, HEAD_DIM // 4, LANES)
    tabs_ck = _rope_tables(seq, C_QK_DIM // 4, C_QK_DIM)
    q_scale = C_QK_DIM ** -0.5 * math.log2(math.e)
    tabs_c = tuple(tab * q_scale for tab in tabs_ck) + tabs_ck
    mix = w_branch.shape[2]
    groups = BATCH_GROUPS if batch % BATCH_GROUPS == 0 else 1
    gb = batch // groups
    state = []
    for gi in range(groups):
        xg = x[gi * gb:(gi + 1) * gb].reshape(gb * seq, d)
        state.append((xg, xg.astype(BF16), mem[gi * gb:(gi + 1) * gb].reshape(-1, d).astype(BF16)))
    for l in range(depth):
        w_c = w_in[l:l + 1, :, 4 * mix:7 * mix]
        merge_w = (w_gate[l].astype(BF16), w_branch[l].astype(BF16), w_branch_x[l].astype(BF16))
        w_out_b = w_out[l].astype(BF16)
        for gi in range(groups):
            xf, xb, memb = state[gi]
            xf, xb = _mixer_layer(xf, xb, memb, l, tabs_a, tabs_c, gb, seq, w_in, w_c, pool_w, pool_scale,
                                  c_lambda, c_norm, conv_w, w_mem_kv, merge_w, b_gate, w_out_b,
                                  ln1_g, ln1_b, alpha)
            xf, xb = _moe_layer(xf, xb, l, router_w[l], router_b[l], e_gate, e_up, e_down, s_gate,
                                s_up, s_down, ln2_g[l], ln2_b[l], alpha)
            state[gi] = (xf, xb, memb)
    return jnp.concatenate([s[0] for s in state], axis=0).reshape(batch, seq, d)
```

```python
import functools
import math

import jax
import jax.numpy as jnp
from jax import lax
from jax.experimental import pallas as pl
from jax.experimental.pallas import tpu as pltpu

F32 = jnp.float32
BF16 = jnp.bfloat16

LANES = 128
MXU_WIDTH = 256
HEAD_DIM = 128
ROPE_THETA = 500000.0
A_PATTERNS = ((128, 1), (512, 4), (2048, 16))
A_HALF = 64
B_WINDOWS = (2, 4, 8, 16)
C_QK_DIM = 64
N_EXPERTS = 64
N_GROUPS = 8
TOP_GROUPS = 4
TOP_K = 8
ROUTE_SCALE = 2.5
LN_EPS = 1e-5
NEG_INF = -1e30
VMEM_LIMIT_BYTES = 56 * 1024 * 1024
MOE_ROWS = 256
BATCH_GROUPS = 1


def _params(*sem):
    return pltpu.CompilerParams(dimension_semantics=sem, vmem_limit_bytes=VMEM_LIMIT_BYTES)


def _rope_tables(seq, rot_dim, period):
    half = rot_dim // 2
    inv = 1.0 / (ROPE_THETA ** (jnp.arange(0, rot_dim, 2, dtype=F32) / rot_dim))
    ang = jnp.arange(seq, dtype=F32)[:, None] * inv[None, :]
    cos, sin = jnp.cos(ang), jnp.sin(ang)
    lane = jnp.arange(LANES) % period
    first = lane < half
    second = (lane >= half) & (lane < 2 * half)
    idx = jnp.where(first, lane, jnp.where(second, lane - half, 0))
    cos_l, sin_l = cos[:, idx], sin[:, idx]
    ct = jnp.where(first | second, cos_l, 1.0)
    s1 = jnp.where(first, -sin_l, 0.0)
    s2 = jnp.where(second, sin_l, 0.0)
    return ct.astype(F32), s1.astype(F32), s2.astype(F32)


def _proj_kernel(x_ref, w_ref, *rest, half, rope_blocks, rope_jmax):
    tab_refs, o_ref, wb_ref = rest[:-2], rest[-2], rest[-1]
    j = pl.program_id(0)
    i = pl.program_id(1)

    @pl.when(i == 0)
    def _():
        wb_ref[...] = w_ref[...].astype(BF16)

    def plain():
        for c in range(o_ref.shape[1] // MXU_WIDTH):
            cs = slice(c * MXU_WIDTH, (c + 1) * MXU_WIDTH)
            o_ref[:, cs] = jnp.dot(x_ref[...], wb_ref[:, cs], preferred_element_type=F32).astype(o_ref.dtype)

    def roped():
        tabs = [r[...] for r in tab_refs]
        per = MXU_WIDTH // LANES
        for c in range(len(rope_blocks) // per):
            acc = jnp.dot(x_ref[...], wb_ref[:, c * MXU_WIDTH:(c + 1) * MXU_WIDTH], preferred_element_type=F32)
            for b in range(c * per, (c + 1) * per):
                use = rope_blocks[b]
                blk = acc[:, (b - c * per) * LANES:(b - c * per + 1) * LANES]
                if use:
                    ct, s1, s2 = tabs[3 * (use - 1):3 * use]
                    packed = pltpu.bitcast(blk.astype(BF16), jnp.uint32)
                    fwd = pltpu.bitcast(pltpu.roll(packed, LANES - half, 1), BF16).astype(F32)
                    bwd = pltpu.bitcast(pltpu.roll(packed, half, 1), BF16).astype(F32)
                    blk = blk * ct + fwd * s1 + bwd * s2
                o_ref[:, b * LANES:(b + 1) * LANES] = blk.astype(o_ref.dtype)

    if not any(rope_blocks):
        plain()
    elif rope_jmax is None:
        roped()
    else:
        pl.when(j < rope_jmax)(roped)
        pl.when(j >= rope_jmax)(plain)


def _proj(xb, w, layer, col_blocks, tn, tables, seq, *, half=0, rope_blocks=None, rope_jmax=None, tm=1024):
    m, k = xb.shape
    nj = len(col_blocks)
    tm = min(tm, m)
    rope_blocks = tuple(int(u) for u in rope_blocks) if rope_blocks is not None else (0,) * (tn // LANES)
    col_blocks = tuple(col_blocks)
    first, contiguous = col_blocks[0], all(col_blocks[a] == col_blocks[0] + a for a in range(nj))
    if contiguous:
        w_map = lambda j, i: (layer, 0, first + j)
    else:
        n0 = next(a for a in range(1, nj) if col_blocks[a] != col_blocks[0] + a)
        second = col_blocks[n0]
        w_map = lambda j, i: (layer, 0, jnp.where(j < n0, first + j, second + j - n0))
    sblocks = seq // tm if seq >= tm else 1
    tab_spec = pl.BlockSpec((tm, LANES), lambda j, i: (i % sblocks, 0))
    kern = functools.partial(_proj_kernel, half=half, rope_blocks=rope_blocks, rope_jmax=rope_jmax)
    return pl.pallas_call(
        kern,
        grid=(nj, m // tm),
        in_specs=[pl.BlockSpec((tm, k), lambda j, i: (i, 0)),
                  pl.BlockSpec((None, k, tn), w_map)] + [tab_spec] * len(tables),
        out_specs=pl.BlockSpec((tm, tn), lambda j, i: (i, j)),
        out_shape=jax.ShapeDtypeStruct((m, nj * tn), BF16),
        scratch_shapes=[pltpu.VMEM((k, tn), BF16)],
        compiler_params=_params("arbitrary", "arbitrary"),
        name="proj",
    )(xb, w, *tables)


def _dilated_kernel(q_ref, kp_ref, km_ref, kn_ref, vp_ref, vm_ref, vn_ref, o_ref, lse_ref, *, tl, seq_len, heads):
    t = pl.program_id(2)
    l0 = t * tl
    sub = 128
    nsub = tl // sub
    scale = HEAD_DIM ** -0.5
    iq = lax.broadcasted_iota(jnp.int32, (sub, sub + 2 * A_HALF), 0)
    ik = lax.broadcasted_iota(jnp.int32, (sub, sub + 2 * A_HALF), 1)
    band = jnp.abs(ik - A_HALF - iq) <= A_HALF
    lane = lax.broadcasted_iota(jnp.int32, (sub, LANES), 1)
    valid = []
    for a in range(nsub):
        kk = l0 + a * sub - A_HALF + ik
        valid.append(band & (kk >= 0) & (kk < seq_len))
    lse_rows = [jnp.zeros((sub, LANES), F32) for _ in range(nsub)]
    for h in range(heads):
        cs = slice(h * HEAD_DIM, (h + 1) * HEAD_DIM)
        kcat = jnp.concatenate([kp_ref[:, cs], km_ref[:, cs], kn_ref[:, cs]], axis=0)
        vcat = jnp.concatenate([vp_ref[:, cs], vm_ref[:, cs], vn_ref[:, cs]], axis=0)
        for a in range(nsub):
            q = q_ref[a * sub:(a + 1) * sub, cs]
            kw = kcat[a * sub:a * sub + sub + 2 * A_HALF]
            vw = vcat[a * sub:a * sub + sub + 2 * A_HALF]
            s = lax.dot_general(q, kw, (((1,), (1,)), ((), ())), preferred_element_type=F32) * scale
            s = jnp.where(valid[a], s, NEG_INF)
            m = jnp.max(s, axis=-1, keepdims=True)
            p = jnp.exp(s - m)
            den = jnp.sum(p, axis=-1, keepdims=True)
            o = jnp.dot(p.astype(BF16), vw, preferred_element_type=F32) / den
            o_ref[a * sub:(a + 1) * sub, cs] = o.astype(o_ref.dtype)
            lse_rows[a] = jnp.where(lane == h, m + jnp.log(den), lse_rows[a])
    for a in range(nsub):
        lse_ref[a * sub:(a + 1) * sub, :] = lse_rows[a]


def _dilated_pattern(pa, batch, seq, dil):
    width = pa.shape[1] // 3
    heads = width // HEAD_DIM
    sl = seq // dil
    tl = min(512, sl)
    nt = sl // tl
    hb = tl // A_HALF
    last_halo = sl // A_HALF - 1
    view = pa.reshape(batch, sl, dil * 3 * width)

    def main(c):
        return pl.BlockSpec((None, tl, width), lambda b, r, t: (b, t, r * 3 + c))

    def prev(c):
        return pl.BlockSpec((None, A_HALF, width), lambda b, r, t: (b, jnp.maximum(t * hb - 1, 0), r * 3 + c))

    def nxt(c):
        return pl.BlockSpec((None, A_HALF, width),
                            lambda b, r, t: (b, jnp.minimum((t + 1) * hb, last_halo), r * 3 + c))

    kern = functools.partial(_dilated_kernel, tl=tl, seq_len=sl, heads=heads)
    o, lse = pl.pallas_call(
        kern,
        grid=(batch, dil, nt),
        in_specs=[main(0), prev(1), main(1), nxt(1), prev(2), main(2), nxt(2)],
        out_specs=[pl.BlockSpec((None, tl, width), lambda b, r, t: (b, t, r)),
                   pl.BlockSpec((None, tl, LANES), lambda b, r, t: (b, t, r))],
        out_shape=[jax.ShapeDtypeStruct((batch, sl, dil * width), BF16),
                   jax.ShapeDtypeStruct((batch, sl, dil * LANES), F32)],
        compiler_params=_params("arbitrary", "arbitrary", "arbitrary"),
        name=f"dilated_d{dil}",
    )(view, view, view, view, view, view, view)
    return o.reshape(batch * seq, width), lse.reshape(batch * seq, LANES)


def _dilated_combine_kernel(o0_ref, o1_ref, o2_ref, l0_ref, l1_ref, l2_ref, y_ref, *, heads):
    l0, l1, l2 = l0_ref[...], l1_ref[...], l2_ref[...]
    m = jnp.maximum(jnp.maximum(l0, l1), l2)
    e0, e1, e2 = jnp.exp(l0 - m), jnp.exp(l1 - m), jnp.exp(l2 - m)
    inv = 1.0 / (e0 + e1 + e2)
    w0, w1, w2 = e0 * inv, e1 * inv, e2 * inv
    for h in range(heads):
        cs = slice(h * HEAD_DIM, (h + 1) * HEAD_DIM)
        y = (w0[:, h:h + 1] * o0_ref[:, cs].astype(F32) + w1[:, h:h + 1] * o1_ref[:, cs].astype(F32)
             + w2[:, h:h + 1] * o2_ref[:, cs].astype(F32))
        y_ref[:, cs] = y.astype(y_ref.dtype)


def _dilated_attention(pa, batch, seq):
    outs, lses = zip(*[_dilated_pattern(pa, batch, seq, d) for _, d in A_PATTERNS])
    m, width = outs[0].shape
    tm = min(512, m)
    ospec = pl.BlockSpec((tm, width), lambda i: (i, 0))
    lspec = pl.BlockSpec((tm, LANES), lambda i: (i, 0))
    return pl.pallas_call(
        functools.partial(_dilated_combine_kernel, heads=width // HEAD_DIM),
        grid=(m // tm,),
        in_specs=[ospec, ospec, ospec, lspec, lspec, lspec],
        out_specs=ospec,
        out_shape=jax.ShapeDtypeStruct((m, width), BF16),
        compiler_params=_params("arbitrary"),
        name="dilated_combine",
    )(*outs, *lses)


def _pool_kernel(u_ref, w_ref, sc_ref, o_ref, pad_ref, *, seq):
    g = pl.program_id(1)
    pad = 8
    uf = u_ref[...].astype(F32)
    pad_ref[0:pad, :] = jnp.zeros((pad, uf.shape[1]), F32)
    pad_ref[pad + seq:pad + seq + pad, :] = jnp.zeros((pad, uf.shape[1]), F32)
    pad_ref[pad:pad + seq, :] = uf
    pos = lax.broadcasted_iota(jnp.int32, (seq, 1), 0)
    for gi, win in enumerate(B_WINDOWS):
        @pl.when(g == gi)
        def _(win=win):
            before, after = win // 2, win - win // 2
            tot = pad_ref[pad - before:pad - before + seq, :]
            for off in range(-before + 1, after):
                tot = tot + pad_ref[pad + off:pad + off + seq, :]
            cnt = (jnp.minimum(pos + after, seq) - jnp.maximum(pos - before, 0)).astype(F32)
            pooled = tot / cnt - uf
            mixed = jnp.dot(pooled.astype(BF16), w_ref[...].astype(BF16), preferred_element_type=F32)
            o_ref[...] = (mixed * sc_ref[...]).astype(o_ref.dtype)


def _pooling_mixer(pbdx, pool_w, pool_scale, batch, seq, col0):
    ng, cg = pool_w.shape[0], pool_w.shape[1]
    view = pbdx.reshape(batch, seq, pbdx.shape[1])
    cb = col0 // cg
    return pl.pallas_call(
        functools.partial(_pool_kernel, seq=seq),
        grid=(batch, ng),
        in_specs=[pl.BlockSpec((None, seq, cg), lambda b, g: (b, 0, cb + g)),
                  pl.BlockSpec((None, cg, cg), lambda b, g: (g, 0, 0)),
                  pl.BlockSpec((1, cg), lambda b, g: (0, g))],
        out_specs=pl.BlockSpec((None, seq, cg), lambda b, g: (b, 0, g)),
        out_shape=jax.ShapeDtypeStruct((batch, seq, ng * cg), BF16),
        scratch_shapes=[pltpu.VMEM((seq + 16, cg), F32)],
        compiler_params=_params("arbitrary", "arbitrary"),
        name="pooling",
    )(view, pool_w, pool_scale.reshape(1, ng * cg)).reshape(batch * seq, ng * cg)


def _conv_kernel(bg_ref, cg_ref, h_ref, w_ref, o_ref, pad_ref, *, seq):
    pad = 8
    u = cg_ref[...].astype(F32) * h_ref[...].astype(F32)
    cols = u.shape[1]
    pad_ref[0:pad, :] = jnp.zeros((pad, cols), F32)
    pad_ref[pad + seq:pad + seq + pad, :] = jnp.zeros((pad, cols), F32)
    pad_ref[pad:pad + seq, :] = u
    w = w_ref[...]
    conv = (pad_ref[pad - 1:pad - 1 + seq, :] * w[0:1, :] + u * w[1:2, :]
            + pad_ref[pad + 1:pad + 1 + seq, :] * w[2:3, :])
    o_ref[...] = (bg_ref[...].astype(F32) * conv).astype(o_ref.dtype)


def _gated_short_conv(pbdx, conv_w, batch, seq, col0):
    width = conv_w.shape[1]
    cb = 256
    nb = width // cb
    view = pbdx.reshape(batch, seq, pbdx.shape[1])
    base = col0 // cb

    def spec(part):
        return pl.BlockSpec((None, seq, cb), lambda b, c: (b, 0, base + part * nb + c))

    return pl.pallas_call(
        functools.partial(_conv_kernel, seq=seq),
        grid=(batch, nb),
        in_specs=[spec(0), spec(1), spec(2), pl.BlockSpec((3, cb), lambda b, c: (0, c))],
        out_specs=pl.BlockSpec((None, seq, cb), lambda b, c: (b, 0, c)),
        out_shape=jax.ShapeDtypeStruct((batch, seq, width), BF16),
        scratch_shapes=[pltpu.VMEM((seq + 16, cb), F32)],
        compiler_params=_params("arbitrary", "arbitrary"),
        name="short_conv",
    )(view, view, view, conv_w).reshape(batch * seq, width)


def _diff_kernel(q_ref, k_ref, v_ref, lam_ref, nrm_ref, o_ref, vone_ref, *, lam_init, chunk):
    t = pl.program_id(2)
    seq = k_ref.shape[0]
    tq = q_ref.shape[0]

    @pl.when(t == 0)
    def _():
        vone_ref[:, :LANES] = v_ref[...]
        vone_ref[:, LANES:] = jnp.ones((seq, LANES), BF16)

    lq = lam_ref[...].astype(F32)
    lam = (jnp.exp(jnp.sum(lq[0:1] * lq[1:2], axis=-1, keepdims=True))
           - jnp.exp(jnp.sum(lq[2:3] * lq[3:4], axis=-1, keepdims=True)) + lam_init)
    q = q_ref[...]
    lane = lax.broadcasted_iota(jnp.int32, q.shape, 1)
    zero = jnp.zeros_like(q)
    q2 = jnp.concatenate([jnp.where(lane < C_QK_DIM, q, zero), jnp.where(lane >= C_QK_DIM, q, zero)], axis=0)
    m = acc = None
    for c in range(seq // chunk):
        ks = slice(c * chunk, (c + 1) * chunk)
        s = lax.dot_general(q2, k_ref[ks, :], (((1,), (1,)), ((), ())), preferred_element_type=F32)
        mc = jnp.max(s, axis=-1, keepdims=True)
        m_new = mc if m is None else jnp.maximum(m, mc)
        e = jnp.exp2((s - m_new).astype(BF16))
        pv = jnp.dot(e, vone_ref[ks, :], preferred_element_type=F32)
        acc = pv if acc is None else acc * jnp.exp2(m - m_new) + pv
        m = m_new
    out = acc[:, :LANES] / acc[:, LANES:]
    o = out[:tq] - lam * out[tq:]
    o = o * lax.rsqrt(jnp.mean(o * o, axis=-1, keepdims=True) + LN_EPS) * nrm_ref[...] * (1.0 - lam_init)
    o_ref[...] = o.astype(o_ref.dtype)


def _diff_attention(pc, c_lambda, c_norm, batch, seq, layer, tq=2048, chunk=256):
    heads = pc.shape[1] // (3 * LANES)
    tq = min(tq, seq)
    chunk = min(chunk, seq)
    lam_init = 0.8 - 0.6 * math.exp(-0.3 * layer)
    view = pc.reshape(batch, seq, pc.shape[1])
    return pl.pallas_call(
        functools.partial(_diff_kernel, lam_init=lam_init, chunk=chunk),
        grid=(batch, heads, seq // tq),
        in_specs=[pl.BlockSpec((None, tq, LANES), lambda b, h, t: (b, t, 3 * h)),
                  pl.BlockSpec((None, seq, LANES), lambda b, h, t: (b, 0, 3 * h + 1)),
                  pl.BlockSpec((None, seq, LANES), lambda b, h, t: (b, 0, 3 * h + 2)),
                  pl.BlockSpec((4, C_QK_DIM), lambda b, h, t: (0, 0)),
                  pl.BlockSpec((1, LANES), lambda b, h, t: (0, 0))],
        out_specs=pl.BlockSpec((None, tq, LANES), lambda b, h, t: (b, t, h)),
        out_shape=jax.ShapeDtypeStruct((batch, seq, heads * LANES), BF16),
        scratch_shapes=[pltpu.VMEM((seq, 2 * LANES), BF16)],
        compiler_params=_params("arbitrary", "arbitrary", "arbitrary"),
        name="diff_attention",
    )(view, view, view, c_lambda, c_norm.reshape(1, LANES)).reshape(batch * seq, heads * LANES)


def _cross_kernel(q_ref, kv_ref, o_ref, *, heads):
    scale = HEAD_DIM ** -0.5
    width = heads * HEAD_DIM
    for h in range(heads):
        cs = slice(h * HEAD_DIM, (h + 1) * HEAD_DIM)
        k = kv_ref[:, h * HEAD_DIM:(h + 1) * HEAD_DIM]
        v = kv_ref[:, width + h * HEAD_DIM:width + (h + 1) * HEAD_DIM]
        s = lax.dot_general(q_ref[:, cs], k, (((1,), (1,)), ((), ())), preferred_element_type=F32) * scale
        m = jnp.max(s, axis=-1, keepdims=True)
        e = jnp.exp(s - m)
        den = jnp.sum(e, axis=-1, keepdims=True)
        o = jnp.dot(e.astype(BF16), v, preferred_element_type=F32) / den
        o_ref[:, cs] = o.astype(o_ref.dtype)


def _cross_attention(pbdx, kv, batch, seq, col0, width, tq=512):
    mem_len = kv.shape[0] // batch
    tq = min(tq, seq)
    view = pbdx.reshape(batch, seq, pbdx.shape[1])
    return pl.pallas_call(
        functools.partial(_cross_kernel, heads=width // HEAD_DIM),
        grid=(batch, seq // tq),
        in_specs=[pl.BlockSpec((None, tq, width), lambda b, t: (b, t, col0 // width)),
                  pl.BlockSpec((None, mem_len, 2 * width), lambda b, t: (b, 0, 0))],
        out_specs=pl.BlockSpec((None, tq, width), lambda b, t: (b, t, 0)),
        out_shape=jax.ShapeDtypeStruct((batch, seq, width), BF16),
        compiler_params=_params("arbitrary", "arbitrary"),
        name="cross_attention",
    )(view, kv.reshape(batch, mem_len, 2 * width)).reshape(batch * seq, width)


def _merge_kernel(x_ref, ya_ref, yb_ref, yc_ref, yd_ref, yx_ref,
                  g0, g1, g2, g3, g4, p0, p1, p2, p3, p4, b0, b1, b2, b3, b4, o_ref):
    x = x_ref[...]
    merged = None
    for y_ref, g_ref, p_ref, b_ref in ((ya_ref, g0, p0, b0), (yb_ref, g1, p1, b1), (yc_ref, g2, p2, b2),
                                       (yd_ref, g3, p3, b3), (yx_ref, g4, p4, b4)):
        gate = jax.nn.sigmoid(jnp.dot(x, g_ref[...], preferred_element_type=F32) + b_ref[...])
        term = gate * jnp.dot(y_ref[...], p_ref[...], preferred_element_type=F32)
        merged = term if merged is None else merged + term
    o_ref[...] = merged.astype(o_ref.dtype)


def _gated_merge(xb, ys, w_gate, w_branch, w_branch_x, b_gate, tm=512, tn=512):
    m, d = xb.shape
    tm = min(tm, m)
    nb = d // tn
    n_br = 5
    act = [pl.BlockSpec((tm, d), lambda j, i: (i, 0))]
    act += [pl.BlockSpec((tm, y.shape[1]), lambda j, i: (i, 0)) for y in ys]
    gates = [pl.BlockSpec((d, tn), functools.partial(lambda j, i, br: (0, br * nb + j), br=br)) for br in range(n_br)]
    projs = [pl.BlockSpec((None, w_branch.shape[1], tn), functools.partial(lambda j, i, br: (br, 0, j), br=br))
             for br in range(4)]
    projs.append(pl.BlockSpec((w_branch_x.shape[0], tn), lambda j, i: (0, j)))
    biases = [pl.BlockSpec((1, tn), functools.partial(lambda j, i, br: (0, br * nb + j), br=br)) for br in range(n_br)]
    bg2 = b_gate.reshape(1, n_br * d)
    return pl.pallas_call(
        _merge_kernel,
        grid=(nb, m // tm),
        in_specs=act + gates + projs + biases,
        out_specs=pl.BlockSpec((tm, tn), lambda j, i: (i, j)),
        out_shape=jax.ShapeDtypeStruct((m, d), BF16),
        compiler_params=_params("arbitrary", "arbitrary"),
        name="gated_merge",
    )(xb, *ys, *([w_gate] * n_br), *([w_branch] * 4), w_branch_x, *([bg2] * n_br))


def _layer_norm_rows(h, g, b):
    mu = jnp.mean(h, axis=-1, keepdims=True)
    hc = h - mu
    var = jnp.mean(hc * hc, axis=-1, keepdims=True)
    return hc * lax.rsqrt(var + LN_EPS) * g + b


def _out_ln_kernel(mg_ref, w_ref, x_ref, g_ref, b_ref, xo_ref, xb_ref, *, alpha):
    y = jnp.dot(mg_ref[...], w_ref[...], preferred_element_type=F32)
    xn = _layer_norm_rows(alpha * x_ref[...] + y, g_ref[...], b_ref[...])
    xo_ref[...] = xn
    xb_ref[...] = xn.astype(BF16)


def _out_proj_ln(merged, w_out_b, x, g, b, alpha, tm=512):
    m, d = x.shape
    tm = min(tm, m)
    row = pl.BlockSpec((tm, d), lambda i: (i, 0))
    vec = pl.BlockSpec((1, d), lambda i: (0, 0))
    return pl.pallas_call(
        functools.partial(_out_ln_kernel, alpha=alpha),
        grid=(m // tm,),
        in_specs=[row, pl.BlockSpec((d, d), lambda i: (0, 0)), row, vec, vec],
        out_specs=[row, row],
        out_shape=[jax.ShapeDtypeStruct((m, d), F32), jax.ShapeDtypeStruct((m, d), BF16)],
        compiler_params=_params("arbitrary"),
        name="out_proj_ln",
    )(merged, w_out_b, x, g.reshape(1, d), b.reshape(1, d))


def _split_bf16(a):
    hi = a.astype(BF16)
    lo = (a - hi.astype(F32)).astype(BF16)
    return hi, lo


def _router_kernel(x_ref, w_ref, b_ref, idx_ref, wt_ref, cnt_ref):
    @pl.when(pl.program_id(0) == 0)
    def _():
        cnt_ref[...] = jnp.zeros_like(cnt_ref)

    x = x_ref[...]
    w = w_ref[...]
    xh, xl = _split_bf16(x)
    wh, wl = _split_bf16(w)
    logits = (jnp.dot(xh, wh, preferred_element_type=F32) + jnp.dot(xl, wh, preferred_element_type=F32)
              + jnp.dot(xh, wl, preferred_element_type=F32))
    lt = logits.T[:N_EXPERTS, :]
    tm = lt.shape[1]
    scores = jax.nn.sigmoid(lt)
    biased = scores + b_ref[...]
    gsz = N_EXPERTS // N_GROUPS
    sub8 = lax.broadcasted_iota(jnp.int32, (gsz, tm), 0).astype(F32)
    grp_rows = []
    for g in range(N_GROUPS):
        blk = biased[g * gsz:(g + 1) * gsz, :]
        m1 = jnp.max(blk, axis=0, keepdims=True)
        i1 = jnp.min(jnp.where(blk == m1, sub8, float(gsz)), axis=0, keepdims=True)
        m2 = jnp.max(jnp.where(sub8 == i1, -jnp.inf, blk), axis=0, keepdims=True)
        grp_rows.append(m1 + m2)
    grp = jnp.concatenate(grp_rows, axis=0)
    subg = lax.broadcasted_iota(jnp.int32, (N_GROUPS, tm), 0).astype(F32)
    gsel = jnp.zeros((N_GROUPS, tm), F32)
    for _ in range(TOP_GROUPS):
        mg = jnp.max(grp, axis=0, keepdims=True)
        ig = jnp.min(jnp.where(grp == mg, subg, float(N_GROUPS)), axis=0, keepdims=True)
        hit = subg == ig
        gsel = jnp.where(hit, 1.0, gsel)
        grp = jnp.where(hit, -jnp.inf, grp)
    emask = jnp.concatenate([jnp.broadcast_to(gsel[g:g + 1, :], (gsz, tm)) for g in range(N_GROUPS)], axis=0)
    cand = jnp.where(emask > 0.5, biased, -jnp.inf)
    sube = lax.broadcasted_iota(jnp.int32, (N_EXPERTS, tm), 0).astype(F32)
    idx_rows, w_rows = [], []
    chosen = jnp.zeros((N_EXPERTS, tm), F32)
    for _ in range(TOP_K):
        mc = jnp.max(cand, axis=0, keepdims=True)
        ic = jnp.min(jnp.where(cand == mc, sube, float(N_EXPERTS)), axis=0, keepdims=True)
        hit = sube == ic
        idx_rows.append(ic)
        w_rows.append(jnp.sum(jnp.where(hit, scores, 0.0), axis=0, keepdims=True))
        chosen = jnp.where(hit, 1.0, chosen)
        cand = jnp.where(hit, -jnp.inf, cand)
    cnt_ref[...] += jnp.sum(chosen, axis=1, keepdims=True)
    wsel = jnp.concatenate(w_rows, axis=0)
    wsel = wsel / jnp.sum(wsel, axis=0, keepdims=True) * ROUTE_SCALE
    idx_ref[...] = jnp.concatenate(idx_rows, axis=0).astype(jnp.int32)
    wt_ref[...] = wsel


def _router(x, router_w, router_b, tm=512):
    m, d = x.shape
    tm = min(tm, m)
    w_pad = jnp.pad(router_w, ((0, 0), (0, LANES - N_EXPERTS)))
    return pl.pallas_call(
        _router_kernel,
        grid=(m // tm,),
        in_specs=[pl.BlockSpec((tm, d), lambda i: (i, 0)),
                  pl.BlockSpec((d, LANES), lambda i: (0, 0)),
                  pl.BlockSpec((N_EXPERTS, 1), lambda i: (0, 0))],
        out_specs=[pl.BlockSpec((TOP_K, tm), lambda i: (0, i)), pl.BlockSpec((TOP_K, tm), lambda i: (0, i)),
                   pl.BlockSpec((N_EXPERTS, LANES), lambda i: (0, 0))],
        out_shape=[jax.ShapeDtypeStruct((TOP_K, m), jnp.int32), jax.ShapeDtypeStruct((TOP_K, m), F32),
                   jax.ShapeDtypeStruct((N_EXPERTS, LANES), F32)],
        compiler_params=_params("arbitrary"),
        name="router",
    )(x, w_pad, router_b.reshape(N_EXPERTS, 1))


def _slot_kernel(idx_ref, base_ref, dest_ref, tri_ref, run_ref):
    tm = idx_ref.shape[1]

    @pl.when(pl.program_id(0) == 0)
    def _():
        row = lax.broadcasted_iota(jnp.int32, (tm, tm), 0)
        col = lax.broadcasted_iota(jnp.int32, (tm, tm), 1)
        tri_ref[...] = jnp.where(row < col, 1.0, 0.0).astype(BF16)
        run_ref[...] = base_ref[...]

    sube = lax.broadcasted_iota(jnp.int32, (N_EXPERTS, tm), 0)
    idx = idx_ref[...]
    base = run_ref[...]
    hits = [sube == idx[k:k + 1, :] for k in range(TOP_K)]
    chosen = jnp.zeros((N_EXPERTS, tm), F32)
    for hit in hits:
        chosen = jnp.where(hit, 1.0, chosen)
    before = jnp.dot(chosen.astype(BF16), tri_ref[...], preferred_element_type=F32)
    slot = base + before
    rows = [jnp.sum(jnp.where(hit, slot, 0.0), axis=0, keepdims=True) for hit in hits]
    run_ref[...] = base + jnp.sum(chosen, axis=1, keepdims=True)
    dest_ref[...] = jnp.concatenate(rows, axis=0).astype(jnp.int32)


def _assign_slots(idx_t, base, tm=512):
    kk, m = idx_t.shape
    tm = min(tm, m)
    return pl.pallas_call(
        _slot_kernel,
        grid=(m // tm,),
        in_specs=[pl.BlockSpec((kk, tm), lambda i: (0, i)), pl.BlockSpec((N_EXPERTS, 1), lambda i: (0, 0))],
        out_specs=pl.BlockSpec((kk, tm), lambda i: (0, i)),
        out_shape=jax.ShapeDtypeStruct((kk, m), jnp.int32),
        scratch_shapes=[pltpu.VMEM((tm, tm), BF16), pltpu.VMEM((N_EXPERTS, 1), F32)],
        compiler_params=_params("arbitrary"),
        name="assign_slots",
    )(idx_t, base.astype(F32).reshape(N_EXPERTS, 1))


def _expert_kernel(be_ref, nb_ref, x_ref, wg_ref, wu_ref, wd_ref, *rest):
    o_ref = rest[-1]
    del be_ref
    i = pl.program_id(0)

    @pl.when(i < nb_ref[0])
    def _():
        x = x_ref[...]
        y = None
        for c in range(wg_ref.shape[1] // MXU_WIDTH):
            cs = slice(c * MXU_WIDTH, (c + 1) * MXU_WIDTH)
            gate = jnp.dot(x, wg_ref[:, cs].astype(BF16), preferred_element_type=F32)
            up = jnp.dot(x, wu_ref[:, cs].astype(BF16), preferred_element_type=F32)
            act = (jax.nn.silu(gate) * up).astype(BF16)
            part = jnp.dot(act, wd_ref[cs, :].astype(BF16), preferred_element_type=F32)
            y = part if y is None else y + part
        o_ref[...] = y.astype(o_ref.dtype)

    @pl.when(i >= nb_ref[0])
    def _():
        o_ref[...] = jnp.zeros_like(o_ref)


def _expert_ffn(xs, block_e, nblocks, w_gate, w_up, w_down, layer, rows, run_after=()):
    ns, d = xs.shape
    de = w_gate.shape[3]
    grid_spec = pltpu.PrefetchScalarGridSpec(
        num_scalar_prefetch=2,
        grid=(ns // rows,),
        in_specs=[pl.BlockSpec((rows, d), lambda i, be, nb: (i, 0)),
                  pl.BlockSpec((None, None, d, de), lambda i, be, nb: (layer, be[i], 0, 0)),
                  pl.BlockSpec((None, None, d, de), lambda i, be, nb: (layer, be[i], 0, 0)),
                  pl.BlockSpec((None, None, de, d), lambda i, be, nb: (layer, be[i], 0, 0))]
        + [pl.BlockSpec(memory_space=pl.ANY)] * len(run_after),
        out_specs=pl.BlockSpec((rows, d), lambda i, be, nb: (i, 0)),
    )
    return pl.pallas_call(
        _expert_kernel,
        grid_spec=grid_spec,
        out_shape=jax.ShapeDtypeStruct((ns, d), BF16),
        compiler_params=_params("arbitrary"),
        name="expert_ffn",
    )(block_e, nblocks, xs, w_gate, w_up, w_down, *run_after)


def _combine_ln_kernel(*refs, alpha):
    yg_refs = refs[:TOP_K]
    wt_ref, sh_ref, x_ref, g_ref, b_ref, xo_ref, xb_ref = refs[TOP_K:]
    wt = wt_ref[...]
    routed = sh_ref[...].astype(F32)
    for k in range(TOP_K):
        routed = routed + wt[:, k:k + 1] * yg_refs[k][...].astype(F32)
    xn = _layer_norm_rows(alpha * x_ref[...] + routed, g_ref[...], b_ref[...])
    xo_ref[...] = xn
    xb_ref[...] = xn.astype(BF16)


def _combine_ln(yg, wt, shared, x, g, b, alpha, tm=256):
    m, d = x.shape
    tm = min(tm, m)
    nb = m // tm
    row = pl.BlockSpec((tm, d), lambda i: (i, 0))
    vec = pl.BlockSpec((1, d), lambda i: (0, 0))
    yg_specs = [pl.BlockSpec((tm, d), functools.partial(lambda i, k: (k * nb + i, 0), k=k)) for k in range(TOP_K)]
    return pl.pallas_call(
        functools.partial(_combine_ln_kernel, alpha=alpha),
        grid=(nb,),
        in_specs=yg_specs + [pl.BlockSpec((tm, TOP_K), lambda i: (i, 0)), row, row, vec, vec],
        out_specs=[row, row],
        out_shape=[jax.ShapeDtypeStruct((m, d), F32), jax.ShapeDtypeStruct((m, d), BF16)],
        compiler_params=_params("arbitrary"),
        name="combine_ln",
    )(*([yg] * TOP_K), wt, shared, x, g.reshape(1, d), b.reshape(1, d))


def _moe_layer(x1, xb1, layer, router_w, router_b, e_gate, e_up, e_down, s_gate, s_up, s_down, g, b, alpha):
    t, d = x1.shape
    rows = min(MOE_ROWS, t)
    idx_t, wt_t, cnt = _router(x1, router_w, router_b)
    n_assign = t * TOP_K
    i32 = jnp.int32
    counts = cnt[:, 0].astype(i32)
    padded = (counts + rows - 1) // rows * rows
    ends_p = jnp.cumsum(padded)
    pad_e = padded - counts
    dest = _assign_slots(idx_t, ends_p - padded).reshape(n_assign)
    nblk = n_assign // rows + N_EXPERTS
    n_pad = nblk * rows - n_assign
    pad_i = jnp.arange(n_pad, dtype=i32)
    pad_before = jnp.cumsum(pad_e) - pad_e
    pad_slot = pad_i + jnp.sum(jnp.where(pad_before[None, :] <= pad_i[:, None], counts[None, :], 0), axis=1)
    keys = jnp.concatenate([dest, pad_slot])
    toks = jnp.concatenate([jnp.arange(n_assign, dtype=i32) % t, pad_i % t])
    _, slot_tok = lax.sort_key_val(keys, toks)
    blk0 = jnp.arange(nblk, dtype=i32) * rows
    block_e = jnp.minimum(jnp.sum((ends_p[None, :] <= blk0[:, None]).astype(i32), axis=1), N_EXPERTS - 1)
    nused = (ends_p[-1] // rows).astype(i32).reshape(1)
    xs = jnp.take(xb1, slot_tok, axis=0, mode='clip')
    srows = min(2 * MOE_ROWS, t)
    shared = _expert_ffn(xb1, jnp.zeros((t // srows,), i32), jnp.full((1,), t // srows, i32),
                         s_gate[:, None], s_up[:, None], s_down[:, None], layer, srows)
    ys = _expert_ffn(xs, block_e, nused, e_gate, e_up, e_down, layer, rows, run_after=(shared,))
    yg = jnp.take(ys, dest, axis=0, mode='clip')
    return _combine_ln(yg, wt_t.T, shared, x1, g, b, alpha)


def _mixer_layer(x, xb, memb, layer, tabs_a, tabs_c, batch, seq, w_in, w_c, pool_w, pool_scale, c_lambda, c_norm,
                 conv_w, w_mem_kv, merge_w, b_gate, w_out_b, g, b, alpha):
    mix = merge_w[1].shape[1]
    xw = merge_w[2].shape[0]
    a_cols, c_cols, d_cols = 3 * mix, 3 * mix, 3 * mix
    c0_b, c0_d = a_cols, a_cols + mix + c_cols
    pa = _proj(xb, w_in, layer, range(3), mix, tabs_a, seq, half=HEAD_DIM // 8,
               rope_blocks=(1,) * (mix // LANES), rope_jmax=2)
    tn_c = 12 * LANES
    pc = _proj(xb, w_c, 0, range(c_cols // tn_c), tn_c, tabs_c, seq,
               half=C_QK_DIM // 8, rope_blocks=(1, 2, 0) * 4, tm=512)
    blocks = list(range(c0_b // mix, c0_b // mix + 1)) + list(range(c0_d // mix, c0_d // mix + d_cols // mix))
    pbd = _proj(xb, w_in, layer, blocks, mix, tabs_a, seq)
    px = _proj(xb, w_in, layer, [(c0_d + d_cols) // xw], xw, tabs_a, seq)
    ya = _dilated_attention(pa, batch, seq)
    yb = _pooling_mixer(pbd, pool_w[layer], pool_scale[layer], batch, seq, 0)
    yc = _diff_attention(pc, c_lambda[layer], c_norm[layer], batch, seq, layer)
    yd = _gated_short_conv(pbd, conv_w[layer], batch, seq, mix)
    kv = _proj(memb, w_mem_kv, layer, range(2), xw, tabs_a, seq)
    yx = _cross_attention(px, kv, batch, seq, 0, xw)
    merged = _gated_merge(xb, (ya, yb, yc, yd, yx), *merge_w, b_gate[layer])
    return _out_proj_ln(merged, w_out_b, x, g[layer], b[layer], alpha)


def kernel(x, mem, w_in, pool_w, pool_scale, c_lambda, c_norm, conv_w, w_mem_kv, w_branch, w_branch_x,
           w_gate, b_gate, w_out, ln1_g, ln1_b, router_w, router_b, e_gate, e_up, e_down, s_gate, s_up,
           s_down, ln2_g, ln2_b):
    batch, seq, d = x.shape
    depth = w_in.shape[0]
    alpha = (2.0 * depth) ** 0.25
    tabs_a = _rope_tables(seq, HEAD_DIM // 4, LANES)
    tabs_ck = _rope_tables(seq, C_QK_DIM // 4, C_QK_DIM)
    q_scale = C_QK_DIM ** -0.5 * math.log2(math.e)
    tabs_c = tuple(tab * q_scale for tab in tabs_ck) + tabs_ck
    mix = w_branch.shape[2]
    groups = BATCH_GROUPS if batch % BATCH_GROUPS == 0 else 1
    gb = batch // groups
    state = []
    for gi in range(groups):
        xg = x[gi * gb:(gi + 1) * gb].reshape(gb * seq, d)
        state.append((xg, xg.astype(BF16), mem[gi * gb:(gi + 1) * gb].reshape(-1, d).astype(BF16)))
    for l in range(depth):
        w_c = w_in[l:l + 1, :, 4 * mix:7 * mix]
        merge_w = (w_gate[l].astype(BF16), w_branch[l].astype(BF16), w_branch_x[l].astype(BF16))
        w_out_b = w_out[l].astype(BF16)
        for gi in range(groups):
            xf, xb, memb = state[gi]
            xf, xb = _mixer_layer(xf, xb, memb, l, tabs_a, tabs_c, gb, seq, w_in, w_c, pool_w, pool_scale,
                                  c_lambda, c_norm, conv_w, w_mem_kv, merge_w, b_gate, w_out_b,
                                  ln1_g, ln1_b, alpha)
            xf, xb = _moe_layer(xf, xb, l, router_w[l], router_b[l], e_gate, e_up, e_down, s_gate,
                                s_up, s_down, ln2_g[l], ln2_b[l], alpha)
            state[gi] = (xf, xb, memb)
    return jnp.concatenate([s[0] for s in state], axis=0).reshape(batch, seq, d)
```

```python
import functools
import math

import jax
import jax.numpy as jnp
from jax import lax
from jax.experimental import pallas as pl
from jax.experimental.pallas import tpu as pltpu

F32 = jnp.float32
BF16 = jnp.bfloat16

LANES = 128
MXU_WIDTH = 256
HEAD_DIM = 128
ROPE_THETA = 500000.0
A_PATTERNS = ((128, 1), (512, 4), (2048, 16))
A_HALF = 64
B_WINDOWS = (2, 4, 8, 16)
C_QK_DIM = 64
N_EXPERTS = 64
N_GROUPS = 8
TOP_GROUPS = 4
TOP_K = 8
ROUTE_SCALE = 2.5
LN_EPS = 1e-5
NEG_INF = -1e30
VMEM_LIMIT_BYTES = 56 * 1024 * 1024
MOE_ROWS = 256
BATCH_GROUPS = 1


def _params(*sem):
    return pltpu.CompilerParams(dimension_semantics=sem, vmem_limit_bytes=VMEM_LIMIT_BYTES)


def _rope_tables(seq, rot_dim, period):
    half = rot_dim // 2
    inv = 1.0 / (ROPE_THETA ** (jnp.arange(0, rot_dim, 2, dtype=F32) / rot_dim))
    ang = jnp.arange(seq, dtype=F32)[:, None] * inv[None, :]
    cos, sin = jnp.cos(ang), jnp.sin(ang)
    lane = jnp.arange(LANES) % period
    first = lane < half
    second = (lane >= half) & (lane < 2 * half)
    idx = jnp.where(first, lane, jnp.where(second, lane - half, 0))
    cos_l, sin_l = cos[:, idx], sin[:, idx]
    ct = jnp.where(first | second, cos_l, 1.0)
    s1 = jnp.where(first, -sin_l, 0.0)
    s2 = jnp.where(second, sin_l, 0.0)
    return ct.astype(F32), s1.astype(F32), s2.astype(F32)


def _proj_kernel(x_ref, w_ref, *rest, half, rope_blocks, rope_jmax):
    tab_refs, o_ref, wb_ref = rest[:-2], rest[-2], rest[-1]
    j = pl.program_id(0)
    i = pl.program_id(1)

    @pl.when(i == 0)
    def _():
        wb_ref[...] = w_ref[...].astype(BF16)

    def plain():
        for c in range(o_ref.shape[1] // MXU_WIDTH):
            cs = slice(c * MXU_WIDTH, (c + 1) * MXU_WIDTH)
            o_ref[:, cs] = jnp.dot(x_ref[...], wb_ref[:, cs], preferred_element_type=F32).astype(o_ref.dtype)

    def roped():
        tabs = [r[...] for r in tab_refs]
        per = MXU_WIDTH // LANES
        for c in range(len(rope_blocks) // per):
            acc = jnp.dot(x_ref[...], wb_ref[:, c * MXU_WIDTH:(c + 1) * MXU_WIDTH], preferred_element_type=F32)
            for b in range(c * per, (c + 1) * per):
                use = rope_blocks[b]
                blk = acc[:, (b - c * per) * LANES:(b - c * per + 1) * LANES]
                if use:
                    ct, s1, s2 = tabs[3 * (use - 1):3 * use]
                    packed = pltpu.bitcast(blk.astype(BF16), jnp.uint32)
                    fwd = pltpu.bitcast(pltpu.roll(packed, LANES - half, 1), BF16).astype(F32)
                    bwd = pltpu.bitcast(pltpu.roll(packed, half, 1), BF16).astype(F32)
                    blk = blk * ct + fwd * s1 + bwd * s2
                o_ref[:, b * LANES:(b + 1) * LANES] = blk.astype(o_ref.dtype)

    if not any(rope_blocks):
        plain()
    elif rope_jmax is None:
        roped()
    else:
        pl.when(j < rope_jmax)(roped)
        pl.when(j >= rope_jmax)(plain)


def _proj(xb, w, layer, col_blocks, tn, tables, seq, *, half=0, rope_blocks=None, rope_jmax=None, tm=1024):
    m, k = xb.shape
    nj = len(col_blocks)
    tm = min(tm, m)
    rope_blocks = tuple(int(u) for u in rope_blocks) if rope_blocks is not None else (0,) * (tn // LANES)
    col_blocks = tuple(col_blocks)
    first, contiguous = col_blocks[0], all(col_blocks[a] == col_blocks[0] + a for a in range(nj))
    if contiguous:
        w_map = lambda j, i: (layer, 0, first + j)
    else:
        n0 = next(a for a in range(1, nj) if col_blocks[a] != col_blocks[0] + a)
        second = col_blocks[n0]
        w_map = lambda j, i: (layer, 0, jnp.where(j < n0, first + j, second + j - n0))
    sblocks = seq // tm if seq >= tm else 1
    tab_spec = pl.BlockSpec((tm, LANES), lambda j, i: (i % sblocks, 0))
    kern = functools.partial(_proj_kernel, half=half, rope_blocks=rope_blocks, rope_jmax=rope_jmax)
    return pl.pallas_call(
        kern,
        grid=(nj, m // tm),
        in_specs=[pl.BlockSpec((tm, k), lambda j, i: (i, 0)),
                  pl.BlockSpec((None, k, tn), w_map)] + [tab_spec] * len(tables),
        out_specs=pl.BlockSpec((tm, tn), lambda j, i: (i, j)),
        out_shape=jax.ShapeDtypeStruct((m, nj * tn), BF16),
        scratch_shapes=[pltpu.VMEM((k, tn), BF16)],
        compiler_params=_params("arbitrary", "arbitrary"),
        name="proj",
    )(xb, w, *tables)


def _dilated_kernel(q_ref, kp_ref, km_ref, kn_ref, vp_ref, vm_ref, vn_ref, o_ref, lse_ref, *, tl, seq_len, heads):
    t = pl.program_id(2)
    l0 = t * tl
    sub = 128
    nsub = tl // sub
    scale = HEAD_DIM ** -0.5
    iq = lax.broadcasted_iota(jnp.int32, (sub, sub + 2 * A_HALF), 0)
    ik = lax.broadcasted_iota(jnp.int32, (sub, sub + 2 * A_HALF), 1)
    band = jnp.abs(ik - A_HALF - iq) <= A_HALF
    lane = lax.broadcasted_iota(jnp.int32, (sub, LANES), 1)
    valid = []
    for a in range(nsub):
        kk = l0 + a * sub - A_HALF + ik
        valid.append(band & (kk >= 0) & (kk < seq_len))
    lse_rows = [jnp.zeros((sub, LANES), F32) for _ in range(nsub)]
    for h in range(heads):
        cs = slice(h * HEAD_DIM, (h + 1) * HEAD_DIM)
        kcat = jnp.concatenate([kp_ref[:, cs], km_ref[:, cs], kn_ref[:, cs]], axis=0)
        vcat = jnp.concatenate([vp_ref[:, cs], vm_ref[:, cs], vn_ref[:, cs]], axis=0)
        for a in range(nsub):
            q = q_ref[a * sub:(a + 1) * sub, cs]
            kw = kcat[a * sub:a * sub + sub + 2 * A_HALF]
            vw = vcat[a * sub:a * sub + sub + 2 * A_HALF]
            s = lax.dot_general(q, kw, (((1,), (1,)), ((), ())), preferred_element_type=F32) * scale
            s = jnp.where(valid[a], s, NEG_INF)
            m = jnp.max(s, axis=-1, keepdims=True)
            p = jnp.exp(s - m)
            den = jnp.sum(p, axis=-1, keepdims=True)
            o = jnp.dot(p.astype(BF16), vw, preferred_element_type=F32) / den
            o_ref[a * sub:(a + 1) * sub, cs] = o.astype(o_ref.dtype)
            lse_rows[a] = jnp.where(lane == h, m + jnp.log(den), lse_rows[a])
    for a in range(nsub):
        lse_ref[a * sub:(a + 1) * sub, :] = lse_rows[a]


def _dilated_pattern(pa, batch, seq, dil):
    width = pa.shape[1] // 3
    heads = width // HEAD_DIM
    sl = seq // dil
    tl = min(512, sl)
    nt = sl // tl
    hb = tl // A_HALF
    last_halo = sl // A_HALF - 1
    view = pa.reshape(batch, sl, dil * 3 * width)

    def main(c):
        return pl.BlockSpec((None, tl, width), lambda b, r, t: (b, t, r * 3 + c))

    def prev(c):
        return pl.BlockSpec((None, A_HALF, width), lambda b, r, t: (b, jnp.maximum(t * hb - 1, 0), r * 3 + c))

    def nxt(c):
        return pl.BlockSpec((None, A_HALF, width),
                            lambda b, r, t: (b, jnp.minimum((t + 1) * hb, last_halo), r * 3 + c))

    kern = functools.partial(_dilated_kernel, tl=tl, seq_len=sl, heads=heads)
    o, lse = pl.pallas_call(
        kern,
        grid=(batch, dil, nt),
        in_specs=[main(0), prev(1), main(1), nxt(1), prev(2), main(2), nxt(2)],
        out_specs=[pl.BlockSpec((None, tl, width), lambda b, r, t: (b, t, r)),
                   pl.BlockSpec((None, tl, LANES), lambda b, r, t: (b, t, r))],
        out_shape=[jax.ShapeDtypeStruct((batch, sl, dil * width), BF16),
                   jax.ShapeDtypeStruct((batch, sl, dil * LANES), F32)],
        compiler_params=_params("arbitrary", "arbitrary", "arbitrary"),
        name=f"dilated_d{dil}",
    )(view, view, view, view, view, view, view)
    return o.reshape(batch * seq, width), lse.reshape(batch * seq, LANES)


def _dilated_combine_kernel(o0_ref, o1_ref, o2_ref, l0_ref, l1_ref, l2_ref, y_ref, *, heads):
    l0, l1, l2 = l0_ref[...], l1_ref[...], l2_ref[...]
    m = jnp.maximum(jnp.maximum(l0, l1), l2)
    e0, e1, e2 = jnp.exp(l0 - m), jnp.exp(l1 - m), jnp.exp(l2 - m)
    inv = 1.0 / (e0 + e1 + e2)
    w0, w1, w2 = e0 * inv, e1 * inv, e2 * inv
    for h in range(heads):
        cs = slice(h * HEAD_DIM, (h + 1) * HEAD_DIM)
        y = (w0[:, h:h + 1] * o0_ref[:, cs].astype(F32) + w1[:, h:h + 1] * o1_ref[:, cs].astype(F32)
             + w2[:, h:h + 1] * o2_ref[:, cs].astype(F32))
        y_ref[:, cs] = y.astype(y_ref.dtype)


def _dilated_attention(pa, batch, seq):
    outs, lses = zip(*[_dilated_pattern(pa, batch, seq, d) for _, d in A_PATTERNS])
    m, width = outs[0].shape
    tm = min(512, m)
    ospec = pl.BlockSpec((tm, width), lambda i: (i, 0))
    lspec = pl.BlockSpec((tm, LANES), lambda i: (i, 0))
    return pl.pallas_call(
        functools.partial(_dilated_combine_kernel, heads=width // HEAD_DIM),
        grid=(m // tm,),
        in_specs=[ospec, ospec, ospec, lspec, lspec, lspec],
        out_specs=ospec,
        out_shape=jax.ShapeDtypeStruct((m, width), BF16),
        compiler_params=_params("arbitrary"),
        name="dilated_combine",
    )(*outs, *lses)


def _pool_kernel(u_ref, w_ref, sc_ref, o_ref, pad_ref, *, seq):
    g = pl.program_id(1)
    pad = 8
    uf = u_ref[...].astype(F32)
    pad_ref[0:pad, :] = jnp.zeros((pad, uf.shape[1]), F32)
    pad_ref[pad + seq:pad + seq + pad, :] = jnp.zeros((pad, uf.shape[1]), F32)
    pad_ref[pad:pad + seq, :] = uf
    pos = lax.broadcasted_iota(jnp.int32, (seq, 1), 0)
    for gi, win in enumerate(B_WINDOWS):
        @pl.when(g == gi)
        def _(win=win):
            before, after = win // 2, win - win // 2
            tot = pad_ref[pad - before:pad - before + seq, :]
            for off in range(-before + 1, after):
                tot = tot + pad_ref[pad + off:pad + off + seq, :]
            cnt = (jnp.minimum(pos + after, seq) - jnp.maximum(pos - before, 0)).astype(F32)
            pooled = tot / cnt - uf
            mixed = jnp.dot(pooled.astype(BF16), w_ref[...].astype(BF16), preferred_element_type=F32)
            o_ref[...] = (mixed * sc_ref[...]).astype(o_ref.dtype)


def _pooling_mixer(pbdx, pool_w, pool_scale, batch, seq, col0):
    ng, cg = pool_w.shape[0], pool_w.shape[1]
    view = pbdx.reshape(batch, seq, pbdx.shape[1])
    cb = col0 // cg
    return pl.pallas_call(
        functools.partial(_pool_kernel, seq=seq),
        grid=(batch, ng),
        in_specs=[pl.BlockSpec((None, seq, cg), lambda b, g: (b, 0, cb + g)),
                  pl.BlockSpec((None, cg, cg), lambda b, g: (g, 0, 0)),
                  pl.BlockSpec((1, cg), lambda b, g: (0, g))],
        out_specs=pl.BlockSpec((None, seq, cg), lambda b, g: (b, 0, g)),
        out_shape=jax.ShapeDtypeStruct((batch, seq, ng * cg), BF16),
        scratch_shapes=[pltpu.VMEM((seq + 16, cg), F32)],
        compiler_params=_params("arbitrary", "arbitrary"),
        name="pooling",
    )(view, pool_w, pool_scale.reshape(1, ng * cg)).reshape(batch * seq, ng * cg)


def _conv_kernel(bg_ref, cg_ref, h_ref, w_ref, o_ref, pad_ref, *, seq):
    pad = 8
    u = cg_ref[...].astype(F32) * h_ref[...].astype(F32)
    cols = u.shape[1]
    pad_ref[0:pad, :] = jnp.zeros((pad, cols), F32)
    pad_ref[pad + seq:pad + seq + pad, :] = jnp.zeros((pad, cols), F32)
    pad_ref[pad:pad + seq, :] = u
    w = w_ref[...]
    conv = (pad_ref[pad - 1:pad - 1 + seq, :] * w[0:1, :] + u * w[1:2, :]
            + pad_ref[pad + 1:pad + 1 + seq, :] * w[2:3, :])
    o_ref[...] = (bg_ref[...].astype(F32) * conv).astype(o_ref.dtype)


def _gated_short_conv(pbdx, conv_w, batch, seq, col0):
    width = conv_w.shape[1]
    cb = 256
    nb = width // cb
    view = pbdx.reshape(batch, seq, pbdx.shape[1])
    base = col0 // cb

    def spec(part):
        return pl.BlockSpec((None, seq, cb), lambda b, c: (b, 0, base + part * nb + c))

    return pl.pallas_call(
        functools.partial(_conv_kernel, seq=seq),
        grid=(batch, nb),
        in_specs=[spec(0), spec(1), spec(2), pl.BlockSpec((3, cb), lambda b, c: (0, c))],
        out_specs=pl.BlockSpec((None, seq, cb), lambda b, c: (b, 0, c)),
        out_shape=jax.ShapeDtypeStruct((batch, seq, width), BF16),
        scratch_shapes=[pltpu.VMEM((seq + 16, cb), F32)],
        compiler_params=_params("arbitrary", "arbitrary"),
        name="short_conv",
    )(view, view, view, conv_w).reshape(batch * seq, width)


def _diff_kernel(q_ref, k_ref, v_ref, lam_ref, nrm_ref, o_ref, vone_ref, *, lam_init, chunk):
    t = pl.program_id(2)
    seq = k_ref.shape[0]
    tq = q_ref.shape[0]

    @pl.when(t == 0)
    def _():
        vone_ref[:, :LANES] = v_ref[...]
        vone_ref[:, LANES:] = jnp.ones((seq, LANES), BF16)

    lq = lam_ref[...].astype(F32)
    lam = (jnp.exp(jnp.sum(lq[0:1] * lq[1:2], axis=-1, keepdims=True))
           - jnp.exp(jnp.sum(lq[2:3] * lq[3:4], axis=-1, keepdims=True)) + lam_init)
    q = q_ref[...]
    lane = lax.broadcasted_iota(jnp.int32, q.shape, 1)
    zero = jnp.zeros_like(q)
    q2 = jnp.concatenate([jnp.where(lane < C_QK_DIM, q, zero), jnp.where(lane >= C_QK_DIM, q, zero)], axis=0)
    m = acc = None
    for c in range(seq // chunk):
        ks = slice(c * chunk, (c + 1) * chunk)
        s = lax.dot_general(q2, k_ref[ks, :], (((1,), (1,)), ((), ())), preferred_element_type=F32)
        mc = jnp.max(s, axis=-1, keepdims=True)
        m_new = mc if m is None else jnp.maximum(m, mc)
        e = jnp.exp2((s - m_new).astype(BF16))
        pv = jnp.dot(e, vone_ref[ks, :], preferred_element_type=F32)
        acc = pv if acc is None else acc * jnp.exp2(m - m_new) + pv
        m = m_new
    out = acc[:, :LANES] / acc[:, LANES:]
    o = out[:tq] - lam * out[tq:]
    o = o * lax.rsqrt(jnp.mean(o * o, axis=-1, keepdims=True) + LN_EPS) * nrm_ref[...] * (1.0 - lam_init)
    o_ref[...] = o.astype(o_ref.dtype)


def _diff_attention(pc, c_lambda, c_norm, batch, seq, layer, tq=2048, chunk=256):
    heads = pc.shape[1] // (3 * LANES)
    tq = min(tq, seq)
    chunk = min(chunk, seq)
    lam_init = 0.8 - 0.6 * math.exp(-0.3 * layer)
    view = pc.reshape(batch, seq, pc.shape[1])
    return pl.pallas_call(
        functools.partial(_diff_kernel, lam_init=lam_init, chunk=chunk),
        grid=(batch, heads, seq // tq),
        in_specs=[pl.BlockSpec((None, tq, LANES), lambda b, h, t: (b, t, 3 * h)),
                  pl.BlockSpec((None, seq, LANES), lambda b, h, t: (b, 0, 3 * h + 1)),
                  pl.BlockSpec((None, seq, LANES), lambda b, h, t: (b, 0, 3 * h + 2)),
                  pl.BlockSpec((4, C_QK_DIM), lambda b, h, t: (0, 0)),
                  pl.BlockSpec((1, LANES), lambda b, h, t: (0, 0))],
        out_specs=pl.BlockSpec((None, tq, LANES), lambda b, h, t: (b, t, h)),
        out_shape=jax.ShapeDtypeStruct((batch, seq, heads * LANES), BF16),
        scratch_shapes=[pltpu.VMEM((seq, 2 * LANES), BF16)],
        compiler_params=_params("arbitrary", "arbitrary", "arbitrary"),
        name="diff_attention",
    )(view, view, view, c_lambda, c_norm.reshape(1, LANES)).reshape(batch * seq, heads * LANES)


def _cross_kernel(q_ref, kv_ref, o_ref, *, heads):
    scale = HEAD_DIM ** -0.5
    width = heads * HEAD_DIM
    for h in range(heads):
        cs = slice(h * HEAD_DIM, (h + 1) * HEAD_DIM)
        k = kv_ref[:, h * HEAD_DIM:(h + 1) * HEAD_DIM]
        v = kv_ref[:, width + h * HEAD_DIM:width + (h + 1) * HEAD_DIM]
        s = lax.dot_general(q_ref[:, cs], k, (((1,), (1,)), ((), ())), preferred_element_type=F32) * scale
        m = jnp.max(s, axis=-1, keepdims=True)
        e = jnp.exp(s - m)
        den = jnp.sum(e, axis=-1, keepdims=True)
        o = jnp.dot(e.astype(BF16), v, preferred_element_type=F32) / den
        o_ref[:, cs] = o.astype(o_ref.dtype)


def _cross_attention(pbdx, kv, batch, seq, col0, width, tq=512):
    mem_len = kv.shape[0] // batch
    tq = min(tq, seq)
    view = pbdx.reshape(batch, seq, pbdx.shape[1])
    return pl.pallas_call(
        functools.partial(_cross_kernel, heads=width // HEAD_DIM),
        grid=(batch, seq // tq),
        in_specs=[pl.BlockSpec((None, tq, width), lambda b, t: (b, t, col0 // width)),
                  pl.BlockSpec((None, mem_len, 2 * width), lambda b, t: (b, 0, 0))],
        out_specs=pl.BlockSpec((None, tq, width), lambda b, t: (b, t, 0)),
        out_shape=jax.ShapeDtypeStruct((batch, seq, width), BF16),
        compiler_params=_params("arbitrary", "arbitrary"),
        name="cross_attention",
    )(view, kv.reshape(batch, mem_len, 2 * width)).reshape(batch * seq, width)


def _merge_kernel(x_ref, ya_ref, yb_ref, yc_ref, yd_ref, yx_ref,
                  g0, g1, g2, g3, g4, p0, p1, p2, p3, p4, b0, b1, b2, b3, b4, o_ref):
    x = x_ref[...]
    branches = ((ya_ref, g0, p0, b0), (yb_ref, g1, p1, b1), (yc_ref, g2, p2, b2),
                (yd_ref, g3, p3, b3), (yx_ref, g4, p4, b4))
    for c in range(o_ref.shape[1] // MXU_WIDTH):
        cs = slice(c * MXU_WIDTH, (c + 1) * MXU_WIDTH)
        merged = None
        for y_ref, g_ref, p_ref, b_ref in branches:
            gate = jax.nn.sigmoid(jnp.dot(x, g_ref[:, cs], preferred_element_type=F32) + b_ref[:, cs])
            term = gate * jnp.dot(y_ref[...], p_ref[:, cs], preferred_element_type=F32)
            merged = term if merged is None else merged + term
        o_ref[:, cs] = merged.astype(o_ref.dtype)


def _gated_merge(xb, ys, w_gate, w_branch, w_branch_x, b_gate, tm=512, tn=512):
    m, d = xb.shape
    tm = min(tm, m)
    nb = d // tn
    n_br = 5
    act = [pl.BlockSpec((tm, d), lambda j, i: (i, 0))]
    act += [pl.BlockSpec((tm, y.shape[1]), lambda j, i: (i, 0)) for y in ys]
    gates = [pl.BlockSpec((d, tn), functools.partial(lambda j, i, br: (0, br * nb + j), br=br)) for br in range(n_br)]
    projs = [pl.BlockSpec((None, w_branch.shape[1], tn), functools.partial(lambda j, i, br: (br, 0, j), br=br))
             for br in range(4)]
    projs.append(pl.BlockSpec((w_branch_x.shape[0], tn), lambda j, i: (0, j)))
    biases = [pl.BlockSpec((1, tn), functools.partial(lambda j, i, br: (0, br * nb + j), br=br)) for br in range(n_br)]
    bg2 = b_gate.reshape(1, n_br * d)
    return pl.pallas_call(
        _merge_kernel,
        grid=(nb, m // tm),
        in_specs=act + gates + projs + biases,
        out_specs=pl.BlockSpec((tm, tn), lambda j, i: (i, j)),
        out_shape=jax.ShapeDtypeStruct((m, d), BF16),
        compiler_params=_params("arbitrary", "arbitrary"),
        name="gated_merge",
    )(xb, *ys, *([w_gate] * n_br), *([w_branch] * 4), w_branch_x, *([bg2] * n_br))


def _layer_norm_rows(h, g, b):
    mu = jnp.mean(h, axis=-1, keepdims=True)
    hc = h - mu
    var = jnp.mean(hc * hc, axis=-1, keepdims=True)
    return hc * lax.rsqrt(var + LN_EPS) * g + b


def _out_ln_kernel(mg_ref, w_ref, x_ref, g_ref, b_ref, xo_ref, xb_ref, *, alpha):
    y = jnp.dot(mg_ref[...], w_ref[...], preferred_element_type=F32)
    xn = _layer_norm_rows(alpha * x_ref[...] + y, g_ref[...], b_ref[...])
    xo_ref[...] = xn
    xb_ref[...] = xn.astype(BF16)


def _out_proj_ln(merged, w_out_b, x, g, b, alpha, tm=512):
    m, d = x.shape
    tm = min(tm, m)
    row = pl.BlockSpec((tm, d), lambda i: (i, 0))
    vec = pl.BlockSpec((1, d), lambda i: (0, 0))
    return pl.pallas_call(
        functools.partial(_out_ln_kernel, alpha=alpha),
        grid=(m // tm,),
        in_specs=[row, pl.BlockSpec((d, d), lambda i: (0, 0)), row, vec, vec],
        out_specs=[row, row],
        out_shape=[jax.ShapeDtypeStruct((m, d), F32), jax.ShapeDtypeStruct((m, d), BF16)],
        compiler_params=_params("arbitrary"),
        name="out_proj_ln",
    )(merged, w_out_b, x, g.reshape(1, d), b.reshape(1, d))


def _split_bf16(a):
    hi = a.astype(BF16)
    lo = (a - hi.astype(F32)).astype(BF16)
    return hi, lo


def _router_kernel(x_ref, w_ref, b_ref, idx_ref, wt_ref, cnt_ref):
    @pl.when(pl.program_id(0) == 0)
    def _():
        cnt_ref[...] = jnp.zeros_like(cnt_ref)

    x = x_ref[...]
    w = w_ref[...]
    xh, xl = _split_bf16(x)
    wh, wl = _split_bf16(w)
    logits = (jnp.dot(xh, wh, preferred_element_type=F32) + jnp.dot(xl, wh, preferred_element_type=F32)
              + jnp.dot(xh, wl, preferred_element_type=F32))
    lt = logits.T[:N_EXPERTS, :]
    tm = lt.shape[1]
    scores = jax.nn.sigmoid(lt)
    biased = scores + b_ref[...]
    gsz = N_EXPERTS // N_GROUPS
    sub8 = lax.broadcasted_iota(jnp.int32, (gsz, tm), 0).astype(F32)
    grp_rows = []
    for g in range(N_GROUPS):
        blk = biased[g * gsz:(g + 1) * gsz, :]
        m1 = jnp.max(blk, axis=0, keepdims=True)
        i1 = jnp.min(jnp.where(blk == m1, sub8, float(gsz)), axis=0, keepdims=True)
        m2 = jnp.max(jnp.where(sub8 == i1, -jnp.inf, blk), axis=0, keepdims=True)
        grp_rows.append(m1 + m2)
    grp = jnp.concatenate(grp_rows, axis=0)
    subg = lax.broadcasted_iota(jnp.int32, (N_GROUPS, tm), 0).astype(F32)
    gsel = jnp.zeros((N_GROUPS, tm), F32)
    for _ in range(TOP_GROUPS):
        mg = jnp.max(grp, axis=0, keepdims=True)
        ig = jnp.min(jnp.where(grp == mg, subg, float(N_GROUPS)), axis=0, keepdims=True)
        hit = subg == ig
        gsel = jnp.where(hit, 1.0, gsel)
        grp = jnp.where(hit, -jnp.inf, grp)
    emask = jnp.concatenate([jnp.broadcast_to(gsel[g:g + 1, :], (gsz, tm)) for g in range(N_GROUPS)], axis=0)
    cand = jnp.where(emask > 0.5, biased, -jnp.inf)
    sube = lax.broadcasted_iota(jnp.int32, (N_EXPERTS, tm), 0).astype(F32)
    idx_rows, w_rows = [], []
    chosen = jnp.zeros((N_EXPERTS, tm), F32)
    for _ in range(TOP_K):
        mc = jnp.max(cand, axis=0, keepdims=True)
        ic = jnp.min(jnp.where(cand == mc, sube, float(N_EXPERTS)), axis=0, keepdims=True)
        hit = sube == ic
        idx_rows.append(ic)
        w_rows.append(jnp.sum(jnp.where(hit, scores, 0.0), axis=0, keepdims=True))
        chosen = jnp.where(hit, 1.0, chosen)
        cand = jnp.where(hit, -jnp.inf, cand)
    cnt_ref[...] += jnp.sum(chosen, axis=1, keepdims=True)
    wsel = jnp.concatenate(w_rows, axis=0)
    wsel = wsel / jnp.sum(wsel, axis=0, keepdims=True) * ROUTE_SCALE
    idx_ref[...] = jnp.concatenate(idx_rows, axis=0).astype(jnp.int32)
    wt_ref[...] = wsel


def _router(x, router_w, router_b, tm=512):
    m, d = x.shape
    tm = min(tm, m)
    w_pad = jnp.pad(router_w, ((0, 0), (0, LANES - N_EXPERTS)))
    return pl.pallas_call(
        _router_kernel,
        grid=(m // tm,),
        in_specs=[pl.BlockSpec((tm, d), lambda i: (i, 0)),
                  pl.BlockSpec((d, LANES), lambda i: (0, 0)),
                  pl.BlockSpec((N_EXPERTS, 1), lambda i: (0, 0))],
        out_specs=[pl.BlockSpec((TOP_K, tm), lambda i: (0, i)), pl.BlockSpec((TOP_K, tm), lambda i: (0, i)),
                   pl.BlockSpec((N_EXPERTS, LANES), lambda i: (0, 0))],
        out_shape=[jax.ShapeDtypeStruct((TOP_K, m), jnp.int32), jax.ShapeDtypeStruct((TOP_K, m), F32),
                   jax.ShapeDtypeStruct((N_EXPERTS, LANES), F32)],
        compiler_params=_params("arbitrary"),
        name="router",
    )(x, w_pad, router_b.reshape(N_EXPERTS, 1))


def _slot_kernel(idx_ref, base_ref, dest_ref, tri_ref, run_ref):
    tm = idx_ref.shape[1]

    @pl.when(pl.program_id(0) == 0)
    def _():
        row = lax.broadcasted_iota(jnp.int32, (tm, tm), 0)
        col = lax.broadcasted_iota(jnp.int32, (tm, tm), 1)
        tri_ref[...] = jnp.where(row < col, 1.0, 0.0).astype(BF16)
        run_ref[...] = base_ref[...]

    sube = lax.broadcasted_iota(jnp.int32, (N_EXPERTS, tm), 0)
    idx = idx_ref[...]
    base = run_ref[...]
    hits = [sube == idx[k:k + 1, :] for k in range(TOP_K)]
    chosen = jnp.zeros((N_EXPERTS, tm), F32)
    for hit in hits:
        chosen = jnp.where(hit, 1.0, chosen)
    before = jnp.dot(chosen.astype(BF16), tri_ref[...], preferred_element_type=F32)
    slot = base + before
    rows = [jnp.sum(jnp.where(hit, slot, 0.0), axis=0, keepdims=True) for hit in hits]
    run_ref[...] = base + jnp.sum(chosen, axis=1, keepdims=True)
    dest_ref[...] = jnp.concatenate(rows, axis=0).astype(jnp.int32)


def _assign_slots(idx_t, base, tm=512):
    kk, m = idx_t.shape
    tm = min(tm, m)
    return pl.pallas_call(
        _slot_kernel,
        grid=(m // tm,),
        in_specs=[pl.BlockSpec((kk, tm), lambda i: (0, i)), pl.BlockSpec((N_EXPERTS, 1), lambda i: (0, 0))],
        out_specs=pl.BlockSpec((kk, tm), lambda i: (0, i)),
        out_shape=jax.ShapeDtypeStruct((kk, m), jnp.int32),
        scratch_shapes=[pltpu.VMEM((tm, tm), BF16), pltpu.VMEM((N_EXPERTS, 1), F32)],
        compiler_params=_params("arbitrary"),
        name="assign_slots",
    )(idx_t, base.astype(F32).reshape(N_EXPERTS, 1))


def _expert_kernel(be_ref, nb_ref, x_ref, wg_ref, wu_ref, wd_ref, *rest):
    o_ref = rest[-1]
    del be_ref
    i = pl.program_id(0)

    @pl.when(i < nb_ref[0])
    def _():
        x = x_ref[...]
        y = None
        for c in range(wg_ref.shape[1] // MXU_WIDTH):
            cs = slice(c * MXU_WIDTH, (c + 1) * MXU_WIDTH)
            gate = jnp.dot(x, wg_ref[:, cs].astype(BF16), preferred_element_type=F32)
            up = jnp.dot(x, wu_ref[:, cs].astype(BF16), preferred_element_type=F32)
            act = (jax.nn.silu(gate) * up).astype(BF16)
            part = jnp.dot(act, wd_ref[cs, :].astype(BF16), preferred_element_type=F32)
            y = part if y is None else y + part
        o_ref[...] = y.astype(o_ref.dtype)

    @pl.when(i >= nb_ref[0])
    def _():
        o_ref[...] = jnp.zeros_like(o_ref)


def _expert_ffn(xs, block_e, nblocks, w_gate, w_up, w_down, layer, rows, run_after=()):
    ns, d = xs.shape
    de = w_gate.shape[3]
    grid_spec = pltpu.PrefetchScalarGridSpec(
        num_scalar_prefetch=2,
        grid=(ns // rows,),
        in_specs=[pl.BlockSpec((rows, d), lambda i, be, nb: (i, 0)),
                  pl.BlockSpec((None, None, d, de), lambda i, be, nb: (layer, be[i], 0, 0)),
                  pl.BlockSpec((None, None, d, de), lambda i, be, nb: (layer, be[i], 0, 0)),
                  pl.BlockSpec((None, None, de, d), lambda i, be, nb: (layer, be[i], 0, 0))]
        + [pl.BlockSpec(memory_space=pl.ANY)] * len(run_after),
        out_specs=pl.BlockSpec((rows, d), lambda i, be, nb: (i, 0)),
    )
    return pl.pallas_call(
        _expert_kernel,
        grid_spec=grid_spec,
        out_shape=jax.ShapeDtypeStruct((ns, d), BF16),
        compiler_params=_params("arbitrary"),
        name="expert_ffn",
    )(block_e, nblocks, xs, w_gate, w_up, w_down, *run_after)


def _combine_ln_kernel(*refs, alpha):
    yg_refs = refs[:TOP_K]
    wt_ref, sh_ref, x_ref, g_ref, b_ref, xo_ref, xb_ref = refs[TOP_K:]
    wt = wt_ref[...]
    routed = sh_ref[...].astype(F32)
    for k in range(TOP_K):
        routed = routed + wt[:, k:k + 1] * yg_refs[k][...].astype(F32)
    xn = _layer_norm_rows(alpha * x_ref[...] + routed, g_ref[...], b_ref[...])
    xo_ref[...] = xn
    xb_ref[...] = xn.astype(BF16)


def _combine_ln(yg, wt, shared, x, g, b, alpha, tm=256):
    m, d = x.shape
    tm = min(tm, m)
    nb = m // tm
    row = pl.BlockSpec((tm, d), lambda i: (i, 0))
    vec = pl.BlockSpec((1, d), lambda i: (0, 0))
    yg_specs = [pl.BlockSpec((tm, d), functools.partial(lambda i, k: (k * nb + i, 0), k=k)) for k in range(TOP_K)]
    return pl.pallas_call(
        functools.partial(_combine_ln_kernel, alpha=alpha),
        grid=(nb,),
        in_specs=yg_specs + [pl.BlockSpec((tm, TOP_K), lambda i: (i, 0)), row, row, vec, vec],
        out_specs=[row, row],
        out_shape=[jax.ShapeDtypeStruct((m, d), F32), jax.ShapeDtypeStruct((m, d), BF16)],
        compiler_params=_params("arbitrary"),
        name="combine_ln",
    )(*([yg] * TOP_K), wt, shared, x, g.reshape(1, d), b.reshape(1, d))


def _moe_layer(x1, xb1, layer, router_w, router_b, e_gate, e_up, e_down, s_gate, s_up, s_down, g, b, alpha):
    t, d = x1.shape
    rows = min(MOE_ROWS, t)
    idx_t, wt_t, cnt = _router(x1, router_w, router_b)
    n_assign = t * TOP_K
    i32 = jnp.int32
    counts = cnt[:, 0].astype(i32)
    padded = (counts + rows - 1) // rows * rows
    ends_p = jnp.cumsum(padded)
    pad_e = padded - counts
    dest = _assign_slots(idx_t, ends_p - padded).reshape(n_assign)
    nblk = n_assign // rows + N_EXPERTS
    n_pad = nblk * rows - n_assign
    pad_i = jnp.arange(n_pad, dtype=i32)
    pad_before = jnp.cumsum(pad_e) - pad_e
    pad_slot = pad_i + jnp.sum(jnp.where(pad_before[None, :] <= pad_i[:, None], counts[None, :], 0), axis=1)
    keys = jnp.concatenate([dest, pad_slot])
    toks = jnp.concatenate([jnp.arange(n_assign, dtype=i32) % t, pad_i % t])
    _, slot_tok = lax.sort_key_val(keys, toks)
    blk0 = jnp.arange(nblk, dtype=i32) * rows
    block_e = jnp.minimum(jnp.sum((ends_p[None, :] <= blk0[:, None]).astype(i32), axis=1), N_EXPERTS - 1)
    nused = (ends_p[-1] // rows).astype(i32).reshape(1)
    xs = jnp.take(xb1, slot_tok, axis=0, mode='clip')
    srows = min(2 * MOE_ROWS, t)
    shared = _expert_ffn(xb1, jnp.zeros((t // srows,), i32), jnp.full((1,), t // srows, i32),
                         s_gate[:, None], s_up[:, None], s_down[:, None], layer, srows)
    ys = _expert_ffn(xs, block_e, nused, e_gate, e_up, e_down, layer, rows, run_after=(shared,))
    yg = jnp.take(ys, dest, axis=0, mode='clip')
    return _combine_ln(yg, wt_t.T, shared, x1, g, b, alpha)


def _mixer_layer(x, xb, memb, layer, tabs_a, tabs_c, batch, seq, w_in, w_c, pool_w, pool_scale, c_lambda, c_norm,
                 conv_w, w_mem_kv, merge_w, b_gate, w_out_b, g, b, alpha):
    mix = merge_w[1].shape[1]
    xw = merge_w[2].shape[0]
    a_cols, c_cols, d_cols = 3 * mix, 3 * mix, 3 * mix
    c0_b, c0_d = a_cols, a_cols + mix + c_cols
    pa = _proj(xb, w_in, layer, range(3), mix, tabs_a, seq, half=HEAD_DIM // 8,
               rope_blocks=(1,) * (mix // LANES), rope_jmax=2)
    tn_c = 12 * LANES
    pc = _proj(xb, w_c, 0, range(c_cols // tn_c), tn_c, tabs_c, seq,
               half=C_QK_DIM // 8, rope_blocks=(1, 2, 0) * 4, tm=512)
    blocks = list(range(c0_b // mix, c0_b // mix + 1)) + list(range(c0_d // mix, c0_d // mix + d_cols // mix))
    pbd = _proj(xb, w_in, layer, blocks, mix, tabs_a, seq)
    px = _proj(xb, w_in, layer, [(c0_d + d_cols) // xw], xw, tabs_a, seq)
    ya = _dilated_attention(pa, batch, seq)
    yb = _pooling_mixer(pbd, pool_w[layer], pool_scale[layer], batch, seq, 0)
    yc = _diff_attention(pc, c_lambda[layer], c_norm[layer], batch, seq, layer)
    yd = _gated_short_conv(pbd, conv_w[layer], batch, seq, mix)
    kv = _proj(memb, w_mem_kv, layer, range(2), xw, tabs_a, seq)
    yx = _cross_attention(px, kv, batch, seq, 0, xw)
    merged = _gated_merge(xb, (ya, yb, yc, yd, yx), *merge_w, b_gate[layer])
    return _out_proj_ln(merged, w_out_b, x, g[layer], b[layer], alpha)


def kernel(x, mem, w_in, pool_w, pool_scale, c_lambda, c_norm, conv_w, w_mem_kv, w_branch, w_branch_x,
           w_gate, b_gate, w_out, ln1_g, ln1_b, router_w, router_b, e_gate, e_up, e_down, s_gate, s_up,
           s_down, ln2_g, ln2_b):
    batch, seq, d = x.shape
    depth = w_in.shape[0]
    alpha = (2.0 * depth) ** 0.25
    tabs_a = _rope_tables(seq, HEAD_DIM // 4, LANES)
    tabs_ck = _rope_tables(seq, C_QK_DIM // 4, C_QK_DIM)
    q_scale = C_QK_DIM ** -0.5 * math.log2(math.e)
    tabs_c = tuple(tab * q_scale for tab in tabs_ck) + tabs_ck
    mix = w_branch.shape[2]
    groups = BATCH_GROUPS if batch % BATCH_GROUPS == 0 else 1
    gb = batch // groups
    state = []
    for gi in range(groups):
        xg = x[gi * gb:(gi + 1) * gb].reshape(gb * seq, d)
        state.append((xg, xg.astype(BF16), mem[gi * gb:(gi + 1) * gb].reshape(-1, d).astype(BF16)))
    for l in range(depth):
        w_c = w_in[l:l + 1, :, 4 * mix:7 * mix]
        merge_w = (w_gate[l].astype(BF16), w_branch[l].astype(BF16), w_branch_x[l].astype(BF16))
        w_out_b = w_out[l].astype(BF16)
        for gi in range(groups):
            xf, xb, memb = state[gi]
            xf, xb = _mixer_layer(xf, xb, memb, l, tabs_a, tabs_c, gb, seq, w_in, w_c, pool_w, pool_scale,
                                  c_lambda, c_norm, conv_w, w_mem_kv, merge_w, b_gate, w_out_b,
                                  ln1_g, ln1_b, alpha)
            xf, xb = _moe_layer(xf, xb, l, router_w[l], router_b[l], e_gate, e_up, e_down, s_gate,
                                s_up, s_down, ln2_g[l], ln2_b[l], alpha)
            state[gi] = (xf, xb, memb)
    return jnp.concatenate([s[0] for s in state], axis=0).reshape(batch, seq, d)
```

```python
import functools
import math

import jax
import jax.numpy as jnp
from jax import lax
from jax.experimental import pallas as pl
from jax.experimental.pallas import tpu as pltpu

F32 = jnp.float32
BF16 = jnp.bfloat16

LANES = 128
MXU_WIDTH = 256
HEAD_DIM = 128
ROPE_THETA = 500000.0
A_PATTERNS = ((128, 1), (512, 4), (2048, 16))
A_HALF = 64
B_WINDOWS = (2, 4, 8, 16)
C_QK_DIM = 64
N_EXPERTS = 64
N_GROUPS = 8
TOP_GROUPS = 4
TOP_K = 8
ROUTE_SCALE = 2.5
LN_EPS = 1e-5
NEG_INF = -1e30
VMEM_LIMIT_BYTES = 56 * 1024 * 1024
MOE_ROWS = 256
BATCH_GROUPS = 1


def _params(*sem):
    return pltpu.CompilerParams(dimension_semantics=sem, vmem_limit_bytes=VMEM_LIMIT_BYTES)


def _rope_tables(seq, rot_dim, period):
    half = rot_dim // 2
    inv = 1.0 / (ROPE_THETA ** (jnp.arange(0, rot_dim, 2, dtype=F32) / rot_dim))
    ang = jnp.arange(seq, dtype=F32)[:, None] * inv[None, :]
    cos, sin = jnp.cos(ang), jnp.sin(ang)
    lane = jnp.arange(LANES) % period
    first = lane < half
    second = (lane >= half) & (lane < 2 * half)
    idx = jnp.where(first, lane, jnp.where(second, lane - half, 0))
    cos_l, sin_l = cos[:, idx], sin[:, idx]
    ct = jnp.where(first | second, cos_l, 1.0)
    s1 = jnp.where(first, -sin_l, 0.0)
    s2 = jnp.where(second, sin_l, 0.0)
    return ct.astype(F32), s1.astype(F32), s2.astype(F32)


def _proj_kernel(x_ref, w_ref, *rest, half, rope_blocks, rope_jmax):
    tab_refs, o_ref, wb_ref = rest[:-2], rest[-2], rest[-1]
    j = pl.program_id(0)
    i = pl.program_id(1)

    @pl.when(i == 0)
    def _():
        wb_ref[...] = w_ref[...].astype(BF16)

    def plain():
        for c in range(o_ref.shape[1] // MXU_WIDTH):
            cs = slice(c * MXU_WIDTH, (c + 1) * MXU_WIDTH)
            o_ref[:, cs] = jnp.dot(x_ref[...], wb_ref[:, cs], preferred_element_type=F32).astype(o_ref.dtype)

    def roped():
        tabs = [r[...] for r in tab_refs]
        per = MXU_WIDTH // LANES
        for c in range(len(rope_blocks) // per):
            acc = jnp.dot(x_ref[...], wb_ref[:, c * MXU_WIDTH:(c + 1) * MXU_WIDTH], preferred_element_type=F32)
            for b in range(c * per, (c + 1) * per):
                use = rope_blocks[b]
                blk = acc[:, (b - c * per) * LANES:(b - c * per + 1) * LANES]
                if use:
                    ct, s1, s2 = tabs[3 * (use - 1):3 * use]
                    packed = pltpu.bitcast(blk.astype(BF16), jnp.uint32)
                    fwd = pltpu.bitcast(pltpu.roll(packed, LANES - half, 1), BF16).astype(F32)
                    bwd = pltpu.bitcast(pltpu.roll(packed, half, 1), BF16).astype(F32)
                    blk = blk * ct + fwd * s1 + bwd * s2
                o_ref[:, b * LANES:(b + 1) * LANES] = blk.astype(o_ref.dtype)

    if not any(rope_blocks):
        plain()
    elif rope_jmax is None:
        roped()
    else:
        pl.when(j < rope_jmax)(roped)
        pl.when(j >= rope_jmax)(plain)


def _proj(xb, w, layer, col_blocks, tn, tables, seq, *, half=0, rope_blocks=None, rope_jmax=None, tm=1024):
    m, k = xb.shape
    nj = len(col_blocks)
    tm = min(tm, m)
    rope_blocks = tuple(int(u) for u in rope_blocks) if rope_blocks is not None else (0,) * (tn // LANES)
    col_blocks = tuple(col_blocks)
    first, contiguous = col_blocks[0], all(col_blocks[a] == col_blocks[0] + a for a in range(nj))
    if contiguous:
        w_map = lambda j, i: (layer, 0, first + j)
    else:
        n0 = next(a for a in range(1, nj) if col_blocks[a] != col_blocks[0] + a)
        second = col_blocks[n0]
        w_map = lambda j, i: (layer, 0, jnp.where(j < n0, first + j, second + j - n0))
    sblocks = seq // tm if seq >= tm else 1
    tab_spec = pl.BlockSpec((tm, LANES), lambda j, i: (i % sblocks, 0))
    kern = functools.partial(_proj_kernel, half=half, rope_blocks=rope_blocks, rope_jmax=rope_jmax)
    return pl.pallas_call(
        kern,
        grid=(nj, m // tm),
        in_specs=[pl.BlockSpec((tm, k), lambda j, i: (i, 0)),
                  pl.BlockSpec((None, k, tn), w_map)] + [tab_spec] * len(tables),
        out_specs=pl.BlockSpec((tm, tn), lambda j, i: (i, j)),
        out_shape=jax.ShapeDtypeStruct((m, nj * tn), BF16),
        scratch_shapes=[pltpu.VMEM((k, tn), BF16)],
        compiler_params=_params("arbitrary", "arbitrary"),
        name="proj",
    )(xb, w, *tables)


def _dilated_kernel(q_ref, kp_ref, km_ref, kn_ref, vp_ref, vm_ref, vn_ref, o_ref, lse_ref, *, tl, seq_len, heads):
    t = pl.program_id(2)
    l0 = t * tl
    sub = 128
    nsub = tl // sub
    scale = HEAD_DIM ** -0.5
    iq = lax.broadcasted_iota(jnp.int32, (sub, sub + 2 * A_HALF), 0)
    ik = lax.broadcasted_iota(jnp.int32, (sub, sub + 2 * A_HALF), 1)
    band = jnp.abs(ik - A_HALF - iq) <= A_HALF
    lane = lax.broadcasted_iota(jnp.int32, (sub, LANES), 1)
    valid = []
    for a in range(nsub):
        kk = l0 + a * sub - A_HALF + ik
        valid.append(band & (kk >= 0) & (kk < seq_len))
    lse_rows = [jnp.zeros((sub, LANES), F32) for _ in range(nsub)]
    for h in range(heads):
        cs = slice(h * HEAD_DIM, (h + 1) * HEAD_DIM)
        kcat = jnp.concatenate([kp_ref[:, cs], km_ref[:, cs], kn_ref[:, cs]], axis=0)
        vcat = jnp.concatenate([vp_ref[:, cs], vm_ref[:, cs], vn_ref[:, cs]], axis=0)
        for a in range(nsub):
            q = q_ref[a * sub:(a + 1) * sub, cs]
            kw = kcat[a * sub:a * sub + sub + 2 * A_HALF]
            vw = vcat[a * sub:a * sub + sub + 2 * A_HALF]
            s = lax.dot_general(q, kw, (((1,), (1,)), ((), ())), preferred_element_type=F32) * scale
            s = jnp.where(valid[a], s, NEG_INF)
            m = jnp.max(s, axis=-1, keepdims=True)
            p = jnp.exp(s - m)
            den = jnp.sum(p, axis=-1, keepdims=True)
            o = jnp.dot(p.astype(BF16), vw, preferred_element_type=F32) / den
            o_ref[a * sub:(a + 1) * sub, cs] = o.astype(o_ref.dtype)
            lse_rows[a] = jnp.where(lane == h, m + jnp.log(den), lse_rows[a])
    for a in range(nsub):
        lse_ref[a * sub:(a + 1) * sub, :] = lse_rows[a]


def _dilated_pattern(pa, batch, seq, dil):
    width = pa.shape[1] // 3
    heads = width // HEAD_DIM
    sl = seq // dil
    tl = min(512, sl)
    nt = sl // tl
    hb = tl // A_HALF
    last_halo = sl // A_HALF - 1
    view = pa.reshape(batch, sl, dil * 3 * width)

    def main(c):
        return pl.BlockSpec((None, tl, width), lambda b, r, t: (b, t, r * 3 + c))

    def prev(c):
        return pl.BlockSpec((None, A_HALF, width), lambda b, r, t: (b, jnp.maximum(t * hb - 1, 0), r * 3 + c))

    def nxt(c):
        return pl.BlockSpec((None, A_HALF, width),
                            lambda b, r, t: (b, jnp.minimum((t + 1) * hb, last_halo), r * 3 + c))

    kern = functools.partial(_dilated_kernel, tl=tl, seq_len=sl, heads=heads)
    o, lse = pl.pallas_call(
        kern,
        grid=(batch, dil, nt),
        in_specs=[main(0), prev(1), main(1), nxt(1), prev(2), main(2), nxt(2)],
        out_specs=[pl.BlockSpec((None, tl, width), lambda b, r, t: (b, t, r)),
                   pl.BlockSpec((None, tl, LANES), lambda b, r, t: (b, t, r))],
        out_shape=[jax.ShapeDtypeStruct((batch, sl, dil * width), BF16),
                   jax.ShapeDtypeStruct((batch, sl, dil * LANES), F32)],
        compiler_params=_params("arbitrary", "arbitrary", "arbitrary"),
        name=f"dilated_d{dil}",
    )(view, view, view, view, view, view, view)
    return o.reshape(batch * seq, width), lse.reshape(batch * seq, LANES)


def _dilated_combine_kernel(o0_ref, o1_ref, o2_ref, l0_ref, l1_ref, l2_ref, y_ref, *, heads):
    l0, l1, l2 = l0_ref[...], l1_ref[...], l2_ref[...]
    m = jnp.maximum(jnp.maximum(l0, l1), l2)
    e0, e1, e2 = jnp.exp(l0 - m), jnp.exp(l1 - m), jnp.exp(l2 - m)
    inv = 1.0 / (e0 + e1 + e2)
    w0, w1, w2 = e0 * inv, e1 * inv, e2 * inv
    for h in range(heads):
        cs = slice(h * HEAD_DIM, (h + 1) * HEAD_DIM)
        y = (w0[:, h:h + 1] * o0_ref[:, cs].astype(F32) + w1[:, h:h + 1] * o1_ref[:, cs].astype(F32)
             + w2[:, h:h + 1] * o2_ref[:, cs].astype(F32))
        y_ref[:, cs] = y.astype(y_ref.dtype)


def _dilated_attention(pa, batch, seq):
    outs, lses = zip(*[_dilated_pattern(pa, batch, seq, d) for _, d in A_PATTERNS])
    m, width = outs[0].shape
    tm = min(512, m)
    ospec = pl.BlockSpec((tm, width), lambda i: (i, 0))
    lspec = pl.BlockSpec((tm, LANES), lambda i: (i, 0))
    return pl.pallas_call(
        functools.partial(_dilated_combine_kernel, heads=width // HEAD_DIM),
        grid=(m // tm,),
        in_specs=[ospec, ospec, ospec, lspec, lspec, lspec],
        out_specs=ospec,
        out_shape=jax.ShapeDtypeStruct((m, width), BF16),
        compiler_params=_params("arbitrary"),
        name="dilated_combine",
    )(*outs, *lses)


def _pool_kernel(u_ref, w_ref, sc_ref, o_ref, pad_ref, *, seq):
    g = pl.program_id(1)
    pad = 8
    uf = u_ref[...].astype(F32)
    pad_ref[0:pad, :] = jnp.zeros((pad, uf.shape[1]), F32)
    pad_ref[pad + seq:pad + seq + pad, :] = jnp.zeros((pad, uf.shape[1]), F32)
    pad_ref[pad:pad + seq, :] = uf
    pos = lax.broadcasted_iota(jnp.int32, (seq, 1), 0)
    for gi, win in enumerate(B_WINDOWS):
        @pl.when(g == gi)
        def _(win=win):
            before, after = win // 2, win - win // 2
            tot = pad_ref[pad - before:pad - before + seq, :]
            for off in range(-before + 1, after):
                tot = tot + pad_ref[pad + off:pad + off + seq, :]
            cnt = (jnp.minimum(pos + after, seq) - jnp.maximum(pos - before, 0)).astype(F32)
            pooled = tot / cnt - uf
            mixed = jnp.dot(pooled.astype(BF16), w_ref[...].astype(BF16), preferred_element_type=F32)
            o_ref[...] = (mixed * sc_ref[...]).astype(o_ref.dtype)


def _pooling_mixer(pbdx, pool_w, pool_scale, batch, seq, col0):
    ng, cg = pool_w.shape[0], pool_w.shape[1]
    view = pbdx.reshape(batch, seq, pbdx.shape[1])
    cb = col0 // cg
    return pl.pallas_call(
        functools.partial(_pool_kernel, seq=seq),
        grid=(batch, ng),
        in_specs=[pl.BlockSpec((None, seq, cg), lambda b, g: (b, 0, cb + g)),
                  pl.BlockSpec((None, cg, cg), lambda b, g: (g, 0, 0)),
                  pl.BlockSpec((1, cg), lambda b, g: (0, g))],
        out_specs=pl.BlockSpec((None, seq, cg), lambda b, g: (b, 0, g)),
        out_shape=jax.ShapeDtypeStruct((batch, seq, ng * cg), BF16),
        scratch_shapes=[pltpu.VMEM((seq + 16, cg), F32)],
        compiler_params=_params("arbitrary", "arbitrary"),
        name="pooling",
    )(view, pool_w, pool_scale.reshape(1, ng * cg)).reshape(batch * seq, ng * cg)


def _conv_kernel(bg_ref, cg_ref, h_ref, w_ref, o_ref, pad_ref, *, seq):
    pad = 8
    u = cg_ref[...].astype(F32) * h_ref[...].astype(F32)
    cols = u.shape[1]
    pad_ref[0:pad, :] = jnp.zeros((pad, cols), F32)
    pad_ref[pad + seq:pad + seq + pad, :] = jnp.zeros((pad, cols), F32)
    pad_ref[pad:pad + seq, :] = u
    w = w_ref[...]
    conv = (pad_ref[pad - 1:pad - 1 + seq, :] * w[0:1, :] + u * w[1:2, :]
            + pad_ref[pad + 1:pad + 1 + seq, :] * w[2:3, :])
    o_ref[...] = (bg_ref[...].astype(F32) * conv).astype(o_ref.dtype)


def _gated_short_conv(pbdx, conv_w, batch, seq, col0):
    width = conv_w.shape[1]
    cb = 256
    nb = width // cb
    view = pbdx.reshape(batch, seq, pbdx.shape[1])
    base = col0 // cb

    def spec(part):
        return pl.BlockSpec((None, seq, cb), lambda b, c: (b, 0, base + part * nb + c))

    return pl.pallas_call(
        functools.partial(_conv_kernel, seq=seq),
        grid=(batch, nb),
        in_specs=[spec(0), spec(1), spec(2), pl.BlockSpec((3, cb), lambda b, c: (0, c))],
        out_specs=pl.BlockSpec((None, seq, cb), lambda b, c: (b, 0, c)),
        out_shape=jax.ShapeDtypeStruct((batch, seq, width), BF16),
        scratch_shapes=[pltpu.VMEM((seq + 16, cb), F32)],
        compiler_params=_params("arbitrary", "arbitrary"),
        name="short_conv",
    )(view, view, view, conv_w).reshape(batch * seq, width)


def _diff_kernel(q_ref, k_ref, v_ref, lam_ref, nrm_ref, o_ref, vone_ref, *, lam_init, chunk):
    t = pl.program_id(2)
    seq = k_ref.shape[0]
    tq = q_ref.shape[0]

    @pl.when(t == 0)
    def _():
        vone_ref[:, :LANES] = v_ref[...]
        vone_ref[:, LANES:] = jnp.ones((seq, LANES), BF16)

    lq = lam_ref[...].astype(F32)
    lam = (jnp.exp(jnp.sum(lq[0:1] * lq[1:2], axis=-1, keepdims=True))
           - jnp.exp(jnp.sum(lq[2:3] * lq[3:4], axis=-1, keepdims=True)) + lam_init)
    q = q_ref[...]
    lane = lax.broadcasted_iota(jnp.int32, q.shape, 1)
    zero = jnp.zeros_like(q)
    q2 = jnp.concatenate([jnp.where(lane < C_QK_DIM, q, zero), jnp.where(lane >= C_QK_DIM, q, zero)], axis=0)
    m = acc = None
    for c in range(seq // chunk):
        ks = slice(c * chunk, (c + 1) * chunk)
        s = lax.dot_general(q2, k_ref[ks, :], (((1,), (1,)), ((), ())), preferred_element_type=F32)
        mc = jnp.max(s, axis=-1, keepdims=True)
        m_new = mc if m is None else jnp.maximum(m, mc)
        e = jnp.exp2((s - m_new).astype(BF16))
        pv = jnp.dot(e, vone_ref[ks, :], preferred_element_type=F32)
        acc = pv if acc is None else acc * jnp.exp2(m - m_new) + pv
        m = m_new
    out = acc[:, :LANES] / acc[:, LANES:]
    o = out[:tq] - lam * out[tq:]
    o = o * lax.rsqrt(jnp.mean(o * o, axis=-1, keepdims=True) + LN_EPS) * nrm_ref[...] * (1.0 - lam_init)
    o_ref[...] = o.astype(o_ref.dtype)


def _diff_attention(pc, c_lambda, c_norm, batch, seq, layer, tq=4096, chunk=256):
    heads = pc.shape[1] // (3 * LANES)
    tq = min(tq, seq)
    chunk = min(chunk, seq)
    lam_init = 0.8 - 0.6 * math.exp(-0.3 * layer)
    view = pc.reshape(batch, seq, pc.shape[1])
    return pl.pallas_call(
        functools.partial(_diff_kernel, lam_init=lam_init, chunk=chunk),
        grid=(batch, heads, seq // tq),
        in_specs=[pl.BlockSpec((None, tq, LANES), lambda b, h, t: (b, t, 3 * h)),
                  pl.BlockSpec((None, seq, LANES), lambda b, h, t: (b, 0, 3 * h + 1)),
                  pl.BlockSpec((None, seq, LANES), lambda b, h, t: (b, 0, 3 * h + 2)),
                  pl.BlockSpec((4, C_QK_DIM), lambda b, h, t: (0, 0)),
                  pl.BlockSpec((1, LANES), lambda b, h, t: (0, 0))],
        out_specs=pl.BlockSpec((None, tq, LANES), lambda b, h, t: (b, t, h)),
        out_shape=jax.ShapeDtypeStruct((batch, seq, heads * LANES), BF16),
        scratch_shapes=[pltpu.VMEM((seq, 2 * LANES), BF16)],
        compiler_params=_params("arbitrary", "arbitrary", "arbitrary"),
        name="diff_attention",
    )(view, view, view, c_lambda, c_norm.reshape(1, LANES)).reshape(batch * seq, heads * LANES)


def _cross_kernel(q_ref, kv_ref, o_ref, *, heads):
    scale = HEAD_DIM ** -0.5
    width = heads * HEAD_DIM
    for h in range(heads):
        cs = slice(h * HEAD_DIM, (h + 1) * HEAD_DIM)
        k = kv_ref[:, h * HEAD_DIM:(h + 1) * HEAD_DIM]
        v = kv_ref[:, width + h * HEAD_DIM:width + (h + 1) * HEAD_DIM]
        s = lax.dot_general(q_ref[:, cs], k, (((1,), (1,)), ((), ())), preferred_element_type=F32) * scale
        m = jnp.max(s, axis=-1, keepdims=True)
        e = jnp.exp(s - m)
        den = jnp.sum(e, axis=-1, keepdims=True)
        o = jnp.dot(e.astype(BF16), v, preferred_element_type=F32) / den
        o_ref[:, cs] = o.astype(o_ref.dtype)


def _cross_attention(pbdx, kv, batch, seq, col0, width, tq=512):
    mem_len = kv.shape[0] // batch
    tq = min(tq, seq)
    view = pbdx.reshape(batch, seq, pbdx.shape[1])
    return pl.pallas_call(
        functools.partial(_cross_kernel, heads=width // HEAD_DIM),
        grid=(batch, seq // tq),
        in_specs=[pl.BlockSpec((None, tq, width), lambda b, t: (b, t, col0 // width)),
                  pl.BlockSpec((None, mem_len, 2 * width), lambda b, t: (b, 0, 0))],
        out_specs=pl.BlockSpec((None, tq, width), lambda b, t: (b, t, 0)),
        out_shape=jax.ShapeDtypeStruct((batch, seq, width), BF16),
        compiler_params=_params("arbitrary", "arbitrary"),
        name="cross_attention",
    )(view, kv.reshape(batch, mem_len, 2 * width)).reshape(batch * seq, width)


def _merge_kernel(x_ref, ya_ref, yb_ref, yc_ref, yd_ref, yx_ref,
                  g0, g1, g2, g3, g4, p0, p1, p2, p3, p4, b0, b1, b2, b3, b4, o_ref):
    x = x_ref[...]
    merged = None
    for y_ref, g_ref, p_ref, b_ref in ((ya_ref, g0, p0, b0), (yb_ref, g1, p1, b1), (yc_ref, g2, p2, b2),
                                       (yd_ref, g3, p3, b3), (yx_ref, g4, p4, b4)):
        gate = jax.nn.sigmoid(jnp.dot(x, g_ref[...], preferred_element_type=F32) + b_ref[...])
        term = gate * jnp.dot(y_ref[...], p_ref[...], preferred_element_type=F32)
        merged = term if merged is None else merged + term
    o_ref[...] = merged.astype(o_ref.dtype)


def _gated_merge(xb, ys, w_gate, w_branch, w_branch_x, b_gate, tm=512, tn=512):
    m, d = xb.shape
    tm = min(tm, m)
    nb = d // tn
    n_br = 5
    act = [pl.BlockSpec((tm, d), lambda j, i: (i, 0))]
    act += [pl.BlockSpec((tm, y.shape[1]), lambda j, i: (i, 0)) for y in ys]
    gates = [pl.BlockSpec((d, tn), functools.partial(lambda j, i, br: (0, br * nb + j), br=br)) for br in range(n_br)]
    projs = [pl.BlockSpec((None, w_branch.shape[1], tn), functools.partial(lambda j, i, br: (br, 0, j), br=br))
             for br in range(4)]
    projs.append(pl.BlockSpec((w_branch_x.shape[0], tn), lambda j, i: (0, j)))
    biases = [pl.BlockSpec((1, tn), functools.partial(lambda j, i, br: (0, br * nb + j), br=br)) for br in range(n_br)]
    bg2 = b_gate.reshape(1, n_br * d)
    return pl.pallas_call(
        _merge_kernel,
        grid=(nb, m // tm),
        in_specs=act + gates + projs + biases,
        out_specs=pl.BlockSpec((tm, tn), lambda j, i: (i, j)),
        out_shape=jax.ShapeDtypeStruct((m, d), BF16),
        compiler_params=_params("arbitrary", "arbitrary"),
        name="gated_merge",
    )(xb, *ys, *([w_gate] * n_br), *([w_branch] * 4), w_branch_x, *([bg2] * n_br))


def _layer_norm_rows(h, g, b):
    mu = jnp.mean(h, axis=-1, keepdims=True)
    hc = h - mu
    var = jnp.mean(hc * hc, axis=-1, keepdims=True)
    return hc * lax.rsqrt(var + LN_EPS) * g + b


def _out_ln_kernel(mg_ref, w_ref, x_ref, g_ref, b_ref, xo_ref, xb_ref, *, alpha):
    y = jnp.dot(mg_ref[...], w_ref[...], preferred_element_type=F32)
    xn = _layer_norm_rows(alpha * x_ref[...] + y, g_ref[...], b_ref[...])
    xo_ref[...] = xn
    xb_ref[...] = xn.astype(BF16)


def _out_proj_ln(merged, w_out_b, x, g, b, alpha, tm=512):
    m, d = x.shape
    tm = min(tm, m)
    row = pl.BlockSpec((tm, d), lambda i: (i, 0))
    vec = pl.BlockSpec((1, d), lambda i: (0, 0))
    return pl.pallas_call(
        functools.partial(_out_ln_kernel, alpha=alpha),
        grid=(m // tm,),
        in_specs=[row, pl.BlockSpec((d, d), lambda i: (0, 0)), row, vec, vec],
        out_specs=[row, row],
        out_shape=[jax.ShapeDtypeStruct((m, d), F32), jax.ShapeDtypeStruct((m, d), BF16)],
        compiler_params=_params("arbitrary"),
        name="out_proj_ln",
    )(merged, w_out_b, x, g.reshape(1, d), b.reshape(1, d))


def _split_bf16(a):
    hi = a.astype(BF16)
    lo = (a - hi.astype(F32)).astype(BF16)
    return hi, lo


def _router_kernel(x_ref, w_ref, b_ref, idx_ref, wt_ref, cnt_ref):
    @pl.when(pl.program_id(0) == 0)
    def _():
        cnt_ref[...] = jnp.zeros_like(cnt_ref)

    x = x_ref[...]
    w = w_ref[...]
    xh, xl = _split_bf16(x)
    wh, wl = _split_bf16(w)
    logits = (jnp.dot(xh, wh, preferred_element_type=F32) + jnp.dot(xl, wh, preferred_element_type=F32)
              + jnp.dot(xh, wl, preferred_element_type=F32))
    lt = logits.T[:N_EXPERTS, :]
    tm = lt.shape[1]
    scores = jax.nn.sigmoid(lt)
    biased = scores + b_ref[...]
    gsz = N_EXPERTS // N_GROUPS
    sub8 = lax.broadcasted_iota(jnp.int32, (gsz, tm), 0).astype(F32)
    grp_rows = []
    for g in range(N_GROUPS):
        blk = biased[g * gsz:(g + 1) * gsz, :]
        m1 = jnp.max(blk, axis=0, keepdims=True)
        i1 = jnp.min(jnp.where(blk == m1, sub8, float(gsz)), axis=0, keepdims=True)
        m2 = jnp.max(jnp.where(sub8 == i1, -jnp.inf, blk), axis=0, keepdims=True)
        grp_rows.append(m1 + m2)
    grp = jnp.concatenate(grp_rows, axis=0)
    subg = lax.broadcasted_iota(jnp.int32, (N_GROUPS, tm), 0).astype(F32)
    gsel = jnp.zeros((N_GROUPS, tm), F32)
    for _ in range(TOP_GROUPS):
        mg = jnp.max(grp, axis=0, keepdims=True)
        ig = jnp.min(jnp.where(grp == mg, subg, float(N_GROUPS)), axis=0, keepdims=True)
        hit = subg == ig
        gsel = jnp.where(hit, 1.0, gsel)
        grp = jnp.where(hit, -jnp.inf, grp)
    emask = jnp.concatenate([jnp.broadcast_to(gsel[g:g + 1, :], (gsz, tm)) for g in range(N_GROUPS)], axis=0)
    cand = jnp.where(emask > 0.5, biased, -jnp.inf)
    sube = lax.broadcasted_iota(jnp.int32, (N_EXPERTS, tm), 0).astype(F32)
    idx_rows, w_rows = [], []
    chosen = jnp.zeros((N_EXPERTS, tm), F32)
    for _ in range(TOP_K):
        mc = jnp.max(cand, axis=0, keepdims=True)
        ic = jnp.min(jnp.where(cand == mc, sube, float(N_EXPERTS)), axis=0, keepdims=True)
        hit = sube == ic
        idx_rows.append(ic)
        w_rows.append(jnp.sum(jnp.where(hit, scores, 0.0), axis=0, keepdims=True))
        chosen = jnp.where(hit, 1.0, chosen)
        cand = jnp.where(hit, -jnp.inf, cand)
    cnt_ref[...] += jnp.sum(chosen, axis=1, keepdims=True)
    wsel = jnp.concatenate(w_rows, axis=0)
    wsel = wsel / jnp.sum(wsel, axis=0, keepdims=True) * ROUTE_SCALE
    idx_ref[...] = jnp.concatenate(idx_rows, axis=0).astype(jnp.int32)
    wt_ref[...] = wsel


def _router(x, router_w, router_b, tm=512):
    m, d = x.shape
    tm = min(tm, m)
    w_pad = jnp.pad(router_w, ((0, 0), (0, LANES - N_EXPERTS)))
    return pl.pallas_call(
        _router_kernel,
        grid=(m // tm,),
        in_specs=[pl.BlockSpec((tm, d), lambda i: (i, 0)),
                  pl.BlockSpec((d, LANES), lambda i: (0, 0)),
                  pl.BlockSpec((N_EXPERTS, 1), lambda i: (0, 0))],
        out_specs=[pl.BlockSpec((TOP_K, tm), lambda i: (0, i)), pl.BlockSpec((TOP_K, tm), lambda i: (0, i)),
                   pl.BlockSpec((N_EXPERTS, LANES), lambda i: (0, 0))],
        out_shape=[jax.ShapeDtypeStruct((TOP_K, m), jnp.int32), jax.ShapeDtypeStruct((TOP_K, m), F32),
                   jax.ShapeDtypeStruct((N_EXPERTS, LANES), F32)],
        compiler_params=_params("arbitrary"),
        name="router",
    )(x, w_pad, router_b.reshape(N_EXPERTS, 1))


def _slot_kernel(idx_ref, base_ref, dest_ref, tri_ref, run_ref):
    tm = idx_ref.shape[1]

    @pl.when(pl.program_id(0) == 0)
    def _():
        row = lax.broadcasted_iota(jnp.int32, (tm, tm), 0)
        col = lax.broadcasted_iota(jnp.int32, (tm, tm), 1)
        tri_ref[...] = jnp.where(row < col, 1.0, 0.0).astype(BF16)
        run_ref[...] = base_ref[...]

    sube = lax.broadcasted_iota(jnp.int32, (N_EXPERTS, tm), 0)
    idx = idx_ref[...]
    base = run_ref[...]
    hits = [sube == idx[k:k + 1, :] for k in range(TOP_K)]
    chosen = jnp.zeros((N_EXPERTS, tm), F32)
    for hit in hits:
        chosen = jnp.where(hit, 1.0, chosen)
    before = jnp.dot(chosen.astype(BF16), tri_ref[...], preferred_element_type=F32)
    slot = base + before
    rows = [jnp.sum(jnp.where(hit, slot, 0.0), axis=0, keepdims=True) for hit in hits]
    run_ref[...] = base + jnp.sum(chosen, axis=1, keepdims=True)
    dest_ref[...] = jnp.concatenate(rows, axis=0).astype(jnp.int32)


def _assign_slots(idx_t, base, tm=512):
    kk, m = idx_t.shape
    tm = min(tm, m)
    return pl.pallas_call(
        _slot_kernel,
        grid=(m // tm,),
        in_specs=[pl.BlockSpec((kk, tm), lambda i: (0, i)), pl.BlockSpec((N_EXPERTS, 1), lambda i: (0, 0))],
        out_specs=pl.BlockSpec((kk, tm), lambda i: (0, i)),
        out_shape=jax.ShapeDtypeStruct((kk, m), jnp.int32),
        scratch_shapes=[pltpu.VMEM((tm, tm), BF16), pltpu.VMEM((N_EXPERTS, 1), F32)],
        compiler_params=_params("arbitrary"),
        name="assign_slots",
    )(idx_t, base.astype(F32).reshape(N_EXPERTS, 1))


def _expert_kernel(be_ref, nb_ref, x_ref, wg_ref, wu_ref, wd_ref, *rest):
    o_ref = rest[-1]
    del be_ref
    i = pl.program_id(0)

    @pl.when(i < nb_ref[0])
    def _():
        x = x_ref[...]
        y = None
        for c in range(wg_ref.shape[1] // MXU_WIDTH):
            cs = slice(c * MXU_WIDTH, (c + 1) * MXU_WIDTH)
            gate = jnp.dot(x, wg_ref[:, cs].astype(BF16), preferred_element_type=F32)
            up = jnp.dot(x, wu_ref[:, cs].astype(BF16), preferred_element_type=F32)
            act = (jax.nn.silu(gate) * up).astype(BF16)
            part = jnp.dot(act, wd_ref[cs, :].astype(BF16), preferred_element_type=F32)
            y = part if y is None else y + part
        o_ref[...] = y.astype(o_ref.dtype)

    @pl.when(i >= nb_ref[0])
    def _():
        o_ref[...] = jnp.zeros_like(o_ref)


def _expert_ffn(xs, block_e, nblocks, w_gate, w_up, w_down, layer, rows, run_after=()):
    ns, d = xs.shape
    de = w_gate.shape[3]
    grid_spec = pltpu.PrefetchScalarGridSpec(
        num_scalar_prefetch=2,
        grid=(ns // rows,),
        in_specs=[pl.BlockSpec((rows, d), lambda i, be, nb: (i, 0)),
                  pl.BlockSpec((None, None, d, de), lambda i, be, nb: (layer, be[i], 0, 0)),
                  pl.BlockSpec((None, None, d, de), lambda i, be, nb: (layer, be[i], 0, 0)),
                  pl.BlockSpec((None, None, de, d), lambda i, be, nb: (layer, be[i], 0, 0))]
        + [pl.BlockSpec(memory_space=pl.ANY)] * len(run_after),
        out_specs=pl.BlockSpec((rows, d), lambda i, be, nb: (i, 0)),
    )
    return pl.pallas_call(
        _expert_kernel,
        grid_spec=grid_spec,
        out_shape=jax.ShapeDtypeStruct((ns, d), BF16),
        compiler_params=_params("arbitrary"),
        name="expert_ffn",
    )(block_e, nblocks, xs, w_gate, w_up, w_down, *run_after)


def _combine_ln_kernel(*refs, alpha):
    yg_refs = refs[:TOP_K]
    wt_ref, sh_ref, x_ref, g_ref, b_ref, xo_ref, xb_ref = refs[TOP_K:]
    wt = wt_ref[...]
    routed = sh_ref[...].astype(F32)
    for k in range(TOP_K):
        routed = routed + wt[:, k:k + 1] * yg_refs[k][...].astype(F32)
    xn = _layer_norm_rows(alpha * x_ref[...] + routed, g_ref[...], b_ref[...])
    xo_ref[...] = xn
    xb_ref[...] = xn.astype(BF16)


def _combine_ln(yg, wt, shared, x, g, b, alpha, tm=256):
    m, d = x.shape
    tm = min(tm, m)
    nb = m // tm
    row = pl.BlockSpec((tm, d), lambda i: (i, 0))
    vec = pl.BlockSpec((1, d), lambda i: (0, 0))
    yg_specs = [pl.BlockSpec((tm, d), functools.partial(lambda i, k: (k * nb + i, 0), k=k)) for k in range(TOP_K)]
    return pl.pallas_call(
        functools.partial(_combine_ln_kernel, alpha=alpha),
        grid=(nb,),
        in_specs=yg_specs + [pl.BlockSpec((tm, TOP_K), lambda i: (i, 0)), row, row, vec, vec],
        out_specs=[row, row],
        out_shape=[jax.ShapeDtypeStruct((m, d), F32), jax.ShapeDtypeStruct((m, d), BF16)],
        compiler_params=_params("arbitrary"),
        name="combine_ln",
    )(*([yg] * TOP_K), wt, shared, x, g.reshape(1, d), b.reshape(1, d))


def _moe_layer(x1, xb1, layer, router_w, router_b, e_gate, e_up, e_down, s_gate, s_up, s_down, g, b, alpha):
    t, d = x1.shape
    rows = min(MOE_ROWS, t)
    idx_t, wt_t, cnt = _router(x1, router_w, router_b)
    n_assign = t * TOP_K
    i32 = jnp.int32
    counts = cnt[:, 0].astype(i32)
    padded = (counts + rows - 1) // rows * rows
    ends_p = jnp.cumsum(padded)
    pad_e = padded - counts
    dest = _assign_slots(idx_t, ends_p - padded).reshape(n_assign)
    nblk = n_assign // rows + N_EXPERTS
    n_pad = nblk * rows - n_assign
    pad_i = jnp.arange(n_pad, dtype=i32)
    pad_before = jnp.cumsum(pad_e) - pad_e
    pad_slot = pad_i + jnp.sum(jnp.where(pad_before[None, :] <= pad_i[:, None], counts[None, :], 0), axis=1)
    keys = jnp.concatenate([dest, pad_slot])
    toks = jnp.concatenate([jnp.arange(n_assign, dtype=i32) % t, pad_i % t])
    _, slot_tok = lax.sort_key_val(keys, toks)
    blk0 = jnp.arange(nblk, dtype=i32) * rows
    block_e = jnp.minimum(jnp.sum((ends_p[None, :] <= blk0[:, None]).astype(i32), axis=1), N_EXPERTS - 1)
    nused = (ends_p[-1] // rows).astype(i32).reshape(1)
    xs = jnp.take(xb1, slot_tok, axis=0, mode='clip')
    srows = min(2 * MOE_ROWS, t)
    shared = _expert_ffn(xb1, jnp.zeros((t // srows,), i32), jnp.full((1,), t // srows, i32),
                         s_gate[:, None], s_up[:, None], s_down[:, None], layer, srows)
    ys = _expert_ffn(xs, block_e, nused, e_gate, e_up, e_down, layer, rows, run_after=(shared,))
    yg = jnp.take(ys, dest, axis=0, mode='clip')
    return _combine_ln(yg, wt_t.T, shared, x1, g, b, alpha)


def _mixer_layer(x, xb, memb, layer, tabs_a, tabs_c, batch, seq, w_in, w_c, pool_w, pool_scale, c_lambda, c_norm,
                 conv_w, w_mem_kv, merge_w, b_gate, w_out_b, g, b, alpha):
    mix = merge_w[1].shape[1]
    xw = merge_w[2].shape[0]
    a_cols, c_cols, d_cols = 3 * mix, 3 * mix, 3 * mix
    c0_b, c0_d = a_cols, a_cols + mix + c_cols
    pa = _proj(xb, w_in, layer, range(3), mix, tabs_a, seq, half=HEAD_DIM // 8,
               rope_blocks=(1,) * (mix // LANES), rope_jmax=2)
    tn_c = 12 * LANES
    pc = _proj(xb, w_c, 0, range(c_cols // tn_c), tn_c, tabs_c, seq,
               half=C_QK_DIM // 8, rope_blocks=(1, 2, 0) * 4, tm=512)
    blocks = list(range(c0_b // mix, c0_b // mix + 1)) + list(range(c0_d // mix, c0_d // mix + d_cols // mix))
    pbd = _proj(xb, w_in, layer, blocks, mix, tabs_a, seq)
    px = _proj(xb, w_in, layer, [(c0_d + d_cols) // xw], xw, tabs_a, seq)
    ya = _dilated_attention(pa, batch, seq)
    yb = _pooling_mixer(pbd, pool_w[layer], pool_scale[layer], batch, seq, 0)
    yc = _diff_attention(pc, c_lambda[layer], c_norm[layer], batch, seq, layer)
    yd = _gated_short_conv(pbd, conv_w[layer], batch, seq, mix)
    kv = _proj(memb, w_mem_kv, layer, range(2), xw, tabs_a, seq)
    yx = _cross_attention(px, kv, batch, seq, 0, xw)
    merged = _gated_merge(xb, (ya, yb, yc, yd, yx), *merge_w, b_gate[layer])
    return _out_proj_ln(merged, w_out_b, x, g[layer], b[layer], alpha)


def kernel(x, mem, w_in, pool_w, pool_scale, c_lambda, c_norm, conv_w, w_mem_kv, w_branch, w_branch_x,
           w_gate, b_gate, w_out, ln1_g, ln1_b, router_w, router_b, e_gate, e_up, e_down, s_gate, s_up,
           s_down, ln2_g, ln2_b):
    batch, seq, d = x.shape
    depth = w_in.shape[0]
    alpha = (2.0 * depth) ** 0.25
    tabs_a = _rope_tables(seq, HEAD_DIM // 4, LANES)
    tabs_ck = _rope_tables(seq, C_QK_DIM // 4, C_QK_DIM)
    q_scale = C_QK_DIM ** -0.5 * math.log2(math.e)
    tabs_c = tuple(tab * q_scale for tab in tabs_ck) + tabs_ck
    mix = w_branch.shape[2]
    groups = BATCH_GROUPS if batch % BATCH_GROUPS == 0 else 1
    gb = batch // groups
    state = []
    for gi in range(groups):
        xg = x[gi * gb:(gi + 1) * gb].reshape(gb * seq, d)
        state.append((xg, xg.astype(BF16), mem[gi * gb:(gi + 1) * gb].reshape(-1, d).astype(BF16)))
    for l in range(depth):
        w_c = w_in[l:l + 1, :, 4 * mix:7 * mix]
        merge_w = (w_gate[l].astype(BF16), w_branch[l].astype(BF16), w_branch_x[l].astype(BF16))
        w_out_b = w_out[l].astype(BF16)
        for gi in range(groups):
            xf, xb, memb = state[gi]
            xf, xb = _mixer_layer(xf, xb, memb, l, tabs_a, tabs_c, gb, seq, w_in, w_c, pool_w, pool_scale,
                                  c_lambda, c_norm, conv_w, w_mem_kv, merge_w, b_gate, w_out_b,
                                  ln1_g, ln1_b, alpha)
            xf, xb = _moe_layer(xf, xb, l, router_w[l], router_b[l], e_gate, e_up, e_down, s_gate,
                                s_up, s_down, ln2_g[l], ln2_b[l], alpha)
            state[gi] = (xf, xb, memb)
    return jnp.concatenate([s[0] for s in state], axis=0).reshape(batch, seq, d)
```

```python
import functools
import math

import jax
import jax.numpy as jnp
from jax import lax
from jax.experimental import pallas as pl
from jax.experimental.pallas import tpu as pltpu

F32 = jnp.float32
BF16 = jnp.bfloat16

LANES = 128
MXU_WIDTH = 256
HEAD_DIM = 128
ROPE_THETA = 500000.0
A_PATTERNS = ((128, 1), (512, 4), (2048, 16))
A_HALF = 64
B_WINDOWS = (2, 4, 8, 16)
C_QK_DIM = 64
N_EXPERTS = 64
N_GROUPS = 8
TOP_GROUPS = 4
TOP_K = 8
ROUTE_SCALE = 2.5
LN_EPS = 1e-5
NEG_INF = -1e30
VMEM_LIMIT_BYTES = 56 * 1024 * 1024
MOE_ROWS = 256
BATCH_GROUPS = 1


def _params(*sem):
    return pltpu.CompilerParams(dimension_semantics=sem, vmem_limit_bytes=VMEM_LIMIT_BYTES)


def _rope_tables(seq, rot_dim, period):
    half = rot_dim // 2
    inv = 1.0 / (ROPE_THETA ** (jnp.arange(0, rot_dim, 2, dtype=F32) / rot_dim))
    ang = jnp.arange(seq, dtype=F32)[:, None] * inv[None, :]
    cos, sin = jnp.cos(ang), jnp.sin(ang)
    lane = jnp.arange(LANES) % period
    first = lane < half
    second = (lane >= half) & (lane < 2 * half)
    idx = jnp.where(first, lane, jnp.where(second, lane - half, 0))
    cos_l, sin_l = cos[:, idx], sin[:, idx]
    ct = jnp.where(first | second, cos_l, 1.0)
    s1 = jnp.where(first, -sin_l, 0.0)
    s2 = jnp.where(second, sin_l, 0.0)
    return ct.astype(F32), s1.astype(F32), s2.astype(F32)


def _proj_kernel(x_ref, w_ref, *rest, half, rope_blocks, rope_jmax):
    tab_refs, o_ref, wb_ref = rest[:-2], rest[-2], rest[-1]
    j = pl.program_id(0)
    i = pl.program_id(1)

    @pl.when(i == 0)
    def _():
        wb_ref[...] = w_ref[...].astype(BF16)

    def plain():
        for c in range(o_ref.shape[1] // MXU_WIDTH):
            cs = slice(c * MXU_WIDTH, (c + 1) * MXU_WIDTH)
            o_ref[:, cs] = jnp.dot(x_ref[...], wb_ref[:, cs], preferred_element_type=F32).astype(o_ref.dtype)

    def roped():
        tabs = [r[...] for r in tab_refs]
        per = MXU_WIDTH // LANES
        for c in range(len(rope_blocks) // per):
            acc = jnp.dot(x_ref[...], wb_ref[:, c * MXU_WIDTH:(c + 1) * MXU_WIDTH], preferred_element_type=F32)
            for b in range(c * per, (c + 1) * per):
                use = rope_blocks[b]
                blk = acc[:, (b - c * per) * LANES:(b - c * per + 1) * LANES]
                if use:
                    ct, s1, s2 = tabs[3 * (use - 1):3 * use]
                    packed = pltpu.bitcast(blk.astype(BF16), jnp.uint32)
                    fwd = pltpu.bitcast(pltpu.roll(packed, LANES - half, 1), BF16).astype(F32)
                    bwd = pltpu.bitcast(pltpu.roll(packed, half, 1), BF16).astype(F32)
                    blk = blk * ct + fwd * s1 + bwd * s2
                o_ref[:, b * LANES:(b + 1) * LANES] = blk.astype(o_ref.dtype)

    if not any(rope_blocks):
        plain()
    elif rope_jmax is None:
        roped()
    else:
        pl.when(j < rope_jmax)(roped)
        pl.when(j >= rope_jmax)(plain)


def _proj(xb, w, layer, col_blocks, tn, tables, seq, *, half=0, rope_blocks=None, rope_jmax=None, tm=1024):
    m, k = xb.shape
    nj = len(col_blocks)
    tm = min(tm, m)
    rope_blocks = tuple(int(u) for u in rope_blocks) if rope_blocks is not None else (0,) * (tn // LANES)
    col_blocks = tuple(col_blocks)
    first, contiguous = col_blocks[0], all(col_blocks[a] == col_blocks[0] + a for a in range(nj))
    if contiguous:
        w_map = lambda j, i: (layer, 0, first + j)
    else:
        n0 = next(a for a in range(1, nj) if col_blocks[a] != col_blocks[0] + a)
        second = col_blocks[n0]
        w_map = lambda j, i: (layer, 0, jnp.where(j < n0, first + j, second + j - n0))
    sblocks = seq // tm if seq >= tm else 1
    tab_spec = pl.BlockSpec((tm, LANES), lambda j, i: (i % sblocks, 0))
    kern = functools.partial(_proj_kernel, half=half, rope_blocks=rope_blocks, rope_jmax=rope_jmax)
    return pl.pallas_call(
        kern,
        grid=(nj, m // tm),
        in_specs=[pl.BlockSpec((tm, k), lambda j, i: (i, 0)),
                  pl.BlockSpec((None, k, tn), w_map)] + [tab_spec] * len(tables),
        out_specs=pl.BlockSpec((tm, tn), lambda j, i: (i, j)),
        out_shape=jax.ShapeDtypeStruct((m, nj * tn), BF16),
        scratch_shapes=[pltpu.VMEM((k, tn), BF16)],
        compiler_params=_params("arbitrary", "arbitrary"),
        name="proj",
    )(xb, w, *tables)


def _dilated_kernel(q_ref, kp_ref, km_ref, kn_ref, vp_ref, vm_ref, vn_ref, o_ref, lse_ref, *, tl, seq_len, heads):
    t = pl.program_id(2)
    l0 = t * tl
    sub = 128
    nsub = tl // sub
    scale = HEAD_DIM ** -0.5
    iq = lax.broadcasted_iota(jnp.int32, (sub, sub + 2 * A_HALF), 0)
    ik = lax.broadcasted_iota(jnp.int32, (sub, sub + 2 * A_HALF), 1)
    band = jnp.abs(ik - A_HALF - iq) <= A_HALF
    lane = lax.broadcasted_iota(jnp.int32, (sub, LANES), 1)
    valid = []
    for a in range(nsub):
        kk = l0 + a * sub - A_HALF + ik
        valid.append(band & (kk >= 0) & (kk < seq_len))
    lse_rows = [jnp.zeros((sub, LANES), F32) for _ in range(nsub)]
    for h in range(heads):
        cs = slice(h * HEAD_DIM, (h + 1) * HEAD_DIM)
        kcat = jnp.concatenate([kp_ref[:, cs], km_ref[:, cs], kn_ref[:, cs]], axis=0)
        vcat = jnp.concatenate([vp_ref[:, cs], vm_ref[:, cs], vn_ref[:, cs]], axis=0)
        for a in range(nsub):
            q = q_ref[a * sub:(a + 1) * sub, cs]
            kw = kcat[a * sub:a * sub + sub + 2 * A_HALF]
            vw = vcat[a * sub:a * sub + sub + 2 * A_HALF]
            s = lax.dot_general(q, kw, (((1,), (1,)), ((), ())), preferred_element_type=F32) * scale
            s = jnp.where(valid[a], s, NEG_INF)
            m = jnp.max(s, axis=-1, keepdims=True)
            p = jnp.exp(s - m)
            den = jnp.sum(p, axis=-1, keepdims=True)
            o = jnp.dot(p.astype(BF16), vw, preferred_element_type=F32) / den
            o_ref[a * sub:(a + 1) * sub, cs] = o.astype(o_ref.dtype)
            lse_rows[a] = jnp.where(lane == h, m + jnp.log(den), lse_rows[a])
    for a in range(nsub):
        lse_ref[a * sub:(a + 1) * sub, :] = lse_rows[a]


def _dilated_pattern(pa, batch, seq, dil):
    width = pa.shape[1] // 3
    heads = width // HEAD_DIM
    sl = seq // dil
    tl = min(512, sl)
    nt = sl // tl
    hb = tl // A_HALF
    last_halo = sl // A_HALF - 1
    view = pa.reshape(batch, sl, dil * 3 * width)

    def main(c):
        return pl.BlockSpec((None, tl, width), lambda b, r, t: (b, t, r * 3 + c))

    def prev(c):
        return pl.BlockSpec((None, A_HALF, width), lambda b, r, t: (b, jnp.maximum(t * hb - 1, 0), r * 3 + c))

    def nxt(c):
        return pl.BlockSpec((None, A_HALF, width),
                            lambda b, r, t: (b, jnp.minimum((t + 1) * hb, last_halo), r * 3 + c))

    kern = functools.partial(_dilated_kernel, tl=tl, seq_len=sl, heads=heads)
    o, lse = pl.pallas_call(
        kern,
        grid=(batch, dil, nt),
        in_specs=[main(0), prev(1), main(1), nxt(1), prev(2), main(2), nxt(2)],
        out_specs=[pl.BlockSpec((None, tl, width), lambda b, r, t: (b, t, r)),
                   pl.BlockSpec((None, tl, LANES), lambda b, r, t: (b, t, r))],
        out_shape=[jax.ShapeDtypeStruct((batch, sl, dil * width), BF16),
                   jax.ShapeDtypeStruct((batch, sl, dil * LANES), F32)],
        compiler_params=_params("arbitrary", "arbitrary", "arbitrary"),
        name=f"dilated_d{dil}",
    )(view, view, view, view, view, view, view)
    return o.reshape(batch * seq, width), lse.reshape(batch * seq, LANES)


def _dilated_combine_kernel(o0_ref, o1_ref, o2_ref, l0_ref, l1_ref, l2_ref, y_ref, *, heads):
    l0, l1, l2 = l0_ref[...], l1_ref[...], l2_ref[...]
    m = jnp.maximum(jnp.maximum(l0, l1), l2)
    e0, e1, e2 = jnp.exp(l0 - m), jnp.exp(l1 - m), jnp.exp(l2 - m)
    inv = 1.0 / (e0 + e1 + e2)
    w0, w1, w2 = e0 * inv, e1 * inv, e2 * inv
    for h in range(heads):
        cs = slice(h * HEAD_DIM, (h + 1) * HEAD_DIM)
        y = (w0[:, h:h + 1] * o0_ref[:, cs].astype(F32) + w1[:, h:h + 1] * o1_ref[:, cs].astype(F32)
             + w2[:, h:h + 1] * o2_ref[:, cs].astype(F32))
        y_ref[:, cs] = y.astype(y_ref.dtype)


def _dilated_attention(pa, batch, seq):
    outs, lses = zip(*[_dilated_pattern(pa, batch, seq, d) for _, d in A_PATTERNS])
    m, width = outs[0].shape
    tm = min(512, m)
    ospec = pl.BlockSpec((tm, width), lambda i: (i, 0))
    lspec = pl.BlockSpec((tm, LANES), lambda i: (i, 0))
    return pl.pallas_call(
        functools.partial(_dilated_combine_kernel, heads=width // HEAD_DIM),
        grid=(m // tm,),
        in_specs=[ospec, ospec, ospec, lspec, lspec, lspec],
        out_specs=ospec,
        out_shape=jax.ShapeDtypeStruct((m, width), BF16),
        compiler_params=_params("arbitrary"),
        name="dilated_combine",
    )(*outs, *lses)


def _pool_kernel(u_ref, w_ref, sc_ref, o_ref, pad_ref, *, seq):
    g = pl.program_id(1)
    pad = 8
    uf = u_ref[...].astype(F32)
    pad_ref[0:pad, :] = jnp.zeros((pad, uf.shape[1]), F32)
    pad_ref[pad + seq:pad + seq + pad, :] = jnp.zeros((pad, uf.shape[1]), F32)
    pad_ref[pad:pad + seq, :] = uf
    pos = lax.broadcasted_iota(jnp.int32, (seq, 1), 0)
    for gi, win in enumerate(B_WINDOWS):
        @pl.when(g == gi)
        def _(win=win):
            before, after = win // 2, win - win // 2
            tot = pad_ref[pad - before:pad - before + seq, :]
            for off in range(-before + 1, after):
                tot = tot + pad_ref[pad + off:pad + off + seq, :]
            cnt = (jnp.minimum(pos + after, seq) - jnp.maximum(pos - before, 0)).astype(F32)
            pooled = tot / cnt - uf
            mixed = jnp.dot(pooled.astype(BF16), w_ref[...].astype(BF16), preferred_element_type=F32)
            o_ref[...] = (mixed * sc_ref[...]).astype(o_ref.dtype)


def _pooling_mixer(pbdx, pool_w, pool_scale, batch, seq, col0):
    ng, cg = pool_w.shape[0], pool_w.shape[1]
    view = pbdx.reshape(batch, seq, pbdx.shape[1])
    cb = col0 // cg
    return pl.pallas_call(
        functools.partial(_pool_kernel, seq=seq),
        grid=(batch, ng),
        in_specs=[pl.BlockSpec((None, seq, cg), lambda b, g: (b, 0, cb + g)),
                  pl.BlockSpec((None, cg, cg), lambda b, g: (g, 0, 0)),
                  pl.BlockSpec((1, cg), lambda b, g: (0, g))],
        out_specs=pl.BlockSpec((None, seq, cg), lambda b, g: (b, 0, g)),
        out_shape=jax.ShapeDtypeStruct((batch, seq, ng * cg), BF16),
        scratch_shapes=[pltpu.VMEM((seq + 16, cg), F32)],
        compiler_params=_params("arbitrary", "arbitrary"),
        name="pooling",
    )(view, pool_w, pool_scale.reshape(1, ng * cg)).reshape(batch * seq, ng * cg)


def _conv_kernel(bg_ref, cg_ref, h_ref, w_ref, o_ref, pad_ref, *, seq):
    pad = 8
    u = cg_ref[...].astype(F32) * h_ref[...].astype(F32)
    cols = u.shape[1]
    pad_ref[0:pad, :] = jnp.zeros((pad, cols), F32)
    pad_ref[pad + seq:pad + seq + pad, :] = jnp.zeros((pad, cols), F32)
    pad_ref[pad:pad + seq, :] = u
    w = w_ref[...]
    conv = (pad_ref[pad - 1:pad - 1 + seq, :] * w[0:1, :] + u * w[1:2, :]
            + pad_ref[pad + 1:pad + 1 + seq, :] * w[2:3, :])
    o_ref[...] = (bg_ref[...].astype(F32) * conv).astype(o_ref.dtype)


def _gated_short_conv(pbdx, conv_w, batch, seq, col0):
    width = conv_w.shape[1]
    cb = 256
    nb = width // cb
    view = pbdx.reshape(batch, seq, pbdx.shape[1])
    base = col0 // cb

    def spec(part):
        return pl.BlockSpec((None, seq, cb), lambda b, c: (b, 0, base + part * nb + c))

    return pl.pallas_call(
        functools.partial(_conv_kernel, seq=seq),
        grid=(batch, nb),
        in_specs=[spec(0), spec(1), spec(2), pl.BlockSpec((3, cb), lambda b, c: (0, c))],
        out_specs=pl.BlockSpec((None, seq, cb), lambda b, c: (b, 0, c)),
        out_shape=jax.ShapeDtypeStruct((batch, seq, width), BF16),
        scratch_shapes=[pltpu.VMEM((seq + 16, cb), F32)],
        compiler_params=_params("arbitrary", "arbitrary"),
        name="short_conv",
    )(view, view, view, conv_w).reshape(batch * seq, width)


def _diff_kernel(q_ref, k_ref, v_ref, lam_ref, nrm_ref, o_ref, vone_ref, *, lam_init, chunk):
    t = pl.program_id(2)
    seq = k_ref.shape[0]
    tq = q_ref.shape[0]

    @pl.when(t == 0)
    def _():
        vone_ref[:, :LANES] = v_ref[...]
        vone_ref[:, LANES:] = jnp.ones((seq, LANES), BF16)

    lq = lam_ref[...].astype(F32)
    lam = (jnp.exp(jnp.sum(lq[0:1] * lq[1:2], axis=-1, keepdims=True))
           - jnp.exp(jnp.sum(lq[2:3] * lq[3:4], axis=-1, keepdims=True)) + lam_init)
    q = q_ref[...]
    lane = lax.broadcasted_iota(jnp.int32, q.shape, 1)
    zero = jnp.zeros_like(q)
    q2 = jnp.concatenate([jnp.where(lane < C_QK_DIM, q, zero), jnp.where(lane >= C_QK_DIM, q, zero)], axis=0)
    m = acc = None
    for c in range(seq // chunk):
        ks = slice(c * chunk, (c + 1) * chunk)
        s = lax.dot_general(q2, k_ref[ks, :], (((1,), (1,)), ((), ())), preferred_element_type=F32)
        mc = jnp.max(s, axis=-1, keepdims=True)
        m_new = mc if m is None else jnp.maximum(m, mc)
        e = jnp.exp2((s - m_new).astype(BF16))
        pv = jnp.dot(e, vone_ref[ks, :], preferred_element_type=F32)
        acc = pv if acc is None else acc * jnp.exp2(m - m_new) + pv
        m = m_new
    out = acc[:, :LANES] / acc[:, LANES:]
    o = out[:tq] - lam * out[tq:]
    o = o * lax.rsqrt(jnp.mean(o * o, axis=-1, keepdims=True) + LN_EPS) * nrm_ref[...] * (1.0 - lam_init)
    o_ref[...] = o.astype(o_ref.dtype)


def _diff_attention(pc, c_lambda, c_norm, batch, seq, layer, tq=2048, chunk=256):
    heads = pc.shape[1] // (3 * LANES)
    tq = min(tq, seq)
    chunk = min(chunk, seq)
    lam_init = 0.8 - 0.6 * math.exp(-0.3 * layer)
    view = pc.reshape(batch, seq, pc.shape[1])
    return pl.pallas_call(
        functools.partial(_diff_kernel, lam_init=lam_init, chunk=chunk),
        grid=(batch, heads, seq // tq),
        in_specs=[pl.BlockSpec((None, tq, LANES), lambda b, h, t: (b, t, 3 * h)),
                  pl.BlockSpec((None, seq, LANES), lambda b, h, t: (b, 0, 3 * h + 1)),
                  pl.BlockSpec((None, seq, LANES), lambda b, h, t: (b, 0, 3 * h + 2)),
                  pl.BlockSpec((4, C_QK_DIM), lambda b, h, t: (0, 0)),
                  pl.BlockSpec((1, LANES), lambda b, h, t: (0, 0))],
        out_specs=pl.BlockSpec((None, tq, LANES), lambda b, h, t: (b, t, h)),
        out_shape=jax.ShapeDtypeStruct((batch, seq, heads * LANES), BF16),
        scratch_shapes=[pltpu.VMEM((seq, 2 * LANES), BF16)],
        compiler_params=_params("arbitrary", "arbitrary", "arbitrary"),
        name="diff_attention",
    )(view, view, view, c_lambda, c_norm.reshape(1, LANES)).reshape(batch * seq, heads * LANES)


def _cross_kernel(q_ref, kv_ref, o_ref, *, heads):
    scale = HEAD_DIM ** -0.5
    width = heads * HEAD_DIM
    for h in range(heads):
        cs = slice(h * HEAD_DIM, (h + 1) * HEAD_DIM)
        k = kv_ref[:, h * HEAD_DIM:(h + 1) * HEAD_DIM]
        v = kv_ref[:, width + h * HEAD_DIM:width + (h + 1) * HEAD_DIM]
        s = lax.dot_general(q_ref[:, cs], k, (((1,), (1,)), ((), ())), preferred_element_type=F32) * scale
        m = jnp.max(s, axis=-1, keepdims=True)
        e = jnp.exp(s - m)
        den = jnp.sum(e, axis=-1, keepdims=True)
        o = jnp.dot(e.astype(BF16), v, preferred_element_type=F32) / den
        o_ref[:, cs] = o.astype(o_ref.dtype)


def _cross_attention(pbdx, kv, batch, seq, col0, width, tq=512):
    mem_len = kv.shape[0] // batch
    tq = min(tq, seq)
    view = pbdx.reshape(batch, seq, pbdx.shape[1])
    return pl.pallas_call(
        functools.partial(_cross_kernel, heads=width // HEAD_DIM),
        grid=(batch, seq // tq),
        in_specs=[pl.BlockSpec((None, tq, width), lambda b, t: (b, t, col0 // width)),
                  pl.BlockSpec((None, mem_len, 2 * width), lambda b, t: (b, 0, 0))],
        out_specs=pl.BlockSpec((None, tq, width), lambda b, t: (b, t, 0)),
        out_shape=jax.ShapeDtypeStruct((batch, seq, width), BF16),
        compiler_params=_params("arbitrary", "arbitrary"),
        name="cross_attention",
    )(view, kv.reshape(batch, mem_len, 2 * width)).reshape(batch * seq, width)


def _merge_kernel(x_ref, ya_ref, yb_ref, yc_ref, yd_ref, yx_ref,
                  g0, g1, g2, g3, g4, p0, p1, p2, p3, p4, b0, b1, b2, b3, b4, o_ref):
    x = x_ref[...]
    merged = None
    for y_ref, g_ref, p_ref, b_ref in ((ya_ref, g0, p0, b0), (yb_ref, g1, p1, b1), (yc_ref, g2, p2, b2),
                                       (yd_ref, g3, p3, b3), (yx_ref, g4, p4, b4)):
        gate = jax.nn.sigmoid(jnp.dot(x, g_ref[...], preferred_element_type=F32) + b_ref[...])
        term = gate * jnp.dot(y_ref[...], p_ref[...], preferred_element_type=F32)
        merged = term if merged is None else merged + term
    o_ref[...] = merged.astype(o_ref.dtype)


def _gated_merge(xb, ys, w_gate, w_branch, w_branch_x, b_gate, tm=512, tn=512):
    m, d = xb.shape
    tm = min(tm, m)
    nb = d // tn
    n_br = 5
    act = [pl.BlockSpec((tm, d), lambda j, i: (i, 0))]
    act += [pl.BlockSpec((tm, y.shape[1]), lambda j, i: (i, 0)) for y in ys]
    gates = [pl.BlockSpec((d, tn), functools.partial(lambda j, i, br: (0, br * nb + j), br=br)) for br in range(n_br)]
    projs = [pl.BlockSpec((None, w_branch.shape[1], tn), functools.partial(lambda j, i, br: (br, 0, j), br=br))
             for br in range(4)]
    projs.append(pl.BlockSpec((w_branch_x.shape[0], tn), lambda j, i: (0, j)))
    biases = [pl.BlockSpec((1, tn), functools.partial(lambda j, i, br: (0, br * nb + j), br=br)) for br in range(n_br)]
    bg2 = b_gate.reshape(1, n_br * d)
    return pl.pallas_call(
        _merge_kernel,
        grid=(nb, m // tm),
        in_specs=act + gates + projs + biases,
        out_specs=pl.BlockSpec((tm, tn), lambda j, i: (i, j)),
        out_shape=jax.ShapeDtypeStruct((m, d), BF16),
        compiler_params=_params("arbitrary", "arbitrary"),
        name="gated_merge",
    )(xb, *ys, *([w_gate] * n_br), *([w_branch] * 4), w_branch_x, *([bg2] * n_br))


def _layer_norm_rows(h, g, b):
    mu = jnp.mean(h, axis=-1, keepdims=True)
    hc = h - mu
    var = jnp.mean(hc * hc, axis=-1, keepdims=True)
    return hc * lax.rsqrt(var + LN_EPS) * g + b


def _out_ln_kernel(mg_ref, w_ref, x_ref, g_ref, b_ref, rw_ref, rb_ref, xo_ref, xb_ref, idx_ref, wt_ref, cnt_ref,
                   *, alpha):
    y = jnp.dot(mg_ref[...], w_ref[...], preferred_element_type=F32)
    xn = _layer_norm_rows(alpha * x_ref[...] + y, g_ref[...], b_ref[...])
    xo_ref[...] = xn
    xb_ref[...] = xn.astype(BF16)
    _route_rows(xn, rw_ref, rb_ref, idx_ref, wt_ref, cnt_ref)


def _out_proj_ln(merged, w_out_b, x, g, b, router_w, router_b, alpha, tm=512):
    m, d = x.shape
    tm = min(tm, m)
    row = pl.BlockSpec((tm, d), lambda i: (i, 0))
    vec = pl.BlockSpec((1, d), lambda i: (0, 0))
    sel = pl.BlockSpec((TOP_K, tm), lambda i: (0, i))
    w_pad = jnp.pad(router_w, ((0, 0), (0, LANES - N_EXPERTS)))
    return pl.pallas_call(
        functools.partial(_out_ln_kernel, alpha=alpha),
        grid=(m // tm,),
        in_specs=[row, pl.BlockSpec((d, d), lambda i: (0, 0)), row, vec, vec,
                  pl.BlockSpec((d, LANES), lambda i: (0, 0)), pl.BlockSpec((N_EXPERTS, 1), lambda i: (0, 0))],
        out_specs=[row, row, sel, sel, pl.BlockSpec((N_EXPERTS, LANES), lambda i: (0, 0))],
        out_shape=[jax.ShapeDtypeStruct((m, d), F32), jax.ShapeDtypeStruct((m, d), BF16),
                   jax.ShapeDtypeStruct((TOP_K, m), jnp.int32), jax.ShapeDtypeStruct((TOP_K, m), F32),
                   jax.ShapeDtypeStruct((N_EXPERTS, LANES), F32)],
        compiler_params=_params("arbitrary"),
        name="out_proj_ln_route",
    )(merged, w_out_b, x, g.reshape(1, d), b.reshape(1, d), w_pad, router_b.reshape(N_EXPERTS, 1))


def _split_bf16(a):
    hi = a.astype(BF16)
    lo = (a - hi.astype(F32)).astype(BF16)
    return hi, lo


def _router_kernel(x_ref, w_ref, b_ref, idx_ref, wt_ref, cnt_ref):
    _route_rows(x_ref[...], w_ref, b_ref, idx_ref, wt_ref, cnt_ref)


def _route_rows(x, w_ref, b_ref, idx_ref, wt_ref, cnt_ref):
    @pl.when(pl.program_id(0) == 0)
    def _():
        cnt_ref[...] = jnp.zeros_like(cnt_ref)

    w = w_ref[...]
    xh, xl = _split_bf16(x)
    wh, wl = _split_bf16(w)
    logits = (jnp.dot(xh, wh, preferred_element_type=F32) + jnp.dot(xl, wh, preferred_element_type=F32)
              + jnp.dot(xh, wl, preferred_element_type=F32))
    lt = logits.T[:N_EXPERTS, :]
    tm = lt.shape[1]
    scores = jax.nn.sigmoid(lt)
    biased = scores + b_ref[...]
    gsz = N_EXPERTS // N_GROUPS
    sub8 = lax.broadcasted_iota(jnp.int32, (gsz, tm), 0).astype(F32)
    grp_rows = []
    for g in range(N_GROUPS):
        blk = biased[g * gsz:(g + 1) * gsz, :]
        m1 = jnp.max(blk, axis=0, keepdims=True)
        i1 = jnp.min(jnp.where(blk == m1, sub8, float(gsz)), axis=0, keepdims=True)
        m2 = jnp.max(jnp.where(sub8 == i1, -jnp.inf, blk), axis=0, keepdims=True)
        grp_rows.append(m1 + m2)
    grp = jnp.concatenate(grp_rows, axis=0)
    subg = lax.broadcasted_iota(jnp.int32, (N_GROUPS, tm), 0).astype(F32)
    gsel = jnp.zeros((N_GROUPS, tm), F32)
    for _ in range(TOP_GROUPS):
        mg = jnp.max(grp, axis=0, keepdims=True)
        ig = jnp.min(jnp.where(grp == mg, subg, float(N_GROUPS)), axis=0, keepdims=True)
        hit = subg == ig
        gsel = jnp.where(hit, 1.0, gsel)
        grp = jnp.where(hit, -jnp.inf, grp)
    emask = jnp.concatenate([jnp.broadcast_to(gsel[g:g + 1, :], (gsz, tm)) for g in range(N_GROUPS)], axis=0)
    cand = jnp.where(emask > 0.5, biased, -jnp.inf)
    sube = lax.broadcasted_iota(jnp.int32, (N_EXPERTS, tm), 0).astype(F32)
    idx_rows, w_rows = [], []
    chosen = jnp.zeros((N_EXPERTS, tm), F32)
    for _ in range(TOP_K):
        mc = jnp.max(cand, axis=0, keepdims=True)
        ic = jnp.min(jnp.where(cand == mc, sube, float(N_EXPERTS)), axis=0, keepdims=True)
        hit = sube == ic
        idx_rows.append(ic)
        w_rows.append(jnp.sum(jnp.where(hit, scores, 0.0), axis=0, keepdims=True))
        chosen = jnp.where(hit, 1.0, chosen)
        cand = jnp.where(hit, -jnp.inf, cand)
    cnt_ref[...] += jnp.sum(chosen, axis=1, keepdims=True)
    wsel = jnp.concatenate(w_rows, axis=0)
    wsel = wsel / jnp.sum(wsel, axis=0, keepdims=True) * ROUTE_SCALE
    idx_ref[...] = jnp.concatenate(idx_rows, axis=0).astype(jnp.int32)
    wt_ref[...] = wsel


def _router(x, router_w, router_b, tm=512):
    m, d = x.shape
    tm = min(tm, m)
    w_pad = jnp.pad(router_w, ((0, 0), (0, LANES - N_EXPERTS)))
    return pl.pallas_call(
        _router_kernel,
        grid=(m // tm,),
        in_specs=[pl.BlockSpec((tm, d), lambda i: (i, 0)),
                  pl.BlockSpec((d, LANES), lambda i: (0, 0)),
                  pl.BlockSpec((N_EXPERTS, 1), lambda i: (0, 0))],
        out_specs=[pl.BlockSpec((TOP_K, tm), lambda i: (0, i)), pl.BlockSpec((TOP_K, tm), lambda i: (0, i)),
                   pl.BlockSpec((N_EXPERTS, LANES), lambda i: (0, 0))],
        out_shape=[jax.ShapeDtypeStruct((TOP_K, m), jnp.int32), jax.ShapeDtypeStruct((TOP_K, m), F32),
                   jax.ShapeDtypeStruct((N_EXPERTS, LANES), F32)],
        compiler_params=_params("arbitrary"),
        name="router",
    )(x, w_pad, router_b.reshape(N_EXPERTS, 1))


def _slot_kernel(idx_ref, base_ref, dest_ref, tri_ref, run_ref):
    tm = idx_ref.shape[1]

    @pl.when(pl.program_id(0) == 0)
    def _():
        row = lax.broadcasted_iota(jnp.int32, (tm, tm), 0)
        col = lax.broadcasted_iota(jnp.int32, (tm, tm), 1)
        tri_ref[...] = jnp.where(row < col, 1.0, 0.0).astype(BF16)
        run_ref[...] = base_ref[...]

    sube = lax.broadcasted_iota(jnp.int32, (N_EXPERTS, tm), 0)
    idx = idx_ref[...]
    base = run_ref[...]
    hits = [sube == idx[k:k + 1, :] for k in range(TOP_K)]
    chosen = jnp.zeros((N_EXPERTS, tm), F32)
    for hit in hits:
        chosen = jnp.where(hit, 1.0, chosen)
    before = jnp.dot(chosen.astype(BF16), tri_ref[...], preferred_element_type=F32)
    slot = base + before
    rows = [jnp.sum(jnp.where(hit, slot, 0.0), axis=0, keepdims=True) for hit in hits]
    run_ref[...] = base + jnp.sum(chosen, axis=1, keepdims=True)
    dest_ref[...] = jnp.concatenate(rows, axis=0).astype(jnp.int32)


def _assign_slots(idx_t, base, tm=512):
    kk, m = idx_t.shape
    tm = min(tm, m)
    return pl.pallas_call(
        _slot_kernel,
        grid=(m // tm,),
        in_specs=[pl.BlockSpec((kk, tm), lambda i: (0, i)), pl.BlockSpec((N_EXPERTS, 1), lambda i: (0, 0))],
        out_specs=pl.BlockSpec((kk, tm), lambda i: (0, i)),
        out_shape=jax.ShapeDtypeStruct((kk, m), jnp.int32),
        scratch_shapes=[pltpu.VMEM((tm, tm), BF16), pltpu.VMEM((N_EXPERTS, 1), F32)],
        compiler_params=_params("arbitrary"),
        name="assign_slots",
    )(idx_t, base.astype(F32).reshape(N_EXPERTS, 1))


def _expert_kernel(be_ref, nb_ref, x_ref, wg_ref, wu_ref, wd_ref, *rest):
    o_ref = rest[-1]
    del be_ref
    i = pl.program_id(0)

    @pl.when(i < nb_ref[0])
    def _():
        x = x_ref[...]
        y = None
        for c in range(wg_ref.shape[1] // MXU_WIDTH):
            cs = slice(c * MXU_WIDTH, (c + 1) * MXU_WIDTH)
            gate = jnp.dot(x, wg_ref[:, cs].astype(BF16), preferred_element_type=F32)
            up = jnp.dot(x, wu_ref[:, cs].astype(BF16), preferred_element_type=F32)
            act = (jax.nn.silu(gate) * up).astype(BF16)
            part = jnp.dot(act, wd_ref[cs, :].astype(BF16), preferred_element_type=F32)
            y = part if y is None else y + part
        o_ref[...] = y.astype(o_ref.dtype)

    @pl.when(i >= nb_ref[0])
    def _():
        o_ref[...] = jnp.zeros_like(o_ref)


def _expert_ffn(xs, block_e, nblocks, w_gate, w_up, w_down, layer, rows, run_after=()):
    ns, d = xs.shape
    de = w_gate.shape[3]
    grid_spec = pltpu.PrefetchScalarGridSpec(
        num_scalar_prefetch=2,
        grid=(ns // rows,),
        in_specs=[pl.BlockSpec((rows, d), lambda i, be, nb: (i, 0)),
                  pl.BlockSpec((None, None, d, de), lambda i, be, nb: (layer, be[i], 0, 0)),
                  pl.BlockSpec((None, None, d, de), lambda i, be, nb: (layer, be[i], 0, 0)),
                  pl.BlockSpec((None, None, de, d), lambda i, be, nb: (layer, be[i], 0, 0))]
        + [pl.BlockSpec(memory_space=pl.ANY)] * len(run_after),
        out_specs=pl.BlockSpec((rows, d), lambda i, be, nb: (i, 0)),
    )
    return pl.pallas_call(
        _expert_kernel,
        grid_spec=grid_spec,
        out_shape=jax.ShapeDtypeStruct((ns, d), BF16),
        compiler_params=_params("arbitrary"),
        name="expert_ffn",
    )(block_e, nblocks, xs, w_gate, w_up, w_down, *run_after)


def _combine_ln_kernel(*refs, alpha):
    yg_refs = refs[:TOP_K]
    wt_ref, sh_ref, x_ref, g_ref, b_ref, xo_ref, xb_ref = refs[TOP_K:]
    wt = wt_ref[...]
    routed = sh_ref[...].astype(F32)
    for k in range(TOP_K):
        routed = routed + wt[:, k:k + 1] * yg_refs[k][...].astype(F32)
    xn = _layer_norm_rows(alpha * x_ref[...] + routed, g_ref[...], b_ref[...])
    xo_ref[...] = xn
    xb_ref[...] = xn.astype(BF16)


def _combine_ln(yg, wt, shared, x, g, b, alpha, tm=256):
    m, d = x.shape
    tm = min(tm, m)
    nb = m // tm
    row = pl.BlockSpec((tm, d), lambda i: (i, 0))
    vec = pl.BlockSpec((1, d), lambda i: (0, 0))
    yg_specs = [pl.BlockSpec((tm, d), functools.partial(lambda i, k: (k * nb + i, 0), k=k)) for k in range(TOP_K)]
    return pl.pallas_call(
        functools.partial(_combine_ln_kernel, alpha=alpha),
        grid=(nb,),
        in_specs=yg_specs + [pl.BlockSpec((tm, TOP_K), lambda i: (i, 0)), row, row, vec, vec],
        out_specs=[row, row],
        out_shape=[jax.ShapeDtypeStruct((m, d), F32), jax.ShapeDtypeStruct((m, d), BF16)],
        compiler_params=_params("arbitrary"),
        name="combine_ln",
    )(*([yg] * TOP_K), wt, shared, x, g.reshape(1, d), b.reshape(1, d))


def _moe_layer(x1, xb1, layer, routing, e_gate, e_up, e_down, s_gate, s_up, s_down, g, b, alpha):
    t, d = x1.shape
    rows = min(MOE_ROWS, t)
    idx_t, wt_t, cnt = routing
    n_assign = t * TOP_K
    i32 = jnp.int32
    counts = cnt[:, 0].astype(i32)
    padded = (counts + rows - 1) // rows * rows
    ends_p = jnp.cumsum(padded)
    pad_e = padded - counts
    dest = _assign_slots(idx_t, ends_p - padded).reshape(n_assign)
    nblk = n_assign // rows + N_EXPERTS
    n_pad = nblk * rows - n_assign
    pad_i = jnp.arange(n_pad, dtype=i32)
    pad_before = jnp.cumsum(pad_e) - pad_e
    pad_slot = pad_i + jnp.sum(jnp.where(pad_before[None, :] <= pad_i[:, None], counts[None, :], 0), axis=1)
    keys = jnp.concatenate([dest, pad_slot])
    toks = jnp.concatenate([jnp.arange(n_assign, dtype=i32) % t, pad_i % t])
    _, slot_tok = lax.sort_key_val(keys, toks)
    blk0 = jnp.arange(nblk, dtype=i32) * rows
    block_e = jnp.minimum(jnp.sum((ends_p[None, :] <= blk0[:, None]).astype(i32), axis=1), N_EXPERTS - 1)
    nused = (ends_p[-1] // rows).astype(i32).reshape(1)
    xs = jnp.take(xb1, slot_tok, axis=0, mode='clip')
    srows = min(2 * MOE_ROWS, t)
    shared = _expert_ffn(xb1, jnp.zeros((t // srows,), i32), jnp.full((1,), t // srows, i32),
                         s_gate[:, None], s_up[:, None], s_down[:, None], layer, srows)
    ys = _expert_ffn(xs, block_e, nused, e_gate, e_up, e_down, layer, rows, run_after=(shared,))
    yg = jnp.take(ys, dest, axis=0, mode='clip')
    return _combine_ln(yg, wt_t.T, shared, x1, g, b, alpha)


def _mixer_layer(x, xb, memb, layer, tabs_a, tabs_c, batch, seq, w_in, w_c, pool_w, pool_scale, c_lambda, c_norm,
                 conv_w, w_mem_kv, merge_w, b_gate, w_out_b, g, b, router_w, router_b, alpha):
    mix = merge_w[1].shape[1]
    xw = merge_w[2].shape[0]
    a_cols, c_cols, d_cols = 3 * mix, 3 * mix, 3 * mix
    c0_b, c0_d = a_cols, a_cols + mix + c_cols
    pa = _proj(xb, w_in, layer, range(3), mix, tabs_a, seq, half=HEAD_DIM // 8,
               rope_blocks=(1,) * (mix // LANES), rope_jmax=2)
    tn_c = 12 * LANES
    pc = _proj(xb, w_c, 0, range(c_cols // tn_c), tn_c, tabs_c, seq,
               half=C_QK_DIM // 8, rope_blocks=(1, 2, 0) * 4, tm=512)
    blocks = list(range(c0_b // mix, c0_b // mix + 1)) + list(range(c0_d // mix, c0_d // mix + d_cols // mix))
    pbd = _proj(xb, w_in, layer, blocks, mix, tabs_a, seq)
    px = _proj(xb, w_in, layer, [(c0_d + d_cols) // xw], xw, tabs_a, seq)
    ya = _dilated_attention(pa, batch, seq)
    yb = _pooling_mixer(pbd, pool_w[layer], pool_scale[layer], batch, seq, 0)
    yc = _diff_attention(pc, c_lambda[layer], c_norm[layer], batch, seq, layer)
    yd = _gated_short_conv(pbd, conv_w[layer], batch, seq, mix)
    kv = _proj(memb, w_mem_kv, layer, range(2), xw, tabs_a, seq)
    yx = _cross_attention(px, kv, batch, seq, 0, xw)
    merged = _gated_merge(xb, (ya, yb, yc, yd, yx), *merge_w, b_gate[layer])
    return _out_proj_ln(merged, w_out_b, x, g[layer], b[layer], router_w[layer], router_b[layer], alpha)


def kernel(x, mem, w_in, pool_w, pool_scale, c_lambda, c_norm, conv_w, w_mem_kv, w_branch, w_branch_x,
           w_gate, b_gate, w_out, ln1_g, ln1_b, router_w, router_b, e_gate, e_up, e_down, s_gate, s_up,
           s_down, ln2_g, ln2_b):
    batch, seq, d = x.shape
    depth = w_in.shape[0]
    alpha = (2.0 * depth) ** 0.25
    tabs_a = _rope_tables(seq, HEAD_DIM // 4, LANES)
    tabs_ck = _rope_tables(seq, C_QK_DIM // 4, C_QK_DIM)
    q_scale = C_QK_DIM ** -0.5 * math.log2(math.e)
    tabs_c = tuple(tab * q_scale for tab in tabs_ck) + tabs_ck
    mix = w_branch.shape[2]
    groups = BATCH_GROUPS if batch % BATCH_GROUPS == 0 else 1
    gb = batch // groups
    state = []
    for gi in range(groups):
        xg = x[gi * gb:(gi + 1) * gb].reshape(gb * seq, d)
        state.append((xg, xg.astype(BF16), mem[gi * gb:(gi + 1) * gb].reshape(-1, d).astype(BF16)))
    for l in range(depth):
        w_c = w_in[l:l + 1, :, 4 * mix:7 * mix]
        merge_w = (w_gate[l].astype(BF16), w_branch[l].astype(BF16), w_branch_x[l].astype(BF16))
        w_out_b = w_out[l].astype(BF16)
        for gi in range(groups):
            xf, xb, memb = state[gi]
            xf, xb, *routing = _mixer_layer(xf, xb, memb, l, tabs_a, tabs_c, gb, seq, w_in, w_c, pool_w, pool_scale,
                                            c_lambda, c_norm, conv_w, w_mem_kv, merge_w, b_gate, w_out_b,
                                            ln1_g, ln1_b, router_w, router_b, alpha)
            xf, xb = _moe_layer(xf, xb, l, routing, e_gate, e_up, e_down, s_gate,
                                s_up, s_down, ln2_g[l], ln2_b[l], alpha)
            state[gi] = (xf, xb, memb)
    return jnp.concatenate([s[0] for s in state], axis=0).reshape(batch, seq, d)
```
